```python
import functools
import jax, jax.numpy as jnp
from jax import lax
import numpy as np

D_MODEL = 1024
BATCH = 4
SEQ = 4096
DEPTH = 1
DEC_BATCH = 128
DEC_SEQ = 4
PAST_LEN = 2048
PAGE_SIZE = 128

HEAD_DIM = 64
A_HEADS = 8
A_WIDTH = A_HEADS * HEAD_DIM
CHUNK = 128
B_HEADS = 8
B_KV_HEADS = 4
B_WIDTH = B_HEADS * HEAD_DIM
KV_WIDTH = B_KV_HEADS * HEAD_DIM
MIX_WIDTH = A_WIDTH + B_WIDTH
IDX_HEADS = 4
IDX_DIM = 64
TOPK_MAX = 256
Q_BLOCK = 128
ROPE_THETA = 500000.0
ROT_DIM = HEAD_DIM // 4
ATTN_SCALE = HEAD_DIM ** -0.5
N_GROUPS = 4
EXPERTS_PER_GROUP = 8
N_EXPERTS = N_GROUPS * EXPERTS_PER_GROUP
TOP_K_EXPERTS = 2
D_EXPERT = 512
DEEPNORM_ALPHA = (2.0 * DEPTH) ** 0.25
DEEPNORM_BETA = (8.0 * DEPTH) ** -0.25
LN_EPS = 1e-5
S_AU = A_WIDTH
S_AV = S_AU + A_WIDTH
S_Q = S_AV + B_WIDTH
S_K = S_Q + KV_WIDTH
S_V = S_K + KV_WIDTH
S_QI = S_V + IDX_HEADS * IDX_DIM
S_KI = S_QI + IDX_DIM
IN_WIDTH = S_KI + IDX_HEADS
IN_SPLITS = (S_AU, S_AV, S_Q, S_K, S_V, S_QI, S_KI)

kernel_name = 'hymba_gmlp_dsa_hmoe_step'


def layer_norm(x, g, b):
    xf = x.astype(jnp.float32)
    mu = jnp.mean(xf, axis=-1, keepdims=True)
    var = jnp.mean(jnp.square(xf - mu), axis=-1, keepdims=True)
    y = (xf - mu) * lax.rsqrt(var + LN_EPS)
    return (y * g.astype(jnp.float32) + b.astype(jnp.float32)).astype(x.dtype)


def rotary(x, pos):
    half = ROT_DIM // 2
    inv_freq = ROPE_THETA ** (-jnp.arange(half, dtype=jnp.float32) * 2.0 / ROT_DIM)
    ang = pos.astype(jnp.float32)[:, None] * inv_freq[None, :]
    cos = jnp.cos(ang)[:, None, :]
    sin = jnp.sin(ang)[:, None, :]
    x1 = x[..., :half].astype(jnp.float32)
    x2 = x[..., half:ROT_DIM].astype(jnp.float32)
    rot = jnp.concatenate([x1 * cos - x2 * sin, x2 * cos + x1 * sin], axis=-1).astype(x.dtype)
    return jnp.concatenate([rot, x[..., ROT_DIM:]], axis=-1)


def chunk_spatial_gate(u, v, w_s, b_s):
    B, T, H, Dh = v.shape
    C = min(T, CHUNK)
    nc = T // C
    w = jnp.tril(w_s[:, :C, :C])
    vb = v.reshape(B, nc, C, H, Dh)
    s = jnp.einsum('hts,bcshd->bcthd', w, vb) + jnp.transpose(b_s[:, :C])[:, :, None]
    return (u * s.reshape(B, T, H, Dh)).reshape(B, T, H * Dh)


def index_select(qi, ki_all, wi, tpos, topk):
    L = ki_all.shape[1]
    s = jax.nn.relu(jnp.einsum('bthd,bsd->bths', qi, ki_all).astype(jnp.float32))
    score = jnp.einsum('bths,bth->bts', s, wi.astype(jnp.float32))
    adm = jnp.arange(L)[None, :] <= tpos[:, None]
    score = jnp.where(adm[None], score, -jnp.inf)
    _, sel = lax.top_k(score, topk)
    valid = sel <= tpos[None, :, None]
    return sel, valid


def attend(q, ks, vs, valid):
    B, T, H, Dh = q.shape
    qg = q.reshape(B, T, B_KV_HEADS, H // B_KV_HEADS, Dh)
    s = jnp.einsum('btngd,btknd->btngk', qg, ks).astype(jnp.float32) * ATTN_SCALE
    s = jnp.where(valid[:, :, None, None, :], s, -jnp.inf)
    p = jax.nn.softmax(s, axis=-1).astype(vs.dtype)
    o = jnp.einsum('btngk,btknd->btngd', p, vs)
    return o.reshape(B, T, H * Dh)


def sparse_attention_prompt(q, k, v, qi, ki, wi):
    B, T = q.shape[:2]
    topk = min(TOPK_MAX, T // 4)
    nb = T // Q_BLOCK
    tpos = jnp.arange(T).reshape(nb, Q_BLOCK)
    bidx = jnp.arange(B)[:, None, None]

    def blocks(a):
        return a.reshape((B, nb, Q_BLOCK) + a.shape[2:]).swapaxes(0, 1)

    def one_block(args):
        qb, qib, wib, tb = args
        sel, valid = index_select(qib, ki, wib, tb, topk)
        return attend(qb, k[bidx, sel], v[bidx, sel], valid)

    out = lax.map(one_block, (blocks(q), blocks(qi), blocks(wi), tpos))
    return out.swapaxes(0, 1).reshape(B, T, B_WIDTH)


def sparse_attention_sample(q, k, v, qi, ki, wi, k_pool, v_pool, ki_pool, page_table):
    DB, T = q.shape[:2]
    page = k_pool.shape[1]
    past = page_table.shape[1] * page
    topk = min(TOPK_MAX, (past + T) // 4)
    ki_past = ki_pool[page_table].reshape(DB, past, IDX_DIM)
    ki_all = jnp.concatenate([ki_past, ki], axis=1)
    tpos = past + jnp.arange(T)
    sel, valid = index_select(qi, ki_all, wi, tpos, topk)
    bidx = jnp.arange(DB)[:, None, None]
    sp = jnp.minimum(sel, past - 1)
    phys = page_table[bidx, sp // page]
    off = sp % page
    sn = jnp.clip(sel - past, 0, T - 1)
    in_past = (sel < past)[..., None, None]
    ks = jnp.where(in_past, k_pool[phys, off], k[bidx, sn])
    vs = jnp.where(in_past, v_pool[phys, off], v[bidx, sn])
    return attend(q, ks, vs, valid)


def hier_moe(x, w_group_router, w_expert_router, w_gate, w_up, w_down):
    N = x.shape[0]
    gp = jax.nn.softmax((x @ w_group_router).astype(jnp.float32), axis=-1)
    g_sel = jnp.argmax(gp, axis=-1)
    g_w = jnp.take_along_axis(gp, g_sel[:, None], axis=-1)
    el = jnp.einsum('nd,dge->nge', x, w_expert_router).astype(jnp.float32)
    el = jnp.take_along_axis(el, g_sel[:, None, None], axis=1)[:, 0]
    top_p, top_e = lax.top_k(jax.nn.softmax(el, axis=-1), TOP_K_EXPERTS)
    wts = (g_w * top_p / jnp.sum(top_p, axis=-1, keepdims=True)).reshape(-1)
    eid = (g_sel[:, None] * EXPERTS_PER_GROUP + top_e).reshape(-1)
    tok = jnp.repeat(jnp.arange(N), TOP_K_EXPERTS)
    order = jnp.argsort(eid)
    tok_s = tok[order]
    sizes = jnp.bincount(eid, length=N_EXPERTS).astype(jnp.int32)
    xs = x[tok_s]
    hid = jax.nn.silu(lax.ragged_dot(xs, w_gate, sizes)) * lax.ragged_dot(xs, w_up, sizes)
    ys = lax.ragged_dot(hid, w_down, sizes) * wts[order][:, None].astype(x.dtype)
    return jax.ops.segment_sum(ys, tok_s, num_segments=N)


def decoder_layer(x, c, pos, attn_core, w_ada, b_ada, w_in, a_ln_g, a_ln_b, w_spatial, b_spatial,
                  w_out, ln1_g, ln1_b, w_group_router, w_expert_router, w_gate, w_up, w_down,
                  ln2_g, ln2_b):
    B, T, D = x.shape
    mod = (jax.nn.silu(c) @ w_ada + b_ada)[:, None, :]
    shift1, scale1, gate1, shift2, scale2, gate2 = jnp.split(mod, 6, axis=-1)
    h = x * (1 + scale1) + shift1
    au, av, q, k, v, qi, ki, wi = jnp.split(h @ w_in, IN_SPLITS, axis=-1)
    u = jax.nn.gelu(au).reshape(B, T, A_HEADS, HEAD_DIM)
    vg = layer_norm(jax.nn.gelu(av).reshape(B, T, A_HEADS, HEAD_DIM), a_ln_g, a_ln_b)
    q = rotary(q.reshape(B, T, B_HEADS, HEAD_DIM), pos)
    k = rotary(k.reshape(B, T, B_KV_HEADS, HEAD_DIM), pos)
    v = v.reshape(B, T, B_KV_HEADS, HEAD_DIM)
    qi = rotary(qi.reshape(B, T, IDX_HEADS, IDX_DIM), pos)
    ki = rotary(ki[:, :, None, :], pos)[:, :, 0, :]
    a_out = chunk_spatial_gate(u, vg, w_spatial, b_spatial)
    b_out = attn_core(q, k, v, qi, ki, wi)
    mixed = jnp.concatenate([a_out, b_out], axis=-1) @ w_out
    x = layer_norm(DEEPNORM_ALPHA * x + gate1 * mixed, ln1_g, ln1_b)
    h = x * (1 + scale2) + shift2
    f = hier_moe(h.reshape(B * T, D), w_group_router, w_expert_router, w_gate, w_up, w_down).reshape(B, T, D)
    x = layer_norm(DEEPNORM_ALPHA * x + gate2 * f, ln2_g, ln2_b)
    return x, (k, v, ki, vg)


def _normal(key, shape, scale):
    return jax.random.normal(key, shape, jnp.float32) * scale


def setup_inputs(seed: int = 0) -> dict:
    key = jax.random.key(seed)
    ks = jax.random.split(key, 26)
    n_pages = PAST_LEN // PAGE_SIZE
    n_used = DEC_BATCH * n_pages
    n_pool = n_used + max(1, n_used // 4)
    L, D = DEPTH, D_MODEL
    page_table = jax.random.permutation(ks[7], n_pool)[:n_used].reshape(DEC_BATCH, n_pages).astype(jnp.int32)
    return {
        'x_prompt': _normal(ks[0], (BATCH, SEQ, D), 1.0),
        'x_sample': _normal(ks[1], (DEC_BATCH, DEC_SEQ, D), 1.0),
        'cache_k': _normal(ks[2], (L, n_pool, PAGE_SIZE, B_KV_HEADS, HEAD_DIM), 1.0),
        'cache_v': _normal(ks[3], (L, n_pool, PAGE_SIZE, B_KV_HEADS, HEAD_DIM), 1.0),
        'cache_kidx': _normal(ks[4], (L, n_pool, PAGE_SIZE, IDX_DIM), 1.0),
        'page_table': page_table,
        'c_prompt': _normal(ks[5], (BATCH, D), 1.0),
        'c_sample': _normal(ks[6], (DEC_BATCH, D), 1.0),
        'w_ada': _normal(ks[8], (L, D, 6 * D), 0.5 * D ** -0.5),
        'b_ada': _normal(ks[9], (L, 6 * D), 0.02),
        'w_in': _normal(ks[10], (L, D, IN_WIDTH), D ** -0.5),
        'a_ln_g': 1.0 + _normal(ks[11], (L, A_HEADS, HEAD_DIM), 0.05),
        'a_ln_b': _normal(ks[12], (L, A_HEADS, HEAD_DIM), 0.02),
        'w_spatial': _normal(ks[13], (L, A_HEADS, CHUNK, CHUNK), 0.5 * CHUNK ** -0.5),
        'b_spatial': 1.0 + _normal(ks[14], (L, A_HEADS, CHUNK), 0.1),
        'w_out': _normal(ks[15], (L, MIX_WIDTH, D), DEEPNORM_BETA * MIX_WIDTH ** -0.5),
        'ln1_g': 1.0 + _normal(ks[16], (L, D), 0.05),
        'ln1_b': _normal(ks[17], (L, D), 0.02),
        'w_group_router': _normal(ks[18], (L, D, N_GROUPS), D ** -0.5),
        'w_expert_router': _normal(ks[19], (L, D, N_GROUPS, EXPERTS_PER_GROUP), D ** -0.5),
        'w_gate': _normal(ks[20], (L, N_EXPERTS, D, D_EXPERT), D ** -0.5),
        'w_up': _normal(ks[21], (L, N_EXPERTS, D, D_EXPERT), D ** -0.5),
        'w_down': _normal(ks[22], (L, N_EXPERTS, D_EXPERT, D), DEEPNORM_BETA * D_EXPERT ** -0.5),
        'ln2_g': 1.0 + _normal(ks[23], (L, D), 0.05),
        'ln2_b': _normal(ks[24], (L, D), 0.02),
    }


def reference(x_prompt, x_sample, cache_k, cache_v, cache_kidx, page_table, c_prompt, c_sample,
              w_ada, b_ada, w_in, a_ln_g, a_ln_b, w_spatial, b_spatial, w_out, ln1_g, ln1_b,
              w_group_router, w_expert_router, w_gate, w_up, w_down, ln2_g, ln2_b):
    past = page_table.shape[1] * cache_k.shape[2]
    pos_prompt = jnp.arange(x_prompt.shape[1])
    pos_sample = past + jnp.arange(x_sample.shape[1])
    y_prompt, y_sample = x_prompt, x_sample
    kp, vp, kip, ksm, vsm, kism, cvs = [], [], [], [], [], [], []
    for l in range(DEPTH):
        weights = (w_ada[l], b_ada[l], w_in[l], a_ln_g[l], a_ln_b[l], w_spatial[l], b_spatial[l],
                   w_out[l], ln1_g[l], ln1_b[l], w_group_router[l], w_expert_router[l],
                   w_gate[l], w_up[l], w_down[l], ln2_g[l], ln2_b[l])
        y_prompt, (k_p, v_p, ki_p, _) = decoder_layer(
            y_prompt, c_prompt, pos_prompt, sparse_attention_prompt, *weights)
        sample_core = functools.partial(sparse_attention_sample, k_pool=cache_k[l], v_pool=cache_v[l],
                                        ki_pool=cache_kidx[l], page_table=page_table)
        y_sample, (k_s, v_s, ki_s, cv_s) = decoder_layer(
            y_sample, c_sample, pos_sample, sample_core, *weights)
        kp.append(k_p); vp.append(v_p); kip.append(ki_p)
        ksm.append(k_s); vsm.append(v_s); kism.append(ki_s); cvs.append(cv_s)
    return (y_prompt, y_sample, jnp.stack(kp), jnp.stack(vp), jnp.stack(kip),
            jnp.stack(ksm), jnp.stack(vsm), jnp.stack(kism), jnp.stack(cvs))
```

```python
import functools

import jax
import jax.numpy as jnp
from jax import lax
from jax.experimental import pallas as pl
from jax.experimental.pallas import tpu as pltpu

F32 = jnp.float32
BF16 = jnp.bfloat16
I32 = jnp.int32

D_MODEL = 1024
HEAD_DIM = 64
A_HEADS = 8
A_WIDTH = A_HEADS * HEAD_DIM
CHUNK = 128
B_HEADS = 8
B_KV_HEADS = 4
B_GROUP = B_HEADS // B_KV_HEADS
B_WIDTH = B_HEADS * HEAD_DIM
KV_WIDTH = B_KV_HEADS * HEAD_DIM
IDX_HEADS = 4
IDX_DIM = 64
TOPK_MAX = 256
ROPE_THETA = 500000.0
ROT_DIM = HEAD_DIM // 4
ROT_HALF = ROT_DIM // 2
ATTN_SCALE = HEAD_DIM ** -0.5
N_GROUPS = 4
EXPERTS_PER_GROUP = 8
N_EXPERTS = N_GROUPS * EXPERTS_PER_GROUP
D_EXPERT = 512
LN_EPS = 1e-5

LANES = 128
SUBLANES = 8
ROW_CHUNKS = D_MODEL // LANES

C_AU, C_AV, C_Q, C_K, C_V, C_QI, C_KI = 0, 512, 1024, 1536, 1792, 2048, 2304
IN_WIDTH = C_KI + IDX_DIM + IDX_HEADS
IN_PAD = 2432
WI_LANE = IDX_DIM

INT_MIN = -(2 ** 31)
VMEM_LIMIT = 48 * 1024 * 1024


def _dot(a, b):
    return jnp.dot(a, b, preferred_element_type=F32)


def _dot_nt(a, b):
    return lax.dot_general(a, b, (((1,), (1,)), ((), ())), preferred_element_type=F32)


def _split_dot(x, m):
    hi = x.astype(BF16)
    lo = (x - hi.astype(F32)).astype(BF16)
    return _dot(hi, m) + _dot(lo, m)


def _ada_kernel(c_ref, w_ref, b_ref, o_ref):
    s = jax.nn.silu(c_ref[...]).astype(BF16)
    o_ref[...] = _dot(s, w_ref[...].astype(BF16)) + b_ref[...]


def _ada(c_all, w_ada, b_ada):
    rows = c_all.shape[0]
    n_out = w_ada.shape[1]
    tn = 1024
    return pl.pallas_call(
        _ada_kernel,
        grid=(n_out // tn,),
        in_specs=[
            pl.BlockSpec((rows, D_MODEL), lambda j: (0, 0)),
            pl.BlockSpec((D_MODEL, tn), lambda j: (0, j)),
            pl.BlockSpec((1, tn), lambda j: (0, j)),
        ],
        out_specs=pl.BlockSpec((rows, tn), lambda j: (0, j)),
        out_shape=jax.ShapeDtypeStruct((rows, n_out), F32),
        compiler_params=pltpu.CompilerParams(
            dimension_semantics=("arbitrary",), vmem_limit_bytes=VMEM_LIMIT),
        name="ada",
    )(c_all, w_ada, b_ada)


def _rotate(x, rt, blk):
    c = rt[:, blk * LANES:(blk + 1) * LANES]
    s_lo = rt[:, 256 + blk * LANES:256 + (blk + 1) * LANES]
    s_hi = rt[:, 512 + blk * LANES:512 + (blk + 1) * LANES]
    outs = []
    for j in range(x.shape[1] // LANES):
        xb = x[:, j * LANES:(j + 1) * LANES]
        up = pltpu.roll(xb, LANES - ROT_HALF, 1)
        dn = pltpu.roll(xb, ROT_HALF, 1)
        outs.append(xb * c + up * s_lo + dn * s_hi)
    return outs[0] if len(outs) == 1 else jnp.concatenate(outs, axis=1)


def _head_ln(g, mavg, gam, bet):
    def seg_mean(x):
        parts = [_split_dot(x[:, j * 256:(j + 1) * 256], mavg) for j in range(A_WIDTH // 256)]
        return jnp.concatenate(parts, axis=1)
    mu = seg_mean(g)
    d = g - mu
    var = seg_mean(d * d)
    return d * lax.rsqrt(var + LN_EPS) * gam + bet


def _project(h, w_ref, rt, mavg, gam, bet):
    u = jax.nn.gelu(_dot(h, w_ref[:, C_AU:C_AV]))
    vg = _head_ln(jax.nn.gelu(_dot(h, w_ref[:, C_AV:C_Q])), mavg, gam, bet)
    q = _rotate(_dot(h, w_ref[:, C_Q:C_K]), rt, 0) * ATTN_SCALE
    k = _rotate(_dot(h, w_ref[:, C_K:C_V]), rt, 0)
    v = _dot(h, w_ref[:, C_V:C_QI])
    qi = _rotate(_dot(h, w_ref[:, C_QI:C_KI]), rt, 0)
    kiwi = _rotate(_dot(h, w_ref[:, C_KI:IN_PAD]), rt, 1)
    return u, vg, q, k, v, qi, kiwi


def _front_prompt_kernel(x_ref, mod_ref, w_ref, rt_ref, mavg_ref, gam_ref, bet_ref, wsp_ref, bsp_ref,
                         a_ref, q_ref, k_ref, v_ref, kb_ref, vb_ref, qi_ref, kiwi_ref):
    shift = mod_ref[0:1, :]
    scale = mod_ref[1:2, :]
    h = (x_ref[...] * (1.0 + scale) + shift).astype(BF16)
    u, vg, q, k, v, qi, kiwi = _project(h, w_ref, rt_ref[...], mavg_ref[...], gam_ref[...], bet_ref[...])
    q_ref[...] = q.astype(BF16)
    k_ref[...] = k
    v_ref[...] = v
    kb_ref[...] = k.astype(BF16)
    vb_ref[...] = v.astype(BF16)
    qi_ref[...] = qi.astype(BF16)
    kiwi_ref[...] = kiwi

    rows = lax.broadcasted_iota(I32, (CHUNK, 2 * CHUNK), 0)
    cols = lax.broadcasted_iota(I32, (CHUNK, 2 * CHUNK), 1) % CHUNK
    causal = cols <= rows
    lane = lax.broadcasted_iota(I32, (CHUNK, LANES), 1)
    tt = x_ref.shape[0]
    for cidx in range(tt // CHUNK):
        rs = slice(cidx * CHUNK, (cidx + 1) * CHUNK)
        blocks = []
        for p in range(A_HEADS // 2):
            wcat = jnp.where(causal, wsp_ref[p], 0.0).astype(BF16)
            vb = vg[rs, p * LANES:(p + 1) * LANES]
            rhs = jnp.concatenate([jnp.where(lane < HEAD_DIM, vb, 0.0),
                                   jnp.where(lane >= HEAD_DIM, vb, 0.0)], axis=0).astype(BF16)
            blocks.append(_dot(wcat, rhs))
        s = jnp.concatenate(blocks, axis=1) + bsp_ref[...]
        a_ref[rs, :] = (u[rs, :] * s).astype(BF16)


def _front_prompt(x, mod, w_in, rt, mavg, gam, bet, wsp, bsp, tt):
    b, t, _ = x.shape
    grid = (b, t // tt)
    tok = lambda w: pl.BlockSpec((None, tt, w), lambda i, j: (i, j, 0))
    const2 = lambda a: pl.BlockSpec(a.shape, lambda i, j: (0,) * a.ndim)
    out_shapes = (
        jax.ShapeDtypeStruct((b, t, A_WIDTH), BF16),
        jax.ShapeDtypeStruct((b, t, B_WIDTH), BF16),
        jax.ShapeDtypeStruct((b, t, KV_WIDTH), F32),
        jax.ShapeDtypeStruct((b, t, KV_WIDTH), F32),
        jax.ShapeDtypeStruct((b, t, KV_WIDTH), BF16),
        jax.ShapeDtypeStruct((b, t, KV_WIDTH), BF16),
        jax.ShapeDtypeStruct((b, t, IDX_HEADS * IDX_DIM), BF16),
        jax.ShapeDtypeStruct((b, t, LANES), F32),
    )
    return pl.pallas_call(
        _front_prompt_kernel,
        grid=grid,
        in_specs=[
            tok(D_MODEL),
            pl.BlockSpec((None, 6, D_MODEL), lambda i, j: (i, 0, 0)),
            const2(w_in),
            pl.BlockSpec((tt, 768), lambda i, j: (j, 0)),
            const2(mavg), const2(gam), const2(bet), const2(wsp), const2(bsp),
        ],
        out_specs=[tok(A_WIDTH), tok(B_WIDTH), tok(KV_WIDTH), tok(KV_WIDTH), tok(KV_WIDTH), tok(KV_WIDTH),
                   tok(IDX_HEADS * IDX_DIM), tok(LANES)],
        out_shape=out_shapes,
        compiler_params=pltpu.CompilerParams(
            dimension_semantics=("arbitrary", "arbitrary"), vmem_limit_bytes=VMEM_LIMIT),
        name="front_prompt",
    )(x, mod, w_in, rt, mavg, gam, bet, wsp, bsp)


def _order_key(score):
    score = jnp.where(score == 0.0, 0.0, score)
    bits = pltpu.bitcast(score, I32)
    return bits ^ ((bits >> 31) & 0x7FFFFFFF)


def _bcast_count(cnt):
    tot = jnp.sum(cnt.astype(F32), axis=1, keepdims=True)
    return jnp.broadcast_to(tot, cnt.shape).astype(I32)


def _select_threshold(count_ge, count_gt_eq_lt, rows, topk, idx_bits):
    def bit_step(i, carry):
        prefix, n_ge = carry
        bit = lax.shift_left(jnp.int32(1), 31 - i)
        cand_u = prefix | bit
        tot = _bcast_count(count_ge(cand_u ^ INT_MIN))
        take = tot >= topk
        return jnp.where(take, cand_u, prefix), jnp.where(take, tot, n_ge)

    zero = jnp.zeros((rows, LANES), I32)
    prefix, n_ge = lax.fori_loop(0, 32, bit_step, (zero, jnp.full((rows, LANES), topk, I32)))
    thr = prefix ^ INT_MIN
    tied = (n_ge > topk) & (thr > INT_MIN)
    thr = jnp.maximum(thr, INT_MIN + 1)
    big = jnp.full((rows, LANES), 1 << idx_bits, I32)

    def resolve_ties(_):
        n_gt, _ = count_gt_eq_lt(thr, big)
        need = topk - _bcast_count(n_gt)

        def idx_step(i, cut):
            cand = cut | lax.shift_left(jnp.int32(1), idx_bits - 1 - i)
            _, n_eq = count_gt_eq_lt(thr, cand)
            return jnp.where(_bcast_count(n_eq) <= need, cand, cut)

        cut = lax.fori_loop(0, idx_bits, idx_step, jnp.zeros((rows, LANES), I32))
        return jnp.where(tied, cut, big)

    any_tied = jnp.max(tied.astype(F32)) > 0.0
    cut = lax.cond(any_tied, resolve_ties, lambda _: big, 0)
    return thr, cut


def _attn_prompt_kernel(qi_ref, kiwiq_ref, q_ref, kiwi_ref, kb_ref, vb_ref, o_ref,
                        key_ref, m_ref, l_ref, acc_ref, *, tq, kc, topk, idx_bits):
    j = pl.program_id(1)
    q0 = j * tq
    n_kc = (q0 + tq + kc - 1) // kc
    sub = kc // LANES

    qi = qi_ref[...]
    qi4 = jnp.concatenate([qi[:, h * IDX_DIM:(h + 1) * IDX_DIM] for h in range(IDX_HEADS)], axis=0)
    kiwiq = kiwiq_ref[...]
    wis = [kiwiq[:, WI_LANE + h:WI_LANE + h + 1] for h in range(IDX_HEADS)]
    qpos = q0 + lax.broadcasted_iota(I32, (tq, kc), 0)
    lane_pos = lax.broadcasted_iota(I32, (tq, kc), 1)

    def score_chunk(c, _):
        k0 = pl.multiple_of(c * kc, kc)
        ki_c = kiwi_ref[pl.ds(k0, kc), :][:, :IDX_DIM].astype(BF16)
        s = _dot_nt(qi4, ki_c)
        tot = jnp.maximum(s[0:tq], 0.0) * wis[0]
        for h in range(1, IDX_HEADS):
            tot = tot + jnp.maximum(s[h * tq:(h + 1) * tq], 0.0) * wis[h]
        key = jnp.where(k0 + lane_pos <= qpos, _order_key(tot), INT_MIN)
        key_ref[c] = key
        return 0

    lax.fori_loop(0, n_kc, score_chunk, 0)

    def count_ge(cand):
        def body(c, cnt):
            key = key_ref[c]
            for s_ in range(sub):
                cnt = cnt + jnp.where(key[:, s_ * LANES:(s_ + 1) * LANES] >= cand, 1, 0)
            return cnt
        return lax.fori_loop(0, n_kc, body, jnp.zeros((tq, LANES), I32))

    lane128 = lax.broadcasted_iota(I32, (tq, LANES), 1)

    def count_gt_eq_lt(thr, pos):
        def body(c, carry):
            n_gt, n_eq = carry
            key = key_ref[c]
            for s_ in range(sub):
                kk = key[:, s_ * LANES:(s_ + 1) * LANES]
                kpos = c * kc + s_ * LANES + lane128
                n_gt = n_gt + jnp.where(kk > thr, 1, 0)
                n_eq = n_eq + jnp.where((kk == thr) & (kpos < pos), 1, 0)
            return n_gt, n_eq
        z = jnp.zeros((tq, LANES), I32)
        return lax.fori_loop(0, n_kc, body, (z, z))

    thr, cut = _select_threshold(count_ge, count_gt_eq_lt, tq, topk, idx_bits)
    thr_b = jnp.concatenate([thr] * sub, axis=1)
    cut_b = jnp.concatenate([cut] * sub, axis=1)

    q = q_ref[...]
    qh = [q[:, h * HEAD_DIM:(h + 1) * HEAD_DIM] for h in range(B_HEADS)]
    m_ref[...] = jnp.full(m_ref.shape, jnp.finfo(F32).min, F32)
    l_ref[...] = jnp.zeros(l_ref.shape, F32)
    acc_ref[...] = jnp.zeros(acc_ref.shape, F32)

    def attend_chunk(c, _):
        k0 = pl.multiple_of(c * kc, kc)
        key = key_ref[c]
        sel = (key > thr_b) | ((key == thr_b) & (k0 + lane_pos < cut_b))
        bias = jnp.where(sel, 0.0, -jnp.inf)
        kblk = kb_ref[pl.ds(k0, kc), :]
        vblk = vb_ref[pl.ds(k0, kc), :]
        for n in range(B_KV_HEADS):
            k_n = kblk[:, n * HEAD_DIM:(n + 1) * HEAD_DIM]
            v_n = vblk[:, n * HEAD_DIM:(n + 1) * HEAD_DIM]
            for g in range(B_GROUP):
                h = n * B_GROUP + g
                s = _dot_nt(qh[h], k_n) + bias
                m_old = m_ref[h]
                m_new = jnp.maximum(m_old, jnp.max(s, axis=1, keepdims=True))
                alpha = jnp.exp(m_old - m_new)
                p = jnp.exp(s - m_new)
                l_ref[h] = alpha * l_ref[h] + jnp.sum(p, axis=1, keepdims=True)
                acc_ref[h] = alpha * acc_ref[h] + _dot(p.astype(BF16), v_n)
                m_ref[h] = m_new
        return 0

    lax.fori_loop(0, n_kc, attend_chunk, 0)
    o_ref[...] = jnp.concatenate([acc_ref[h] / l_ref[h] for h in range(B_HEADS)], axis=1).astype(BF16)


def _attn_prompt(qi, kiwi, q, kb, vb, tq, kc):
    b, t, _ = q.shape
    topk = min(TOPK_MAX, t // 4)
    idx_bits = max(1, (t - 1).bit_length())
    n_chunks = pl.cdiv(t, kc)
    blk = lambda w: pl.BlockSpec((None, tq, w), lambda i, j: (i, j, 0))
    full = lambda w: pl.BlockSpec((None, t, w), lambda i, j: (i, 0, 0))
    kernel = functools.partial(_attn_prompt_kernel, tq=tq, kc=kc, topk=topk, idx_bits=idx_bits)
    return pl.pallas_call(
        kernel,
        grid=(b, t // tq),
        in_specs=[blk(IDX_HEADS * IDX_DIM), blk(LANES), blk(B_WIDTH), full(LANES), full(KV_WIDTH), full(KV_WIDTH)],
        out_specs=blk(B_WIDTH),
        out_shape=jax.ShapeDtypeStruct((b, t, B_WIDTH), BF16),
        scratch_shapes=[
            pltpu.VMEM((n_chunks, tq, kc), I32),
            pltpu.VMEM((B_HEADS, tq, 1), F32),
            pltpu.VMEM((B_HEADS, tq, 1), F32),
            pltpu.VMEM((B_HEADS, tq, HEAD_DIM), F32),
        ],
        compiler_params=pltpu.CompilerParams(
            dimension_semantics=("arbitrary", "arbitrary"), vmem_limit_bytes=VMEM_LIMIT),
        name="attn_prompt",
    )(qi, kiwi, q, kiwi, kb, vb)


def _rot_tables(pos):
    r = pos.shape[0]
    inv_freq = ROPE_THETA ** (-jnp.arange(ROT_HALF, dtype=F32) * 2.0 / ROT_DIM)
    ang = pos.astype(F32)[:, None] * inv_freq[None, :]
    cos, sin = jnp.cos(ang), jnp.sin(ang)
    rest = HEAD_DIM - ROT_DIM
    c64 = jnp.concatenate([cos, cos, jnp.ones((r, rest), F32)], axis=1)
    lo64 = jnp.concatenate([-sin, jnp.zeros((r, HEAD_DIM - ROT_HALF), F32)], axis=1)
    hi64 = jnp.concatenate([jnp.zeros((r, ROT_HALF), F32), sin, jnp.zeros((r, rest), F32)], axis=1)
    both = lambda a, fill: jnp.concatenate([a, a, a, jnp.full((r, HEAD_DIM), fill, F32)], axis=1)
    return jnp.concatenate([both(c64, 1.0), both(lo64, 0.0), both(hi64, 0.0)], axis=1)


def _head_avg_matrix():
    return jnp.kron(jnp.eye(256 // HEAD_DIM, dtype=F32), jnp.full((HEAD_DIM, HEAD_DIM), 1.0 / HEAD_DIM, F32)).astype(BF16)


def _pad_w_in(w_in):
    return jnp.pad(w_in, ((0, 0), (0, IN_PAD - IN_WIDTH))).astype(BF16)


def _front_sample_kernel(x_ref, shift_ref, scale_ref, w_ref, rt_ref, mavg_ref, gam_ref, bet_ref, wl_ref, bl_ref,
                         a_ref, q_ref, k_ref, v_ref, qi_ref, kiwi_ref, vg_ref, *, ts, db):
    one_scale = 1.0 + scale_ref[...]
    shift = shift_ref[...]
    h = jnp.concatenate([x_ref[t * db:(t + 1) * db, :] * one_scale + shift for t in range(ts)], axis=0).astype(BF16)
    u, vg, q, k, v, qi, kiwi = _project(h, w_ref, rt_ref[...], mavg_ref[...], gam_ref[...], bet_ref[...])
    q_ref[...] = q.astype(BF16)
    k_ref[...] = k
    v_ref[...] = v
    qi_ref[...] = qi.astype(BF16)
    kiwi_ref[...] = kiwi
    vg_ref[...] = vg
    for t in range(ts):
        s = bl_ref[t:t + 1, :]
        for src in range(t + 1):
            s = s + wl_ref[t * ts + src:t * ts + src + 1, :] * vg[src * db:(src + 1) * db, :]
        a_ref[t * db:(t + 1) * db, :] = (u[t * db:(t + 1) * db, :] * s).astype(BF16)


def _front_sample(x_tm, mod_s, w_in, rt, mavg, gam, bet, wl, bl, ts, db):
    r = ts * db
    whole = lambda a: pl.BlockSpec(a.shape, lambda i: (0,) * a.ndim)
    out = lambda w, dt: jax.ShapeDtypeStruct((r, w), dt)
    outs = (out(A_WIDTH, BF16), out(B_WIDTH, BF16), out(KV_WIDTH, F32), out(KV_WIDTH, F32),
            out(IDX_HEADS * IDX_DIM, BF16), out(LANES, F32), out(A_WIDTH, F32))
    return pl.pallas_call(
        functools.partial(_front_sample_kernel, ts=ts, db=db),
        grid=(1,),
        in_specs=[
            whole(x_tm),
            pl.BlockSpec((db, D_MODEL), lambda i: (0, 0)),
            pl.BlockSpec((db, D_MODEL), lambda i: (0, 1)),
            whole(w_in), whole(rt), whole(mavg), whole(gam), whole(bet), whole(wl), whole(bl),
        ],
        out_specs=[pl.BlockSpec((r, s.shape[1]), lambda i: (0, 0)) for s in outs],
        out_shape=outs,
        compiler_params=pltpu.CompilerParams(dimension_semantics=("arbitrary",), vmem_limit_bytes=VMEM_LIMIT),
        name="front_sample",
    )(x_tm, mod_s, mod_s, w_in, rt, mavg, gam, bet, wl, bl)


def _attn_sample_kernel(pt_ref, qi_ref, kiwiq_ref, q_ref, knew_ref, vnew_ref, kidx_hbm, k_hbm, v_hbm, o_ref,
                        ki_buf, k_buf, v_buf, key_ref, sem, *, ts, n_pages, page, lpad, topk, idx_bits):
    b = pl.program_id(0)
    past = n_pages * page
    rows = qi_ref.shape[0]

    def page_copies(p):
        phys = pt_ref[b, p]
        dst = pl.ds(p * page, page)
        return (pltpu.make_async_copy(kidx_hbm.at[phys], ki_buf.at[dst], sem.at[0]),
                pltpu.make_async_copy(k_hbm.at[phys], k_buf.at[dst], sem.at[1]),
                pltpu.make_async_copy(v_hbm.at[phys], v_buf.at[dst], sem.at[2]))

    for p in range(n_pages):
        for cp in page_copies(p):
            cp.start()

    tail = lpad - past
    ki_buf[pl.ds(past, tail), :] = jnp.zeros((tail, IDX_DIM), F32)
    k_buf[pl.ds(past, tail)] = jnp.zeros((tail, B_KV_HEADS, HEAD_DIM), F32)
    v_buf[pl.ds(past, tail)] = jnp.zeros((tail, B_KV_HEADS, HEAD_DIM), F32)
    ki_buf[pl.ds(past, ts), :] = kiwiq_ref[0:ts, :][:, :IDX_DIM]
    k_buf[pl.ds(past, ts)] = knew_ref[...]
    v_buf[pl.ds(past, ts)] = vnew_ref[...]

    for p in range(n_pages):
        for cp in page_copies(p):
            cp.wait()

    qi = qi_ref[...]
    qi4 = jnp.concatenate([qi[:, h * IDX_DIM:(h + 1) * IDX_DIM] for h in range(IDX_HEADS)], axis=0)
    kiwiq = kiwiq_ref[...]
    s = _dot_nt(qi4, ki_buf[...].astype(BF16))
    tot = jnp.maximum(s[0:rows], 0.0) * kiwiq[:, WI_LANE:WI_LANE + 1]
    for h in range(1, IDX_HEADS):
        tot = tot + jnp.maximum(s[h * rows:(h + 1) * rows], 0.0) * kiwiq[:, WI_LANE + h:WI_LANE + h + 1]
    kpos = lax.broadcasted_iota(I32, (rows, lpad), 1)
    qpos = past + lax.broadcasted_iota(I32, (rows, lpad), 0)
    adm = (kpos <= qpos) & (kpos < past + ts)
    key_ref[...] = jnp.where(adm, _order_key(tot), INT_MIN)
    sub = lpad // LANES
    lane128 = lax.broadcasted_iota(I32, (rows, LANES), 1)

    def count_ge(cand):
        cnt = jnp.zeros((rows, LANES), I32)
        for s_ in range(sub):
            cnt = cnt + jnp.where(key_ref[:, s_ * LANES:(s_ + 1) * LANES] >= cand, 1, 0)
        return cnt

    def count_gt_eq_lt(thr, pos):
        n_gt = jnp.zeros((rows, LANES), I32)
        n_eq = jnp.zeros((rows, LANES), I32)
        for s_ in range(sub):
            kk = key_ref[:, s_ * LANES:(s_ + 1) * LANES]
            n_gt = n_gt + jnp.where(kk > thr, 1, 0)
            n_eq = n_eq + jnp.where((kk == thr) & (s_ * LANES + lane128 < pos), 1, 0)
        return n_gt, n_eq

    thr, cut = _select_threshold(count_ge, count_gt_eq_lt, rows, topk, idx_bits)
    thr_b = jnp.concatenate([thr] * sub, axis=1)
    cut_b = jnp.concatenate([cut] * sub, axis=1)
    key = key_ref[...]
    sel = (key > thr_b) | ((key == thr_b) & (kpos < cut_b))
    bias = jnp.where(sel, 0.0, -jnp.inf)
    bias2 = jnp.concatenate([bias] * B_GROUP, axis=0)

    q = q_ref[...]
    outs = [None] * B_HEADS
    for n in range(B_KV_HEADS):
        k_n = k_buf[:, n, :].astype(BF16)
        v_n = v_buf[:, n, :].astype(BF16)
        qs = jnp.concatenate([q[:, (n * B_GROUP + g) * HEAD_DIM:(n * B_GROUP + g + 1) * HEAD_DIM]
                              for g in range(B_GROUP)], axis=0)
        sc = _dot_nt(qs, k_n) + bias2
        m = jnp.max(sc, axis=1, keepdims=True)
        p_ = jnp.exp(sc - m)
        o = _dot(p_.astype(BF16), v_n) / jnp.sum(p_, axis=1, keepdims=True)
        for g in range(B_GROUP):
            outs[n * B_GROUP + g] = o[g * rows:(g + 1) * rows]
    o_ref[...] = jnp.concatenate(outs, axis=1).astype(BF16)


def _attn_sample(page_table, qi, kiwi, q, knew, vnew, cache_kidx, cache_k, cache_v, ts):
    db, rows, _ = q.shape
    n_pages = page_table.shape[1]
    page = cache_k.shape[1]
    past = n_pages * page
    lpad = pl.cdiv(past + rows, LANES) * LANES
    topk = min(TOPK_MAX, (past + ts) // 4)
    idx_bits = max(1, (lpad - 1).bit_length())
    blk = lambda w: pl.BlockSpec((None, rows, w), lambda b, pt: (b, 0, 0))
    new = pl.BlockSpec((None, ts, B_KV_HEADS, HEAD_DIM), lambda b, pt: (b, 0, 0, 0))
    anyspec = pl.BlockSpec(memory_space=pl.ANY)
    kernel = functools.partial(_attn_sample_kernel, ts=ts, n_pages=n_pages, page=page, lpad=lpad,
                               topk=topk, idx_bits=idx_bits)
    return pl.pallas_call(
        kernel,
        grid_spec=pltpu.PrefetchScalarGridSpec(
            num_scalar_prefetch=1,
            grid=(db,),
            in_specs=[blk(IDX_HEADS * IDX_DIM), blk(LANES), blk(B_WIDTH), new, new, anyspec, anyspec, anyspec],
            out_specs=blk(B_WIDTH),
            scratch_shapes=[
                pltpu.VMEM((lpad, IDX_DIM), F32),
                pltpu.VMEM((lpad, B_KV_HEADS, HEAD_DIM), F32),
                pltpu.VMEM((lpad, B_KV_HEADS, HEAD_DIM), F32),
                pltpu.VMEM((rows, lpad), I32),
                pltpu.SemaphoreType.DMA((3,)),
            ]),
        out_shape=jax.ShapeDtypeStruct((db, rows, B_WIDTH), BF16),
        compiler_params=pltpu.CompilerParams(dimension_semantics=("arbitrary",), vmem_limit_bytes=VMEM_LIMIT),
        name="attn_sample",
    )(page_table, qi, kiwi, q, knew, vnew, cache_kidx, cache_k, cache_v)


def _layer_norm_rows(y, gam, bet):
    mu = jnp.mean(y, axis=1, keepdims=True)
    d = y - mu
    var = jnp.mean(d * d, axis=1, keepdims=True)
    return d * lax.rsqrt(var + LN_EPS) * gam + bet


def _store_row_tiles(ref, val):
    r = val.shape[0]
    for c in range(ROW_CHUNKS):
        ref[pl.ds(c, r, stride=ROW_CHUNKS), :] = val[:, c * LANES:(c + 1) * LANES]


def _load_row_tiles(ref, r):
    return jnp.concatenate([ref[pl.ds(c, r, stride=ROW_CHUNKS), :] for c in range(ROW_CHUNKS)], axis=1)


def _route(logits):
    lane = lax.broadcasted_iota(I32, logits.shape, 1)
    lane_f = lane.astype(F32)
    far = float(LANES)

    def softmax_in(mask):
        x = jnp.where(mask, logits, -jnp.inf)
        e = jnp.exp(x - jnp.max(x, axis=1, keepdims=True))
        return e / jnp.sum(e, axis=1, keepdims=True)

    def first_max(p, mask):
        best = jnp.max(jnp.where(mask, p, -1.0), axis=1, keepdims=True)
        where_best = jnp.min(jnp.where(mask & (p == best), lane_f, far), axis=1, keepdims=True)
        return best, where_best

    gmask = lane < N_GROUPS
    g_w, g_sel = first_max(softmax_in(gmask), gmask)
    lo = N_GROUPS + EXPERTS_PER_GROUP * g_sel
    emask = (lane_f >= lo) & (lane_f < lo + EXPERTS_PER_GROUP)
    ep = softmax_in(emask)
    p1, i1 = first_max(ep, emask)
    rest = emask & (lane_f != i1)
    p2, i2 = first_max(ep, rest)
    denom = p1 + p2
    w1 = g_w * p1 / denom
    w2 = g_w * p2 / denom
    out = jnp.where(lane == 0, i1 - N_GROUPS, 0.0)
    out = jnp.where(lane == 1, i2 - N_GROUPS, out)
    out = jnp.where(lane == 2, w1, out)
    return jnp.where(lane == 3, w2, out)


def _mid_kernel(x_ref, a_ref, b_ref, g1_ref, sh2_ref, sc2_ref, wo_ref, gam_ref, bet_ref, wr_ref,
                x1_ref, h2_ref, route_ref, *, alpha):
    mixed = _dot(a_ref[...], wo_ref[0:A_WIDTH, :]) + _dot(b_ref[...], wo_ref[A_WIDTH:A_WIDTH + B_WIDTH, :])
    x1 = _layer_norm_rows(alpha * x_ref[...] + g1_ref[...] * mixed, gam_ref[...], bet_ref[...])
    x1_ref[...] = x1
    h2 = x1 * (1.0 + sc2_ref[...]) + sh2_ref[...]
    _store_row_tiles(h2_ref, h2)
    logits = jnp.dot(h2, wr_ref[...], precision=lax.Precision.HIGHEST, preferred_element_type=F32)
    route_ref[...] = _route(logits)


def _mod_spec(mod3, comp, tiles_per_group):
    rm = mod3.shape[1]
    return pl.BlockSpec((None, rm, D_MODEL), lambda i: (i // tiles_per_group, 0, comp))


def _mid(x, a, b, mod3, tiles_per_group, tile, w_out, gam, bet, w_router, alpha):
    n = x.shape[0]
    rows = lambda w: pl.BlockSpec((tile, w), lambda i: (i, 0))
    whole = lambda arr: pl.BlockSpec(arr.shape, lambda i: (0,) * arr.ndim)
    return pl.pallas_call(
        functools.partial(_mid_kernel, alpha=alpha),
        grid=(n // tile,),
        in_specs=[rows(D_MODEL), rows(A_WIDTH), rows(B_WIDTH),
                  _mod_spec(mod3, 2, tiles_per_group), _mod_spec(mod3, 3, tiles_per_group),
                  _mod_spec(mod3, 4, tiles_per_group),
                  whole(w_out), whole(gam), whole(bet), whole(w_router)],
        out_specs=[rows(D_MODEL), pl.BlockSpec((tile * ROW_CHUNKS, LANES), lambda i: (i, 0)), rows(LANES)],
        out_shape=(jax.ShapeDtypeStruct((n, D_MODEL), F32),
                   jax.ShapeDtypeStruct((n * ROW_CHUNKS, LANES), F32),
                   jax.ShapeDtypeStruct((n, LANES), F32)),
        compiler_params=pltpu.CompilerParams(dimension_semantics=("arbitrary",), vmem_limit_bytes=VMEM_LIMIT),
        name="mid",
    )(x, a, b, mod3, mod3, mod3, w_out, gam, bet, w_router)


def _rank_kernel(eid_ref, rank_ref, cnt_ref, carry_ref, *, nt):
    i = pl.program_id(0)

    @pl.when(i == 0)
    def _():
        carry_ref[...] = jnp.zeros(carry_ref.shape, F32)

    e_iota = lax.broadcasted_iota(I32, (N_EXPERTS, nt), 0)
    hit0 = eid_ref[0:1, :] == e_iota
    hit1 = eid_ref[1:2, :] == e_iota
    onehot = jnp.where(hit0 | hit1, 1.0, 0.0)
    upper = (lax.broadcasted_iota(I32, (nt, nt), 0) <= lax.broadcasted_iota(I32, (nt, nt), 1))
    incl = _dot(onehot.astype(BF16), jnp.where(upper, 1.0, 0.0).astype(BF16))
    before = carry_ref[...] + incl - 1.0
    rank_ref[0:1, :] = jnp.sum(jnp.where(hit0, before, 0.0), axis=0, keepdims=True).astype(I32)
    rank_ref[1:2, :] = jnp.sum(jnp.where(hit1, before, 0.0), axis=0, keepdims=True).astype(I32)
    carry_ref[...] = carry_ref[...] + jnp.sum(onehot, axis=1, keepdims=True)
    cnt_ref[...] = carry_ref[...].astype(I32)


def _rank(eids, nt):
    n = eids.shape[1]
    return pl.pallas_call(
        functools.partial(_rank_kernel, nt=nt),
        grid=(n // nt,),
        in_specs=[pl.BlockSpec((2, nt), lambda i: (0, i))],
        out_specs=[pl.BlockSpec((2, nt), lambda i: (0, i)), pl.BlockSpec((N_EXPERTS, 1), lambda i: (0, 0))],
        out_shape=(jax.ShapeDtypeStruct((2, n), I32), jax.ShapeDtypeStruct((N_EXPERTS, 1), I32)),
        scratch_shapes=[pltpu.VMEM((N_EXPERTS, 1), F32)],
        compiler_params=pltpu.CompilerParams(dimension_semantics=("arbitrary",)),
        name="rank",
    )(eids)


def _row_copy(src, src_row, dst, dst_row, sem):
    return pltpu.make_async_copy(src.at[pl.ds(src_row * ROW_CHUNKS, ROW_CHUNKS)],
                                 dst.at[pl.ds(dst_row * ROW_CHUNKS, ROW_CHUNKS)], sem)


def _dispatch_kernel(pos_ref, h_ref, xs_in, xs_out, sem, *, tt):
    del xs_in

    def issue(j, _):
        for k in range(2):
            _row_copy(h_ref, j, xs_out, pos_ref[k, j], sem).start()
        return 0

    def drain(j, _):
        for k in range(2):
            _row_copy(h_ref, j, xs_out, pos_ref[k, j], sem).wait()
        return 0

    lax.fori_loop(0, tt, issue, 0)
    lax.fori_loop(0, tt, drain, 0)


def _dispatch(pos, h2, xs_init, tt):
    n = pos.shape[1]
    return pl.pallas_call(
        functools.partial(_dispatch_kernel, tt=tt),
        grid=(n // tt,),
        in_specs=[pl.BlockSpec((2, tt), lambda i: (0, i), memory_space=pltpu.SMEM),
                  pl.BlockSpec((tt * ROW_CHUNKS, LANES), lambda i: (i, 0)),
                  pl.BlockSpec(memory_space=pl.ANY)],
        out_specs=pl.BlockSpec(memory_space=pl.ANY),
        out_shape=jax.ShapeDtypeStruct(xs_init.shape, xs_init.dtype),
        scratch_shapes=[pltpu.SemaphoreType.DMA(())],
        input_output_aliases={2: 0},
        compiler_params=pltpu.CompilerParams(dimension_semantics=("arbitrary",)),
        name="dispatch",
    )(pos, h2, xs_init)


def _experts_kernel(te_ref, nact_ref, xs_ref, wg_ref, wu_ref, wd_ref, ys_ref, wg_b, wu_b, wd_b, *, tm):
    i = pl.program_id(0)
    active = i < nact_ref[0]
    fresh = (i == 0) | (te_ref[i] != te_ref[jnp.maximum(i - 1, 0)])

    @pl.when(active & fresh)
    def _():
        wg_b[...] = wg_ref[...].astype(BF16)
        wu_b[...] = wu_ref[...].astype(BF16)
        wd_b[...] = wd_ref[...].astype(BF16)

    @pl.when(active)
    def _():
        x = _load_row_tiles(xs_ref, tm).astype(BF16)
        hid = (jax.nn.silu(_dot(x, wg_b[...])) * _dot(x, wu_b[...])).astype(BF16)
        _store_row_tiles(ys_ref, _dot(hid, wd_b[...]))

    @pl.when(jnp.logical_not(active))
    def _():
        ys_ref[...] = jnp.zeros(ys_ref.shape, F32)


def _experts(te, nact, xs, w_gate, w_up, w_down, tm):
    n_tiles = te.shape[0]
    tile = pl.BlockSpec((tm * ROW_CHUNKS, LANES), lambda i, te, na: (jnp.minimum(i, na[0] - 1), 0))
    w_in_spec = pl.BlockSpec((None, D_MODEL, D_EXPERT), lambda i, te, na: (te[i], 0, 0))
    w_out_spec = pl.BlockSpec((None, D_EXPERT, D_MODEL), lambda i, te, na: (te[i], 0, 0))
    return pl.pallas_call(
        functools.partial(_experts_kernel, tm=tm),
        grid_spec=pltpu.PrefetchScalarGridSpec(
            num_scalar_prefetch=2,
            grid=(n_tiles,),
            in_specs=[tile, w_in_spec, w_in_spec, w_out_spec],
            out_specs=pl.BlockSpec((tm * ROW_CHUNKS, LANES), lambda i, te, na: (i, 0)),
            scratch_shapes=[pltpu.VMEM((D_MODEL, D_EXPERT), BF16), pltpu.VMEM((D_MODEL, D_EXPERT), BF16),
                            pltpu.VMEM((D_EXPERT, D_MODEL), BF16)]),
        out_shape=jax.ShapeDtypeStruct(xs.shape, F32),
        compiler_params=pltpu.CompilerParams(dimension_semantics=("arbitrary",), vmem_limit_bytes=VMEM_LIMIT),
        name="experts",
    )(te, nact, xs, w_gate, w_up, w_down)


def _combine_kernel(pos_ref, route_ref, x1_ref, g2_ref, gam_ref, bet_ref, ys_hbm, y_ref, buf0, buf1, sem,
                    *, tt, alpha):
    bufs = (buf0, buf1)

    def issue(j, _):
        for k in range(2):
            _row_copy(ys_hbm, pos_ref[k, j], bufs[k], j, sem).start()
        return 0

    def drain(j, _):
        for k in range(2):
            _row_copy(ys_hbm, pos_ref[k, j], bufs[k], j, sem).wait()
        return 0

    lax.fori_loop(0, tt, issue, 0)
    lax.fori_loop(0, tt, drain, 0)
    route = route_ref[...]
    f = route[:, 2:3] * _load_row_tiles(buf0, tt) + route[:, 3:4] * _load_row_tiles(buf1, tt)
    y_ref[...] = _layer_norm_rows(alpha * x1_ref[...] + g2_ref[...] * f, gam_ref[...], bet_ref[...])


def _combine(pos, route, x1, mod3, tiles_per_group, tt, gam, bet, ys, alpha):
    n = x1.shape[0]
    rows = lambda w: pl.BlockSpec((tt, w), lambda i: (i, 0))
    whole = lambda arr: pl.BlockSpec(arr.shape, lambda i: (0,) * arr.ndim)
    return pl.pallas_call(
        functools.partial(_combine_kernel, tt=tt, alpha=alpha),
        grid=(n // tt,),
        in_specs=[pl.BlockSpec((2, tt), lambda i: (0, i), memory_space=pltpu.SMEM),
                  rows(LANES), rows(D_MODEL), _mod_spec(mod3, 5, tiles_per_group), whole(gam), whole(bet),
                  pl.BlockSpec(memory_space=pl.ANY)],
        out_specs=rows(D_MODEL),
        out_shape=jax.ShapeDtypeStruct((n, D_MODEL), F32),
        scratch_shapes=[pltpu.VMEM((tt * ROW_CHUNKS, LANES), F32), pltpu.VMEM((tt * ROW_CHUNKS, LANES), F32),
                        pltpu.SemaphoreType.DMA(())],
        compiler_params=pltpu.CompilerParams(dimension_semantics=("arbitrary",), vmem_limit_bytes=VMEM_LIMIT),
        name="combine",
    )(pos, route, x1, mod3, gam, bet, ys)


MOE_TILE = 256
TOKEN_TILE = 256
RANK_TILE = 512


def _moe(h2, route, w_gate, w_up, w_down):
    n = route.shape[0]
    eids = route[:, 0:2].astype(I32).T
    rank, counts = _rank(eids, min(RANK_TILE, n))
    counts = counts[:, 0]
    padded = (counts + MOE_TILE - 1) // MOE_TILE * MOE_TILE
    ends = jnp.cumsum(padded)
    starts = ends - padded
    pos = starts[eids] + rank
    n_tiles = (2 * n) // MOE_TILE + N_EXPERTS
    nact = (ends[-1] // MOE_TILE).astype(I32).reshape(1)
    tile_start = jnp.arange(n_tiles, dtype=I32) * MOE_TILE
    te = jnp.searchsorted(ends, jnp.minimum(tile_start, ends[-1] - 1), side="right").astype(I32)
    xs = _dispatch(pos, h2, jnp.zeros((n_tiles * MOE_TILE * ROW_CHUNKS, LANES), F32), min(TOKEN_TILE, n))
    ys = _experts(te, nact, xs, w_gate, w_up, w_down, MOE_TILE)
    return pos, ys


def kernel(x_prompt, x_sample, cache_k, cache_v, cache_kidx, page_table, c_prompt, c_sample, w_ada, b_ada, w_in,
           a_ln_g, a_ln_b, w_spatial, b_spatial, w_out, ln1_g, ln1_b, w_group_router, w_expert_router, w_gate,
           w_up, w_down, ln2_g, ln2_b):
    depth = w_ada.shape[0]
    assert depth == 1, "one trunk layer"
    alpha = (2.0 * depth) ** 0.25
    bsz, seq, d = x_prompt.shape
    db, ts, _ = x_sample.shape
    n_pages = page_table.shape[1]
    page = cache_k.shape[2]
    past = n_pages * page
    assert d == D_MODEL and seq % CHUNK == 0 and ts <= SUBLANES and db % SUBLANES == 0
    l = 0

    n_c = bsz + db
    n_c_pad = pl.cdiv(n_c, SUBLANES) * SUBLANES
    c_all = jnp.pad(jnp.concatenate([c_prompt, c_sample], axis=0), ((0, n_c_pad - n_c), (0, 0)))
    mod = _ada(c_all, w_ada[l], b_ada[l][None, :])
    mod_p = mod[:bsz]
    mod_s = mod[bsz:n_c]

    w_in_b = _pad_w_in(w_in[l])
    w_out_b = w_out[l].astype(BF16)
    mavg = _head_avg_matrix()
    gam_a = a_ln_g[l].reshape(1, A_WIDTH)
    bet_a = a_ln_b[l].reshape(1, A_WIDTH)
    w_router = jnp.pad(jnp.concatenate([w_group_router[l], w_expert_router[l].reshape(d, N_EXPERTS)], axis=1),
                       ((0, 0), (0, LANES - N_GROUPS - N_EXPERTS)))
    ln1 = (ln1_g[l][None, :], ln1_b[l][None, :])
    ln2 = (ln2_g[l][None, :], ln2_b[l][None, :])

    wsp = w_spatial[l].reshape(A_HEADS // 2, 2, CHUNK, CHUNK).transpose(0, 2, 1, 3).reshape(A_HEADS // 2, CHUNK, 2 * CHUNK)
    bsp = jnp.repeat(b_spatial[l].T, HEAD_DIM, axis=1)
    a_p, q_p, k_p, v_p, kb_p, vb_p, qi_p, kiwi_p = _front_prompt(
        x_prompt, mod_p.reshape(bsz, 6, d), w_in_b, _rot_tables(jnp.arange(seq)), mavg, gam_a, bet_a, wsp, bsp, tt=256)
    b_p = _attn_prompt(qi_p, kiwi_p, q_p, kb_p, vb_p, tq=128, kc=512)
    n_p = bsz * seq
    mid_tile = 256
    x1_p, h2_p, route_p = _mid(x_prompt.reshape(n_p, d), a_p.reshape(n_p, A_WIDTH), b_p.reshape(n_p, B_WIDTH),
                               mod_p.reshape(bsz, 1, 6 * d), seq // mid_tile, mid_tile, w_out_b, *ln1, w_router, alpha)
    pos_p, ys_p = _moe(h2_p, route_p, w_gate[l], w_up[l], w_down[l])
    y_p = _combine(pos_p, route_p, x1_p, mod_p.reshape(bsz, 1, 6 * d), seq // TOKEN_TILE, TOKEN_TILE, *ln2, ys_p, alpha)

    r_s = ts * db
    x_tm = x_sample.transpose(1, 0, 2).reshape(r_s, d)
    rt_s = _rot_tables(jnp.repeat(past + jnp.arange(ts), db))
    w_small = w_spatial[l][:, :ts, :ts]
    wl = jnp.repeat(w_small.transpose(1, 2, 0).reshape(ts * ts, A_HEADS), HEAD_DIM, axis=1)
    bl = jnp.repeat(b_spatial[l][:, :ts].T, HEAD_DIM, axis=1)
    a_s, q_s, k_s, v_s, qi_s, kiwi_s, vg_s = _front_sample(x_tm, mod_s, w_in_b, rt_s, mavg, gam_a, bet_a, wl, bl, ts, db)

    def seq_major(a, pad_rows):
        a = a.reshape(ts, db, a.shape[-1]).transpose(1, 0, 2)
        return jnp.pad(a, ((0, 0), (0, pad_rows - ts), (0, 0))) if pad_rows > ts else a

    k_s_bm = seq_major(k_s, ts)
    v_s_bm = seq_major(v_s, ts)
    b_s = _attn_sample(page_table, seq_major(qi_s, SUBLANES), seq_major(kiwi_s, SUBLANES), seq_major(q_s, SUBLANES),
                       k_s_bm.reshape(db, ts, B_KV_HEADS, HEAD_DIM), v_s_bm.reshape(db, ts, B_KV_HEADS, HEAD_DIM),
                       cache_kidx[l], cache_k[l], cache_v[l], ts)
    b_s_tm = b_s[:, :ts].transpose(1, 0, 2).reshape(r_s, B_WIDTH)
    x1_s, h2_s, route_s = _mid(x_tm, a_s, b_s_tm, mod_s.reshape(1, db, 6 * d), ts, db, w_out_b, *ln1, w_router, alpha)
    pos_s, ys_s = _moe(h2_s, route_s, w_gate[l], w_up[l], w_down[l])
    y_s_tm = _combine(pos_s, route_s, x1_s, mod_s.reshape(1, db, 6 * d), ts, db, *ln2, ys_s, alpha)
    y_s = y_s_tm.reshape(ts, db, d).transpose(1, 0, 2)

    kv5 = lambda a, n, t: a.reshape(1, n, t, B_KV_HEADS, HEAD_DIM)
    return (y_p.reshape(bsz, seq, d), y_s,
            kv5(k_p, bsz, seq), kv5(v_p, bsz, seq), kiwi_p[..., :IDX_DIM][None],
            kv5(k_s_bm, db, ts), kv5(v_s_bm, db, ts), seq_major(kiwi_s, ts)[..., :IDX_DIM][None],
            seq_major(vg_s, ts).reshape(1, db, ts, A_HEADS, HEAD_DIM))
```

```python
import functools

import jax
import jax.numpy as jnp
from jax import lax
from jax.experimental import pallas as pl
from jax.experimental.pallas import tpu as pltpu

F32 = jnp.float32
BF16 = jnp.bfloat16
I32 = jnp.int32

D_MODEL = 1024
HEAD_DIM = 64
A_HEADS = 8
A_WIDTH = A_HEADS * HEAD_DIM
CHUNK = 128
B_HEADS = 8
B_KV_HEADS = 4
B_GROUP = B_HEADS // B_KV_HEADS
B_WIDTH = B_HEADS * HEAD_DIM
KV_WIDTH = B_KV_HEADS * HEAD_DIM
IDX_HEADS = 4
IDX_DIM = 64
IDX_WIDTH = IDX_HEADS * IDX_DIM
TOPK_MAX = 256
ROPE_THETA = 500000.0
ROT_DIM = HEAD_DIM // 4
ROT_HALF = ROT_DIM // 2
ATTN_SCALE = HEAD_DIM ** -0.5
N_GROUPS = 4
EXPERTS_PER_GROUP = 8
N_EXPERTS = N_GROUPS * EXPERTS_PER_GROUP
D_EXPERT = 512
LN_EPS = 1e-5

LANES = 128
SUBLANES = 8
ROW_CHUNKS = D_MODEL // LANES

C_AU, C_AV, C_Q, C_K, C_V, C_QI, C_KI = 0, 512, 1024, 1536, 1792, 2048, 2304
IN_WIDTH = C_KI + IDX_DIM + IDX_HEADS
IN_PAD = 2432
WI_LANE = IDX_DIM

INT_MIN = -(2 ** 31)
VMEM_LIMIT = 48 * 1024 * 1024

PROMPT_TILE = 256
MOE_TILE = 256
TOKEN_TILE = 256
RANK_TILE = 512
IDX_SEQS = 16
BITS_PER_CHECK = 4


def _dot(a, b):
    return jnp.dot(a, b, preferred_element_type=F32)


def _dot_nt(a, b):
    return lax.dot_general(a, b, (((1,), (1,)), ((), ())), preferred_element_type=F32)


def _split_dot(x, m):
    hi = x.astype(BF16)
    lo = (x - hi.astype(F32)).astype(BF16)
    return _dot(hi, m) + _dot(lo, m)


def _ada_kernel(c_ref, w_ref, b_ref, o_ref):
    s = jax.nn.silu(c_ref[...]).astype(BF16)
    o_ref[...] = _dot(s, w_ref[...].astype(BF16)) + b_ref[...]


def _ada(c_all, w_ada, b_ada):
    rows = c_all.shape[0]
    n_out = w_ada.shape[1]
    tn = 1024
    return pl.pallas_call(
        _ada_kernel,
        grid=(n_out // tn,),
        in_specs=[
            pl.BlockSpec((rows, D_MODEL), lambda j: (0, 0)),
            pl.BlockSpec((D_MODEL, tn), lambda j: (0, j)),
            pl.BlockSpec((1, tn), lambda j: (0, j)),
        ],
        out_specs=pl.BlockSpec((rows, tn), lambda j: (0, j)),
        out_shape=jax.ShapeDtypeStruct((rows, n_out), F32),
        compiler_params=pltpu.CompilerParams(
            dimension_semantics=("arbitrary",), vmem_limit_bytes=VMEM_LIMIT),
        name="ada",
    )(c_all, w_ada, b_ada)


def _rotate(x, rt, blk):
    c = rt[:, blk * LANES:(blk + 1) * LANES]
    s_lo = rt[:, 256 + blk * LANES:256 + (blk + 1) * LANES]
    s_hi = rt[:, 512 + blk * LANES:512 + (blk + 1) * LANES]
    outs = []
    for j in range(x.shape[1] // LANES):
        xb = x[:, j * LANES:(j + 1) * LANES]
        up = pltpu.roll(xb, LANES - ROT_HALF, 1)
        dn = pltpu.roll(xb, ROT_HALF, 1)
        outs.append(xb * c + up * s_lo + dn * s_hi)
    return outs[0] if len(outs) == 1 else jnp.concatenate(outs, axis=1)


def _head_ln(g, mavg, gam, bet):
    def seg_mean(x):
        parts = [_split_dot(x[:, j * 256:(j + 1) * 256], mavg) for j in range(A_WIDTH // 256)]
        return jnp.concatenate(parts, axis=1)
    mu = seg_mean(g)
    d = g - mu
    var = seg_mean(d * d)
    return d * lax.rsqrt(var + LN_EPS) * gam + bet


def _project(h, w_ref, rt, mavg, gam, bet):
    u = jax.nn.gelu(_dot(h, w_ref[:, C_AU:C_AV]))
    vg = _head_ln(jax.nn.gelu(_dot(h, w_ref[:, C_AV:C_Q])), mavg, gam, bet)
    q = _rotate(_dot(h, w_ref[:, C_Q:C_K]), rt, 0) * ATTN_SCALE
    k = _rotate(_dot(h, w_ref[:, C_K:C_V]), rt, 0)
    v = _dot(h, w_ref[:, C_V:C_QI])
    qi = _rotate(_dot(h, w_ref[:, C_QI:C_KI]), rt, 0)
    kiwi = _rotate(_dot(h, w_ref[:, C_KI:IN_PAD]), rt, 1)
    return u, vg, q, k, v, qi, kiwi


def _front_prompt_kernel(x_ref, mod_ref, w_ref, rt_ref, mavg_ref, gam_ref, bet_ref, wsp_ref, bsp_ref,
                         a_ref, kt_ref, vt_ref, kit_ref, kb_ref, kiwib_ref, vtb_ref, qt_ref, qit_ref, wit_ref):
    shift = mod_ref[0:1, :]
    scale = mod_ref[1:2, :]
    h = (x_ref[...] * (1.0 + scale) + shift).astype(BF16)
    u, vg, q, k, v, qi, kiwi = _project(h, w_ref, rt_ref[...], mavg_ref[...], gam_ref[...], bet_ref[...])
    v_t = jnp.transpose(v)
    kiwi_t = jnp.transpose(kiwi)
    kt_ref[...] = jnp.transpose(k)
    vt_ref[...] = v_t
    kit_ref[...] = kiwi_t[0:IDX_DIM, :]
    kb_ref[...] = k.astype(BF16)
    kiwib_ref[...] = kiwi.astype(BF16)
    vtb_ref[...] = v_t.astype(BF16)
    qt_ref[...] = jnp.transpose(q).astype(BF16)
    qit_ref[...] = jnp.transpose(qi).astype(BF16)
    wit_ref[...] = kiwi_t[WI_LANE:WI_LANE + SUBLANES, :]

    rows = lax.broadcasted_iota(I32, (CHUNK, 2 * CHUNK), 0)
    cols = lax.broadcasted_iota(I32, (CHUNK, 2 * CHUNK), 1) % CHUNK
    causal = cols <= rows
    lane = lax.broadcasted_iota(I32, (CHUNK, LANES), 1)
    tt = x_ref.shape[0]
    for cidx in range(tt // CHUNK):
        rs = slice(cidx * CHUNK, (cidx + 1) * CHUNK)
        blocks = []
        for p in range(A_HEADS // 2):
            wcat = jnp.where(causal, wsp_ref[p], 0.0).astype(BF16)
            vb = vg[rs, p * LANES:(p + 1) * LANES]
            rhs = jnp.concatenate([jnp.where(lane < HEAD_DIM, vb, 0.0),
                                   jnp.where(lane >= HEAD_DIM, vb, 0.0)], axis=0).astype(BF16)
            blocks.append(_dot(wcat, rhs))
        s = jnp.concatenate(blocks, axis=1) + bsp_ref[...]
        a_ref[rs, :] = (u[rs, :] * s).astype(BF16)


def _front_prompt(x, mod, w_in, rt, mavg, gam, bet, wsp, bsp, tt):
    b, t, _ = x.shape
    nc = t // tt
    tok = lambda w: pl.BlockSpec((None, tt, w), lambda i, j: (i, j, 0))
    tr = lambda r: pl.BlockSpec((None, None, r, tt), lambda i, j: (i, j, 0, 0))
    pos_minor = lambda r: pl.BlockSpec((None, r, tt), lambda i, j: (i, 0, j))
    const2 = lambda a: pl.BlockSpec(a.shape, lambda i, j: (0,) * a.ndim)
    out_shapes = (
        jax.ShapeDtypeStruct((b, t, A_WIDTH), BF16),
        jax.ShapeDtypeStruct((b, KV_WIDTH, t), F32),
        jax.ShapeDtypeStruct((b, KV_WIDTH, t), F32),
        jax.ShapeDtypeStruct((b, IDX_DIM, t), F32),
        jax.ShapeDtypeStruct((b, t, KV_WIDTH), BF16),
        jax.ShapeDtypeStruct((b, t, LANES), BF16),
        jax.ShapeDtypeStruct((b, nc, KV_WIDTH, tt), BF16),
        jax.ShapeDtypeStruct((b, nc, B_WIDTH, tt), BF16),
        jax.ShapeDtypeStruct((b, nc, IDX_WIDTH, tt), BF16),
        jax.ShapeDtypeStruct((b, nc, SUBLANES, tt), F32),
    )
    return pl.pallas_call(
        _front_prompt_kernel,
        grid=(b, nc),
        in_specs=[
            tok(D_MODEL),
            pl.BlockSpec((None, 6, D_MODEL), lambda i, j: (i, 0, 0)),
            const2(w_in),
            pl.BlockSpec((tt, 768), lambda i, j: (j, 0)),
            const2(mavg), const2(gam), const2(bet), const2(wsp), const2(bsp),
        ],
        out_specs=[tok(A_WIDTH), pos_minor(KV_WIDTH), pos_minor(KV_WIDTH), pos_minor(IDX_DIM), tok(KV_WIDTH),
                   tok(LANES), tr(KV_WIDTH), tr(B_WIDTH), tr(IDX_WIDTH), tr(SUBLANES)],
        out_shape=out_shapes,
        compiler_params=pltpu.CompilerParams(
            dimension_semantics=("arbitrary", "arbitrary"), vmem_limit_bytes=VMEM_LIMIT),
        name="front_prompt",
    )(x, mod, w_in, rt, mavg, gam, bet, wsp, bsp)


def _order_key(score):
    score = jnp.where(score == 0.0, 0.0, score)
    bits = pltpu.bitcast(score, I32)
    return bits ^ ((bits >> 31) & 0x7FFFFFFF)


def _select_threshold(count_ge, count_gt_eq_lt, shape, topk, idx_bits):
    def unresolved(carry):
        i, _, n_ge = carry
        return (i < 32) & (jnp.max((n_ge != topk).astype(F32)) > 0.0)

    def bit_step(carry):
        i, prefix, n_ge = carry
        for _ in range(BITS_PER_CHECK):
            bit = lax.shift_left(jnp.int32(1), 31 - i)
            cand_u = prefix | bit
            tot = count_ge(cand_u ^ INT_MIN)
            take = tot >= topk
            prefix = jnp.where(take, cand_u, prefix)
            n_ge = jnp.where(take, tot, n_ge)
            i = i + 1
        return i, prefix, n_ge

    never = jnp.full(shape, 2 ** 31 - 1, I32)
    _, prefix, n_ge = lax.while_loop(unresolved, bit_step, (jnp.int32(0), jnp.zeros(shape, I32), never))
    thr = prefix ^ INT_MIN
    tied = (n_ge > topk) & (thr > INT_MIN)
    thr = jnp.maximum(thr, INT_MIN + 1)
    big = jnp.full(shape, 1 << idx_bits, I32)

    def resolve_ties(_):
        n_gt, _ = count_gt_eq_lt(thr, big)
        need = topk - n_gt

        def idx_step(i, cut):
            cand = cut | lax.shift_left(jnp.int32(1), idx_bits - 1 - i)
            _, n_eq = count_gt_eq_lt(thr, cand)
            return jnp.where(n_eq <= need, cand, cut)

        cut = lax.fori_loop(0, idx_bits, idx_step, jnp.zeros(shape, I32))
        return jnp.where(tied, cut, big)

    any_tied = jnp.max(tied.astype(F32)) > 0.0
    cut = lax.cond(any_tied, resolve_ties, lambda _: big, 0)
    return thr, cut


def _fold_rows(x):
    acc = x[0:SUBLANES]
    for r in range(1, x.shape[0] // SUBLANES):
        acc = acc + x[r * SUBLANES:(r + 1) * SUBLANES]
    return acc


def _col_total(cnt8):
    return jnp.sum(cnt8.astype(F32), axis=0, keepdims=True).astype(I32)


def _attn_prompt_kernel(qit_ref, wit_ref, qt_ref, kiwib_ref, kb_ref, vt_ref, o_ref,
                        key_ref, w4_ref, wq_ref, s0_ref, s1_ref, m_ref, l_ref, acc_ref, *, tq, topk, idx_bits):
    j = pl.program_id(1)
    kc = tq
    n_kc = j + 1

    @pl.when((pl.program_id(0) == 0) & (j == 0))
    def _():
        w4_ref[...] = jnp.zeros(w4_ref.shape, BF16)
        wq_ref[...] = jnp.zeros(wq_ref.shape, BF16)

    for h in range(IDX_HEADS):
        w4_ref[0:IDX_DIM, h * tq:(h + 1) * tq] = qit_ref[h * IDX_DIM:(h + 1) * IDX_DIM, :]
    for h in range(B_HEADS):
        n = h // B_GROUP
        wq_ref[h, n * HEAD_DIM:(n + 1) * HEAD_DIM, :] = qt_ref[h * HEAD_DIM:(h + 1) * HEAD_DIM, :]

    wit = wit_ref[...]
    krow = lax.broadcasted_iota(I32, (kc, tq), 0)
    qpos = j * tq + lax.broadcasted_iota(I32, (kc, tq), 1)

    def score_chunk(c, _):
        k0 = pl.multiple_of(c * kc, kc)
        s = _dot(kiwib_ref[pl.ds(k0, kc), :], w4_ref[...])
        tot = jnp.maximum(s[:, 0:tq], 0.0) * wit[0:1, :]
        for h in range(1, IDX_HEADS):
            tot = tot + jnp.maximum(s[:, h * tq:(h + 1) * tq], 0.0) * wit[h:h + 1, :]
        key_ref[c] = jnp.where(k0 + krow <= qpos, _order_key(tot), INT_MIN)
        return 0

    lax.fori_loop(0, n_kc, score_chunk, 0)

    def count_ge(cand):
        def body(c, cnt):
            return cnt + _fold_rows(jnp.where(key_ref[c] >= cand, 1, 0))
        return _col_total(lax.fori_loop(0, n_kc, body, jnp.zeros((SUBLANES, tq), I32)))

    def count_gt_eq_lt(thr, pos):
        def body(c, carry):
            n_gt, n_eq = carry
            key = key_ref[c]
            n_gt = n_gt + _fold_rows(jnp.where(key > thr, 1, 0))
            n_eq = n_eq + _fold_rows(jnp.where((key == thr) & (c * kc + krow < pos), 1, 0))
            return n_gt, n_eq
        z = jnp.zeros((SUBLANES, tq), I32)
        n_gt, n_eq = lax.fori_loop(0, n_kc, body, (z, z))
        return _col_total(n_gt), _col_total(n_eq)

    thr, cut = _select_threshold(count_ge, count_gt_eq_lt, (1, tq), topk, idx_bits)

    m_ref[...] = jnp.full(m_ref.shape, jnp.finfo(F32).min, F32)
    l_ref[...] = jnp.zeros(l_ref.shape, F32)
    acc_ref[...] = jnp.zeros(acc_ref.shape, F32)

    def key_block(c):
        return kb_ref[pl.ds(pl.multiple_of(c * kc, kc), kc), :]

    kblk0 = key_block(0)
    for h in range(B_HEADS):
        s0_ref[h] = _dot(kblk0, wq_ref[h])

    def stage(c, open_bias, src, dst, c_next):
        key = key_ref[c]
        sel = (key > thr) | ((key == thr) & (c * kc + krow < cut))
        bias = jnp.where(sel, open_bias, -jnp.inf)
        kblk_next = key_block(c_next)
        vtc = vt_ref[c]
        for h in range(B_HEADS):
            n = h // B_GROUP
            dst[h] = _dot(kblk_next, wq_ref[h])
            s = src[h] + bias
            m_old = m_ref[h]
            m_new = jnp.maximum(m_old, jnp.max(s, axis=0, keepdims=True))
            alpha = jnp.exp(m_old - m_new)
            p = jnp.exp(s - m_new)
            l_ref[h] = alpha * l_ref[h] + jnp.sum(p, axis=0, keepdims=True)
            acc_ref[h] = alpha * acc_ref[h] + _dot(vtc[n * HEAD_DIM:(n + 1) * HEAD_DIM, :], p.astype(BF16))
            m_ref[h] = m_new

    last = n_kc - 1

    def attend_pair(i, _):
        c0 = 2 * i
        c1 = jnp.minimum(c0 + 1, last)
        stage(c0, 0.0, s0_ref, s1_ref, c1)
        stage(c1, jnp.where(c0 + 1 <= last, 0.0, -jnp.inf), s1_ref, s0_ref, jnp.minimum(c0 + 2, last))
        return 0

    lax.fori_loop(0, (n_kc + 1) // 2, attend_pair, 0)
    out_t = jnp.concatenate([acc_ref[h] / l_ref[h] for h in range(B_HEADS)], axis=0)
    o_ref[...] = jnp.transpose(out_t).astype(BF16)


def _attn_prompt(qit, wit, qt, kiwib, kb, vt):
    b, nc, _, tq = qt.shape
    t = nc * tq
    topk = min(TOPK_MAX, t // 4)
    idx_bits = max(1, (t - 1).bit_length())
    tr = lambda r: pl.BlockSpec((None, None, r, tq), lambda i, j: (i, j, 0, 0))
    full = lambda w: pl.BlockSpec((None, t, w), lambda i, j: (i, 0, 0))
    kernel = functools.partial(_attn_prompt_kernel, tq=tq, topk=topk, idx_bits=idx_bits)
    return pl.pallas_call(
        kernel,
        grid=(b, nc),
        in_specs=[tr(IDX_WIDTH), tr(SUBLANES), tr(B_WIDTH), full(LANES), full(KV_WIDTH),
                  pl.BlockSpec((None, nc, KV_WIDTH, tq), lambda i, j: (i, 0, 0, 0))],
        out_specs=pl.BlockSpec((None, tq, B_WIDTH), lambda i, j: (i, j, 0)),
        out_shape=jax.ShapeDtypeStruct((b, t, B_WIDTH), BF16),
        scratch_shapes=[
            pltpu.VMEM((nc, tq, tq), I32),
            pltpu.VMEM((LANES, IDX_HEADS * tq), BF16),
            pltpu.VMEM((B_HEADS, KV_WIDTH, tq), BF16),
            pltpu.VMEM((B_HEADS, tq, tq), F32),
            pltpu.VMEM((B_HEADS, tq, tq), F32),
            pltpu.VMEM((B_HEADS, 1, tq), F32),
            pltpu.VMEM((B_HEADS, 1, tq), F32),
            pltpu.VMEM((B_HEADS, HEAD_DIM, tq), F32),
        ],
        compiler_params=pltpu.CompilerParams(
            dimension_semantics=("arbitrary", "arbitrary"), vmem_limit_bytes=VMEM_LIMIT),
        name="attn_prompt",
    )(qit, wit, qt, kiwib, kb, vt)


def _rot_tables(pos):
    r = pos.shape[0]
    inv_freq = ROPE_THETA ** (-jnp.arange(ROT_HALF, dtype=F32) * 2.0 / ROT_DIM)
    ang = pos.astype(F32)[:, None] * inv_freq[None, :]
    cos, sin = jnp.cos(ang), jnp.sin(ang)
    rest = HEAD_DIM - ROT_DIM
    c64 = jnp.concatenate([cos, cos, jnp.ones((r, rest), F32)], axis=1)
    lo64 = jnp.concatenate([-sin, jnp.zeros((r, HEAD_DIM - ROT_HALF), F32)], axis=1)
    hi64 = jnp.concatenate([jnp.zeros((r, ROT_HALF), F32), sin, jnp.zeros((r, rest), F32)], axis=1)
    both = lambda a, fill: jnp.concatenate([a, a, a, jnp.full((r, HEAD_DIM), fill, F32)], axis=1)
    return jnp.concatenate([both(c64, 1.0), both(lo64, 0.0), both(hi64, 0.0)], axis=1)


def _head_avg_matrix():
    return jnp.kron(jnp.eye(256 // HEAD_DIM, dtype=F32), jnp.full((HEAD_DIM, HEAD_DIM), 1.0 / HEAD_DIM, F32)).astype(BF16)


def _pad_w_in(w_in):
    return jnp.pad(w_in, ((0, 0), (0, IN_PAD - IN_WIDTH))).astype(BF16)


def _front_sample_kernel(x_ref, shift_ref, scale_ref, w_ref, rt_ref, mavg_ref, gam_ref, bet_ref, wl_ref, bl_ref,
                         a_ref, q_ref, k_ref, v_ref, qi_ref, kiwi_ref, vg_ref, *, ts, db):
    one_scale = 1.0 + scale_ref[...]
    shift = shift_ref[...]
    h = jnp.concatenate([x_ref[t * db:(t + 1) * db, :] * one_scale + shift for t in range(ts)], axis=0).astype(BF16)
    u, vg, q, k, v, qi, kiwi = _project(h, w_ref, rt_ref[...], mavg_ref[...], gam_ref[...], bet_ref[...])
    q_ref[...] = q.astype(BF16)
    k_ref[...] = k
    v_ref[...] = v
    qi_ref[...] = qi.astype(BF16)
    kiwi_ref[...] = kiwi
    vg_ref[...] = vg
    for t in range(ts):
        s = bl_ref[t:t + 1, :]
        for src in range(t + 1):
            s = s + wl_ref[t * ts + src:t * ts + src + 1, :] * vg[src * db:(src + 1) * db, :]
        a_ref[t * db:(t + 1) * db, :] = (u[t * db:(t + 1) * db, :] * s).astype(BF16)


def _front_sample(x_tm, mod_s, w_in, rt, mavg, gam, bet, wl, bl, ts, db):
    r = ts * db
    whole = lambda a: pl.BlockSpec(a.shape, lambda i: (0,) * a.ndim)
    out = lambda w, dt: jax.ShapeDtypeStruct((r, w), dt)
    outs = (out(A_WIDTH, BF16), out(B_WIDTH, BF16), out(KV_WIDTH, F32), out(KV_WIDTH, F32),
            out(IDX_WIDTH, BF16), out(LANES, F32), out(A_WIDTH, F32))
    return pl.pallas_call(
        functools.partial(_front_sample_kernel, ts=ts, db=db),
        grid=(1,),
        in_specs=[
            whole(x_tm),
            pl.BlockSpec((db, D_MODEL), lambda i: (0, 0)),
            pl.BlockSpec((db, D_MODEL), lambda i: (0, 1)),
            whole(w_in), whole(rt), whole(mavg), whole(gam), whole(bet), whole(wl), whole(bl),
        ],
        out_specs=[pl.BlockSpec((r, s.shape[1]), lambda i: (0, 0)) for s in outs],
        out_shape=outs,
        compiler_params=pltpu.CompilerParams(dimension_semantics=("arbitrary",), vmem_limit_bytes=VMEM_LIMIT),
        name="front_sample",
    )(x_tm, mod_s, mod_s, w_in, rt, mavg, gam, bet, wl, bl)


def _idx_sample_kernel(pt_ref, qi4_ref, wcol_ref, qpos_ref, kinew_ref, kidx_hbm, bias_ref,
                       ki_buf, key_ref, sem, *, ga, ts, n_pages, page, lpad, topk, idx_bits):
    i = pl.program_id(0)
    n_steps = pl.num_programs(0)
    past = n_pages * page
    rows = ga * ts
    slot = i % 2

    def page_copy(step, to_slot, g, p):
        phys = pt_ref[step * ga + g, p]
        return pltpu.make_async_copy(kidx_hbm.at[phys], ki_buf.at[to_slot, g, :, pl.ds(p * page, page)],
                                     sem.at[to_slot])

    def start_all(step, to_slot):
        for g in range(ga):
            for p in range(n_pages):
                page_copy(step, to_slot, g, p).start()

    @pl.when(i == 0)
    def _():
        start_all(0, 0)

    @pl.when(i + 1 < n_steps)
    def _():
        start_all(i + 1, 1 - slot)

    ki_buf[slot, :, :, pl.ds(past, LANES)] = kinew_ref[...]
    for g in range(ga):
        for p in range(n_pages):
            page_copy(i, slot, g, p).wait()

    kpos = lax.broadcasted_iota(I32, (ts, lpad), 1)
    for g in range(ga):
        s = _dot(qi4_ref[g], ki_buf[slot, g].astype(BF16))
        r = jnp.maximum(s, 0.0) * wcol_ref[g]
        tot = r[0:ts]
        for h in range(1, IDX_HEADS):
            tot = tot + r[h * ts:(h + 1) * ts]
        adm = (kpos <= qpos_ref[g * ts:(g + 1) * ts, :]) & (kpos < past + ts)
        key_ref[g * ts:(g + 1) * ts, :] = jnp.where(adm, _order_key(tot), INT_MIN)

    sub = lpad // LANES
    lane128 = lax.broadcasted_iota(I32, (rows, LANES), 1)

    def row_total(cnt):
        tot = jnp.sum(cnt.astype(F32), axis=1, keepdims=True)
        return jnp.broadcast_to(tot, cnt.shape).astype(I32)

    def count_ge(cand):
        cnt = jnp.zeros((rows, LANES), I32)
        for s_ in range(sub):
            cnt = cnt + jnp.where(key_ref[:, s_ * LANES:(s_ + 1) * LANES] >= cand, 1, 0)
        return row_total(cnt)

    def count_gt_eq_lt(thr, pos):
        n_gt = jnp.zeros((rows, LANES), I32)
        n_eq = jnp.zeros((rows, LANES), I32)
        for s_ in range(sub):
            kk = key_ref[:, s_ * LANES:(s_ + 1) * LANES]
            n_gt = n_gt + jnp.where(kk > thr, 1, 0)
            n_eq = n_eq + jnp.where((kk == thr) & (s_ * LANES + lane128 < pos), 1, 0)
        return row_total(n_gt), row_total(n_eq)

    thr, cut = _select_threshold(count_ge, count_gt_eq_lt, (rows, LANES), topk, idx_bits)
    for s_ in range(sub):
        kk = key_ref[:, s_ * LANES:(s_ + 1) * LANES]
        sel = (kk > thr) | ((kk == thr) & (s_ * LANES + lane128 < cut))
        bias = jnp.where(sel, 0.0, -jnp.inf)
        for g in range(ga):
            bias_ref[g, :, s_ * LANES:(s_ + 1) * LANES] = bias[g * ts:(g + 1) * ts]


def _idx_sample(page_table, qi4, wcol, qpos, kinew_t, kidx_t, ts, ga):
    db = qi4.shape[0]
    n_pages = page_table.shape[1]
    page = kidx_t.shape[2]
    past = n_pages * page
    lpad = past + LANES
    topk = min(TOPK_MAX, (past + ts) // 4)
    idx_bits = max(1, (lpad - 1).bit_length())
    kernel = functools.partial(_idx_sample_kernel, ga=ga, ts=ts, n_pages=n_pages, page=page, lpad=lpad,
                               topk=topk, idx_bits=idx_bits)
    return pl.pallas_call(
        kernel,
        grid_spec=pltpu.PrefetchScalarGridSpec(
            num_scalar_prefetch=1,
            grid=(db // ga,),
            in_specs=[pl.BlockSpec((ga, IDX_HEADS * ts, IDX_DIM), lambda i, pt: (i, 0, 0)),
                      pl.BlockSpec((ga, IDX_HEADS * ts, 1), lambda i, pt: (i, 0, 0)),
                      pl.BlockSpec((ga * ts, 1), lambda i, pt: (i, 0)),
                      pl.BlockSpec((ga, IDX_DIM, LANES), lambda i, pt: (i, 0, 0)),
                      pl.BlockSpec(memory_space=pl.ANY)],
            out_specs=pl.BlockSpec((ga, ts, lpad), lambda i, pt: (i, 0, 0)),
            scratch_shapes=[
                pltpu.VMEM((2, ga, IDX_DIM, lpad), F32),
                pltpu.VMEM((ga * ts, lpad), I32),
                pltpu.SemaphoreType.DMA((2,)),
            ]),
        out_shape=jax.ShapeDtypeStruct((db, ts, lpad), F32),
        compiler_params=pltpu.CompilerParams(dimension_semantics=("arbitrary",), vmem_limit_bytes=VMEM_LIMIT),
        name="idx_sample",
    )(page_table, qi4, wcol, qpos, kinew_t, kidx_t)


def _attn_sample_kernel(pt_ref, q_ref, bias_ref, knew_ref, vnew_ref, k_hbm, v_hbm, o_ref,
                        k_buf, v_buf, sem, *, ts, n_pages, page):
    b = pl.program_id(0)
    nb = pl.num_programs(0)
    past = n_pages * page
    slot = b % 2

    def page_copies(seq, to_slot, p):
        phys = pt_ref[seq, p]
        dst = pl.ds(p * page, page)
        return (pltpu.make_async_copy(k_hbm.at[phys], k_buf.at[to_slot, :, :, dst], sem.at[0, to_slot]),
                pltpu.make_async_copy(v_hbm.at[phys], v_buf.at[to_slot, :, :, dst], sem.at[1, to_slot]))

    def start_all(seq, to_slot):
        for p in range(n_pages):
            for cp in page_copies(seq, to_slot, p):
                cp.start()

    @pl.when(b == 0)
    def _():
        start_all(0, 0)

    @pl.when(b + 1 < nb)
    def _():
        start_all(b + 1, 1 - slot)

    k_buf[slot, :, :, pl.ds(past, LANES)] = knew_ref[...]
    v_buf[slot, :, :, pl.ds(past, LANES)] = vnew_ref[...]
    for p in range(n_pages):
        for cp in page_copies(b, slot, p):
            cp.wait()

    q = q_ref[...]
    bias = bias_ref[...]
    bias2 = jnp.concatenate([bias] * B_GROUP, axis=0)
    outs = [None] * B_HEADS
    for n in range(B_KV_HEADS):
        kt = k_buf[slot, n].astype(BF16)
        vt = v_buf[slot, n].astype(BF16)
        qs = jnp.concatenate([q[:, (n * B_GROUP + g) * HEAD_DIM:(n * B_GROUP + g + 1) * HEAD_DIM]
                              for g in range(B_GROUP)], axis=0)
        sc = _dot(qs, kt) + bias2
        m = jnp.max(sc, axis=1, keepdims=True)
        p_ = jnp.exp(sc - m)
        o = _dot_nt(p_.astype(BF16), vt) / jnp.sum(p_, axis=1, keepdims=True)
        for g in range(B_GROUP):
            outs[n * B_GROUP + g] = o[g * ts:(g + 1) * ts]
    o_ref[...] = jnp.concatenate(outs, axis=1).astype(BF16)


def _attn_sample(page_table, q, bias, knew_t, vnew_t, k_t, v_t):
    db, ts, _ = q.shape
    n_pages = page_table.shape[1]
    page = k_t.shape[3]
    lpad = bias.shape[2]
    seq3 = lambda a: pl.BlockSpec((None,) + a.shape[1:], lambda b, pt: (b,) + (0,) * (a.ndim - 1))
    anyspec = pl.BlockSpec(memory_space=pl.ANY)
    kernel = functools.partial(_attn_sample_kernel, ts=ts, n_pages=n_pages, page=page)
    return pl.pallas_call(
        kernel,
        grid_spec=pltpu.PrefetchScalarGridSpec(
            num_scalar_prefetch=1,
            grid=(db,),
            in_specs=[seq3(q), seq3(bias), seq3(knew_t), seq3(vnew_t), anyspec, anyspec],
            out_specs=pl.BlockSpec((None, ts, B_WIDTH), lambda b, pt: (b, 0, 0)),
            scratch_shapes=[
                pltpu.VMEM((2, B_KV_HEADS, HEAD_DIM, lpad), F32),
                pltpu.VMEM((2, B_KV_HEADS, HEAD_DIM, lpad), F32),
                pltpu.SemaphoreType.DMA((2, 2)),
            ]),
        out_shape=jax.ShapeDtypeStruct((db, ts, B_WIDTH), BF16),
        compiler_params=pltpu.CompilerParams(dimension_semantics=("arbitrary",), vmem_limit_bytes=VMEM_LIMIT),
        name="attn_sample",
    )(page_table, q, bias, knew_t, vnew_t, k_t, v_t)


def _layer_norm_rows(y, gam, bet):
    mu = jnp.mean(y, axis=1, keepdims=True)
    d = y - mu
    var = jnp.mean(d * d, axis=1, keepdims=True)
    return d * lax.rsqrt(var + LN_EPS) * gam + bet


def _store_row_tiles(ref, val):
    r = val.shape[0]
    for c in range(ROW_CHUNKS):
        ref[pl.ds(c, r, stride=ROW_CHUNKS), :] = val[:, c * LANES:(c + 1) * LANES]


def _load_row_tiles(ref, r):
    return jnp.concatenate([ref[pl.ds(c, r, stride=ROW_CHUNKS), :] for c in range(ROW_CHUNKS)], axis=1)


def _route(logits):
    lane = lax.broadcasted_iota(I32, logits.shape, 1)
    lane_f = lane.astype(F32)
    far = float(LANES)

    def softmax_in(mask):
        x = jnp.where(mask, logits, -jnp.inf)
        e = jnp.exp(x - jnp.max(x, axis=1, keepdims=True))
        return e / jnp.sum(e, axis=1, keepdims=True)

    def first_max(p, mask):
        best = jnp.max(jnp.where(mask, p, -1.0), axis=1, keepdims=True)
        where_best = jnp.min(jnp.where(mask & (p == best), lane_f, far), axis=1, keepdims=True)
        return best, where_best

    gmask = lane < N_GROUPS
    g_w, g_sel = first_max(softmax_in(gmask), gmask)
    lo = N_GROUPS + EXPERTS_PER_GROUP * g_sel
    emask = (lane_f >= lo) & (lane_f < lo + EXPERTS_PER_GROUP)
    ep = softmax_in(emask)
    p1, i1 = first_max(ep, emask)
    rest = emask & (lane_f != i1)
    p2, i2 = first_max(ep, rest)
    denom = p1 + p2
    w1 = g_w * p1 / denom
    w2 = g_w * p2 / denom
    out = jnp.where(lane == 0, i1 - N_GROUPS, 0.0)
    out = jnp.where(lane == 1, i2 - N_GROUPS, out)
    out = jnp.where(lane == 2, w1, out)
    return jnp.where(lane == 3, w2, out)


def _mid_kernel(x_ref, a_ref, b_ref, g1_ref, sh2_ref, sc2_ref, wo_ref, gam_ref, bet_ref, wr_ref,
                x1_ref, h2_ref, route_ref, *, alpha):
    mixed = _dot(a_ref[...], wo_ref[0:A_WIDTH, :]) + _dot(b_ref[...], wo_ref[A_WIDTH:A_WIDTH + B_WIDTH, :])
    x1 = _layer_norm_rows(alpha * x_ref[...] + g1_ref[...] * mixed, gam_ref[...], bet_ref[...])
    x1_ref[...] = x1
    h2 = x1 * (1.0 + sc2_ref[...]) + sh2_ref[...]
    _store_row_tiles(h2_ref, h2)
    logits = jnp.dot(h2, wr_ref[...], precision=lax.Precision.HIGHEST, preferred_element_type=F32)
    route_ref[...] = _route(logits)


def _mod_spec(mod3, comp, tiles_per_group):
    rm = mod3.shape[1]
    return pl.BlockSpec((None, rm, D_MODEL), lambda i: (i // tiles_per_group, 0, comp))


def _mid(x, a, b, mod3, tiles_per_group, tile, w_out, gam, bet, w_router, alpha):
    n = x.shape[0]
    rows = lambda w: pl.BlockSpec((tile, w), lambda i: (i, 0))
    whole = lambda arr: pl.BlockSpec(arr.shape, lambda i: (0,) * arr.ndim)
    return pl.pallas_call(
        functools.partial(_mid_kernel, alpha=alpha),
        grid=(n // tile,),
        in_specs=[rows(D_MODEL), rows(A_WIDTH), rows(B_WIDTH),
                  _mod_spec(mod3, 2, tiles_per_group), _mod_spec(mod3, 3, tiles_per_group),
                  _mod_spec(mod3, 4, tiles_per_group),
                  whole(w_out), whole(gam), whole(bet), whole(w_router)],
        out_specs=[rows(D_MODEL), pl.BlockSpec((tile * ROW_CHUNKS, LANES), lambda i: (i, 0)), rows(LANES)],
        out_shape=(jax.ShapeDtypeStruct((n, D_MODEL), F32),
                   jax.ShapeDtypeStruct((n * ROW_CHUNKS, LANES), F32),
                   jax.ShapeDtypeStruct((n, LANES), F32)),
        compiler_params=pltpu.CompilerParams(dimension_semantics=("arbitrary",), vmem_limit_bytes=VMEM_LIMIT),
        name="mid",
    )(x, a, b, mod3, mod3, mod3, w_out, gam, bet, w_router)


def _rank_kernel(eid_ref, rank_ref, cnt_ref, carry_ref, *, nt):
    i = pl.program_id(0)

    @pl.when(i == 0)
    def _():
        carry_ref[...] = jnp.zeros(carry_ref.shape, F32)

    e_iota = lax.broadcasted_iota(I32, (N_EXPERTS, nt), 0)
    hit0 = eid_ref[0:1, :] == e_iota
    hit1 = eid_ref[1:2, :] == e_iota
    onehot = jnp.where(hit0 | hit1, 1.0, 0.0)
    upper = (lax.broadcasted_iota(I32, (nt, nt), 0) <= lax.broadcasted_iota(I32, (nt, nt), 1))
    incl = _dot(onehot.astype(BF16), jnp.where(upper, 1.0, 0.0).astype(BF16))
    before = carry_ref[...] + incl - 1.0
    rank_ref[0:1, :] = jnp.sum(jnp.where(hit0, before, 0.0), axis=0, keepdims=True).astype(I32)
    rank_ref[1:2, :] = jnp.sum(jnp.where(hit1, before, 0.0), axis=0, keepdims=True).astype(I32)
    carry_ref[...] = carry_ref[...] + jnp.sum(onehot, axis=1, keepdims=True)
    cnt_ref[...] = carry_ref[...].astype(I32)


def _rank(eids, nt):
    n = eids.shape[1]
    return pl.pallas_call(
        functools.partial(_rank_kernel, nt=nt),
        grid=(n // nt,),
        in_specs=[pl.BlockSpec((2, nt), lambda i: (0, i))],
        out_specs=[pl.BlockSpec((2, nt), lambda i: (0, i)), pl.BlockSpec((N_EXPERTS, 1), lambda i: (0, 0))],
        out_shape=(jax.ShapeDtypeStruct((2, n), I32), jax.ShapeDtypeStruct((N_EXPERTS, 1), I32)),
        scratch_shapes=[pltpu.VMEM((N_EXPERTS, 1), F32)],
        compiler_params=pltpu.CompilerParams(dimension_semantics=("arbitrary",)),
        name="rank",
    )(eids)


def _row_copy(src, src_row, dst, dst_row, sem):
    return pltpu.make_async_copy(src.at[pl.ds(src_row * ROW_CHUNKS, ROW_CHUNKS)],
                                 dst.at[pl.ds(dst_row * ROW_CHUNKS, ROW_CHUNKS)], sem)


def _dispatch_kernel(pos_ref, h_ref, xs_in, xs_out, sem, *, tt):
    del xs_in

    def issue(j, _):
        for k in range(2):
            _row_copy(h_ref, j, xs_out, pos_ref[k, j], sem).start()
        return 0

    def drain(j, _):
        for k in range(2):
            _row_copy(h_ref, j, xs_out, pos_ref[k, j], sem).wait()
        return 0

    lax.fori_loop(0, tt, issue, 0)
    lax.fori_loop(0, tt, drain, 0)


def _dispatch(pos, h2, xs_init, tt):
    n = pos.shape[1]
    return pl.pallas_call(
        functools.partial(_dispatch_kernel, tt=tt),
        grid=(n // tt,),
        in_specs=[pl.BlockSpec((2, tt), lambda i: (0, i), memory_space=pltpu.SMEM),
                  pl.BlockSpec((tt * ROW_CHUNKS, LANES), lambda i: (i, 0)),
                  pl.BlockSpec(memory_space=pl.ANY)],
        out_specs=pl.BlockSpec(memory_space=pl.ANY),
        out_shape=jax.ShapeDtypeStruct(xs_init.shape, xs_init.dtype),
        scratch_shapes=[pltpu.SemaphoreType.DMA(())],
        input_output_aliases={2: 0},
        compiler_params=pltpu.CompilerParams(dimension_semantics=("arbitrary",)),
        name="dispatch",
    )(pos, h2, xs_init)


def _experts_kernel(te_ref, nact_ref, xs_ref, wg_ref, wu_ref, wd_ref, ys_ref, wg_b, wu_b, wd_b, *, tm):
    i = pl.program_id(0)
    active = i < nact_ref[0]
    fresh = (i == 0) | (te_ref[i] != te_ref[jnp.maximum(i - 1, 0)])

    @pl.when(active & fresh)
    def _():
        wg_b[...] = wg_ref[...].astype(BF16)
        wu_b[...] = wu_ref[...].astype(BF16)
        wd_b[...] = wd_ref[...].astype(BF16)

    @pl.when(active)
    def _():
        x = _load_row_tiles(xs_ref, tm).astype(BF16)
        hid = (jax.nn.silu(_dot(x, wg_b[...])) * _dot(x, wu_b[...])).astype(BF16)
        _store_row_tiles(ys_ref, _dot(hid, wd_b[...]))

    @pl.when(jnp.logical_not(active))
    def _():
        ys_ref[...] = jnp.zeros(ys_ref.shape, F32)


def _experts(te, nact, xs, w_gate, w_up, w_down, tm):
    n_tiles = te.shape[0]
    tile = pl.BlockSpec((tm * ROW_CHUNKS, LANES), lambda i, te, na: (jnp.minimum(i, na[0] - 1), 0))
    w_in_spec = pl.BlockSpec((None, D_MODEL, D_EXPERT), lambda i, te, na: (te[i], 0, 0))
    w_out_spec = pl.BlockSpec((None, D_EXPERT, D_MODEL), lambda i, te, na: (te[i], 0, 0))
    return pl.pallas_call(
        functools.partial(_experts_kernel, tm=tm),
        grid_spec=pltpu.PrefetchScalarGridSpec(
            num_scalar_prefetch=2,
            grid=(n_tiles,),
            in_specs=[tile, w_in_spec, w_in_spec, w_out_spec],
            out_specs=pl.BlockSpec((tm * ROW_CHUNKS, LANES), lambda i, te, na: (i, 0)),
            scratch_shapes=[pltpu.VMEM((D_MODEL, D_EXPERT), BF16), pltpu.VMEM((D_MODEL, D_EXPERT), BF16),
                            pltpu.VMEM((D_EXPERT, D_MODEL), BF16)]),
        out_shape=jax.ShapeDtypeStruct(xs.shape, F32),
        compiler_params=pltpu.CompilerParams(dimension_semantics=("arbitrary",), vmem_limit_bytes=VMEM_LIMIT),
        name="experts",
    )(te, nact, xs, w_gate, w_up, w_down)


def _combine_kernel(pos_ref, route_ref, x1_ref, g2_ref, gam_ref, bet_ref, ys_hbm, y_ref, buf0, buf1, sem,
                    *, tt, alpha):
    bufs = (buf0, buf1)

    def issue(j, _):
        for k in range(2):
            _row_copy(ys_hbm, pos_ref[k, j], bufs[k], j, sem).start()
        return 0

    def drain(j, _):
        for k in range(2):
            _row_copy(ys_hbm, pos_ref[k, j], bufs[k], j, sem).wait()
        return 0

    lax.fori_loop(0, tt, issue, 0)
    lax.fori_loop(0, tt, drain, 0)
    route = route_ref[...]
    f = route[:, 2:3] * _load_row_tiles(buf0, tt) + route[:, 3:4] * _load_row_tiles(buf1, tt)
    y_ref[...] = _layer_norm_rows(alpha * x1_ref[...] + g2_ref[...] * f, gam_ref[...], bet_ref[...])


def _combine(pos, route, x1, mod3, tiles_per_group, tt, gam, bet, ys, alpha):
    n = x1.shape[0]
    rows = lambda w: pl.BlockSpec((tt, w), lambda i: (i, 0))
    whole = lambda arr: pl.BlockSpec(arr.shape, lambda i: (0,) * arr.ndim)
    return pl.pallas_call(
        functools.partial(_combine_kernel, tt=tt, alpha=alpha),
        grid=(n // tt,),
        in_specs=[pl.BlockSpec((2, tt), lambda i: (0, i), memory_space=pltpu.SMEM),
                  rows(LANES), rows(D_MODEL), _mod_spec(mod3, 5, tiles_per_group), whole(gam), whole(bet),
                  pl.BlockSpec(memory_space=pl.ANY)],
        out_specs=rows(D_MODEL),
        out_shape=jax.ShapeDtypeStruct((n, D_MODEL), F32),
        scratch_shapes=[pltpu.VMEM((tt * ROW_CHUNKS, LANES), F32), pltpu.VMEM((tt * ROW_CHUNKS, LANES), F32),
                        pltpu.SemaphoreType.DMA(())],
        compiler_params=pltpu.CompilerParams(dimension_semantics=("arbitrary",), vmem_limit_bytes=VMEM_LIMIT),
        name="combine",
    )(pos, route, x1, mod3, gam, bet, ys)


def _moe(h2, route, w_gate, w_up, w_down):
    n = route.shape[0]
    eids = route[:, 0:2].astype(I32).T
    rank, counts = _rank(eids, min(RANK_TILE, n))
    counts = counts[:, 0]
    padded = (counts + MOE_TILE - 1) // MOE_TILE * MOE_TILE
    ends = jnp.cumsum(padded)
    starts = ends - padded
    pos = starts[eids] + rank
    n_tiles = (2 * n) // MOE_TILE + N_EXPERTS
    nact = (ends[-1] // MOE_TILE).astype(I32).reshape(1)
    tile_start = jnp.arange(n_tiles, dtype=I32) * MOE_TILE
    te = jnp.searchsorted(ends, jnp.minimum(tile_start, ends[-1] - 1), side="right").astype(I32)
    xs = _dispatch(pos, h2, jnp.zeros((n_tiles * MOE_TILE * ROW_CHUNKS, LANES), F32), min(TOKEN_TILE, n))
    ys = _experts(te, nact, xs, w_gate, w_up, w_down, MOE_TILE)
    return pos, ys


def kernel(x_prompt, x_sample, cache_k, cache_v, cache_kidx, page_table, c_prompt, c_sample, w_ada, b_ada, w_in,
           a_ln_g, a_ln_b, w_spatial, b_spatial, w_out, ln1_g, ln1_b, w_group_router, w_expert_router, w_gate,
           w_up, w_down, ln2_g, ln2_b):
    depth = w_ada.shape[0]
    assert depth == 1, "one trunk layer"
    alpha = (2.0 * depth) ** 0.25
    bsz, seq, d = x_prompt.shape
    db, ts, _ = x_sample.shape
    n_pages = page_table.shape[1]
    page = cache_k.shape[2]
    past = n_pages * page
    tile_p = min(PROMPT_TILE, seq)
    assert d == D_MODEL and seq % tile_p == 0 and tile_p % CHUNK == 0 and ts <= SUBLANES and page == LANES
    l = 0

    n_c = bsz + db
    n_c_pad = pl.cdiv(n_c, SUBLANES) * SUBLANES
    c_all = jnp.pad(jnp.concatenate([c_prompt, c_sample], axis=0), ((0, n_c_pad - n_c), (0, 0)))
    mod = _ada(c_all, w_ada[l], b_ada[l][None, :])
    mod_p = mod[:bsz]
    mod_s = mod[bsz:n_c]

    w_in_b = _pad_w_in(w_in[l])
    w_out_b = w_out[l].astype(BF16)
    mavg = _head_avg_matrix()
    gam_a = a_ln_g[l].reshape(1, A_WIDTH)
    bet_a = a_ln_b[l].reshape(1, A_WIDTH)
    w_router = jnp.pad(jnp.concatenate([w_group_router[l], w_expert_router[l].reshape(d, N_EXPERTS)], axis=1),
                       ((0, 0), (0, LANES - N_GROUPS - N_EXPERTS)))
    ln1 = (ln1_g[l][None, :], ln1_b[l][None, :])
    ln2 = (ln2_g[l][None, :], ln2_b[l][None, :])

    wsp = w_spatial[l].reshape(A_HEADS // 2, 2, CHUNK, CHUNK).transpose(0, 2, 1, 3).reshape(A_HEADS // 2, CHUNK, 2 * CHUNK)
    bsp = jnp.repeat(b_spatial[l].T, HEAD_DIM, axis=1)
    a_p, kt_p, vt32_p, kit_p, kb_p, kiwib_p, vt_p, qt_p, qit_p, wit_p = _front_prompt(
        x_prompt, mod_p.reshape(bsz, 6, d), w_in_b, _rot_tables(jnp.arange(seq)), mavg, gam_a, bet_a, wsp, bsp, tile_p)
    b_p = _attn_prompt(qit_p, wit_p, qt_p, kiwib_p, kb_p, vt_p)
    n_p = bsz * seq
    x1_p, h2_p, route_p = _mid(x_prompt.reshape(n_p, d), a_p.reshape(n_p, A_WIDTH), b_p.reshape(n_p, B_WIDTH),
                               mod_p.reshape(bsz, 1, 6 * d), seq // tile_p, tile_p, w_out_b, *ln1, w_router, alpha)
    pos_p, ys_p = _moe(h2_p, route_p, w_gate[l], w_up[l], w_down[l])
    tok_tile = min(TOKEN_TILE, seq)
    y_p = _combine(pos_p, route_p, x1_p, mod_p.reshape(bsz, 1, 6 * d), seq // tok_tile, tok_tile, *ln2, ys_p, alpha)

    r_s = ts * db
    x_tm = x_sample.transpose(1, 0, 2).reshape(r_s, d)
    rt_s = _rot_tables(jnp.repeat(past + jnp.arange(ts), db))
    w_small = w_spatial[l][:, :ts, :ts]
    wl = jnp.repeat(w_small.transpose(1, 2, 0).reshape(ts * ts, A_HEADS), HEAD_DIM, axis=1)
    bl = jnp.repeat(b_spatial[l][:, :ts].T, HEAD_DIM, axis=1)
    a_s, q_s, k_s, v_s, qi_s, kiwi_s, vg_s = _front_sample(x_tm, mod_s, w_in_b, rt_s, mavg, gam_a, bet_a, wl, bl, ts, db)

    def seq_major(a):
        return a.reshape(ts, db, a.shape[-1]).transpose(1, 0, 2)

    def new_t(a, heads):
        a = a.reshape(ts, db, heads, HEAD_DIM).transpose(1, 2, 3, 0)
        return jnp.pad(a, ((0, 0), (0, 0), (0, 0), (0, LANES - ts)))

    ga = min(IDX_SEQS, db)
    assert db % ga == 0 and (ga * ts) % SUBLANES == 0
    qi4 = qi_s.reshape(ts, db, IDX_HEADS, IDX_DIM).transpose(1, 2, 0, 3).reshape(db, IDX_HEADS * ts, IDX_DIM)
    wcol = kiwi_s[:, WI_LANE:WI_LANE + IDX_HEADS].reshape(ts, db, IDX_HEADS).transpose(1, 2, 0).reshape(db, IDX_HEADS * ts, 1)
    qpos = jnp.tile(past + jnp.arange(ts, dtype=I32), db).reshape(db * ts, 1)
    kinew_t = new_t(kiwi_s[:, :IDX_DIM], 1)[:, 0]
    kidx_t = jnp.transpose(cache_kidx[l], (0, 2, 1))
    k_t = jnp.transpose(cache_k[l], (0, 2, 3, 1))
    v_t = jnp.transpose(cache_v[l], (0, 2, 3, 1))
    bias_s = _idx_sample(page_table, qi4, wcol, qpos, kinew_t, kidx_t, ts, ga)
    b_s = _attn_sample(page_table, seq_major(q_s), bias_s, new_t(k_s, B_KV_HEADS), new_t(v_s, B_KV_HEADS), k_t, v_t)
    b_s_tm = b_s.transpose(1, 0, 2).reshape(r_s, B_WIDTH)
    x1_s, h2_s, route_s = _mid(x_tm, a_s, b_s_tm, mod_s.reshape(1, db, 6 * d), ts, db, w_out_b, *ln1, w_router, alpha)
    pos_s, ys_s = _moe(h2_s, route_s, w_gate[l], w_up[l], w_down[l])
    y_s_tm = _combine(pos_s, route_s, x1_s, mod_s.reshape(1, db, 6 * d), ts, db, *ln2, ys_s, alpha)
    y_s = y_s_tm.reshape(ts, db, d).transpose(1, 0, 2)

    kv5 = lambda a, n, t: a.reshape(1, n, t, B_KV_HEADS, HEAD_DIM)
    kv5_t = lambda a: a.reshape(1, bsz, B_KV_HEADS, HEAD_DIM, seq).transpose(0, 1, 4, 2, 3)
    return (y_p.reshape(bsz, seq, d), y_s,
            kv5_t(kt_p), kv5_t(vt32_p), kit_p.transpose(0, 2, 1)[None],
            kv5(seq_major(k_s), db, ts), kv5(seq_major(v_s), db, ts), seq_major(kiwi_s)[..., :IDX_DIM][None],
            seq_major(vg_s).reshape(1, db, ts, A_HEADS, HEAD_DIM))
```

```python
import functools

import jax
import jax.numpy as jnp
import numpy as np
from jax import lax
from jax.experimental import pallas as pl
from jax.experimental.pallas import tpu as pltpu

F32 = jnp.float32
BF16 = jnp.bfloat16
I32 = jnp.int32

D_MODEL = 1024
HEAD_DIM = 64
A_HEADS = 8
A_WIDTH = A_HEADS * HEAD_DIM
CHUNK = 128
B_HEADS = 8
B_KV_HEADS = 4
B_GROUP = B_HEADS // B_KV_HEADS
B_WIDTH = B_HEADS * HEAD_DIM
KV_WIDTH = B_KV_HEADS * HEAD_DIM
IDX_HEADS = 4
IDX_DIM = 64
IDX_WIDTH = IDX_HEADS * IDX_DIM
TOPK_MAX = 256
ROPE_THETA = 500000.0
ROT_DIM = HEAD_DIM // 4
ROT_HALF = ROT_DIM // 2
ATTN_SCALE = HEAD_DIM ** -0.5
N_GROUPS = 4
EXPERTS_PER_GROUP = 8
N_EXPERTS = N_GROUPS * EXPERTS_PER_GROUP
D_EXPERT = 512
LN_EPS = 1e-5

LANES = 128
SUBLANES = 8
ROW_CHUNKS = D_MODEL // LANES

C_AU, C_AV, C_Q, C_K, C_V, C_QI, C_KI = 0, 512, 1024, 1536, 1792, 2048, 2304
IN_WIDTH = C_KI + IDX_DIM + IDX_HEADS
IN_PAD = 2432
WI_LANE = IDX_DIM

INT_MIN = -(2 ** 31)
VMEM_LIMIT = 48 * 1024 * 1024

PROMPT_TILE = 256
MOE_TILE = 256
TOKEN_TILE = 256
RANK_TILE = 512
IDX_SEQS = 16
BITS_PER_CHECK = 4


def _dot(a, b):
    return jnp.dot(a, b, preferred_element_type=F32)


def _dot_nt(a, b):
    return lax.dot_general(a, b, (((1,), (1,)), ((), ())), preferred_element_type=F32)


def _split_dot(x, m):
    hi = x.astype(BF16)
    lo = (x - hi.astype(F32)).astype(BF16)
    return _dot(hi, m) + _dot(lo, m)


def _ada_kernel(c_ref, w_ref, b_ref, o_ref):
    s = jax.nn.silu(c_ref[...]).astype(BF16)
    o_ref[...] = _dot(s, w_ref[...].astype(BF16)) + b_ref[...]


def _ada(c_all, w_ada, b_ada):
    rows = c_all.shape[0]
    n_out = w_ada.shape[1]
    tn = 1024
    return pl.pallas_call(
        _ada_kernel,
        grid=(n_out // tn,),
        in_specs=[
            pl.BlockSpec((rows, D_MODEL), lambda j: (0, 0)),
            pl.BlockSpec((D_MODEL, tn), lambda j: (0, j)),
            pl.BlockSpec((1, tn), lambda j: (0, j)),
        ],
        out_specs=pl.BlockSpec((rows, tn), lambda j: (0, j)),
        out_shape=jax.ShapeDtypeStruct((rows, n_out), F32),
        compiler_params=pltpu.CompilerParams(
            dimension_semantics=("arbitrary",), vmem_limit_bytes=VMEM_LIMIT),
        name="ada",
    )(c_all, w_ada, b_ada)


def _rotate(x, rt, blk):
    c = rt[:, 0:LANES]
    s_lo = rt[:, LANES:2 * LANES]
    s_hi = rt[:, 2 * LANES:3 * LANES]
    if blk == 1:
        head = lax.broadcasted_iota(I32, c.shape, 1) < IDX_DIM
        c = jnp.where(head, c, 1.0)
        s_lo = jnp.where(head, s_lo, 0.0)
        s_hi = jnp.where(head, s_hi, 0.0)
    outs = []
    for j in range(x.shape[1] // LANES):
        xb = x[:, j * LANES:(j + 1) * LANES]
        up = pltpu.roll(xb, LANES - ROT_HALF, 1)
        dn = pltpu.roll(xb, ROT_HALF, 1)
        outs.append(xb * c + up * s_lo + dn * s_hi)
    return outs[0] if len(outs) == 1 else jnp.concatenate(outs, axis=1)


def _head_ln(g, mavg, gam, bet):
    def seg_mean(x):
        parts = [_split_dot(x[:, j * 256:(j + 1) * 256], mavg) for j in range(A_WIDTH // 256)]
        return jnp.concatenate(parts, axis=1)
    mu = seg_mean(g)
    d = g - mu
    var = seg_mean(d * d)
    return d * lax.rsqrt(var + LN_EPS) * gam + bet


def _project(h, w_ref, rt, mavg, gam, bet):
    u = jax.nn.gelu(_dot(h, w_ref[:, C_AU:C_AV]))
    vg = _head_ln(jax.nn.gelu(_dot(h, w_ref[:, C_AV:C_Q])), mavg, gam, bet)
    q = _rotate(_dot(h, w_ref[:, C_Q:C_K]), rt, 0) * ATTN_SCALE
    k = _rotate(_dot(h, w_ref[:, C_K:C_V]), rt, 0)
    v = _dot(h, w_ref[:, C_V:C_QI])
    qi = _rotate(_dot(h, w_ref[:, C_QI:C_KI]), rt, 0)
    kiwi = _rotate(_dot(h, w_ref[:, C_KI:IN_PAD]), rt, 1)
    return u, vg, q, k, v, qi, kiwi


def _front_prompt_kernel(x_ref, mod_ref, w_ref, rt_ref, mavg_ref, gam_ref, bet_ref, wsp_ref, bsp_ref,
                         a_ref, kt_ref, vt_ref, kit_ref, kb_ref, kiwib_ref, vtb_ref, qt_ref, qit_ref, wit_ref):
    shift = mod_ref[0:1, :]
    scale = mod_ref[1:2, :]
    h = (x_ref[...] * (1.0 + scale) + shift).astype(BF16)
    u, vg, q, k, v, qi, kiwi = _project(h, w_ref, rt_ref[...], mavg_ref[...], gam_ref[...], bet_ref[...])
    v_t = jnp.transpose(v)
    kiwi_t = jnp.transpose(kiwi)
    kt_ref[...] = jnp.transpose(k)
    vt_ref[...] = v_t
    kit_ref[...] = kiwi_t[0:IDX_DIM, :]
    kb_ref[...] = k.astype(BF16)
    kiwib_ref[...] = kiwi.astype(BF16)
    vtb_ref[...] = v_t.astype(BF16)
    qt_ref[...] = jnp.transpose(q).astype(BF16)
    qit_ref[...] = jnp.transpose(qi).astype(BF16)
    wit_ref[...] = kiwi_t[WI_LANE:WI_LANE + SUBLANES, :]

    rows = lax.broadcasted_iota(I32, (CHUNK, 2 * CHUNK), 0)
    cols = lax.broadcasted_iota(I32, (CHUNK, 2 * CHUNK), 1) % CHUNK
    causal = cols <= rows
    lane = lax.broadcasted_iota(I32, (CHUNK, LANES), 1)
    tt = x_ref.shape[0]
    for cidx in range(tt // CHUNK):
        rs = slice(cidx * CHUNK, (cidx + 1) * CHUNK)
        blocks = []
        for p in range(A_HEADS // 2):
            wcat = jnp.where(causal, wsp_ref[p], 0.0).astype(BF16)
            vb = vg[rs, p * LANES:(p + 1) * LANES]
            rhs = jnp.concatenate([jnp.where(lane < HEAD_DIM, vb, 0.0),
                                   jnp.where(lane >= HEAD_DIM, vb, 0.0)], axis=0).astype(BF16)
            blocks.append(_dot(wcat, rhs))
        s = jnp.concatenate(blocks, axis=1) + bsp_ref[...]
        a_ref[rs, :] = (u[rs, :] * s).astype(BF16)


def _front_prompt(x, mod, w_in, rt, mavg, gam, bet, wsp, bsp, tt):
    b, t, _ = x.shape
    nc = t // tt
    tok = lambda w: pl.BlockSpec((None, tt, w), lambda i, j: (i, j, 0))
    tr = lambda r: pl.BlockSpec((None, None, r, tt), lambda i, j: (i, j, 0, 0))
    pos_minor = lambda r: pl.BlockSpec((None, r, tt), lambda i, j: (i, 0, j))
    const2 = lambda a: pl.BlockSpec(a.shape, lambda i, j: (0,) * a.ndim)
    out_shapes = (
        jax.ShapeDtypeStruct((b, t, A_WIDTH), BF16),
        jax.ShapeDtypeStruct((b, KV_WIDTH, t), F32),
        jax.ShapeDtypeStruct((b, KV_WIDTH, t), F32),
        jax.ShapeDtypeStruct((b, IDX_DIM, t), F32),
        jax.ShapeDtypeStruct((b, t, KV_WIDTH), BF16),
        jax.ShapeDtypeStruct((b, t, LANES), BF16),
        jax.ShapeDtypeStruct((b, nc, KV_WIDTH, tt), BF16),
        jax.ShapeDtypeStruct((b, nc, B_WIDTH, tt), BF16),
        jax.ShapeDtypeStruct((b, nc, IDX_WIDTH, tt), BF16),
        jax.ShapeDtypeStruct((b, nc, SUBLANES, tt), F32),
    )
    return pl.pallas_call(
        _front_prompt_kernel,
        grid=(b, nc),
        in_specs=[
            tok(D_MODEL),
            pl.BlockSpec((None, 6, D_MODEL), lambda i, j: (i, 0, 0)),
            const2(w_in),
            pl.BlockSpec((tt, 3 * LANES), lambda i, j: (j, 0)),
            const2(mavg), const2(gam), const2(bet), const2(wsp), const2(bsp),
        ],
        out_specs=[tok(A_WIDTH), pos_minor(KV_WIDTH), pos_minor(KV_WIDTH), pos_minor(IDX_DIM), tok(KV_WIDTH),
                   tok(LANES), tr(KV_WIDTH), tr(B_WIDTH), tr(IDX_WIDTH), tr(SUBLANES)],
        out_shape=out_shapes,
        compiler_params=pltpu.CompilerParams(
            dimension_semantics=("arbitrary", "arbitrary"), vmem_limit_bytes=VMEM_LIMIT),
        name="front_prompt",
    )(x, mod, w_in, rt, mavg, gam, bet, wsp, bsp)


def _order_key(score):
    score = jnp.where(score == 0.0, 0.0, score)
    bits = pltpu.bitcast(score, I32)
    return bits ^ ((bits >> 31) & 0x7FFFFFFF)


def _select_threshold(count_ge, count_gt_eq_lt, shape, topk, idx_bits):
    def unresolved(carry):
        i, _, n_ge = carry
        return (i < 32) & (jnp.max((n_ge != topk).astype(F32)) > 0.0)

    def bit_step(carry):
        i, prefix, n_ge = carry
        for _ in range(BITS_PER_CHECK):
            bit = lax.shift_left(jnp.int32(1), 31 - i)
            cand_u = prefix | bit
            tot = count_ge(cand_u ^ INT_MIN)
            take = tot >= topk
            prefix = jnp.where(take, cand_u, prefix)
            n_ge = jnp.where(take, tot, n_ge)
            i = i + 1
        return i, prefix, n_ge

    never = jnp.full(shape, 2 ** 31 - 1, I32)
    _, prefix, n_ge = lax.while_loop(unresolved, bit_step, (jnp.int32(0), jnp.zeros(shape, I32), never))
    thr = prefix ^ INT_MIN
    tied = (n_ge > topk) & (thr > INT_MIN)
    thr = jnp.maximum(thr, INT_MIN + 1)
    big = jnp.full(shape, 1 << idx_bits, I32)

    def resolve_ties(_):
        n_gt, _ = count_gt_eq_lt(thr, big)
        need = topk - n_gt

        def idx_step(i, cut):
            cand = cut | lax.shift_left(jnp.int32(1), idx_bits - 1 - i)
            _, n_eq = count_gt_eq_lt(thr, cand)
            return jnp.where(n_eq <= need, cand, cut)

        cut = lax.fori_loop(0, idx_bits, idx_step, jnp.zeros(shape, I32))
        return jnp.where(tied, cut, big)

    any_tied = jnp.max(tied.astype(F32)) > 0.0
    cut = lax.cond(any_tied, resolve_ties, lambda _: big, 0)
    return thr, cut


def _fold_rows(x):
    acc = x[0:SUBLANES]
    for r in range(1, x.shape[0] // SUBLANES):
        acc = acc + x[r * SUBLANES:(r + 1) * SUBLANES]
    return acc


def _col_total(cnt8):
    return jnp.sum(cnt8.astype(F32), axis=0, keepdims=True).astype(I32)


def _attn_prompt_kernel(qit_ref, wit_ref, qt_ref, kiwib_ref, kb_ref, vt_ref, o_ref,
                        key_ref, w4_ref, wq_ref, s0_ref, s1_ref, m_ref, l_ref, acc_ref, *, tq, topk, idx_bits):
    j = pl.program_id(1)
    kc = tq
    n_kc = j + 1

    @pl.when((pl.program_id(0) == 0) & (j == 0))
    def _():
        w4_ref[...] = jnp.zeros(w4_ref.shape, BF16)
        wq_ref[...] = jnp.zeros(wq_ref.shape, BF16)

    for h in range(IDX_HEADS):
        w4_ref[0:IDX_DIM, h * tq:(h + 1) * tq] = qit_ref[h * IDX_DIM:(h + 1) * IDX_DIM, :]
    for h in range(B_HEADS):
        n = h // B_GROUP
        wq_ref[h, n * HEAD_DIM:(n + 1) * HEAD_DIM, :] = qt_ref[h * HEAD_DIM:(h + 1) * HEAD_DIM, :]

    wit = wit_ref[...]
    krow = lax.broadcasted_iota(I32, (kc, tq), 0)
    qpos = j * tq + lax.broadcasted_iota(I32, (kc, tq), 1)

    def score_chunk(c, _):
        k0 = pl.multiple_of(c * kc, kc)
        s = _dot(kiwib_ref[pl.ds(k0, kc), :], w4_ref[...])
        tot = jnp.maximum(s[:, 0:tq], 0.0) * wit[0:1, :]
        for h in range(1, IDX_HEADS):
            tot = tot + jnp.maximum(s[:, h * tq:(h + 1) * tq], 0.0) * wit[h:h + 1, :]
        key_ref[c] = jnp.where(k0 + krow <= qpos, _order_key(tot), INT_MIN)
        return 0

    lax.fori_loop(0, n_kc, score_chunk, 0)

    def count_ge(cand):
        def body(c, cnt):
            return cnt + _fold_rows(jnp.where(key_ref[c] >= cand, 1, 0))
        return _col_total(lax.fori_loop(0, n_kc, body, jnp.zeros((SUBLANES, tq), I32)))

    def count_gt_eq_lt(thr, pos):
        def body(c, carry):
            n_gt, n_eq = carry
            key = key_ref[c]
            n_gt = n_gt + _fold_rows(jnp.where(key > thr, 1, 0))
            n_eq = n_eq + _fold_rows(jnp.where((key == thr) & (c * kc + krow < pos), 1, 0))
            return n_gt, n_eq
        z = jnp.zeros((SUBLANES, tq), I32)
        n_gt, n_eq = lax.fori_loop(0, n_kc, body, (z, z))
        return _col_total(n_gt), _col_total(n_eq)

    thr, cut = _select_threshold(count_ge, count_gt_eq_lt, (1, tq), topk, idx_bits)

    m_ref[...] = jnp.full(m_ref.shape, jnp.finfo(F32).min, F32)
    l_ref[...] = jnp.zeros(l_ref.shape, F32)
    acc_ref[...] = jnp.zeros(acc_ref.shape, F32)

    def key_block(c):
        return kb_ref[pl.ds(pl.multiple_of(c * kc, kc), kc), :]

    kblk0 = key_block(0)
    for h in range(B_HEADS):
        s0_ref[h] = _dot(kblk0, wq_ref[h])

    def stage(c, open_bias, src, dst, c_next):
        key = key_ref[c]
        sel = (key > thr) | ((key == thr) & (c * kc + krow < cut))
        bias = jnp.where(sel, open_bias, -jnp.inf)
        kblk_next = key_block(c_next)
        vtc = vt_ref[c]
        for h in range(B_HEADS):
            n = h // B_GROUP
            dst[h] = _dot(kblk_next, wq_ref[h])
            s = src[h] + bias
            m_old = m_ref[h]
            m_new = jnp.maximum(m_old, jnp.max(s, axis=0, keepdims=True))
            alpha = jnp.exp(m_old - m_new)
            p = jnp.exp(s - m_new)
            l_ref[h] = alpha * l_ref[h] + jnp.sum(p, axis=0, keepdims=True)
            acc_ref[h] = alpha * acc_ref[h] + _dot(vtc[n * HEAD_DIM:(n + 1) * HEAD_DIM, :], p.astype(BF16))
            m_ref[h] = m_new

    last = n_kc - 1

    def attend_pair(i, _):
        c0 = 2 * i
        c1 = jnp.minimum(c0 + 1, last)
        stage(c0, 0.0, s0_ref, s1_ref, c1)
        stage(c1, jnp.where(c0 + 1 <= last, 0.0, -jnp.inf), s1_ref, s0_ref, jnp.minimum(c0 + 2, last))
        return 0

    lax.fori_loop(0, (n_kc + 1) // 2, attend_pair, 0)
    out_t = jnp.concatenate([acc_ref[h] / l_ref[h] for h in range(B_HEADS)], axis=0)
    o_ref[...] = jnp.transpose(out_t).astype(BF16)


def _attn_prompt(qit, wit, qt, kiwib, kb, vt):
    b, nc, _, tq = qt.shape
    t = nc * tq
    topk = min(TOPK_MAX, t // 4)
    idx_bits = max(1, (t - 1).bit_length())
    tr = lambda r: pl.BlockSpec((None, None, r, tq), lambda i, j: (i, j, 0, 0))
    full = lambda w: pl.BlockSpec((None, t, w), lambda i, j: (i, 0, 0))
    kernel = functools.partial(_attn_prompt_kernel, tq=tq, topk=topk, idx_bits=idx_bits)
    return pl.pallas_call(
        kernel,
        grid=(b, nc),
        in_specs=[tr(IDX_WIDTH), tr(SUBLANES), tr(B_WIDTH), full(LANES), full(KV_WIDTH),
                  pl.BlockSpec((None, nc, KV_WIDTH, tq), lambda i, j: (i, 0, 0, 0))],
        out_specs=pl.BlockSpec((None, tq, B_WIDTH), lambda i, j: (i, j, 0)),
        out_shape=jax.ShapeDtypeStruct((b, t, B_WIDTH), BF16),
        scratch_shapes=[
            pltpu.VMEM((nc, tq, tq), I32),
            pltpu.VMEM((LANES, IDX_HEADS * tq), BF16),
            pltpu.VMEM((B_HEADS, KV_WIDTH, tq), BF16),
            pltpu.VMEM((B_HEADS, tq, tq), F32),
            pltpu.VMEM((B_HEADS, tq, tq), F32),
            pltpu.VMEM((B_HEADS, 1, tq), F32),
            pltpu.VMEM((B_HEADS, 1, tq), F32),
            pltpu.VMEM((B_HEADS, HEAD_DIM, tq), F32),
        ],
        compiler_params=pltpu.CompilerParams(
            dimension_semantics=("arbitrary", "arbitrary"), vmem_limit_bytes=VMEM_LIMIT),
        name="attn_prompt",
    )(qit, wit, qt, kiwib, kb, vt)


def _rot_tables(pos):
    r = pos.shape[0]
    inv_freq = np.float32(ROPE_THETA) ** (-np.arange(ROT_HALF, dtype=np.float32) * np.float32(2.0) / np.float32(ROT_DIM))
    ang = pos.astype(np.float32)[:, None] * inv_freq[None, :]
    cos, sin = np.cos(ang), np.sin(ang)
    rest = HEAD_DIM - ROT_DIM
    c64 = np.concatenate([cos, cos, np.ones((r, rest), np.float32)], axis=1)
    lo64 = np.concatenate([-sin, np.zeros((r, HEAD_DIM - ROT_HALF), np.float32)], axis=1)
    hi64 = np.concatenate([np.zeros((r, ROT_HALF), np.float32), sin, np.zeros((r, rest), np.float32)], axis=1)
    return jnp.asarray(np.concatenate([c64, c64, lo64, lo64, hi64, hi64], axis=1).astype(np.float32))


def _head_avg_matrix():
    return jnp.kron(jnp.eye(256 // HEAD_DIM, dtype=F32), jnp.full((HEAD_DIM, HEAD_DIM), 1.0 / HEAD_DIM, F32)).astype(BF16)


def _pad_w_in(w_in):
    return jnp.pad(w_in, ((0, 0), (0, IN_PAD - IN_WIDTH))).astype(BF16)


def _front_sample_kernel(x_ref, shift_ref, scale_ref, w_ref, rt_ref, mavg_ref, gam_ref, bet_ref, wl_ref, bl_ref,
                         a_ref, q_ref, k_ref, v_ref, qi_ref, kiwi_ref, vg_ref, *, ts, db):
    one_scale = 1.0 + scale_ref[...]
    shift = shift_ref[...]
    h = jnp.concatenate([x_ref[t * db:(t + 1) * db, :] * one_scale + shift for t in range(ts)], axis=0).astype(BF16)
    u, vg, q, k, v, qi, kiwi = _project(h, w_ref, rt_ref[...], mavg_ref[...], gam_ref[...], bet_ref[...])
    q_ref[...] = q.astype(BF16)
    k_ref[...] = k
    v_ref[...] = v
    qi_ref[...] = qi.astype(BF16)
    kiwi_ref[...] = kiwi
    vg_ref[...] = vg
    for t in range(ts):
        s = bl_ref[t:t + 1, :]
        for src in range(t + 1):
            s = s + wl_ref[t * ts + src:t * ts + src + 1, :] * vg[src * db:(src + 1) * db, :]
        a_ref[t * db:(t + 1) * db, :] = (u[t * db:(t + 1) * db, :] * s).astype(BF16)


def _front_sample(x_tm, mod_s, w_in, rt, mavg, gam, bet, wl, bl, ts, db):
    r = ts * db
    whole = lambda a: pl.BlockSpec(a.shape, lambda i: (0,) * a.ndim)
    out = lambda w, dt: jax.ShapeDtypeStruct((r, w), dt)
    outs = (out(A_WIDTH, BF16), out(B_WIDTH, BF16), out(KV_WIDTH, F32), out(KV_WIDTH, F32),
            out(IDX_WIDTH, BF16), out(LANES, F32), out(A_WIDTH, F32))
    return pl.pallas_call(
        functools.partial(_front_sample_kernel, ts=ts, db=db),
        grid=(1,),
        in_specs=[
            whole(x_tm),
            pl.BlockSpec((db, D_MODEL), lambda i: (0, 0)),
            pl.BlockSpec((db, D_MODEL), lambda i: (0, 1)),
            whole(w_in), whole(rt), whole(mavg), whole(gam), whole(bet), whole(wl), whole(bl),
        ],
        out_specs=[pl.BlockSpec((r, s.shape[1]), lambda i: (0, 0)) for s in outs],
        out_shape=outs,
        compiler_params=pltpu.CompilerParams(dimension_semantics=("arbitrary",), vmem_limit_bytes=VMEM_LIMIT),
        name="front_sample",
    )(x_tm, mod_s, mod_s, w_in, rt, mavg, gam, bet, wl, bl)


def _idx_sample_kernel(pt_ref, qi4_ref, wcol_ref, qpos_ref, kinew_ref, kidx_hbm, bias_ref,
                       ki_buf, key_ref, sem, *, ga, ts, n_pages, page, lpad, topk, idx_bits):
    i = pl.program_id(0)
    n_steps = pl.num_programs(0)
    past = n_pages * page
    rows = ga * ts
    slot = i % 2

    def page_copy(step, to_slot, g, p):
        phys = pt_ref[step * ga + g, p]
        return pltpu.make_async_copy(kidx_hbm.at[phys], ki_buf.at[to_slot, g, :, pl.ds(p * page, page)],
                                     sem.at[to_slot])

    def start_all(step, to_slot):
        for g in range(ga):
            for p in range(n_pages):
                page_copy(step, to_slot, g, p).start()

    @pl.when(i == 0)
    def _():
        start_all(0, 0)

    @pl.when(i + 1 < n_steps)
    def _():
        start_all(i + 1, 1 - slot)

    ki_buf[slot, :, :, pl.ds(past, LANES)] = kinew_ref[...]
    for g in range(ga):
        for p in range(n_pages):
            page_copy(i, slot, g, p).wait()

    kpos = lax.broadcasted_iota(I32, (ts, lpad), 1)
    for g in range(ga):
        s = _dot(qi4_ref[g], ki_buf[slot, g].astype(BF16))
        r = jnp.maximum(s, 0.0) * wcol_ref[g]
        tot = r[0:ts]
        for h in range(1, IDX_HEADS):
            tot = tot + r[h * ts:(h + 1) * ts]
        adm = (kpos <= qpos_ref[g * ts:(g + 1) * ts, :]) & (kpos < past + ts)
        key_ref[g * ts:(g + 1) * ts, :] = jnp.where(adm, _order_key(tot), INT_MIN)

    sub = lpad // LANES
    lane128 = lax.broadcasted_iota(I32, (rows, LANES), 1)

    def row_total(cnt):
        tot = jnp.sum(cnt.astype(F32), axis=1, keepdims=True)
        return jnp.broadcast_to(tot, cnt.shape).astype(I32)

    def count_ge(cand):
        cnt = jnp.zeros((rows, LANES), I32)
        for s_ in range(sub):
            cnt = cnt + jnp.where(key_ref[:, s_ * LANES:(s_ + 1) * LANES] >= cand, 1, 0)
        return row_total(cnt)

    def count_gt_eq_lt(thr, pos):
        n_gt = jnp.zeros((rows, LANES), I32)
        n_eq = jnp.zeros((rows, LANES), I32)
        for s_ in range(sub):
            kk = key_ref[:, s_ * LANES:(s_ + 1) * LANES]
            n_gt = n_gt + jnp.where(kk > thr, 1, 0)
            n_eq = n_eq + jnp.where((kk == thr) & (s_ * LANES + lane128 < pos), 1, 0)
        return row_total(n_gt), row_total(n_eq)

    thr, cut = _select_threshold(count_ge, count_gt_eq_lt, (rows, LANES), topk, idx_bits)
    for s_ in range(sub):
        kk = key_ref[:, s_ * LANES:(s_ + 1) * LANES]
        sel = (kk > thr) | ((kk == thr) & (s_ * LANES + lane128 < cut))
        bias = jnp.where(sel, 0.0, -jnp.inf)
        for g in range(ga):
            bias_ref[g, :, s_ * LANES:(s_ + 1) * LANES] = bias[g * ts:(g + 1) * ts]


def _idx_sample(page_table, qi4, wcol, qpos, kinew_t, kidx_t, ts, ga):
    db = qi4.shape[0]
    n_pages = page_table.shape[1]
    page = kidx_t.shape[2]
    past = n_pages * page
    lpad = past + LANES
    topk = min(TOPK_MAX, (past + ts) // 4)
    idx_bits = max(1, (lpad - 1).bit_length())
    kernel = functools.partial(_idx_sample_kernel, ga=ga, ts=ts, n_pages=n_pages, page=page, lpad=lpad,
                               topk=topk, idx_bits=idx_bits)
    return pl.pallas_call(
        kernel,
        grid_spec=pltpu.PrefetchScalarGridSpec(
            num_scalar_prefetch=1,
            grid=(db // ga,),
            in_specs=[pl.BlockSpec((ga, IDX_HEADS * ts, IDX_DIM), lambda i, pt: (i, 0, 0)),
                      pl.BlockSpec((ga, IDX_HEADS * ts, 1), lambda i, pt: (i, 0, 0)),
                      pl.BlockSpec((ga * ts, 1), lambda i, pt: (i, 0)),
                      pl.BlockSpec((ga, IDX_DIM, LANES), lambda i, pt: (i, 0, 0)),
                      pl.BlockSpec(memory_space=pl.ANY)],
            out_specs=pl.BlockSpec((ga, ts, lpad), lambda i, pt: (i, 0, 0)),
            scratch_shapes=[
                pltpu.VMEM((2, ga, IDX_DIM, lpad), F32),
                pltpu.VMEM((ga * ts, lpad), I32),
                pltpu.SemaphoreType.DMA((2,)),
            ]),
        out_shape=jax.ShapeDtypeStruct((db, ts, lpad), F32),
        compiler_params=pltpu.CompilerParams(dimension_semantics=("arbitrary",), vmem_limit_bytes=VMEM_LIMIT),
        name="idx_sample",
    )(page_table, qi4, wcol, qpos, kinew_t, kidx_t)


def _attn_sample_kernel(pt_ref, q_ref, bias_ref, knew_ref, vnew_ref, k_hbm, v_hbm, o_ref,
                        k_buf, v_buf, sem, *, ts, n_pages, page):
    b = pl.program_id(0)
    nb = pl.num_programs(0)
    past = n_pages * page
    slot = b % 2

    def page_copies(seq, to_slot, p):
        phys = pt_ref[seq, p]
        dst = pl.ds(p * page, page)
        return (pltpu.make_async_copy(k_hbm.at[phys], k_buf.at[to_slot, :, :, dst], sem.at[0, to_slot]),
                pltpu.make_async_copy(v_hbm.at[phys], v_buf.at[to_slot, :, :, dst], sem.at[1, to_slot]))

    def start_all(seq, to_slot):
        for p in range(n_pages):
            for cp in page_copies(seq, to_slot, p):
                cp.start()

    @pl.when(b == 0)
    def _():
        start_all(0, 0)

    @pl.when(b + 1 < nb)
    def _():
        start_all(b + 1, 1 - slot)

    k_buf[slot, :, :, pl.ds(past, LANES)] = knew_ref[...]
    v_buf[slot, :, :, pl.ds(past, LANES)] = vnew_ref[...]
    for p in range(n_pages):
        for cp in page_copies(b, slot, p):
            cp.wait()

    q = q_ref[...]
    bias = bias_ref[...]
    bias2 = jnp.concatenate([bias] * B_GROUP, axis=0)
    outs = [None] * B_HEADS
    for n in range(B_KV_HEADS):
        kt = k_buf[slot, n].astype(BF16)
        vt = v_buf[slot, n].astype(BF16)
        qs = jnp.concatenate([q[:, (n * B_GROUP + g) * HEAD_DIM:(n * B_GROUP + g + 1) * HEAD_DIM]
                              for g in range(B_GROUP)], axis=0)
        sc = _dot(qs, kt) + bias2
        m = jnp.max(sc, axis=1, keepdims=True)
        p_ = jnp.exp(sc - m)
        o = _dot_nt(p_.astype(BF16), vt) / jnp.sum(p_, axis=1, keepdims=True)
        for g in range(B_GROUP):
            outs[n * B_GROUP + g] = o[g * ts:(g + 1) * ts]
    o_ref[...] = jnp.concatenate(outs, axis=1).astype(BF16)


def _attn_sample(page_table, q, bias, knew_t, vnew_t, k_t, v_t):
    db, ts, _ = q.shape
    n_pages = page_table.shape[1]
    page = k_t.shape[3]
    lpad = bias.shape[2]
    seq3 = lambda a: pl.BlockSpec((None,) + a.shape[1:], lambda b, pt: (b,) + (0,) * (a.ndim - 1))
    anyspec = pl.BlockSpec(memory_space=pl.ANY)
    kernel = functools.partial(_attn_sample_kernel, ts=ts, n_pages=n_pages, page=page)
    return pl.pallas_call(
        kernel,
        grid_spec=pltpu.PrefetchScalarGridSpec(
            num_scalar_prefetch=1,
            grid=(db,),
            in_specs=[seq3(q), seq3(bias), seq3(knew_t), seq3(vnew_t), anyspec, anyspec],
            out_specs=pl.BlockSpec((None, ts, B_WIDTH), lambda b, pt: (b, 0, 0)),
            scratch_shapes=[
                pltpu.VMEM((2, B_KV_HEADS, HEAD_DIM, lpad), F32),
                pltpu.VMEM((2, B_KV_HEADS, HEAD_DIM, lpad), F32),
                pltpu.SemaphoreType.DMA((2, 2)),
            ]),
        out_shape=jax.ShapeDtypeStruct((db, ts, B_WIDTH), BF16),
        compiler_params=pltpu.CompilerParams(dimension_semantics=("arbitrary",), vmem_limit_bytes=VMEM_LIMIT),
        name="attn_sample",
    )(page_table, q, bias, knew_t, vnew_t, k_t, v_t)


def _layer_norm_rows(y, gam, bet):
    mu = jnp.mean(y, axis=1, keepdims=True)
    d = y - mu
    var = jnp.mean(d * d, axis=1, keepdims=True)
    return d * lax.rsqrt(var + LN_EPS) * gam + bet


def _store_row_tiles(ref, val):
    r = val.shape[0]
    for c in range(ROW_CHUNKS):
        ref[pl.ds(c, r, stride=ROW_CHUNKS), :] = val[:, c * LANES:(c + 1) * LANES]


def _load_row_tiles(ref, r):
    return jnp.concatenate([ref[pl.ds(c, r, stride=ROW_CHUNKS), :] for c in range(ROW_CHUNKS)], axis=1)


ROUTER_ROWS = SUBLANES + N_EXPERTS


def _route(logits_t):
    r = logits_t.shape[1]
    far = float(LANES)

    def softmax_rows(x):
        e = jnp.exp(x - jnp.max(x, axis=0, keepdims=True))
        return e / jnp.sum(e, axis=0, keepdims=True)

    def first_max(p):
        rows = lax.broadcasted_iota(I32, p.shape, 0).astype(F32)
        best = jnp.max(p, axis=0, keepdims=True)
        return best, jnp.min(jnp.where(p == best, rows, far), axis=0, keepdims=True), rows

    g_w, g_sel, _ = first_max(softmax_rows(logits_t[0:N_GROUPS]))
    el = jnp.zeros((EXPERTS_PER_GROUP, r), F32)
    for g in range(N_GROUPS):
        lo = SUBLANES + g * EXPERTS_PER_GROUP
        el = jnp.where(g_sel == float(g), logits_t[lo:lo + EXPERTS_PER_GROUP], el)
    ep = softmax_rows(el)
    p1, i1, rows = first_max(ep)
    p2, i2, _ = first_max(jnp.where(rows == i1, -1.0, ep))
    denom = p1 + p2
    base = g_sel * float(EXPERTS_PER_GROUP)
    return jnp.concatenate([base + i1, base + i2, g_w * p1 / denom, g_w * p2 / denom,
                            jnp.zeros((SUBLANES - 4, r), F32)], axis=0)


def _split_bf16(x):
    hi = x.astype(BF16)
    return hi, (x - hi.astype(F32)).astype(BF16)


def _mid_kernel(x_ref, a_ref, b_ref, g1_ref, sh2_ref, sc2_ref, wo_ref, gam_ref, bet_ref, wr_ref,
                x1_ref, h2_ref, route_ref, *, alpha):
    mixed = _dot(a_ref[...], wo_ref[0:A_WIDTH, :]) + _dot(b_ref[...], wo_ref[A_WIDTH:A_WIDTH + B_WIDTH, :])
    x1 = _layer_norm_rows(alpha * x_ref[...] + g1_ref[...] * mixed, gam_ref[...], bet_ref[...])
    x1_ref[...] = x1
    h2 = x1 * (1.0 + sc2_ref[...]) + sh2_ref[...]
    _store_row_tiles(h2_ref, h2)
    w_hi, w_lo = _split_bf16(wr_ref[...])
    h_hi, h_lo = _split_bf16(h2)
    logits_t = _dot_nt(w_hi, h_hi) + _dot_nt(w_hi, h_lo) + _dot_nt(w_lo, h_hi)
    route_ref[...] = _route(logits_t)


def _mod_spec(mod3, comp, tiles_per_group):
    rm = mod3.shape[1]
    return pl.BlockSpec((None, rm, D_MODEL), lambda i: (i // tiles_per_group, 0, comp))


def _mid(x, a, b, mod3, tiles_per_group, tile, w_out, gam, bet, w_router, alpha):
    n = x.shape[0]
    rows = lambda w: pl.BlockSpec((tile, w), lambda i: (i, 0))
    whole = lambda arr: pl.BlockSpec(arr.shape, lambda i: (0,) * arr.ndim)
    return pl.pallas_call(
        functools.partial(_mid_kernel, alpha=alpha),
        grid=(n // tile,),
        in_specs=[rows(D_MODEL), rows(A_WIDTH), rows(B_WIDTH),
                  _mod_spec(mod3, 2, tiles_per_group), _mod_spec(mod3, 3, tiles_per_group),
                  _mod_spec(mod3, 4, tiles_per_group),
                  whole(w_out), whole(gam), whole(bet), whole(w_router)],
        out_specs=[rows(D_MODEL), pl.BlockSpec((tile * ROW_CHUNKS, LANES), lambda i: (i, 0)),
                   pl.BlockSpec((SUBLANES, tile), lambda i: (0, i))],
        out_shape=(jax.ShapeDtypeStruct((n, D_MODEL), F32),
                   jax.ShapeDtypeStruct((n * ROW_CHUNKS, LANES), F32),
                   jax.ShapeDtypeStruct((SUBLANES, n), F32)),
        compiler_params=pltpu.CompilerParams(dimension_semantics=("arbitrary",), vmem_limit_bytes=VMEM_LIMIT),
        name="mid",
    )(x, a, b, mod3, mod3, mod3, w_out, gam, bet, w_router)


META_CNT, META_START, META_END, META_NACT = 0, 1, 2, 3


def _plan_kernel(route_ref, pos_ref, meta_ref, te_ref, carry_ref, starts_ref, *, nt, tm):
    phase = pl.program_id(0)
    i = pl.program_id(1)

    @pl.when(i == 0)
    def _():
        carry_ref[...] = jnp.zeros(carry_ref.shape, F32)

    eid = route_ref[0:2, :].astype(I32)
    e_iota = lax.broadcasted_iota(I32, (N_EXPERTS, nt), 0)
    hit0 = eid[0:1, :] == e_iota
    hit1 = eid[1:2, :] == e_iota
    onehot = jnp.where(hit0 | hit1, 1.0, 0.0)

    @pl.when(phase == 0)
    def _():
        pos_ref[...] = jnp.zeros(pos_ref.shape, I32)
        carry_ref[...] = carry_ref[...] + jnp.sum(onehot, axis=1, keepdims=True)

        @pl.when(i == pl.num_programs(1) - 1)
        def _():
            cnt = carry_ref[...]
            tiles = jnp.floor((cnt + float(tm - 1)) * (1.0 / tm))
            r = lax.broadcasted_iota(I32, (N_EXPERTS, N_EXPERTS), 0)
            c = lax.broadcasted_iota(I32, (N_EXPERTS, N_EXPERTS), 1)
            lower = jnp.where(r > c, 1.0, 0.0).astype(BF16)
            tiles_before = _dot(lower, jnp.broadcast_to(tiles, (N_EXPERTS, LANES)).astype(BF16))[:, 0:1]
            tiles_end = tiles_before + tiles
            starts_ref[...] = tiles_before * float(tm)

            diag = (lax.broadcasted_iota(I32, (N_EXPERTS, LANES), 0)
                    == lax.broadcasted_iota(I32, (N_EXPERTS, LANES), 1))

            def as_row(col):
                return jnp.sum(jnp.where(diag, jnp.broadcast_to(col, (N_EXPERTS, LANES)), 0.0),
                               axis=0, keepdims=True)

            nact = jnp.broadcast_to(jnp.max(tiles_end, axis=0, keepdims=True), (1, LANES))
            meta_ref[...] = jnp.concatenate(
                [as_row(cnt), as_row(tiles_before * float(tm)), as_row(tiles_end * float(tm)), nact,
                 jnp.zeros((SUBLANES - 4, LANES), F32)], axis=0).astype(I32)
            tile_i = lax.broadcasted_iota(I32, (N_EXPERTS, te_ref.shape[1]), 1).astype(F32)
            owner = jnp.sum(jnp.where(tiles_end <= tile_i, 1.0, 0.0), axis=0, keepdims=True)
            te_ref[...] = jnp.minimum(owner, float(N_EXPERTS - 1)).astype(I32)

    @pl.when(phase == 1)
    def _():
        upper = (lax.broadcasted_iota(I32, (nt, nt), 0) <= lax.broadcasted_iota(I32, (nt, nt), 1))
        incl = _dot(onehot.astype(BF16), jnp.where(upper, 1.0, 0.0).astype(BF16))
        slot = starts_ref[...] + carry_ref[...] + incl - 1.0
        pos_ref[0:1, :] = jnp.sum(jnp.where(hit0, slot, 0.0), axis=0, keepdims=True).astype(I32)
        pos_ref[1:2, :] = jnp.sum(jnp.where(hit1, slot, 0.0), axis=0, keepdims=True).astype(I32)
        carry_ref[...] = carry_ref[...] + jnp.sum(onehot, axis=1, keepdims=True)


def _plan(route_t, nt, tm):
    n = route_t.shape[1]
    n_tiles = (2 * n) // tm + N_EXPERTS
    te_width = pl.cdiv(n_tiles, LANES) * LANES
    pos, meta, te = pl.pallas_call(
        functools.partial(_plan_kernel, nt=nt, tm=tm),
        grid=(2, n // nt),
        in_specs=[pl.BlockSpec((SUBLANES, nt), lambda p, i: (0, i))],
        out_specs=[pl.BlockSpec((2, nt), lambda p, i: (0, i * p)),
                   pl.BlockSpec((SUBLANES, LANES), lambda p, i: (0, 0)),
                   pl.BlockSpec((1, te_width), lambda p, i: (0, 0))],
        out_shape=(jax.ShapeDtypeStruct((2, n), I32), jax.ShapeDtypeStruct((SUBLANES, LANES), I32),
                   jax.ShapeDtypeStruct((1, te_width), I32)),
        scratch_shapes=[pltpu.VMEM((N_EXPERTS, 1), F32), pltpu.VMEM((N_EXPERTS, 1), F32)],
        compiler_params=pltpu.CompilerParams(dimension_semantics=("arbitrary", "arbitrary")),
        name="plan",
    )(route_t)
    return pos, meta, te, n_tiles


def _row_copy(src, src_row, dst, dst_row, sem):
    return pltpu.make_async_copy(src.at[pl.ds(src_row * ROW_CHUNKS, ROW_CHUNKS)],
                                 dst.at[pl.ds(dst_row * ROW_CHUNKS, ROW_CHUNKS)], sem)


def _dispatch_kernel(meta_ref, pos_ref, hp_ref, hs_ref, xs_out, zero_ref, sem, *, tt, blocks_p, tm, n_tiles):
    i = pl.program_id(0)

    def scatter(h_ref):
        def issue(j, _):
            for k in range(2):
                _row_copy(h_ref, j, xs_out, pos_ref[k, j], sem.at[k]).start(priority=k)
            return 0

        def drain(j, _):
            for k in range(2):
                _row_copy(h_ref, j, xs_out, pos_ref[k, j], sem.at[k]).wait()
            return 0

        lax.fori_loop(0, tt, issue, 0)
        lax.fori_loop(0, tt, drain, 0)

    @pl.when(i < blocks_p)
    def _():
        scatter(hp_ref)

    @pl.when(i >= blocks_p)
    def _():
        scatter(hs_ref)

    @pl.when(i == pl.num_programs(0) - 1)
    def _():
        zero_ref[...] = jnp.zeros(zero_ref.shape, F32)

        def zero_rows(first_row, n_rows):
            return pltpu.make_async_copy(zero_ref.at[pl.ds(0, n_rows * ROW_CHUNKS)],
                                         xs_out.at[pl.ds(first_row * ROW_CHUNKS, n_rows * ROW_CHUNKS)], sem.at[0])

        def start_row(r, c):
            zero_rows(r, 1).start()
            return c

        def wait_row(r, c):
            zero_rows(r, 1).wait()
            return c

        def per_expert(e, c):
            lo = meta_ref[META_START, e] + meta_ref[META_CNT, e]
            hi = meta_ref[META_END, e]
            lax.fori_loop(lo, hi, start_row, 0)
            lax.fori_loop(lo, hi, wait_row, 0)
            return c

        def start_tile(t, c):
            zero_rows(t * tm, tm).start()
            return c

        def wait_tile(t, c):
            zero_rows(t * tm, tm).wait()
            return c

        lax.fori_loop(0, N_EXPERTS, per_expert, 0)
        nact = meta_ref[META_NACT, 0]
        lax.fori_loop(nact, n_tiles, start_tile, 0)
        lax.fori_loop(nact, n_tiles, wait_tile, 0)


def _dispatch(meta, pos, h2_p, h2_s, tt, tm, n_tiles):
    blocks_p = h2_p.shape[0] // (tt * ROW_CHUNKS)
    blocks_s = h2_s.shape[0] // (tt * ROW_CHUNKS)
    return pl.pallas_call(
        functools.partial(_dispatch_kernel, tt=tt, blocks_p=blocks_p, tm=tm, n_tiles=n_tiles),
        grid_spec=pltpu.PrefetchScalarGridSpec(
            num_scalar_prefetch=1,
            grid=(blocks_p + blocks_s,),
            in_specs=[pl.BlockSpec((2, tt), lambda i, m: (0, i), memory_space=pltpu.SMEM),
                      pl.BlockSpec((tt * ROW_CHUNKS, LANES), lambda i, m: (jnp.minimum(i, blocks_p - 1), 0)),
                      pl.BlockSpec((tt * ROW_CHUNKS, LANES), lambda i, m: (jnp.maximum(i - blocks_p, 0), 0))],
            out_specs=pl.BlockSpec(memory_space=pl.ANY),
            scratch_shapes=[pltpu.VMEM((tm * ROW_CHUNKS, LANES), F32), pltpu.SemaphoreType.DMA((2,))]),
        out_shape=jax.ShapeDtypeStruct((n_tiles * tm * ROW_CHUNKS, LANES), F32),
        compiler_params=pltpu.CompilerParams(dimension_semantics=("arbitrary",)),
        name="dispatch",
    )(meta, pos, h2_p, h2_s)


def _experts_kernel(te_ref, meta_ref, xs_ref, wg_ref, wu_ref, wd_ref, ys_ref, wg_b, wu_b, wd_b, *, tm):
    i = pl.program_id(0)
    active = i < meta_ref[META_NACT, 0]
    fresh = (i == 0) | (te_ref[0, i] != te_ref[0, jnp.maximum(i - 1, 0)])

    @pl.when(active & fresh)
    def _():
        wg_b[...] = wg_ref[...].astype(BF16)
        wu_b[...] = wu_ref[...].astype(BF16)
        wd_b[...] = wd_ref[...].astype(BF16)

    @pl.when(active)
    def _():
        x = _load_row_tiles(xs_ref, tm).astype(BF16)
        hid = (jax.nn.silu(_dot(x, wg_b[...])) * _dot(x, wu_b[...])).astype(BF16)
        _store_row_tiles(ys_ref, _dot(hid, wd_b[...]))

    @pl.when(jnp.logical_not(active))
    def _():
        ys_ref[...] = jnp.zeros(ys_ref.shape, F32)


def _experts(te, meta, xs, w_gate, w_up, w_down, tm, n_tiles):
    last_active = lambda i, m: jnp.minimum(i, m[META_NACT, 0] - 1)
    tile = pl.BlockSpec((tm * ROW_CHUNKS, LANES), lambda i, te, m: (last_active(i, m), 0))
    w_in_spec = pl.BlockSpec((None, D_MODEL, D_EXPERT), lambda i, te, m: (te[0, last_active(i, m)], 0, 0))
    w_out_spec = pl.BlockSpec((None, D_EXPERT, D_MODEL), lambda i, te, m: (te[0, last_active(i, m)], 0, 0))
    return pl.pallas_call(
        functools.partial(_experts_kernel, tm=tm),
        grid_spec=pltpu.PrefetchScalarGridSpec(
            num_scalar_prefetch=2,
            grid=(n_tiles,),
            in_specs=[tile, w_in_spec, w_in_spec, w_out_spec],
            out_specs=pl.BlockSpec((tm * ROW_CHUNKS, LANES), lambda i, te, m: (i, 0)),
            scratch_shapes=[pltpu.VMEM((D_MODEL, D_EXPERT), BF16), pltpu.VMEM((D_MODEL, D_EXPERT), BF16),
                            pltpu.VMEM((D_EXPERT, D_MODEL), BF16)]),
        out_shape=jax.ShapeDtypeStruct(xs.shape, F32),
        compiler_params=pltpu.CompilerParams(dimension_semantics=("arbitrary",), vmem_limit_bytes=VMEM_LIMIT),
        name="experts",
    )(te, meta, xs, w_gate, w_up, w_down)


def _combine_kernel(pos_ref, route_ref, x1_ref, g2_ref, gam_ref, bet_ref, ys_hbm, y_ref, buf0, buf1, sem,
                    *, tt, alpha):
    bufs = (buf0, buf1)

    def issue(j, _):
        for k in range(2):
            _row_copy(ys_hbm, pos_ref[k, j], bufs[k], j, sem.at[k]).start(priority=k)
        return 0

    def drain(j, _):
        for k in range(2):
            _row_copy(ys_hbm, pos_ref[k, j], bufs[k], j, sem.at[k]).wait()
        return 0

    lax.fori_loop(0, tt, issue, 0)
    lax.fori_loop(0, tt, drain, 0)
    route = route_ref[...]
    eye = lax.broadcasted_iota(I32, (tt, tt), 0) == lax.broadcasted_iota(I32, (tt, tt), 1)
    as_col = lambda row: jnp.sum(jnp.where(eye, jnp.broadcast_to(row, (tt, tt)), 0.0), axis=1, keepdims=True)
    f = as_col(route[2:3, :]) * _load_row_tiles(buf0, tt) + as_col(route[3:4, :]) * _load_row_tiles(buf1, tt)
    y_ref[...] = _layer_norm_rows(alpha * x1_ref[...] + g2_ref[...] * f, gam_ref[...], bet_ref[...])


def _combine(pos, route_t, col_block0, x1, mod3, tiles_per_group, tt, gam, bet, ys, alpha):
    n = x1.shape[0]
    rows = lambda w: pl.BlockSpec((tt, w), lambda i: (i, 0))
    whole = lambda arr: pl.BlockSpec(arr.shape, lambda i: (0,) * arr.ndim)
    return pl.pallas_call(
        functools.partial(_combine_kernel, tt=tt, alpha=alpha),
        grid=(n // tt,),
        in_specs=[pl.BlockSpec((2, tt), lambda i: (0, i + col_block0), memory_space=pltpu.SMEM),
                  pl.BlockSpec((SUBLANES, tt), lambda i: (0, i + col_block0)),
                  rows(D_MODEL), _mod_spec(mod3, 5, tiles_per_group), whole(gam), whole(bet),
                  pl.BlockSpec(memory_space=pl.ANY)],
        out_specs=rows(D_MODEL),
        out_shape=jax.ShapeDtypeStruct((n, D_MODEL), F32),
        scratch_shapes=[pltpu.VMEM((tt * ROW_CHUNKS, LANES), F32), pltpu.VMEM((tt * ROW_CHUNKS, LANES), F32),
                        pltpu.SemaphoreType.DMA((2,))],
        compiler_params=pltpu.CompilerParams(dimension_semantics=("arbitrary",), vmem_limit_bytes=VMEM_LIMIT),
        name="combine",
    )(pos, route_t, x1, mod3, gam, bet, ys)


def kernel(x_prompt, x_sample, cache_k, cache_v, cache_kidx, page_table, c_prompt, c_sample, w_ada, b_ada, w_in,
           a_ln_g, a_ln_b, w_spatial, b_spatial, w_out, ln1_g, ln1_b, w_group_router, w_expert_router, w_gate,
           w_up, w_down, ln2_g, ln2_b):
    depth = w_ada.shape[0]
    assert depth == 1, "one trunk layer"
    alpha = (2.0 * depth) ** 0.25
    bsz, seq, d = x_prompt.shape
    db, ts, _ = x_sample.shape
    n_pages = page_table.shape[1]
    page = cache_k.shape[2]
    past = n_pages * page
    tile_p = min(PROMPT_TILE, seq)
    assert d == D_MODEL and seq % tile_p == 0 and tile_p % CHUNK == 0 and ts <= SUBLANES and page == LANES
    l = 0

    n_c = bsz + db
    n_c_pad = pl.cdiv(n_c, SUBLANES) * SUBLANES
    c_all = jnp.pad(jnp.concatenate([c_prompt, c_sample], axis=0), ((0, n_c_pad - n_c), (0, 0)))
    mod = _ada(c_all, w_ada[l], b_ada[l][None, :])
    mod_p = mod[:bsz]
    mod_s = mod[bsz:n_c]

    w_in_b = _pad_w_in(w_in[l])
    w_out_b = w_out[l].astype(BF16)
    mavg = _head_avg_matrix()
    gam_a = a_ln_g[l].reshape(1, A_WIDTH)
    bet_a = a_ln_b[l].reshape(1, A_WIDTH)
    w_router = jnp.concatenate([w_group_router[l].T, jnp.zeros((SUBLANES - N_GROUPS, d), F32),
                                w_expert_router[l].reshape(d, N_EXPERTS).T], axis=0)
    ln1 = (ln1_g[l][None, :], ln1_b[l][None, :])
    ln2 = (ln2_g[l][None, :], ln2_b[l][None, :])

    wsp = w_spatial[l].reshape(A_HEADS // 2, 2, CHUNK, CHUNK).transpose(0, 2, 1, 3).reshape(A_HEADS // 2, CHUNK, 2 * CHUNK)
    bsp = jnp.repeat(b_spatial[l].T, HEAD_DIM, axis=1)
    a_p, kt_p, vt32_p, kit_p, kb_p, kiwib_p, vt_p, qt_p, qit_p, wit_p = _front_prompt(
        x_prompt, mod_p.reshape(bsz, 6, d), w_in_b, _rot_tables(np.arange(seq)), mavg, gam_a, bet_a, wsp, bsp, tile_p)
    b_p = _attn_prompt(qit_p, wit_p, qt_p, kiwib_p, kb_p, vt_p)
    n_p = bsz * seq
    x1_p, h2_p, route_p = _mid(x_prompt.reshape(n_p, d), a_p.reshape(n_p, A_WIDTH), b_p.reshape(n_p, B_WIDTH),
                               mod_p.reshape(bsz, 1, 6 * d), seq // tile_p, tile_p, w_out_b, *ln1, w_router, alpha)

    r_s = ts * db
    x_tm = x_sample.transpose(1, 0, 2).reshape(r_s, d)
    rt_s = _rot_tables(np.repeat(past + np.arange(ts), db))
    w_small = w_spatial[l][:, :ts, :ts]
    wl = jnp.repeat(w_small.transpose(1, 2, 0).reshape(ts * ts, A_HEADS), HEAD_DIM, axis=1)
    bl = jnp.repeat(b_spatial[l][:, :ts].T, HEAD_DIM, axis=1)
    a_s, q_s, k_s, v_s, qi_s, kiwi_s, vg_s = _front_sample(x_tm, mod_s, w_in_b, rt_s, mavg, gam_a, bet_a, wl, bl, ts, db)

    def seq_major(a):
        return a.reshape(ts, db, a.shape[-1]).transpose(1, 0, 2)

    def new_t(a, heads):
        a = a.reshape(ts, db, heads, HEAD_DIM).transpose(1, 2, 3, 0)
        return jnp.pad(a, ((0, 0), (0, 0), (0, 0), (0, LANES - ts)))

    ga = min(IDX_SEQS, db)
    assert db % ga == 0 and (ga * ts) % SUBLANES == 0
    qi4 = qi_s.reshape(ts, db, IDX_HEADS, IDX_DIM).transpose(1, 2, 0, 3).reshape(db, IDX_HEADS * ts, IDX_DIM)
    wcol = kiwi_s[:, WI_LANE:WI_LANE + IDX_HEADS].reshape(ts, db, IDX_HEADS).transpose(1, 2, 0).reshape(db, IDX_HEADS * ts, 1)
    qpos = jnp.tile(past + jnp.arange(ts, dtype=I32), db).reshape(db * ts, 1)
    kinew_t = new_t(kiwi_s[:, :IDX_DIM], 1)[:, 0]
    kidx_t = jnp.transpose(cache_kidx[l], (0, 2, 1))
    k_t = jnp.transpose(cache_k[l], (0, 2, 3, 1))
    v_t = jnp.transpose(cache_v[l], (0, 2, 3, 1))
    bias_s = _idx_sample(page_table, qi4, wcol, qpos, kinew_t, kidx_t, ts, ga)
    b_s = _attn_sample(page_table, seq_major(q_s), bias_s, new_t(k_s, B_KV_HEADS), new_t(v_s, B_KV_HEADS), k_t, v_t)
    b_s_tm = b_s.transpose(1, 0, 2).reshape(r_s, B_WIDTH)
    x1_s, h2_s, route_s = _mid(x_tm, a_s, b_s_tm, mod_s.reshape(1, db, 6 * d), ts, db, w_out_b, *ln1, w_router, alpha)

    n_all = n_p + r_s
    tok_p = min(TOKEN_TILE, seq)
    rank_tile = min(RANK_TILE, n_all)
    assert n_all % rank_tile == 0 and n_p % tok_p == 0 and r_s % tok_p == 0 and n_p % db == 0
    route_all = jnp.concatenate([route_p, route_s], axis=1)
    pos, meta, te, n_tiles = _plan(route_all, rank_tile, MOE_TILE)
    xs = _dispatch(meta, pos, h2_p, h2_s, tok_p, MOE_TILE, n_tiles)
    ys = _experts(te, meta, xs, w_gate[l], w_up[l], w_down[l], MOE_TILE, n_tiles)
    y_p = _combine(pos, route_all, 0, x1_p, mod_p.reshape(bsz, 1, 6 * d), seq // tok_p, tok_p, *ln2, ys, alpha)
    y_s_tm = _combine(pos, route_all, n_p // db, x1_s, mod_s.reshape(1, db, 6 * d), ts, db, *ln2, ys, alpha)
    y_s = y_s_tm.reshape(ts, db, d).transpose(1, 0, 2)

    kv5 = lambda a, n, t: a.reshape(1, n, t, B_KV_HEADS, HEAD_DIM)
    kv5_t = lambda a: a.reshape(1, bsz, B_KV_HEADS, HEAD_DIM, seq).transpose(0, 1, 4, 2, 3)
    return (y_p.reshape(bsz, seq, d), y_s,
            kv5_t(kt_p), kv5_t(vt32_p), kit_p.transpose(0, 2, 1)[None],
            kv5(seq_major(k_s), db, ts), kv5(seq_major(v_s), db, ts), seq_major(kiwi_s)[..., :IDX_DIM][None],
            seq_major(vg_s).reshape(1, db, ts, A_HEADS, HEAD_DIM))
```

```python
import functools

import jax
import jax.numpy as jnp
import numpy as np
from jax import lax
from jax.experimental import pallas as pl
from jax.experimental.pallas import tpu as pltpu

F32 = jnp.float32
BF16 = jnp.bfloat16
I32 = jnp.int32

D_MODEL = 1024
HEAD_DIM = 64
A_HEADS = 8
A_WIDTH = A_HEADS * HEAD_DIM
CHUNK = 128
B_HEADS = 8
B_KV_HEADS = 4
B_GROUP = B_HEADS // B_KV_HEADS
B_WIDTH = B_HEADS * HEAD_DIM
KV_WIDTH = B_KV_HEADS * HEAD_DIM
IDX_HEADS = 4
IDX_DIM = 64
IDX_WIDTH = IDX_HEADS * IDX_DIM
TOPK_MAX = 256
ROPE_THETA = 500000.0
ROT_DIM = HEAD_DIM // 4
ROT_HALF = ROT_DIM // 2
ATTN_SCALE = HEAD_DIM ** -0.5
N_GROUPS = 4
EXPERTS_PER_GROUP = 8
N_EXPERTS = N_GROUPS * EXPERTS_PER_GROUP
D_EXPERT = 512
LN_EPS = 1e-5

LANES = 128
SUBLANES = 8
ROW_CHUNKS = D_MODEL // LANES

C_AU, C_AV, C_Q, C_K, C_V, C_QI, C_KI = 0, 512, 1024, 1536, 1792, 2048, 2304
IN_WIDTH = C_KI + IDX_DIM + IDX_HEADS
IN_PAD = 2432
WI_LANE = IDX_DIM

INT_MIN = -(2 ** 31)
INT_MAX = 2 ** 31 - 1
VMEM_LIMIT = 48 * 1024 * 1024

PROMPT_TILE = 256
MOE_TILE = 256
TOKEN_TILE = 256
RANK_TILE = 512
IDX_SEQS = 16
STEPS_PER_CHECK = 4


def _dot(a, b):
    return jnp.dot(a, b, preferred_element_type=F32)


def _dot_nt(a, b):
    return lax.dot_general(a, b, (((1,), (1,)), ((), ())), preferred_element_type=F32)


def _split_dot(x, m):
    hi = x.astype(BF16)
    lo = (x - hi.astype(F32)).astype(BF16)
    return _dot(hi, m) + _dot(lo, m)


def _ada_kernel(c_ref, w_ref, b_ref, o_ref):
    s = jax.nn.silu(c_ref[...]).astype(BF16)
    o_ref[...] = _dot(s, w_ref[...].astype(BF16)) + b_ref[...]


def _ada(c_all, w_ada, b_ada):
    rows = c_all.shape[0]
    n_out = w_ada.shape[1]
    tn = 1024
    return pl.pallas_call(
        _ada_kernel,
        grid=(n_out // tn,),
        in_specs=[
            pl.BlockSpec((rows, D_MODEL), lambda j: (0, 0)),
            pl.BlockSpec((D_MODEL, tn), lambda j: (0, j)),
            pl.BlockSpec((1, tn), lambda j: (0, j)),
        ],
        out_specs=pl.BlockSpec((rows, tn), lambda j: (0, j)),
        out_shape=jax.ShapeDtypeStruct((rows, n_out), F32),
        compiler_params=pltpu.CompilerParams(
            dimension_semantics=("arbitrary",), vmem_limit_bytes=VMEM_LIMIT),
        name="ada",
    )(c_all, w_ada, b_ada)


def _rotate(x, rt, blk):
    c = rt[:, 0:LANES]
    s_lo = rt[:, LANES:2 * LANES]
    s_hi = rt[:, 2 * LANES:3 * LANES]
    if blk == 1:
        head = lax.broadcasted_iota(I32, c.shape, 1) < IDX_DIM
        c = jnp.where(head, c, 1.0)
        s_lo = jnp.where(head, s_lo, 0.0)
        s_hi = jnp.where(head, s_hi, 0.0)
    outs = []
    for j in range(x.shape[1] // LANES):
        xb = x[:, j * LANES:(j + 1) * LANES]
        up = pltpu.roll(xb, LANES - ROT_HALF, 1)
        dn = pltpu.roll(xb, ROT_HALF, 1)
        outs.append(xb * c + up * s_lo + dn * s_hi)
    return outs[0] if len(outs) == 1 else jnp.concatenate(outs, axis=1)


def _head_ln(g, mavg, gam, bet):
    def seg_mean(x):
        parts = [_split_dot(x[:, j * 256:(j + 1) * 256], mavg) for j in range(A_WIDTH // 256)]
        return jnp.concatenate(parts, axis=1)
    mu = seg_mean(g)
    d = g - mu
    var = seg_mean(d * d)
    return d * lax.rsqrt(var + LN_EPS) * gam + bet


def _project(h, w_ref, rt, mavg, gam, bet):
    u = jax.nn.gelu(_dot(h, w_ref[:, C_AU:C_AV]))
    vg = _head_ln(jax.nn.gelu(_dot(h, w_ref[:, C_AV:C_Q])), mavg, gam, bet)
    q = _rotate(_dot(h, w_ref[:, C_Q:C_K]), rt, 0) * ATTN_SCALE
    k = _rotate(_dot(h, w_ref[:, C_K:C_V]), rt, 0)
    v = _dot(h, w_ref[:, C_V:C_QI])
    qi = _rotate(_dot(h, w_ref[:, C_QI:C_KI]), rt, 0)
    kiwi = _rotate(_dot(h, w_ref[:, C_KI:IN_PAD]), rt, 1)
    return u, vg, q, k, v, qi, kiwi


def _front_prompt_kernel(x_ref, mod_ref, w_ref, rt_ref, mavg_ref, gam_ref, bet_ref, wsp_ref, bsp_ref,
                         a_ref, kt_ref, vt_ref, kit_ref, kb_ref, kiwib_ref, vtb_ref, qt_ref, qit_ref, wit_ref):
    shift = mod_ref[0:1, :]
    scale = mod_ref[1:2, :]
    h = (x_ref[...] * (1.0 + scale) + shift).astype(BF16)
    u, vg, q, k, v, qi, kiwi = _project(h, w_ref, rt_ref[...], mavg_ref[...], gam_ref[...], bet_ref[...])
    v_t = jnp.transpose(v)
    kiwi_t = jnp.transpose(kiwi)
    kt_ref[...] = jnp.transpose(k)
    vt_ref[...] = v_t
    kit_ref[...] = kiwi_t[0:IDX_DIM, :]
    kb_ref[...] = k.astype(BF16)
    kiwib_ref[...] = kiwi.astype(BF16)
    vtb_ref[...] = v_t.astype(BF16)
    qt_ref[...] = jnp.transpose(q).astype(BF16)
    qit_ref[...] = jnp.transpose(qi).astype(BF16)
    wit_ref[...] = kiwi_t[WI_LANE:WI_LANE + SUBLANES, :]

    rows = lax.broadcasted_iota(I32, (CHUNK, 2 * CHUNK), 0)
    cols = lax.broadcasted_iota(I32, (CHUNK, 2 * CHUNK), 1) % CHUNK
    causal = cols <= rows
    lane = lax.broadcasted_iota(I32, (CHUNK, LANES), 1)
    tt = x_ref.shape[0]
    for cidx in range(tt // CHUNK):
        rs = slice(cidx * CHUNK, (cidx + 1) * CHUNK)
        blocks = []
        for p in range(A_HEADS // 2):
            wcat = jnp.where(causal, wsp_ref[p], 0.0).astype(BF16)
            vb = vg[rs, p * LANES:(p + 1) * LANES]
            rhs = jnp.concatenate([jnp.where(lane < HEAD_DIM, vb, 0.0),
                                   jnp.where(lane >= HEAD_DIM, vb, 0.0)], axis=0).astype(BF16)
            blocks.append(_dot(wcat, rhs))
        s = jnp.concatenate(blocks, axis=1) + bsp_ref[...]
        a_ref[rs, :] = (u[rs, :] * s).astype(BF16)


def _front_prompt(x, mod, w_in, rt, mavg, gam, bet, wsp, bsp, tt):
    b, t, _ = x.shape
    nc = t // tt
    tok = lambda w: pl.BlockSpec((None, tt, w), lambda i, j: (i, j, 0))
    tr = lambda r: pl.BlockSpec((None, None, r, tt), lambda i, j: (i, j, 0, 0))
    pos_minor = lambda r: pl.BlockSpec((None, r, tt), lambda i, j: (i, 0, j))
    const2 = lambda a: pl.BlockSpec(a.shape, lambda i, j: (0,) * a.ndim)
    out_shapes = (
        jax.ShapeDtypeStruct((b, t, A_WIDTH), BF16),
        jax.ShapeDtypeStruct((b, KV_WIDTH, t), F32),
        jax.ShapeDtypeStruct((b, KV_WIDTH, t), F32),
        jax.ShapeDtypeStruct((b, IDX_DIM, t), F32),
        jax.ShapeDtypeStruct((b, t, KV_WIDTH), BF16),
        jax.ShapeDtypeStruct((b, t, LANES), BF16),
        jax.ShapeDtypeStruct((b, nc, KV_WIDTH, tt), BF16),
        jax.ShapeDtypeStruct((b, nc, B_WIDTH, tt), BF16),
        jax.ShapeDtypeStruct((b, nc, IDX_WIDTH, tt), BF16),
        jax.ShapeDtypeStruct((b, nc, SUBLANES, tt), F32),
    )
    return pl.pallas_call(
        _front_prompt_kernel,
        grid=(b, nc),
        in_specs=[
            tok(D_MODEL),
            pl.BlockSpec((None, 6, D_MODEL), lambda i, j: (i, 0, 0)),
            const2(w_in),
            pl.BlockSpec((tt, 3 * LANES), lambda i, j: (j, 0)),
            const2(mavg), const2(gam), const2(bet), const2(wsp), const2(bsp),
        ],
        out_specs=[tok(A_WIDTH), pos_minor(KV_WIDTH), pos_minor(KV_WIDTH), pos_minor(IDX_DIM), tok(KV_WIDTH),
                   tok(LANES), tr(KV_WIDTH), tr(B_WIDTH), tr(IDX_WIDTH), tr(SUBLANES)],
        out_shape=out_shapes,
        compiler_params=pltpu.CompilerParams(
            dimension_semantics=("arbitrary", "arbitrary"), vmem_limit_bytes=VMEM_LIMIT),
        name="front_prompt",
    )(x, mod, w_in, rt, mavg, gam, bet, wsp, bsp)


def _order_key(score, kpos, idx_bits):
    offs = 1 << idx_bits
    bits = pltpu.bitcast(score, I32)
    key = bits ^ ((bits >> 31) & 0x7FFFFFFF)
    key = key + jnp.where(score > 0.0, offs, 0)
    return jnp.where(score == 0.0, offs - kpos, key)


def _select_threshold(count_ge, count_gt_eq_lt, shape, topk, idx_bits):
    offs = 1 << idx_bits
    first_candidates = (offs + 1, 1)
    max_steps = -(-(32 + len(first_candidates)) // STEPS_PER_CHECK) * STEPS_PER_CHECK

    def unresolved(carry):
        i, lo, hi, n_lo = carry
        pending = (n_lo != topk) & (hi - 1 > lo)
        return (i < max_steps) & (jnp.max(pending.astype(F32)) > 0.0)

    def bisect(carry):
        i, lo, hi, n_lo = carry
        for _ in range(STEPS_PER_CHECK):
            mid = (lo >> 1) + (hi >> 1) + (lo & hi & 1)
            for step, value in enumerate(first_candidates):
                forced = jnp.where(i == step, value, INT_MIN)
                mid = jnp.where((lo < forced) & (forced < hi), forced, mid)
            tot = count_ge(mid)
            take = tot >= topk
            lo = jnp.where(take, mid, lo)
            n_lo = jnp.where(take, tot, n_lo)
            hi = jnp.where(take, hi, mid)
            i = i + 1
        return i, lo, hi, n_lo

    never = jnp.full(shape, INT_MAX, I32)
    _, thr, _, n_lo = lax.while_loop(
        unresolved, bisect, (jnp.int32(0), jnp.full(shape, INT_MIN + 1, I32), jnp.full(shape, INT_MAX, I32), never))
    tied = (n_lo > topk) & (n_lo != INT_MAX)
    big = jnp.full(shape, offs, I32)

    def resolve_ties(_):
        n_gt, _ = count_gt_eq_lt(thr, big)
        need = topk - n_gt

        def idx_step(i, cut):
            cand = cut | lax.shift_left(jnp.int32(1), idx_bits - 1 - i)
            _, n_eq = count_gt_eq_lt(thr, cand)
            return jnp.where(n_eq <= need, cand, cut)

        cut = lax.fori_loop(0, idx_bits, idx_step, jnp.zeros(shape, I32))
        return jnp.where(tied, cut, big)

    any_tied = jnp.max(tied.astype(F32)) > 0.0
    cut = lax.cond(any_tied, resolve_ties, lambda _: big, 0)
    return thr, cut


def _fold_rows(x):
    acc = x[0:SUBLANES]
    for r in range(1, x.shape[0] // SUBLANES):
        acc = acc + x[r * SUBLANES:(r + 1) * SUBLANES]
    return acc


def _col_total(cnt8):
    return jnp.sum(cnt8.astype(F32), axis=0, keepdims=True).astype(I32)


def _attn_prompt_kernel(qit_ref, wit_ref, qt_ref, kiwib_ref, kb_ref, vt_ref, o_ref,
                        key_ref, w4_ref, wq_ref, s0_ref, s1_ref, m_ref, l_ref, acc_ref, *, tq, topk, idx_bits):
    j = pl.program_id(1)
    kc = tq
    n_kc = j + 1

    @pl.when((pl.program_id(0) == 0) & (j == 0))
    def _():
        w4_ref[...] = jnp.zeros(w4_ref.shape, BF16)
        wq_ref[...] = jnp.zeros(wq_ref.shape, BF16)

    for h in range(IDX_HEADS):
        w4_ref[0:IDX_DIM, h * tq:(h + 1) * tq] = qit_ref[h * IDX_DIM:(h + 1) * IDX_DIM, :]
    for h in range(B_HEADS):
        n = h // B_GROUP
        wq_ref[h, n * HEAD_DIM:(n + 1) * HEAD_DIM, :] = qt_ref[h * HEAD_DIM:(h + 1) * HEAD_DIM, :]

    wit = wit_ref[...]
    krow = lax.broadcasted_iota(I32, (kc, tq), 0)
    qpos = j * tq + lax.broadcasted_iota(I32, (kc, tq), 1)

    def score_chunk(c, _):
        k0 = pl.multiple_of(c * kc, kc)
        s = _dot(kiwib_ref[pl.ds(k0, kc), :], w4_ref[...])
        tot = jnp.maximum(s[:, 0:tq], 0.0) * wit[0:1, :]
        for h in range(1, IDX_HEADS):
            tot = tot + jnp.maximum(s[:, h * tq:(h + 1) * tq], 0.0) * wit[h:h + 1, :]
        kpos = k0 + krow
        key_ref[c] = jnp.where(kpos <= qpos, _order_key(tot, kpos, idx_bits), INT_MIN)
        return 0

    lax.fori_loop(0, n_kc, score_chunk, 0)

    def count_ge(cand):
        def body(c, cnt):
            return cnt + _fold_rows(jnp.where(key_ref[c] >= cand, 1, 0))
        return _col_total(lax.fori_loop(0, n_kc, body, jnp.zeros((SUBLANES, tq), I32)))

    def count_gt_eq_lt(thr, pos):
        def body(c, carry):
            n_gt, n_eq = carry
            key = key_ref[c]
            n_gt = n_gt + _fold_rows(jnp.where(key > thr, 1, 0))
            n_eq = n_eq + _fold_rows(jnp.where((key == thr) & (c * kc + krow < pos), 1, 0))
            return n_gt, n_eq
        z = jnp.zeros((SUBLANES, tq), I32)
        n_gt, n_eq = lax.fori_loop(0, n_kc, body, (z, z))
        return _col_total(n_gt), _col_total(n_eq)

    thr, cut = _select_threshold(count_ge, count_gt_eq_lt, (1, tq), topk, idx_bits)

    m_ref[...] = jnp.full(m_ref.shape, jnp.finfo(F32).min, F32)
    l_ref[...] = jnp.zeros(l_ref.shape, F32)
    acc_ref[...] = jnp.zeros(acc_ref.shape, F32)

    def key_block(c):
        return kb_ref[pl.ds(pl.multiple_of(c * kc, kc), kc), :]

    kblk0 = key_block(0)
    for h in range(B_HEADS):
        s0_ref[h] = _dot(kblk0, wq_ref[h])

    def stage(c, open_bias, src, dst, c_next):
        key = key_ref[c]
        sel = (key > thr) | ((key == thr) & (c * kc + krow < cut))
        bias = jnp.where(sel, open_bias, -jnp.inf)
        kblk_next = key_block(c_next)
        vtc = vt_ref[c]
        for h in range(B_HEADS):
            n = h // B_GROUP
            dst[h] = _dot(kblk_next, wq_ref[h])
            s = src[h] + bias
            m_old = m_ref[h]
            m_new = jnp.maximum(m_old, jnp.max(s, axis=0, keepdims=True))
            alpha = jnp.exp(m_old - m_new)
            p = jnp.exp(s - m_new)
            l_ref[h] = alpha * l_ref[h] + jnp.sum(p, axis=0, keepdims=True)
            acc_ref[h] = alpha * acc_ref[h] + _dot(vtc[n * HEAD_DIM:(n + 1) * HEAD_DIM, :], p.astype(BF16))
            m_ref[h] = m_new

    last = n_kc - 1

    def attend_pair(i, _):
        c0 = 2 * i
        c1 = jnp.minimum(c0 + 1, last)
        stage(c0, 0.0, s0_ref, s1_ref, c1)
        stage(c1, jnp.where(c0 + 1 <= last, 0.0, -jnp.inf), s1_ref, s0_ref, jnp.minimum(c0 + 2, last))
        return 0

    lax.fori_loop(0, (n_kc + 1) // 2, attend_pair, 0)
    out_t = jnp.concatenate([acc_ref[h] / l_ref[h] for h in range(B_HEADS)], axis=0)
    o_ref[...] = jnp.transpose(out_t).astype(BF16)


def _attn_prompt(qit, wit, qt, kiwib, kb, vt):
    b, nc, _, tq = qt.shape
    t = nc * tq
    topk = min(TOPK_MAX, t // 4)
    idx_bits = max(1, (t - 1).bit_length())
    tr = lambda r: pl.BlockSpec((None, None, r, tq), lambda i, j: (i, j, 0, 0))
    full = lambda w: pl.BlockSpec((None, t, w), lambda i, j: (i, 0, 0))
    kernel = functools.partial(_attn_prompt_kernel, tq=tq, topk=topk, idx_bits=idx_bits)
    return pl.pallas_call(
        kernel,
        grid=(b, nc),
        in_specs=[tr(IDX_WIDTH), tr(SUBLANES), tr(B_WIDTH), full(LANES), full(KV_WIDTH),
                  pl.BlockSpec((None, nc, KV_WIDTH, tq), lambda i, j: (i, 0, 0, 0))],
        out_specs=pl.BlockSpec((None, tq, B_WIDTH), lambda i, j: (i, j, 0)),
        out_shape=jax.ShapeDtypeStruct((b, t, B_WIDTH), BF16),
        scratch_shapes=[
            pltpu.VMEM((nc, tq, tq), I32),
            pltpu.VMEM((LANES, IDX_HEADS * tq), BF16),
            pltpu.VMEM((B_HEADS, KV_WIDTH, tq), BF16),
            pltpu.VMEM((B_HEADS, tq, tq), F32),
            pltpu.VMEM((B_HEADS, tq, tq), F32),
            pltpu.VMEM((B_HEADS, 1, tq), F32),
            pltpu.VMEM((B_HEADS, 1, tq), F32),
            pltpu.VMEM((B_HEADS, HEAD_DIM, tq), F32),
        ],
        compiler_params=pltpu.CompilerParams(
            dimension_semantics=("arbitrary", "arbitrary"), vmem_limit_bytes=VMEM_LIMIT),
        name="attn_prompt",
    )(qit, wit, qt, kiwib, kb, vt)


def _rot_tables(pos):
    r = pos.shape[0]
    inv_freq = np.float32(ROPE_THETA) ** (-np.arange(ROT_HALF, dtype=np.float32) * np.float32(2.0) / np.float32(ROT_DIM))
    ang = pos.astype(np.float32)[:, None] * inv_freq[None, :]
    cos, sin = np.cos(ang), np.sin(ang)
    rest = HEAD_DIM - ROT_DIM
    c64 = np.concatenate([cos, cos, np.ones((r, rest), np.float32)], axis=1)
    lo64 = np.concatenate([-sin, np.zeros((r, HEAD_DIM - ROT_HALF), np.float32)], axis=1)
    hi64 = np.concatenate([np.zeros((r, ROT_HALF), np.float32), sin, np.zeros((r, rest), np.float32)], axis=1)
    return jnp.asarray(np.concatenate([c64, c64, lo64, lo64, hi64, hi64], axis=1).astype(np.float32))


def _head_avg_matrix():
    return jnp.kron(jnp.eye(256 // HEAD_DIM, dtype=F32), jnp.full((HEAD_DIM, HEAD_DIM), 1.0 / HEAD_DIM, F32)).astype(BF16)


def _pad_w_in(w_in):
    return jnp.pad(w_in, ((0, 0), (0, IN_PAD - IN_WIDTH))).astype(BF16)


def _front_sample_kernel(x_ref, shift_ref, scale_ref, w_ref, rt_ref, mavg_ref, gam_ref, bet_ref, wl_ref, bl_ref,
                         a_ref, q_ref, k_ref, v_ref, qi_ref, kiwi_ref, vg_ref, *, ts, db):
    one_scale = 1.0 + scale_ref[...]
    shift = shift_ref[...]
    h = jnp.concatenate([x_ref[t * db:(t + 1) * db, :] * one_scale + shift for t in range(ts)], axis=0).astype(BF16)
    u, vg, q, k, v, qi, kiwi = _project(h, w_ref, rt_ref[...], mavg_ref[...], gam_ref[...], bet_ref[...])
    q_ref[...] = q.astype(BF16)
    k_ref[...] = k
    v_ref[...] = v
    qi_ref[...] = qi.astype(BF16)
    kiwi_ref[...] = kiwi
    vg_ref[...] = vg
    for t in range(ts):
        s = bl_ref[t:t + 1, :]
        for src in range(t + 1):
            s = s + wl_ref[t * ts + src:t * ts + src + 1, :] * vg[src * db:(src + 1) * db, :]
        a_ref[t * db:(t + 1) * db, :] = (u[t * db:(t + 1) * db, :] * s).astype(BF16)


def _front_sample(x_tm, mod_s, w_in, rt, mavg, gam, bet, wl, bl, ts, db):
    r = ts * db
    whole = lambda a: pl.BlockSpec(a.shape, lambda i: (0,) * a.ndim)
    out = lambda w, dt: jax.ShapeDtypeStruct((r, w), dt)
    outs = (out(A_WIDTH, BF16), out(B_WIDTH, BF16), out(KV_WIDTH, F32), out(KV_WIDTH, F32),
            out(IDX_WIDTH, BF16), out(LANES, F32), out(A_WIDTH, F32))
    return pl.pallas_call(
        functools.partial(_front_sample_kernel, ts=ts, db=db),
        grid=(1,),
        in_specs=[
            whole(x_tm),
            pl.BlockSpec((db, D_MODEL), lambda i: (0, 0)),
            pl.BlockSpec((db, D_MODEL), lambda i: (0, 1)),
            whole(w_in), whole(rt), whole(mavg), whole(gam), whole(bet), whole(wl), whole(bl),
        ],
        out_specs=[pl.BlockSpec((r, s.shape[1]), lambda i: (0, 0)) for s in outs],
        out_shape=outs,
        compiler_params=pltpu.CompilerParams(dimension_semantics=("arbitrary",), vmem_limit_bytes=VMEM_LIMIT),
        name="front_sample",
    )(x_tm, mod_s, mod_s, w_in, rt, mavg, gam, bet, wl, bl)


def _idx_sample_kernel(pt_ref, qi4_ref, wcol_ref, qpos_ref, kinew_ref, kidx_hbm, bias_ref,
                       ki_buf, key_ref, sem, *, ga, ts, n_pages, page, lpad, topk, idx_bits):
    i = pl.program_id(0)
    n_steps = pl.num_programs(0)
    past = n_pages * page
    rows = ga * ts
    slot = i % 2

    def page_copy(step, to_slot, g, p):
        phys = pt_ref[step * ga + g, p]
        return pltpu.make_async_copy(kidx_hbm.at[phys], ki_buf.at[to_slot, g, :, pl.ds(p * page, page)],
                                     sem.at[to_slot])

    def start_all(step, to_slot):
        for g in range(ga):
            for p in range(n_pages):
                page_copy(step, to_slot, g, p).start()

    @pl.when(i == 0)
    def _():
        start_all(0, 0)

    @pl.when(i + 1 < n_steps)
    def _():
        start_all(i + 1, 1 - slot)

    ki_buf[slot, :, :, pl.ds(past, LANES)] = kinew_ref[...]
    for g in range(ga):
        for p in range(n_pages):
            page_copy(i, slot, g, p).wait()

    kpos = lax.broadcasted_iota(I32, (ts, lpad), 1)
    for g in range(ga):
        s = _dot(qi4_ref[g], ki_buf[slot, g].astype(BF16))
        r = jnp.maximum(s, 0.0) * wcol_ref[g]
        tot = r[0:ts]
        for h in range(1, IDX_HEADS):
            tot = tot + r[h * ts:(h + 1) * ts]
        adm = (kpos <= qpos_ref[g * ts:(g + 1) * ts, :]) & (kpos < past + ts)
        key_ref[g * ts:(g + 1) * ts, :] = jnp.where(adm, _order_key(tot, kpos, idx_bits), INT_MIN)

    sub = lpad // LANES
    lane128 = lax.broadcasted_iota(I32, (rows, LANES), 1)

    def row_total(cnt):
        tot = jnp.sum(cnt.astype(F32), axis=1, keepdims=True)
        return jnp.broadcast_to(tot, cnt.shape).astype(I32)

    def count_ge(cand):
        cnt = jnp.zeros((rows, LANES), I32)
        for s_ in range(sub):
            cnt = cnt + jnp.where(key_ref[:, s_ * LANES:(s_ + 1) * LANES] >= cand, 1, 0)
        return row_total(cnt)

    def count_gt_eq_lt(thr, pos):
        n_gt = jnp.zeros((rows, LANES), I32)
        n_eq = jnp.zeros((rows, LANES), I32)
        for s_ in range(sub):
            kk = key_ref[:, s_ * LANES:(s_ + 1) * LANES]
            n_gt = n_gt + jnp.where(kk > thr, 1, 0)
            n_eq = n_eq + jnp.where((kk == thr) & (s_ * LANES + lane128 < pos), 1, 0)
        return row_total(n_gt), row_total(n_eq)

    thr, cut = _select_threshold(count_ge, count_gt_eq_lt, (rows, LANES), topk, idx_bits)
    for s_ in range(sub):
        kk = key_ref[:, s_ * LANES:(s_ + 1) * LANES]
        sel = (kk > thr) | ((kk == thr) & (s_ * LANES + lane128 < cut))
        bias = jnp.where(sel, 0.0, -jnp.inf)
        for g in range(ga):
            bias_ref[g, :, s_ * LANES:(s_ + 1) * LANES] = bias[g * ts:(g + 1) * ts]


def _idx_sample(page_table, qi4, wcol, qpos, kinew_t, kidx_t, ts, ga):
    db = qi4.shape[0]
    n_pages = page_table.shape[1]
    page = kidx_t.shape[2]
    past = n_pages * page
    lpad = past + LANES
    topk = min(TOPK_MAX, (past + ts) // 4)
    idx_bits = max(1, (lpad - 1).bit_length())
    kernel = functools.partial(_idx_sample_kernel, ga=ga, ts=ts, n_pages=n_pages, page=page, lpad=lpad,
                               topk=topk, idx_bits=idx_bits)
    return pl.pallas_call(
        kernel,
        grid_spec=pltpu.PrefetchScalarGridSpec(
            num_scalar_prefetch=1,
            grid=(db // ga,),
            in_specs=[pl.BlockSpec((ga, IDX_HEADS * ts, IDX_DIM), lambda i, pt: (i, 0, 0)),
                      pl.BlockSpec((ga, IDX_HEADS * ts, 1), lambda i, pt: (i, 0, 0)),
                      pl.BlockSpec((ga * ts, 1), lambda i, pt: (i, 0)),
                      pl.BlockSpec((ga, IDX_DIM, LANES), lambda i, pt: (i, 0, 0)),
                      pl.BlockSpec(memory_space=pl.ANY)],
            out_specs=pl.BlockSpec((ga, ts, lpad), lambda i, pt: (i, 0, 0)),
            scratch_shapes=[
                pltpu.VMEM((2, ga, IDX_DIM, lpad), F32),
                pltpu.VMEM((ga * ts, lpad), I32),
                pltpu.SemaphoreType.DMA((2,)),
            ]),
        out_shape=jax.ShapeDtypeStruct((db, ts, lpad), F32),
        compiler_params=pltpu.CompilerParams(dimension_semantics=("arbitrary",), vmem_limit_bytes=VMEM_LIMIT),
        name="idx_sample",
    )(page_table, qi4, wcol, qpos, kinew_t, kidx_t)


def _attn_sample_kernel(pt_ref, q_ref, bias_ref, knew_ref, vnew_ref, k_hbm, v_hbm, o_ref,
                        k_buf, v_buf, sem, *, ts, n_pages, page):
    b = pl.program_id(0)
    nb = pl.num_programs(0)
    past = n_pages * page
    slot = b % 2

    def page_copies(seq, to_slot, p):
        phys = pt_ref[seq, p]
        dst = pl.ds(p * page, page)
        return (pltpu.make_async_copy(k_hbm.at[phys], k_buf.at[to_slot, :, :, dst], sem.at[0, to_slot]),
                pltpu.make_async_copy(v_hbm.at[phys], v_buf.at[to_slot, :, :, dst], sem.at[1, to_slot]))

    def start_all(seq, to_slot):
        for p in range(n_pages):
            for cp in page_copies(seq, to_slot, p):
                cp.start()

    @pl.when(b == 0)
    def _():
        start_all(0, 0)

    @pl.when(b + 1 < nb)
    def _():
        start_all(b + 1, 1 - slot)

    k_buf[slot, :, :, pl.ds(past, LANES)] = knew_ref[...]
    v_buf[slot, :, :, pl.ds(past, LANES)] = vnew_ref[...]
    for p in range(n_pages):
        for cp in page_copies(b, slot, p):
            cp.wait()

    q = q_ref[...]
    bias = bias_ref[...]
    bias2 = jnp.concatenate([bias] * B_GROUP, axis=0)
    outs = [None] * B_HEADS
    for n in range(B_KV_HEADS):
        kt = k_buf[slot, n].astype(BF16)
        vt = v_buf[slot, n].astype(BF16)
        qs = jnp.concatenate([q[:, (n * B_GROUP + g) * HEAD_DIM:(n * B_GROUP + g + 1) * HEAD_DIM]
                              for g in range(B_GROUP)], axis=0)
        sc = _dot(qs, kt) + bias2
        m = jnp.max(sc, axis=1, keepdims=True)
        p_ = jnp.exp(sc - m)
        o = _dot_nt(p_.astype(BF16), vt) / jnp.sum(p_, axis=1, keepdims=True)
        for g in range(B_GROUP):
            outs[n * B_GROUP + g] = o[g * ts:(g + 1) * ts]
    o_ref[...] = jnp.concatenate(outs, axis=1).astype(BF16)


def _attn_sample(page_table, q, bias, knew_t, vnew_t, k_t, v_t):
    db, ts, _ = q.shape
    n_pages = page_table.shape[1]
    page = k_t.shape[3]
    lpad = bias.shape[2]
    seq3 = lambda a: pl.BlockSpec((None,) + a.shape[1:], lambda b, pt: (b,) + (0,) * (a.ndim - 1))
    anyspec = pl.BlockSpec(memory_space=pl.ANY)
    kernel = functools.partial(_attn_sample_kernel, ts=ts, n_pages=n_pages, page=page)
    return pl.pallas_call(
        kernel,
        grid_spec=pltpu.PrefetchScalarGridSpec(
            num_scalar_prefetch=1,
            grid=(db,),
            in_specs=[seq3(q), seq3(bias), seq3(knew_t), seq3(vnew_t), anyspec, anyspec],
            out_specs=pl.BlockSpec((None, ts, B_WIDTH), lambda b, pt: (b, 0, 0)),
            scratch_shapes=[
                pltpu.VMEM((2, B_KV_HEADS, HEAD_DIM, lpad), F32),
                pltpu.VMEM((2, B_KV_HEADS, HEAD_DIM, lpad), F32),
                pltpu.SemaphoreType.DMA((2, 2)),
            ]),
        out_shape=jax.ShapeDtypeStruct((db, ts, B_WIDTH), BF16),
        compiler_params=pltpu.CompilerParams(dimension_semantics=("arbitrary",), vmem_limit_bytes=VMEM_LIMIT),
        name="attn_sample",
    )(page_table, q, bias, knew_t, vnew_t, k_t, v_t)


def _layer_norm_rows(y, gam, bet):
    mu = jnp.mean(y, axis=1, keepdims=True)
    d = y - mu
    var = jnp.mean(d * d, axis=1, keepdims=True)
    return d * lax.rsqrt(var + LN_EPS) * gam + bet


def _store_row_tiles(ref, val):
    r = val.shape[0]
    for c in range(ROW_CHUNKS):
        ref[pl.ds(c, r, stride=ROW_CHUNKS), :] = val[:, c * LANES:(c + 1) * LANES]


def _load_row_tiles(ref, r):
    return jnp.concatenate([ref[pl.ds(c, r, stride=ROW_CHUNKS), :] for c in range(ROW_CHUNKS)], axis=1)


ROUTER_ROWS = SUBLANES + N_EXPERTS


def _route(logits_t):
    r = logits_t.shape[1]
    far = float(LANES)

    def softmax_rows(x):
        e = jnp.exp(x - jnp.max(x, axis=0, keepdims=True))
        return e / jnp.sum(e, axis=0, keepdims=True)

    def first_max(p):
        rows = lax.broadcasted_iota(I32, p.shape, 0).astype(F32)
        best = jnp.max(p, axis=0, keepdims=True)
        return best, jnp.min(jnp.where(p == best, rows, far), axis=0, keepdims=True), rows

    g_w, g_sel, _ = first_max(softmax_rows(logits_t[0:N_GROUPS]))
    el = jnp.zeros((EXPERTS_PER_GROUP, r), F32)
    for g in range(N_GROUPS):
        lo = SUBLANES + g * EXPERTS_PER_GROUP
        el = jnp.where(g_sel == float(g), logits_t[lo:lo + EXPERTS_PER_GROUP], el)
    ep = softmax_rows(el)
    p1, i1, rows = first_max(ep)
    p2, i2, _ = first_max(jnp.where(rows == i1, -1.0, ep))
    denom = p1 + p2
    base = g_sel * float(EXPERTS_PER_GROUP)
    return jnp.concatenate([base + i1, base + i2, g_w * p1 / denom, g_w * p2 / denom,
                            jnp.zeros((SUBLANES - 4, r), F32)], axis=0)


def _split_bf16(x):
    hi = x.astype(BF16)
    return hi, (x - hi.astype(F32)).astype(BF16)


def _mid_kernel(x_ref, a_ref, b_ref, g1_ref, sh2_ref, sc2_ref, wo_ref, gam_ref, bet_ref, wr_ref,
                x1_ref, h2_ref, route_ref, *, alpha):
    mixed = _dot(a_ref[...], wo_ref[0:A_WIDTH, :]) + _dot(b_ref[...], wo_ref[A_WIDTH:A_WIDTH + B_WIDTH, :])
    x1 = _layer_norm_rows(alpha * x_ref[...] + g1_ref[...] * mixed, gam_ref[...], bet_ref[...])
    x1_ref[...] = x1
    h2 = x1 * (1.0 + sc2_ref[...]) + sh2_ref[...]
    _store_row_tiles(h2_ref, h2)
    w_hi, w_lo = _split_bf16(wr_ref[...])
    h_hi, h_lo = _split_bf16(h2)
    logits_t = _dot_nt(w_hi, h_hi) + _dot_nt(w_hi, h_lo) + _dot_nt(w_lo, h_hi)
    route_ref[...] = _route(logits_t)


def _mod_spec(mod3, comp, tiles_per_group):
    rm = mod3.shape[1]
    return pl.BlockSpec((None, rm, D_MODEL), lambda i: (i // tiles_per_group, 0, comp))


def _mid(x, a, b, mod3, tiles_per_group, tile, w_out, gam, bet, w_router, alpha):
    n = x.shape[0]
    rows = lambda w: pl.BlockSpec((tile, w), lambda i: (i, 0))
    whole = lambda arr: pl.BlockSpec(arr.shape, lambda i: (0,) * arr.ndim)
    return pl.pallas_call(
        functools.partial(_mid_kernel, alpha=alpha),
        grid=(n // tile,),
        in_specs=[rows(D_MODEL), rows(A_WIDTH), rows(B_WIDTH),
                  _mod_spec(mod3, 2, tiles_per_group), _mod_spec(mod3, 3, tiles_per_group),
                  _mod_spec(mod3, 4, tiles_per_group),
                  whole(w_out), whole(gam), whole(bet), whole(w_router)],
        out_specs=[rows(D_MODEL), pl.BlockSpec((tile * ROW_CHUNKS, LANES), lambda i: (i, 0)),
                   pl.BlockSpec((SUBLANES, tile), lambda i: (0, i))],
        out_shape=(jax.ShapeDtypeStruct((n, D_MODEL), F32),
                   jax.ShapeDtypeStruct((n * ROW_CHUNKS, LANES), F32),
                   jax.ShapeDtypeStruct((SUBLANES, n), F32)),
        compiler_params=pltpu.CompilerParams(dimension_semantics=("arbitrary",), vmem_limit_bytes=VMEM_LIMIT),
        name="mid",
    )(x, a, b, mod3, mod3, mod3, w_out, gam, bet, w_router)


META_CNT, META_START, META_END, META_NACT = 0, 1, 2, 3


def _plan_kernel(route_ref, pos_ref, meta_ref, te_ref, carry_ref, starts_ref, *, nt, tm):
    phase = pl.program_id(0)
    i = pl.program_id(1)

    @pl.when(i == 0)
    def _():
        carry_ref[...] = jnp.zeros(carry_ref.shape, F32)

    eid = route_ref[0:2, :].astype(I32)
    e_iota = lax.broadcasted_iota(I32, (N_EXPERTS, nt), 0)
    hit0 = eid[0:1, :] == e_iota
    hit1 = eid[1:2, :] == e_iota
    onehot = jnp.where(hit0 | hit1, 1.0, 0.0)

    @pl.when(phase == 0)
    def _():
        pos_ref[...] = jnp.zeros(pos_ref.shape, I32)
        carry_ref[...] = carry_ref[...] + jnp.sum(onehot, axis=1, keepdims=True)

        @pl.when(i == pl.num_programs(1) - 1)
        def _():
            cnt = carry_ref[...]
            tiles = jnp.floor((cnt + float(tm - 1)) * (1.0 / tm))
            r = lax.broadcasted_iota(I32, (N_EXPERTS, N_EXPERTS), 0)
            c = lax.broadcasted_iota(I32, (N_EXPERTS, N_EXPERTS), 1)
            lower = jnp.where(r > c, 1.0, 0.0).astype(BF16)
            tiles_before = _dot(lower, jnp.broadcast_to(tiles, (N_EXPERTS, LANES)).astype(BF16))[:, 0:1]
            tiles_end = tiles_before + tiles
            starts_ref[...] = tiles_before * float(tm)

            diag = (lax.broadcasted_iota(I32, (N_EXPERTS, LANES), 0)
                    == lax.broadcasted_iota(I32, (N_EXPERTS, LANES), 1))

            def as_row(col):
                return jnp.sum(jnp.where(diag, jnp.broadcast_to(col, (N_EXPERTS, LANES)), 0.0),
                               axis=0, keepdims=True)

            nact = jnp.broadcast_to(jnp.max(tiles_end, axis=0, keepdims=True), (1, LANES))
            meta_ref[...] = jnp.concatenate(
                [as_row(cnt), as_row(tiles_before * float(tm)), as_row(tiles_end * float(tm)), nact,
                 jnp.zeros((SUBLANES - 4, LANES), F32)], axis=0).astype(I32)
            tile_i = lax.broadcasted_iota(I32, (N_EXPERTS, te_ref.shape[1]), 1).astype(F32)
            owner = jnp.sum(jnp.where(tiles_end <= tile_i, 1.0, 0.0), axis=0, keepdims=True)
            te_ref[...] = jnp.minimum(owner, float(N_EXPERTS - 1)).astype(I32)

    @pl.when(phase == 1)
    def _():
        upper = (lax.broadcasted_iota(I32, (nt, nt), 0) <= lax.broadcasted_iota(I32, (nt, nt), 1))
        incl = _dot(onehot.astype(BF16), jnp.where(upper, 1.0, 0.0).astype(BF16))
        slot = starts_ref[...] + carry_ref[...] + incl - 1.0
        pos_ref[0:1, :] = jnp.sum(jnp.where(hit0, slot, 0.0), axis=0, keepdims=True).astype(I32)
        pos_ref[1:2, :] = jnp.sum(jnp.where(hit1, slot, 0.0), axis=0, keepdims=True).astype(I32)
        carry_ref[...] = carry_ref[...] + jnp.sum(onehot, axis=1, keepdims=True)


def _plan(route_t, nt, tm):
    n = route_t.shape[1]
    n_tiles = (2 * n) // tm + N_EXPERTS
    te_width = pl.cdiv(n_tiles, LANES) * LANES
    pos, meta, te = pl.pallas_call(
        functools.partial(_plan_kernel, nt=nt, tm=tm),
        grid=(2, n // nt),
        in_specs=[pl.BlockSpec((SUBLANES, nt), lambda p, i: (0, i))],
        out_specs=[pl.BlockSpec((2, nt), lambda p, i: (0, i * p)),
                   pl.BlockSpec((SUBLANES, LANES), lambda p, i: (0, 0)),
                   pl.BlockSpec((1, te_width), lambda p, i: (0, 0))],
        out_shape=(jax.ShapeDtypeStruct((2, n), I32), jax.ShapeDtypeStruct((SUBLANES, LANES), I32),
                   jax.ShapeDtypeStruct((1, te_width), I32)),
        scratch_shapes=[pltpu.VMEM((N_EXPERTS, 1), F32), pltpu.VMEM((N_EXPERTS, 1), F32)],
        compiler_params=pltpu.CompilerParams(dimension_semantics=("arbitrary", "arbitrary")),
        name="plan",
    )(route_t)
    return pos, meta, te, n_tiles


def _row_copy(src, src_row, dst, dst_row, sem):
    return pltpu.make_async_copy(src.at[pl.ds(src_row * ROW_CHUNKS, ROW_CHUNKS)],
                                 dst.at[pl.ds(dst_row * ROW_CHUNKS, ROW_CHUNKS)], sem)


def _dispatch_kernel(meta_ref, pos_ref, hp_ref, hs_ref, xs_out, zero_ref, sem, *, tt, blocks_p, tm, n_tiles):
    i = pl.program_id(0)

    def scatter(h_ref):
        def issue(j, _):
            for k in range(2):
                _row_copy(h_ref, j, xs_out, pos_ref[k, j], sem.at[k]).start(priority=k)
            return 0

        def drain(j, _):
            for k in range(2):
                _row_copy(h_ref, j, xs_out, pos_ref[k, j], sem.at[k]).wait()
            return 0

        lax.fori_loop(0, tt, issue, 0)
        lax.fori_loop(0, tt, drain, 0)

    @pl.when(i < blocks_p)
    def _():
        scatter(hp_ref)

    @pl.when(i >= blocks_p)
    def _():
        scatter(hs_ref)

    @pl.when(i == pl.num_programs(0) - 1)
    def _():
        zero_ref[...] = jnp.zeros(zero_ref.shape, F32)

        def zero_rows(first_row, n_rows):
            return pltpu.make_async_copy(zero_ref.at[pl.ds(0, n_rows * ROW_CHUNKS)],
                                         xs_out.at[pl.ds(first_row * ROW_CHUNKS, n_rows * ROW_CHUNKS)], sem.at[0])

        def start_row(r, c):
            zero_rows(r, 1).start()
            return c

        def wait_row(r, c):
            zero_rows(r, 1).wait()
            return c

        def per_expert(row_fn):
            def body(e, c):
                lo = meta_ref[META_START, e] + meta_ref[META_CNT, e]
                return lax.fori_loop(lo, meta_ref[META_END, e], row_fn, c)
            return body

        def start_tile(t, c):
            zero_rows(t * tm, tm).start()
            return c

        def wait_tile(t, c):
            zero_rows(t * tm, tm).wait()
            return c

        nact = meta_ref[META_NACT, 0]
        lax.fori_loop(0, N_EXPERTS, per_expert(start_row), 0)
        lax.fori_loop(nact, n_tiles, start_tile, 0)
        lax.fori_loop(0, N_EXPERTS, per_expert(wait_row), 0)
        lax.fori_loop(nact, n_tiles, wait_tile, 0)


def _dispatch(meta, pos, h2_p, h2_s, tt, tm, n_tiles):
    blocks_p = h2_p.shape[0] // (tt * ROW_CHUNKS)
    blocks_s = h2_s.shape[0] // (tt * ROW_CHUNKS)
    return pl.pallas_call(
        functools.partial(_dispatch_kernel, tt=tt, blocks_p=blocks_p, tm=tm, n_tiles=n_tiles),
        grid_spec=pltpu.PrefetchScalarGridSpec(
            num_scalar_prefetch=1,
            grid=(blocks_p + blocks_s,),
            in_specs=[pl.BlockSpec((2, tt), lambda i, m: (0, i), memory_space=pltpu.SMEM),
                      pl.BlockSpec((tt * ROW_CHUNKS, LANES), lambda i, m: (jnp.minimum(i, blocks_p - 1), 0)),
                      pl.BlockSpec((tt * ROW_CHUNKS, LANES), lambda i, m: (jnp.maximum(i - blocks_p, 0), 0))],
            out_specs=pl.BlockSpec(memory_space=pl.ANY),
            scratch_shapes=[pltpu.VMEM((tm * ROW_CHUNKS, LANES), F32), pltpu.SemaphoreType.DMA((2,))]),
        out_shape=jax.ShapeDtypeStruct((n_tiles * tm * ROW_CHUNKS, LANES), F32),
        compiler_params=pltpu.CompilerParams(dimension_semantics=("arbitrary",)),
        name="dispatch",
    )(meta, pos, h2_p, h2_s)


def _experts_kernel(te_ref, meta_ref, xs_ref, wg_ref, wu_ref, wd_ref, ys_ref, wg_b, wu_b, wd_b, *, tm):
    i = pl.program_id(0)
    active = i < meta_ref[META_NACT, 0]
    fresh = (i == 0) | (te_ref[0, i] != te_ref[0, jnp.maximum(i - 1, 0)])

    @pl.when(active & fresh)
    def _():
        wg_b[...] = wg_ref[...].astype(BF16)
        wu_b[...] = wu_ref[...].astype(BF16)
        wd_b[...] = wd_ref[...].astype(BF16)

    @pl.when(active)
    def _():
        x = _load_row_tiles(xs_ref, tm).astype(BF16)
        hid = (jax.nn.silu(_dot(x, wg_b[...])) * _dot(x, wu_b[...])).astype(BF16)
        _store_row_tiles(ys_ref, _dot(hid, wd_b[...]))

    @pl.when(jnp.logical_not(active))
    def _():
        ys_ref[...] = jnp.zeros(ys_ref.shape, F32)


def _experts(te, meta, xs, w_gate, w_up, w_down, tm, n_tiles):
    last_active = lambda i, m: jnp.minimum(i, m[META_NACT, 0] - 1)
    tile = pl.BlockSpec((tm * ROW_CHUNKS, LANES), lambda i, te, m: (last_active(i, m), 0))
    w_in_spec = pl.BlockSpec((None, D_MODEL, D_EXPERT), lambda i, te, m: (te[0, last_active(i, m)], 0, 0))
    w_out_spec = pl.BlockSpec((None, D_EXPERT, D_MODEL), lambda i, te, m: (te[0, last_active(i, m)], 0, 0))
    return pl.pallas_call(
        functools.partial(_experts_kernel, tm=tm),
        grid_spec=pltpu.PrefetchScalarGridSpec(
            num_scalar_prefetch=2,
            grid=(n_tiles,),
            in_specs=[tile, w_in_spec, w_in_spec, w_out_spec],
            out_specs=pl.BlockSpec((tm * ROW_CHUNKS, LANES), lambda i, te, m: (i, 0)),
            scratch_shapes=[pltpu.VMEM((D_MODEL, D_EXPERT), BF16), pltpu.VMEM((D_MODEL, D_EXPERT), BF16),
                            pltpu.VMEM((D_EXPERT, D_MODEL), BF16)]),
        out_shape=jax.ShapeDtypeStruct(xs.shape, F32),
        compiler_params=pltpu.CompilerParams(dimension_semantics=("arbitrary",), vmem_limit_bytes=VMEM_LIMIT),
        name="experts",
    )(te, meta, xs, w_gate, w_up, w_down)


def _combine_kernel(pos_ref, route_ref, x1_ref, g2_ref, gam_ref, bet_ref, ys_hbm, y_ref, buf0, buf1, sem,
                    *, tt, alpha):
    bufs = (buf0, buf1)

    def issue(j, _):
        for k in range(2):
            _row_copy(ys_hbm, pos_ref[k, j], bufs[k], j, sem.at[k]).start(priority=k)
        return 0

    def drain(j, _):
        for k in range(2):
            _row_copy(ys_hbm, pos_ref[k, j], bufs[k], j, sem.at[k]).wait()
        return 0

    lax.fori_loop(0, tt, issue, 0)
    lax.fori_loop(0, tt, drain, 0)
    route = route_ref[...]
    eye = lax.broadcasted_iota(I32, (tt, tt), 0) == lax.broadcasted_iota(I32, (tt, tt), 1)
    as_col = lambda row: jnp.sum(jnp.where(eye, jnp.broadcast_to(row, (tt, tt)), 0.0), axis=1, keepdims=True)
    f = as_col(route[2:3, :]) * _load_row_tiles(buf0, tt) + as_col(route[3:4, :]) * _load_row_tiles(buf1, tt)
    y_ref[...] = _layer_norm_rows(alpha * x1_ref[...] + g2_ref[...] * f, gam_ref[...], bet_ref[...])


def _combine(pos, route_t, col_block0, x1, mod3, tiles_per_group, tt, gam, bet, ys, alpha):
    n = x1.shape[0]
    rows = lambda w: pl.BlockSpec((tt, w), lambda i: (i, 0))
    whole = lambda arr: pl.BlockSpec(arr.shape, lambda i: (0,) * arr.ndim)
    return pl.pallas_call(
        functools.partial(_combine_kernel, tt=tt, alpha=alpha),
        grid=(n // tt,),
        in_specs=[pl.BlockSpec((2, tt), lambda i: (0, i + col_block0), memory_space=pltpu.SMEM),
                  pl.BlockSpec((SUBLANES, tt), lambda i: (0, i + col_block0)),
                  rows(D_MODEL), _mod_spec(mod3, 5, tiles_per_group), whole(gam), whole(bet),
                  pl.BlockSpec(memory_space=pl.ANY)],
        out_specs=rows(D_MODEL),
        out_shape=jax.ShapeDtypeStruct((n, D_MODEL), F32),
        scratch_shapes=[pltpu.VMEM((tt * ROW_CHUNKS, LANES), F32), pltpu.VMEM((tt * ROW_CHUNKS, LANES), F32),
                        pltpu.SemaphoreType.DMA((2,))],
        compiler_params=pltpu.CompilerParams(dimension_semantics=("arbitrary",), vmem_limit_bytes=VMEM_LIMIT),
        name="combine",
    )(pos, route_t, x1, mod3, gam, bet, ys)


def kernel(x_prompt, x_sample, cache_k, cache_v, cache_kidx, page_table, c_prompt, c_sample, w_ada, b_ada, w_in,
           a_ln_g, a_ln_b, w_spatial, b_spatial, w_out, ln1_g, ln1_b, w_group_router, w_expert_router, w_gate,
           w_up, w_down, ln2_g, ln2_b):
    depth = w_ada.shape[0]
    assert depth == 1, "one trunk layer"
    alpha = (2.0 * depth) ** 0.25
    bsz, seq, d = x_prompt.shape
    db, ts, _ = x_sample.shape
    n_pages = page_table.shape[1]
    page = cache_k.shape[2]
    past = n_pages * page
    tile_p = min(PROMPT_TILE, seq)
    assert d == D_MODEL and seq % tile_p == 0 and tile_p % CHUNK == 0 and ts <= SUBLANES and page == LANES
    l = 0

    n_c = bsz + db
    n_c_pad = pl.cdiv(n_c, SUBLANES) * SUBLANES
    c_all = jnp.pad(jnp.concatenate([c_prompt, c_sample], axis=0), ((0, n_c_pad - n_c), (0, 0)))
    mod = _ada(c_all, w_ada[l], b_ada[l][None, :])
    mod_p = mod[:bsz]
    mod_s = mod[bsz:n_c]

    w_in_b = _pad_w_in(w_in[l])
    w_out_b = w_out[l].astype(BF16)
    mavg = _head_avg_matrix()
    gam_a = a_ln_g[l].reshape(1, A_WIDTH)
    bet_a = a_ln_b[l].reshape(1, A_WIDTH)
    w_router = jnp.concatenate([w_group_router[l].T, jnp.zeros((SUBLANES - N_GROUPS, d), F32),
                                w_expert_router[l].reshape(d, N_EXPERTS).T], axis=0)
    ln1 = (ln1_g[l][None, :], ln1_b[l][None, :])
    ln2 = (ln2_g[l][None, :], ln2_b[l][None, :])

    wsp = w_spatial[l].reshape(A_HEADS // 2, 2, CHUNK, CHUNK).transpose(0, 2, 1, 3).reshape(A_HEADS // 2, CHUNK, 2 * CHUNK)
    bsp = jnp.repeat(b_spatial[l].T, HEAD_DIM, axis=1)
    a_p, kt_p, vt32_p, kit_p, kb_p, kiwib_p, vt_p, qt_p, qit_p, wit_p = _front_prompt(
        x_prompt, mod_p.reshape(bsz, 6, d), w_in_b, _rot_tables(np.arange(seq)), mavg, gam_a, bet_a, wsp, bsp, tile_p)
    b_p = _attn_prompt(qit_p, wit_p, qt_p, kiwib_p, kb_p, vt_p)
    n_p = bsz * seq
    x1_p, h2_p, route_p = _mid(x_prompt.reshape(n_p, d), a_p.reshape(n_p, A_WIDTH), b_p.reshape(n_p, B_WIDTH),
                               mod_p.reshape(bsz, 1, 6 * d), seq // tile_p, tile_p, w_out_b, *ln1, w_router, alpha)

    r_s = ts * db
    x_tm = x_sample.transpose(1, 0, 2).reshape(r_s, d)
    rt_s = _rot_tables(np.repeat(past + np.arange(ts), db))
    w_small = w_spatial[l][:, :ts, :ts]
    wl = jnp.repeat(w_small.transpose(1, 2, 0).reshape(ts * ts, A_HEADS), HEAD_DIM, axis=1)
    bl = jnp.repeat(b_spatial[l][:, :ts].T, HEAD_DIM, axis=1)
    a_s, q_s, k_s, v_s, qi_s, kiwi_s, vg_s = _front_sample(x_tm, mod_s, w_in_b, rt_s, mavg, gam_a, bet_a, wl, bl, ts, db)

    def seq_major(a):
        return a.reshape(ts, db, a.shape[-1]).transpose(1, 0, 2)

    def new_t(a, heads):
        a = a.reshape(ts, db, heads, HEAD_DIM).transpose(1, 2, 3, 0)
        return jnp.pad(a, ((0, 0), (0, 0), (0, 0), (0, LANES - ts)))

    ga = min(IDX_SEQS, db)
    assert db % ga == 0 and (ga * ts) % SUBLANES == 0
    qi4 = qi_s.reshape(ts, db, IDX_HEADS, IDX_DIM).transpose(1, 2, 0, 3).reshape(db, IDX_HEADS * ts, IDX_DIM)
    wcol = kiwi_s[:, WI_LANE:WI_LANE + IDX_HEADS].reshape(ts, db, IDX_HEADS).transpose(1, 2, 0).reshape(db, IDX_HEADS * ts, 1)
    qpos = jnp.tile(past + jnp.arange(ts, dtype=I32), db).reshape(db * ts, 1)
    kinew_t = new_t(kiwi_s[:, :IDX_DIM], 1)[:, 0]
    kidx_t = jnp.transpose(cache_kidx[l], (0, 2, 1))
    k_t = jnp.transpose(cache_k[l], (0, 2, 3, 1))
    v_t = jnp.transpose(cache_v[l], (0, 2, 3, 1))
    bias_s = _idx_sample(page_table, qi4, wcol, qpos, kinew_t, kidx_t, ts, ga)
    b_s = _attn_sample(page_table, seq_major(q_s), bias_s, new_t(k_s, B_KV_HEADS), new_t(v_s, B_KV_HEADS), k_t, v_t)
    b_s_tm = b_s.transpose(1, 0, 2).reshape(r_s, B_WIDTH)
    x1_s, h2_s, route_s = _mid(x_tm, a_s, b_s_tm, mod_s.reshape(1, db, 6 * d), ts, db, w_out_b, *ln1, w_router, alpha)

    n_all = n_p + r_s
    tok_p = min(TOKEN_TILE, seq)
    rank_tile = min(RANK_TILE, n_all)
    assert n_all % rank_tile == 0 and n_p % tok_p == 0 and r_s % tok_p == 0 and n_p % db == 0
    route_all = jnp.concatenate([route_p, route_s], axis=1)
    pos, meta, te, n_tiles = _plan(route_all, rank_tile, MOE_TILE)
    xs = _dispatch(meta, pos, h2_p, h2_s, tok_p, MOE_TILE, n_tiles)
    ys = _experts(te, meta, xs, w_gate[l], w_up[l], w_down[l], MOE_TILE, n_tiles)
    y_p = _combine(pos, route_all, 0, x1_p, mod_p.reshape(bsz, 1, 6 * d), seq // tok_p, tok_p, *ln2, ys, alpha)
    y_s_tm = _combine(pos, route_all, n_p // db, x1_s, mod_s.reshape(1, db, 6 * d), ts, db, *ln2, ys, alpha)
    y_s = y_s_tm.reshape(ts, db, d).transpose(1, 0, 2)

    kv5 = lambda a, n, t: a.reshape(1, n, t, B_KV_HEADS, HEAD_DIM)
    kv5_t = lambda a: a.reshape(1, bsz, B_KV_HEADS, HEAD_DIM, seq).transpose(0, 1, 4, 2, 3)
    return (y_p.reshape(bsz, seq, d), y_s,
            kv5_t(kt_p), kv5_t(vt32_p), kit_p.transpose(0, 2, 1)[None],
            kv5(seq_major(k_s), db, ts), kv5(seq_major(v_s), db, ts), seq_major(kiwi_s)[..., :IDX_DIM][None],
            seq_major(vg_s).reshape(1, db, ts, A_HEADS, HEAD_DIM))
```

```python
import functools

import jax
import jax.numpy as jnp
import numpy as np
from jax import lax
from jax.experimental import pallas as pl
from jax.experimental.pallas import tpu as pltpu

F32 = jnp.float32
BF16 = jnp.bfloat16
I32 = jnp.int32

D_MODEL = 1024
HEAD_DIM = 64
A_HEADS = 8
A_WIDTH = A_HEADS * HEAD_DIM
CHUNK = 128
B_HEADS = 8
B_KV_HEADS = 4
B_GROUP = B_HEADS // B_KV_HEADS
B_WIDTH = B_HEADS * HEAD_DIM
KV_WIDTH = B_KV_HEADS * HEAD_DIM
IDX_HEADS = 4
IDX_DIM = 64
IDX_WIDTH = IDX_HEADS * IDX_DIM
TOPK_MAX = 256
ROPE_THETA = 500000.0
ROT_DIM = HEAD_DIM // 4
ROT_HALF = ROT_DIM // 2
ATTN_SCALE = HEAD_DIM ** -0.5
N_GROUPS = 4
EXPERTS_PER_GROUP = 8
N_EXPERTS = N_GROUPS * EXPERTS_PER_GROUP
D_EXPERT = 512
LN_EPS = 1e-5

LANES = 128
SUBLANES = 8
ROW_CHUNKS = D_MODEL // LANES

C_AU, C_AV, C_Q, C_K, C_V, C_QI, C_KI = 0, 512, 1024, 1536, 1792, 2048, 2304
IN_WIDTH = C_KI + IDX_DIM + IDX_HEADS
IN_PAD = 2432
WI_LANE = IDX_DIM

INT_MIN = -(2 ** 31)
INT_MAX = 2 ** 31 - 1
VMEM_LIMIT = 48 * 1024 * 1024

PROMPT_TILE = 256
MOE_TILE = 256
TOKEN_TILE = 256
RANK_TILE = 512
IDX_SEQS = 16
STEPS_PER_CHECK = 4
ROW_DMA_UNROLL = 8


def _dot(a, b):
    return jnp.dot(a, b, preferred_element_type=F32)


def _dot_nt(a, b):
    return lax.dot_general(a, b, (((1,), (1,)), ((), ())), preferred_element_type=F32)


def _split_dot(x, m):
    hi = x.astype(BF16)
    lo = (x - hi.astype(F32)).astype(BF16)
    return _dot(hi, m) + _dot(lo, m)


def _ada_kernel(c_ref, w_ref, b_ref, o_ref):
    s = jax.nn.silu(c_ref[...]).astype(BF16)
    o_ref[...] = _dot(s, w_ref[...].astype(BF16)) + b_ref[...]


def _ada(c_all, w_ada, b_ada):
    rows = c_all.shape[0]
    n_out = w_ada.shape[1]
    tn = 1024
    return pl.pallas_call(
        _ada_kernel,
        grid=(n_out // tn,),
        in_specs=[
            pl.BlockSpec((rows, D_MODEL), lambda j: (0, 0)),
            pl.BlockSpec((D_MODEL, tn), lambda j: (0, j)),
            pl.BlockSpec((1, tn), lambda j: (0, j)),
        ],
        out_specs=pl.BlockSpec((rows, tn), lambda j: (0, j)),
        out_shape=jax.ShapeDtypeStruct((rows, n_out), F32),
        compiler_params=pltpu.CompilerParams(
            dimension_semantics=("arbitrary",), vmem_limit_bytes=VMEM_LIMIT),
        name="ada",
    )(c_all, w_ada, b_ada)


def _rotate(x, rt, blk):
    c = rt[:, 0:LANES]
    s_lo = rt[:, LANES:2 * LANES]
    s_hi = rt[:, 2 * LANES:3 * LANES]
    if blk == 1:
        head = lax.broadcasted_iota(I32, c.shape, 1) < IDX_DIM
        c = jnp.where(head, c, 1.0)
        s_lo = jnp.where(head, s_lo, 0.0)
        s_hi = jnp.where(head, s_hi, 0.0)
    outs = []
    for j in range(x.shape[1] // LANES):
        xb = x[:, j * LANES:(j + 1) * LANES]
        up = pltpu.roll(xb, LANES - ROT_HALF, 1)
        dn = pltpu.roll(xb, ROT_HALF, 1)
        outs.append(xb * c + up * s_lo + dn * s_hi)
    return outs[0] if len(outs) == 1 else jnp.concatenate(outs, axis=1)


def _head_ln(g, mavg, gam, bet):
    def seg_mean(x):
        parts = [_split_dot(x[:, j * 256:(j + 1) * 256], mavg) for j in range(A_WIDTH // 256)]
        return jnp.concatenate(parts, axis=1)
    mu = seg_mean(g)
    d = g - mu
    var = seg_mean(d * d)
    return d * lax.rsqrt(var + LN_EPS) * gam + bet


def _project(h, w_ref, rt, mavg, gam, bet):
    u = jax.nn.gelu(_dot(h, w_ref[:, C_AU:C_AV]))
    vg = _head_ln(jax.nn.gelu(_dot(h, w_ref[:, C_AV:C_Q])), mavg, gam, bet)
    q = _rotate(_dot(h, w_ref[:, C_Q:C_K]), rt, 0) * ATTN_SCALE
    k = _rotate(_dot(h, w_ref[:, C_K:C_V]), rt, 0)
    v = _dot(h, w_ref[:, C_V:C_QI])
    qi = _rotate(_dot(h, w_ref[:, C_QI:C_KI]), rt, 0)
    kiwi = _rotate(_dot(h, w_ref[:, C_KI:IN_PAD]), rt, 1)
    return u, vg, q, k, v, qi, kiwi


def _front_prompt_kernel(x_ref, mod_ref, w_ref, rt_ref, mavg_ref, gam_ref, bet_ref, wsp_ref, bsp_ref,
                         a_ref, kt_ref, vt_ref, kit_ref, kb_ref, kiwib_ref, vtb_ref, qt_ref, qit_ref, wit_ref):
    shift = mod_ref[0:1, :]
    scale = mod_ref[1:2, :]
    h = (x_ref[...] * (1.0 + scale) + shift).astype(BF16)
    u, vg, q, k, v, qi, kiwi = _project(h, w_ref, rt_ref[...], mavg_ref[...], gam_ref[...], bet_ref[...])
    v_t = jnp.transpose(v)
    kiwi_t = jnp.transpose(kiwi)
    kt_ref[...] = jnp.transpose(k)
    vt_ref[...] = v_t
    kit_ref[...] = kiwi_t[0:IDX_DIM, :]
    kb_ref[...] = k.astype(BF16)
    kiwib_ref[...] = kiwi.astype(BF16)
    vtb_ref[...] = v_t.astype(BF16)
    qt_ref[...] = jnp.transpose(q).astype(BF16)
    qit_ref[...] = jnp.transpose(qi).astype(BF16)
    wit_ref[...] = kiwi_t[WI_LANE:WI_LANE + SUBLANES, :]

    rows = lax.broadcasted_iota(I32, (CHUNK, 2 * CHUNK), 0)
    cols = lax.broadcasted_iota(I32, (CHUNK, 2 * CHUNK), 1) % CHUNK
    causal = cols <= rows
    lane = lax.broadcasted_iota(I32, (CHUNK, LANES), 1)
    tt = x_ref.shape[0]
    for cidx in range(tt // CHUNK):
        rs = slice(cidx * CHUNK, (cidx + 1) * CHUNK)
        blocks = []
        for p in range(A_HEADS // 2):
            wcat = jnp.where(causal, wsp_ref[p], 0.0).astype(BF16)
            vb = vg[rs, p * LANES:(p + 1) * LANES]
            rhs = jnp.concatenate([jnp.where(lane < HEAD_DIM, vb, 0.0),
                                   jnp.where(lane >= HEAD_DIM, vb, 0.0)], axis=0).astype(BF16)
            blocks.append(_dot(wcat, rhs))
        s = jnp.concatenate(blocks, axis=1) + bsp_ref[...]
        a_ref[rs, :] = (u[rs, :] * s).astype(BF16)


def _front_prompt(x, mod, w_in, rt, mavg, gam, bet, wsp, bsp, tt):
    b, t, _ = x.shape
    nc = t // tt
    tok = lambda w: pl.BlockSpec((None, tt, w), lambda i, j: (i, j, 0))
    tr = lambda r: pl.BlockSpec((None, None, r, tt), lambda i, j: (i, j, 0, 0))
    pos_minor = lambda r: pl.BlockSpec((None, r, tt), lambda i, j: (i, 0, j))
    const2 = lambda a: pl.BlockSpec(a.shape, lambda i, j: (0,) * a.ndim)
    out_shapes = (
        jax.ShapeDtypeStruct((b, t, A_WIDTH), BF16),
        jax.ShapeDtypeStruct((b, KV_WIDTH, t), F32),
        jax.ShapeDtypeStruct((b, KV_WIDTH, t), F32),
        jax.ShapeDtypeStruct((b, IDX_DIM, t), F32),
        jax.ShapeDtypeStruct((b, t, KV_WIDTH), BF16),
        jax.ShapeDtypeStruct((b, t, LANES), BF16),
        jax.ShapeDtypeStruct((b, nc, KV_WIDTH, tt), BF16),
        jax.ShapeDtypeStruct((b, nc, B_WIDTH, tt), BF16),
        jax.ShapeDtypeStruct((b, nc, IDX_WIDTH, tt), BF16),
        jax.ShapeDtypeStruct((b, nc, SUBLANES, tt), F32),
    )
    return pl.pallas_call(
        _front_prompt_kernel,
        grid=(b, nc),
        in_specs=[
            tok(D_MODEL),
            pl.BlockSpec((None, 6, D_MODEL), lambda i, j: (i, 0, 0)),
            const2(w_in),
            pl.BlockSpec((tt, 3 * LANES), lambda i, j: (j, 0)),
            const2(mavg), const2(gam), const2(bet), const2(wsp), const2(bsp),
        ],
        out_specs=[tok(A_WIDTH), pos_minor(KV_WIDTH), pos_minor(KV_WIDTH), pos_minor(IDX_DIM), tok(KV_WIDTH),
                   tok(LANES), tr(KV_WIDTH), tr(B_WIDTH), tr(IDX_WIDTH), tr(SUBLANES)],
        out_shape=out_shapes,
        compiler_params=pltpu.CompilerParams(
            dimension_semantics=("arbitrary", "arbitrary"), vmem_limit_bytes=VMEM_LIMIT),
        name="front_prompt",
    )(x, mod, w_in, rt, mavg, gam, bet, wsp, bsp)


def _order_key(score, kpos, idx_bits):
    offs = 1 << idx_bits
    bits = pltpu.bitcast(score, I32)
    key = bits ^ ((bits >> 31) & 0x7FFFFFFF)
    key = key + jnp.where(score > 0.0, offs, 0)
    return jnp.where(score == 0.0, offs - kpos, key)


def _select_threshold(count_ge, count_gt_eq_lt, shape, topk, idx_bits):
    offs = 1 << idx_bits
    first_candidates = (offs + 1, 1)
    max_steps = -(-(32 + len(first_candidates)) // STEPS_PER_CHECK) * STEPS_PER_CHECK

    def unresolved(carry):
        i, lo, hi, n_lo = carry
        pending = (n_lo != topk) & (hi - 1 > lo)
        return (i < max_steps) & (jnp.max(pending.astype(F32)) > 0.0)

    def bisect(carry):
        i, lo, hi, n_lo = carry
        for _ in range(STEPS_PER_CHECK):
            mid = (lo >> 1) + (hi >> 1) + (lo & hi & 1)
            for step, value in enumerate(first_candidates):
                forced = jnp.where(i == step, value, INT_MIN)
                mid = jnp.where((lo < forced) & (forced < hi), forced, mid)
            tot = count_ge(mid)
            take = tot >= topk
            lo = jnp.where(take, mid, lo)
            n_lo = jnp.where(take, tot, n_lo)
            hi = jnp.where(take, hi, mid)
            i = i + 1
        return i, lo, hi, n_lo

    never = jnp.full(shape, INT_MAX, I32)
    _, thr, _, n_lo = lax.while_loop(
        unresolved, bisect, (jnp.int32(0), jnp.full(shape, INT_MIN + 1, I32), jnp.full(shape, INT_MAX, I32), never))
    tied = (n_lo > topk) & (n_lo != INT_MAX)
    big = jnp.full(shape, offs, I32)

    def resolve_ties(_):
        n_gt, _ = count_gt_eq_lt(thr, big)
        need = topk - n_gt

        def idx_step(i, cut):
            cand = cut | lax.shift_left(jnp.int32(1), idx_bits - 1 - i)
            _, n_eq = count_gt_eq_lt(thr, cand)
            return jnp.where(n_eq <= need, cand, cut)

        cut = lax.fori_loop(0, idx_bits, idx_step, jnp.zeros(shape, I32))
        return jnp.where(tied, cut, big)

    any_tied = jnp.max(tied.astype(F32)) > 0.0
    cut = lax.cond(any_tied, resolve_ties, lambda _: big, 0)
    return thr, cut


def _fold_rows(x):
    acc = x[0:SUBLANES]
    for r in range(1, x.shape[0] // SUBLANES):
        acc = acc + x[r * SUBLANES:(r + 1) * SUBLANES]
    return acc


def _col_total(cnt8):
    return jnp.sum(cnt8.astype(F32), axis=0, keepdims=True).astype(I32)


def _attn_prompt_kernel(qit_ref, wit_ref, qt_ref, kiwib_ref, kb_ref, vt_ref, o_ref,
                        key_ref, w4_ref, wq_ref, s0_ref, s1_ref, m_ref, l_ref, acc_ref, *, tq, topk, idx_bits):
    j = pl.program_id(1)
    kc = tq
    n_kc = j + 1

    @pl.when((pl.program_id(0) == 0) & (j == 0))
    def _():
        w4_ref[...] = jnp.zeros(w4_ref.shape, BF16)
        wq_ref[...] = jnp.zeros(wq_ref.shape, BF16)

    for h in range(IDX_HEADS):
        w4_ref[0:IDX_DIM, h * tq:(h + 1) * tq] = qit_ref[h * IDX_DIM:(h + 1) * IDX_DIM, :]
    for h in range(B_HEADS):
        n = h // B_GROUP
        wq_ref[h, n * HEAD_DIM:(n + 1) * HEAD_DIM, :] = qt_ref[h * HEAD_DIM:(h + 1) * HEAD_DIM, :]

    wit = wit_ref[...]
    krow = lax.broadcasted_iota(I32, (kc, tq), 0)
    qpos = j * tq + lax.broadcasted_iota(I32, (kc, tq), 1)

    def score_chunk(c, _):
        k0 = pl.multiple_of(c * kc, kc)
        s = _dot(kiwib_ref[pl.ds(k0, kc), :], w4_ref[...])
        tot = jnp.maximum(s[:, 0:tq], 0.0) * wit[0:1, :]
        for h in range(1, IDX_HEADS):
            tot = tot + jnp.maximum(s[:, h * tq:(h + 1) * tq], 0.0) * wit[h:h + 1, :]
        kpos = k0 + krow
        key_ref[c] = jnp.where(kpos <= qpos, _order_key(tot, kpos, idx_bits), INT_MIN)
        return 0

    lax.fori_loop(0, n_kc, score_chunk, 0)

    def count_ge(cand):
        def body(c, cnt):
            return cnt + _fold_rows(jnp.where(key_ref[c] >= cand, 1, 0))
        return _col_total(lax.fori_loop(0, n_kc, body, jnp.zeros((SUBLANES, tq), I32)))

    def count_gt_eq_lt(thr, pos):
        def body(c, carry):
            n_gt, n_eq = carry
            key = key_ref[c]
            n_gt = n_gt + _fold_rows(jnp.where(key > thr, 1, 0))
            n_eq = n_eq + _fold_rows(jnp.where((key == thr) & (c * kc + krow < pos), 1, 0))
            return n_gt, n_eq
        z = jnp.zeros((SUBLANES, tq), I32)
        n_gt, n_eq = lax.fori_loop(0, n_kc, body, (z, z))
        return _col_total(n_gt), _col_total(n_eq)

    thr, cut = _select_threshold(count_ge, count_gt_eq_lt, (1, tq), topk, idx_bits)

    m_ref[...] = jnp.full(m_ref.shape, jnp.finfo(F32).min, F32)
    l_ref[...] = jnp.zeros(l_ref.shape, F32)
    acc_ref[...] = jnp.zeros(acc_ref.shape, F32)

    def key_block(c):
        return kb_ref[pl.ds(pl.multiple_of(c * kc, kc), kc), :]

    kblk0 = key_block(0)
    for h in range(B_HEADS):
        s0_ref[h] = _dot(kblk0, wq_ref[h])

    def stage(c, open_bias, src, dst, c_next):
        key = key_ref[c]
        sel = (key > thr) | ((key == thr) & (c * kc + krow < cut))
        bias = jnp.where(sel, open_bias, -jnp.inf)
        kblk_next = key_block(c_next)
        vtc = vt_ref[c]
        for h in range(B_HEADS):
            n = h // B_GROUP
            dst[h] = _dot(kblk_next, wq_ref[h])
            s = src[h] + bias
            m_old = m_ref[h]
            m_new = jnp.maximum(m_old, jnp.max(s, axis=0, keepdims=True))
            alpha = jnp.exp(m_old - m_new)
            p = jnp.exp(s - m_new)
            l_ref[h] = alpha * l_ref[h] + jnp.sum(p, axis=0, keepdims=True)
            acc_ref[h] = alpha * acc_ref[h] + _dot(vtc[n * HEAD_DIM:(n + 1) * HEAD_DIM, :], p.astype(BF16))
            m_ref[h] = m_new

    last = n_kc - 1

    def attend_pair(i, _):
        c0 = 2 * i
        c1 = jnp.minimum(c0 + 1, last)
        stage(c0, 0.0, s0_ref, s1_ref, c1)
        stage(c1, jnp.where(c0 + 1 <= last, 0.0, -jnp.inf), s1_ref, s0_ref, jnp.minimum(c0 + 2, last))
        return 0

    lax.fori_loop(0, (n_kc + 1) // 2, attend_pair, 0)
    out_t = jnp.concatenate([acc_ref[h] / l_ref[h] for h in range(B_HEADS)], axis=0)
    o_ref[...] = jnp.transpose(out_t).astype(BF16)


def _attn_prompt(qit, wit, qt, kiwib, kb, vt):
    b, nc, _, tq = qt.shape
    t = nc * tq
    topk = min(TOPK_MAX, t // 4)
    idx_bits = max(1, (t - 1).bit_length())
    tr = lambda r: pl.BlockSpec((None, None, r, tq), lambda i, j: (i, j, 0, 0))
    full = lambda w: pl.BlockSpec((None, t, w), lambda i, j: (i, 0, 0))
    kernel = functools.partial(_attn_prompt_kernel, tq=tq, topk=topk, idx_bits=idx_bits)
    return pl.pallas_call(
        kernel,
        grid=(b, nc),
        in_specs=[tr(IDX_WIDTH), tr(SUBLANES), tr(B_WIDTH), full(LANES), full(KV_WIDTH),
                  pl.BlockSpec((None, nc, KV_WIDTH, tq), lambda i, j: (i, 0, 0, 0))],
        out_specs=pl.BlockSpec((None, tq, B_WIDTH), lambda i, j: (i, j, 0)),
        out_shape=jax.ShapeDtypeStruct((b, t, B_WIDTH), BF16),
        scratch_shapes=[
            pltpu.VMEM((nc, tq, tq), I32),
            pltpu.VMEM((LANES, IDX_HEADS * tq), BF16),
            pltpu.VMEM((B_HEADS, KV_WIDTH, tq), BF16),
            pltpu.VMEM((B_HEADS, tq, tq), F32),
            pltpu.VMEM((B_HEADS, tq, tq), F32),
            pltpu.VMEM((B_HEADS, 1, tq), F32),
            pltpu.VMEM((B_HEADS, 1, tq), F32),
            pltpu.VMEM((B_HEADS, HEAD_DIM, tq), F32),
        ],
        compiler_params=pltpu.CompilerParams(
            dimension_semantics=("arbitrary", "arbitrary"), vmem_limit_bytes=VMEM_LIMIT),
        name="attn_prompt",
    )(qit, wit, qt, kiwib, kb, vt)


def _rot_tables(pos):
    r = pos.shape[0]
    inv_freq = np.float32(ROPE_THETA) ** (-np.arange(ROT_HALF, dtype=np.float32) * np.float32(2.0) / np.float32(ROT_DIM))
    ang = pos.astype(np.float32)[:, None] * inv_freq[None, :]
    cos, sin = np.cos(ang), np.sin(ang)
    rest = HEAD_DIM - ROT_DIM
    c64 = np.concatenate([cos, cos, np.ones((r, rest), np.float32)], axis=1)
    lo64 = np.concatenate([-sin, np.zeros((r, HEAD_DIM - ROT_HALF), np.float32)], axis=1)
    hi64 = np.concatenate([np.zeros((r, ROT_HALF), np.float32), sin, np.zeros((r, rest), np.float32)], axis=1)
    return jnp.asarray(np.concatenate([c64, c64, lo64, lo64, hi64, hi64], axis=1).astype(np.float32))


def _head_avg_matrix():
    return jnp.kron(jnp.eye(256 // HEAD_DIM, dtype=F32), jnp.full((HEAD_DIM, HEAD_DIM), 1.0 / HEAD_DIM, F32)).astype(BF16)


def _pad_w_in(w_in):
    return jnp.pad(w_in, ((0, 0), (0, IN_PAD - IN_WIDTH))).astype(BF16)


def _front_sample_kernel(x_ref, shift_ref, scale_ref, w_ref, rt_ref, mavg_ref, gam_ref, bet_ref, wl_ref, bl_ref,
                         a_ref, q_ref, k_ref, v_ref, qi_ref, kiwi_ref, vg_ref, *, ts, db):
    one_scale = 1.0 + scale_ref[...]
    shift = shift_ref[...]
    h = jnp.concatenate([x_ref[t * db:(t + 1) * db, :] * one_scale + shift for t in range(ts)], axis=0).astype(BF16)
    u, vg, q, k, v, qi, kiwi = _project(h, w_ref, rt_ref[...], mavg_ref[...], gam_ref[...], bet_ref[...])
    q_ref[...] = q.astype(BF16)
    k_ref[...] = k
    v_ref[...] = v
    qi_ref[...] = qi.astype(BF16)
    kiwi_ref[...] = kiwi
    vg_ref[...] = vg
    for t in range(ts):
        s = bl_ref[t:t + 1, :]
        for src in range(t + 1):
            s = s + wl_ref[t * ts + src:t * ts + src + 1, :] * vg[src * db:(src + 1) * db, :]
        a_ref[t * db:(t + 1) * db, :] = (u[t * db:(t + 1) * db, :] * s).astype(BF16)


def _front_sample(x_tm, mod_s, w_in, rt, mavg, gam, bet, wl, bl, ts, db):
    r = ts * db
    whole = lambda a: pl.BlockSpec(a.shape, lambda i: (0,) * a.ndim)
    out = lambda w, dt: jax.ShapeDtypeStruct((r, w), dt)
    outs = (out(A_WIDTH, BF16), out(B_WIDTH, BF16), out(KV_WIDTH, F32), out(KV_WIDTH, F32),
            out(IDX_WIDTH, BF16), out(LANES, F32), out(A_WIDTH, F32))
    return pl.pallas_call(
        functools.partial(_front_sample_kernel, ts=ts, db=db),
        grid=(1,),
        in_specs=[
            whole(x_tm),
            pl.BlockSpec((db, D_MODEL), lambda i: (0, 0)),
            pl.BlockSpec((db, D_MODEL), lambda i: (0, 1)),
            whole(w_in), whole(rt), whole(mavg), whole(gam), whole(bet), whole(wl), whole(bl),
        ],
        out_specs=[pl.BlockSpec((r, s.shape[1]), lambda i: (0, 0)) for s in outs],
        out_shape=outs,
        compiler_params=pltpu.CompilerParams(dimension_semantics=("arbitrary",), vmem_limit_bytes=VMEM_LIMIT),
        name="front_sample",
    )(x_tm, mod_s, mod_s, w_in, rt, mavg, gam, bet, wl, bl)


def _idx_sample_kernel(pt_ref, qi4_ref, wcol_ref, qpos_ref, kinew_ref, kidx_hbm, bias_ref,
                       ki_buf, key_ref, sem, *, ga, ts, n_pages, page, lpad, topk, idx_bits):
    i = pl.program_id(0)
    n_steps = pl.num_programs(0)
    past = n_pages * page
    rows = ga * ts
    slot = i % 2

    def page_copy(step, to_slot, g, p):
        phys = pt_ref[step * ga + g, p]
        return pltpu.make_async_copy(kidx_hbm.at[phys], ki_buf.at[to_slot, g, :, pl.ds(p * page, page)],
                                     sem.at[to_slot])

    def start_all(step, to_slot):
        for g in range(ga):
            for p in range(n_pages):
                page_copy(step, to_slot, g, p).start()

    @pl.when(i == 0)
    def _():
        start_all(0, 0)

    @pl.when(i + 1 < n_steps)
    def _():
        start_all(i + 1, 1 - slot)

    ki_buf[slot, :, :, pl.ds(past, LANES)] = kinew_ref[...]
    for g in range(ga):
        for p in range(n_pages):
            page_copy(i, slot, g, p).wait()

    kpos = lax.broadcasted_iota(I32, (ts, lpad), 1)
    for g in range(ga):
        s = _dot(qi4_ref[g], ki_buf[slot, g].astype(BF16))
        r = jnp.maximum(s, 0.0) * wcol_ref[g]
        tot = r[0:ts]
        for h in range(1, IDX_HEADS):
            tot = tot + r[h * ts:(h + 1) * ts]
        adm = (kpos <= qpos_ref[g * ts:(g + 1) * ts, :]) & (kpos < past + ts)
        key_ref[g * ts:(g + 1) * ts, :] = jnp.where(adm, _order_key(tot, kpos, idx_bits), INT_MIN)

    sub = lpad // LANES
    lane128 = lax.broadcasted_iota(I32, (rows, LANES), 1)

    def row_total(cnt):
        tot = jnp.sum(cnt.astype(F32), axis=1, keepdims=True)
        return jnp.broadcast_to(tot, cnt.shape).astype(I32)

    def count_ge(cand):
        cnt = jnp.zeros((rows, LANES), I32)
        for s_ in range(sub):
            cnt = cnt + jnp.where(key_ref[:, s_ * LANES:(s_ + 1) * LANES] >= cand, 1, 0)
        return row_total(cnt)

    def count_gt_eq_lt(thr, pos):
        n_gt = jnp.zeros((rows, LANES), I32)
        n_eq = jnp.zeros((rows, LANES), I32)
        for s_ in range(sub):
            kk = key_ref[:, s_ * LANES:(s_ + 1) * LANES]
            n_gt = n_gt + jnp.where(kk > thr, 1, 0)
            n_eq = n_eq + jnp.where((kk == thr) & (s_ * LANES + lane128 < pos), 1, 0)
        return row_total(n_gt), row_total(n_eq)

    thr, cut = _select_threshold(count_ge, count_gt_eq_lt, (rows, LANES), topk, idx_bits)
    for s_ in range(sub):
        kk = key_ref[:, s_ * LANES:(s_ + 1) * LANES]
        sel = (kk > thr) | ((kk == thr) & (s_ * LANES + lane128 < cut))
        bias = jnp.where(sel, 0.0, -jnp.inf)
        for g in range(ga):
            bias_ref[g, :, s_ * LANES:(s_ + 1) * LANES] = bias[g * ts:(g + 1) * ts]


def _idx_sample(page_table, qi4, wcol, qpos, kinew_t, kidx_t, ts, ga):
    db = qi4.shape[0]
    n_pages = page_table.shape[1]
    page = kidx_t.shape[2]
    past = n_pages * page
    lpad = past + LANES
    topk = min(TOPK_MAX, (past + ts) // 4)
    idx_bits = max(1, (lpad - 1).bit_length())
    kernel = functools.partial(_idx_sample_kernel, ga=ga, ts=ts, n_pages=n_pages, page=page, lpad=lpad,
                               topk=topk, idx_bits=idx_bits)
    return pl.pallas_call(
        kernel,
        grid_spec=pltpu.PrefetchScalarGridSpec(
            num_scalar_prefetch=1,
            grid=(db // ga,),
            in_specs=[pl.BlockSpec((ga, IDX_HEADS * ts, IDX_DIM), lambda i, pt: (i, 0, 0)),
                      pl.BlockSpec((ga, IDX_HEADS * ts, 1), lambda i, pt: (i, 0, 0)),
                      pl.BlockSpec((ga * ts, 1), lambda i, pt: (i, 0)),
                      pl.BlockSpec((ga, IDX_DIM, LANES), lambda i, pt: (i, 0, 0)),
                      pl.BlockSpec(memory_space=pl.ANY)],
            out_specs=pl.BlockSpec((ga, ts, lpad), lambda i, pt: (i, 0, 0)),
            scratch_shapes=[
                pltpu.VMEM((2, ga, IDX_DIM, lpad), F32),
                pltpu.VMEM((ga * ts, lpad), I32),
                pltpu.SemaphoreType.DMA((2,)),
            ]),
        out_shape=jax.ShapeDtypeStruct((db, ts, lpad), F32),
        compiler_params=pltpu.CompilerParams(dimension_semantics=("arbitrary",), vmem_limit_bytes=VMEM_LIMIT),
        name="idx_sample",
    )(page_table, qi4, wcol, qpos, kinew_t, kidx_t)


def _attn_sample_kernel(pt_ref, q_ref, bias_ref, knew_ref, vnew_ref, k_hbm, v_hbm, o_ref,
                        k_buf, v_buf, sem, *, ts, n_pages, page):
    b = pl.program_id(0)
    nb = pl.num_programs(0)
    past = n_pages * page
    slot = b % 2

    def page_copies(seq, to_slot, p):
        phys = pt_ref[seq, p]
        dst = pl.ds(p * page, page)
        return (pltpu.make_async_copy(k_hbm.at[phys], k_buf.at[to_slot, :, :, dst], sem.at[0, to_slot]),
                pltpu.make_async_copy(v_hbm.at[phys], v_buf.at[to_slot, :, :, dst], sem.at[1, to_slot]))

    def start_all(seq, to_slot):
        for p in range(n_pages):
            for cp in page_copies(seq, to_slot, p):
                cp.start()

    @pl.when(b == 0)
    def _():
        start_all(0, 0)

    @pl.when(b + 1 < nb)
    def _():
        start_all(b + 1, 1 - slot)

    k_buf[slot, :, :, pl.ds(past, LANES)] = knew_ref[...]
    v_buf[slot, :, :, pl.ds(past, LANES)] = vnew_ref[...]
    for p in range(n_pages):
        for cp in page_copies(b, slot, p):
            cp.wait()

    q = q_ref[...]
    bias = bias_ref[...]
    bias2 = jnp.concatenate([bias] * B_GROUP, axis=0)
    outs = [None] * B_HEADS
    for n in range(B_KV_HEADS):
        kt = k_buf[slot, n].astype(BF16)
        vt = v_buf[slot, n].astype(BF16)
        qs = jnp.concatenate([q[:, (n * B_GROUP + g) * HEAD_DIM:(n * B_GROUP + g + 1) * HEAD_DIM]
                              for g in range(B_GROUP)], axis=0)
        sc = _dot(qs, kt) + bias2
        m = jnp.max(sc, axis=1, keepdims=True)
        p_ = jnp.exp(sc - m)
        o = _dot_nt(p_.astype(BF16), vt) / jnp.sum(p_, axis=1, keepdims=True)
        for g in range(B_GROUP):
            outs[n * B_GROUP + g] = o[g * ts:(g + 1) * ts]
    o_ref[...] = jnp.concatenate(outs, axis=1).astype(BF16)


def _attn_sample(page_table, q, bias, knew_t, vnew_t, k_t, v_t):
    db, ts, _ = q.shape
    n_pages = page_table.shape[1]
    page = k_t.shape[3]
    lpad = bias.shape[2]
    seq3 = lambda a: pl.BlockSpec((None,) + a.shape[1:], lambda b, pt: (b,) + (0,) * (a.ndim - 1))
    anyspec = pl.BlockSpec(memory_space=pl.ANY)
    kernel = functools.partial(_attn_sample_kernel, ts=ts, n_pages=n_pages, page=page)
    return pl.pallas_call(
        kernel,
        grid_spec=pltpu.PrefetchScalarGridSpec(
            num_scalar_prefetch=1,
            grid=(db,),
            in_specs=[seq3(q), seq3(bias), seq3(knew_t), seq3(vnew_t), anyspec, anyspec],
            out_specs=pl.BlockSpec((None, ts, B_WIDTH), lambda b, pt: (b, 0, 0)),
            scratch_shapes=[
                pltpu.VMEM((2, B_KV_HEADS, HEAD_DIM, lpad), F32),
                pltpu.VMEM((2, B_KV_HEADS, HEAD_DIM, lpad), F32),
                pltpu.SemaphoreType.DMA((2, 2)),
            ]),
        out_shape=jax.ShapeDtypeStruct((db, ts, B_WIDTH), BF16),
        compiler_params=pltpu.CompilerParams(dimension_semantics=("arbitrary",), vmem_limit_bytes=VMEM_LIMIT),
        name="attn_sample",
    )(page_table, q, bias, knew_t, vnew_t, k_t, v_t)


def _layer_norm_rows(y, gam, bet):
    mu = jnp.mean(y, axis=1, keepdims=True)
    d = y - mu
    var = jnp.mean(d * d, axis=1, keepdims=True)
    return d * lax.rsqrt(var + LN_EPS) * gam + bet


def _store_row_tiles(ref, val):
    r = val.shape[0]
    for c in range(ROW_CHUNKS):
        ref[pl.ds(c, r, stride=ROW_CHUNKS), :] = val[:, c * LANES:(c + 1) * LANES]


def _load_row_tiles(ref, r):
    return jnp.concatenate([ref[pl.ds(c, r, stride=ROW_CHUNKS), :] for c in range(ROW_CHUNKS)], axis=1)


ROUTER_ROWS = SUBLANES + N_EXPERTS


def _route(logits_t):
    r = logits_t.shape[1]
    far = float(LANES)

    def softmax_rows(x):
        e = jnp.exp(x - jnp.max(x, axis=0, keepdims=True))
        return e / jnp.sum(e, axis=0, keepdims=True)

    def first_max(p):
        rows = lax.broadcasted_iota(I32, p.shape, 0).astype(F32)
        best = jnp.max(p, axis=0, keepdims=True)
        return best, jnp.min(jnp.where(p == best, rows, far), axis=0, keepdims=True), rows

    g_w, g_sel, _ = first_max(softmax_rows(logits_t[0:N_GROUPS]))
    el = jnp.zeros((EXPERTS_PER_GROUP, r), F32)
    for g in range(N_GROUPS):
        lo = SUBLANES + g * EXPERTS_PER_GROUP
        el = jnp.where(g_sel == float(g), logits_t[lo:lo + EXPERTS_PER_GROUP], el)
    ep = softmax_rows(el)
    p1, i1, rows = first_max(ep)
    p2, i2, _ = first_max(jnp.where(rows == i1, -1.0, ep))
    denom = p1 + p2
    base = g_sel * float(EXPERTS_PER_GROUP)
    return jnp.concatenate([base + i1, base + i2, g_w * p1 / denom, g_w * p2 / denom,
                            jnp.zeros((SUBLANES - 4, r), F32)], axis=0)


def _split_bf16(x):
    hi = x.astype(BF16)
    return hi, (x - hi.astype(F32)).astype(BF16)


def _mid_kernel(x_ref, a_ref, b_ref, g1_ref, sh2_ref, sc2_ref, wo_ref, gam_ref, bet_ref, wr_ref,
                x1_ref, h2_ref, route_ref, *, alpha):
    mixed = _dot(a_ref[...], wo_ref[0:A_WIDTH, :]) + _dot(b_ref[...], wo_ref[A_WIDTH:A_WIDTH + B_WIDTH, :])
    x1 = _layer_norm_rows(alpha * x_ref[...] + g1_ref[...] * mixed, gam_ref[...], bet_ref[...])
    x1_ref[...] = x1
    h2 = x1 * (1.0 + sc2_ref[...]) + sh2_ref[...]
    _store_row_tiles(h2_ref, h2)
    w_hi, w_lo = _split_bf16(wr_ref[...])
    h_hi, h_lo = _split_bf16(h2)
    logits_t = _dot_nt(w_hi, h_hi) + _dot_nt(w_hi, h_lo) + _dot_nt(w_lo, h_hi)
    route_ref[...] = _route(logits_t)


def _mod_spec(mod3, comp, tiles_per_group):
    rm = mod3.shape[1]
    return pl.BlockSpec((None, rm, D_MODEL), lambda i: (i // tiles_per_group, 0, comp))


def _mid(x, a, b, mod3, tiles_per_group, tile, w_out, gam, bet, w_router, alpha):
    n = x.shape[0]
    rows = lambda w: pl.BlockSpec((tile, w), lambda i: (i, 0))
    whole = lambda arr: pl.BlockSpec(arr.shape, lambda i: (0,) * arr.ndim)
    return pl.pallas_call(
        functools.partial(_mid_kernel, alpha=alpha),
        grid=(n // tile,),
        in_specs=[rows(D_MODEL), rows(A_WIDTH), rows(B_WIDTH),
                  _mod_spec(mod3, 2, tiles_per_group), _mod_spec(mod3, 3, tiles_per_group),
                  _mod_spec(mod3, 4, tiles_per_group),
                  whole(w_out), whole(gam), whole(bet), whole(w_router)],
        out_specs=[rows(D_MODEL), pl.BlockSpec((tile * ROW_CHUNKS, LANES), lambda i: (i, 0)),
                   pl.BlockSpec((SUBLANES, tile), lambda i: (0, i))],
        out_shape=(jax.ShapeDtypeStruct((n, D_MODEL), F32),
                   jax.ShapeDtypeStruct((n * ROW_CHUNKS, LANES), F32),
                   jax.ShapeDtypeStruct((SUBLANES, n), F32)),
        compiler_params=pltpu.CompilerParams(dimension_semantics=("arbitrary",), vmem_limit_bytes=VMEM_LIMIT),
        name="mid",
    )(x, a, b, mod3, mod3, mod3, w_out, gam, bet, w_router)


META_CNT, META_START, META_END, META_NACT = 0, 1, 2, 3


def _plan_kernel(route_ref, pos_ref, meta_ref, te_ref, carry_ref, starts_ref, *, nt, tm):
    phase = pl.program_id(0)
    i = pl.program_id(1)

    @pl.when(i == 0)
    def _():
        carry_ref[...] = jnp.zeros(carry_ref.shape, F32)

    eid = route_ref[0:2, :].astype(I32)
    e_iota = lax.broadcasted_iota(I32, (N_EXPERTS, nt), 0)
    hit0 = eid[0:1, :] == e_iota
    hit1 = eid[1:2, :] == e_iota
    onehot = jnp.where(hit0 | hit1, 1.0, 0.0)

    @pl.when(phase == 0)
    def _():
        pos_ref[...] = jnp.zeros(pos_ref.shape, I32)
        carry_ref[...] = carry_ref[...] + jnp.sum(onehot, axis=1, keepdims=True)

        @pl.when(i == pl.num_programs(1) - 1)
        def _():
            cnt = carry_ref[...]
            tiles = jnp.floor((cnt + float(tm - 1)) * (1.0 / tm))
            r = lax.broadcasted_iota(I32, (N_EXPERTS, N_EXPERTS), 0)
            c = lax.broadcasted_iota(I32, (N_EXPERTS, N_EXPERTS), 1)
            lower = jnp.where(r > c, 1.0, 0.0).astype(BF16)
            tiles_before = _dot(lower, jnp.broadcast_to(tiles, (N_EXPERTS, LANES)).astype(BF16))[:, 0:1]
            tiles_end = tiles_before + tiles
            starts_ref[...] = tiles_before * float(tm)

            diag = (lax.broadcasted_iota(I32, (N_EXPERTS, LANES), 0)
                    == lax.broadcasted_iota(I32, (N_EXPERTS, LANES), 1))

            def as_row(col):
                return jnp.sum(jnp.where(diag, jnp.broadcast_to(col, (N_EXPERTS, LANES)), 0.0),
                               axis=0, keepdims=True)

            nact = jnp.broadcast_to(jnp.max(tiles_end, axis=0, keepdims=True), (1, LANES))
            meta_ref[...] = jnp.concatenate(
                [as_row(cnt), as_row(tiles_before * float(tm)), as_row(tiles_end * float(tm)), nact,
                 jnp.zeros((SUBLANES - 4, LANES), F32)], axis=0).astype(I32)
            tile_i = lax.broadcasted_iota(I32, (N_EXPERTS, te_ref.shape[1]), 1).astype(F32)
            owner = jnp.sum(jnp.where(tiles_end <= tile_i, 1.0, 0.0), axis=0, keepdims=True)
            te_ref[...] = jnp.minimum(owner, float(N_EXPERTS - 1)).astype(I32)

    @pl.when(phase == 1)
    def _():
        upper = (lax.broadcasted_iota(I32, (nt, nt), 0) <= lax.broadcasted_iota(I32, (nt, nt), 1))
        incl = _dot(onehot.astype(BF16), jnp.where(upper, 1.0, 0.0).astype(BF16))
        slot = starts_ref[...] + carry_ref[...] + incl - 1.0
        pos_ref[0:1, :] = jnp.sum(jnp.where(hit0, slot, 0.0), axis=0, keepdims=True).astype(I32)
        pos_ref[1:2, :] = jnp.sum(jnp.where(hit1, slot, 0.0), axis=0, keepdims=True).astype(I32)
        carry_ref[...] = carry_ref[...] + jnp.sum(onehot, axis=1, keepdims=True)


def _plan(route_t, nt, tm):
    n = route_t.shape[1]
    n_tiles = (2 * n) // tm + N_EXPERTS
    te_width = pl.cdiv(n_tiles, LANES) * LANES
    pos, meta, te = pl.pallas_call(
        functools.partial(_plan_kernel, nt=nt, tm=tm),
        grid=(2, n // nt),
        in_specs=[pl.BlockSpec((SUBLANES, nt), lambda p, i: (0, i))],
        out_specs=[pl.BlockSpec((2, nt), lambda p, i: (0, i * p)),
                   pl.BlockSpec((SUBLANES, LANES), lambda p, i: (0, 0)),
                   pl.BlockSpec((1, te_width), lambda p, i: (0, 0))],
        out_shape=(jax.ShapeDtypeStruct((2, n), I32), jax.ShapeDtypeStruct((SUBLANES, LANES), I32),
                   jax.ShapeDtypeStruct((1, te_width), I32)),
        scratch_shapes=[pltpu.VMEM((N_EXPERTS, 1), F32), pltpu.VMEM((N_EXPERTS, 1), F32)],
        compiler_params=pltpu.CompilerParams(dimension_semantics=("arbitrary", "arbitrary")),
        name="plan",
    )(route_t)
    return pos, meta, te, n_tiles


def _row_copy(src, src_row, dst, dst_row, sem):
    return pltpu.make_async_copy(src.at[pl.ds(src_row * ROW_CHUNKS, ROW_CHUNKS)],
                                 dst.at[pl.ds(dst_row * ROW_CHUNKS, ROW_CHUNKS)], sem)


def _dispatch_kernel(meta_ref, pos_ref, hp_ref, hs_ref, xs_out, zero_ref, sem, *, tt, blocks_p, tm, n_tiles):
    i = pl.program_id(0)

    def scatter(h_ref):
        def issue(j, _):
            for k in range(2):
                _row_copy(h_ref, j, xs_out, pos_ref[k, j], sem.at[k]).start()
            return 0

        def drain(j, _):
            for k in range(2):
                _row_copy(h_ref, j, xs_out, pos_ref[k, j], sem.at[k]).wait()
            return 0

        lax.fori_loop(0, tt, issue, 0, unroll=ROW_DMA_UNROLL)
        lax.fori_loop(0, tt, drain, 0, unroll=ROW_DMA_UNROLL)

    @pl.when(i < blocks_p)
    def _():
        scatter(hp_ref)

    @pl.when(i >= blocks_p)
    def _():
        scatter(hs_ref)

    @pl.when(i == pl.num_programs(0) - 1)
    def _():
        zero_ref[...] = jnp.zeros(zero_ref.shape, F32)

        def zero_rows(first_row, n_rows):
            return pltpu.make_async_copy(zero_ref.at[pl.ds(0, n_rows * ROW_CHUNKS)],
                                         xs_out.at[pl.ds(first_row * ROW_CHUNKS, n_rows * ROW_CHUNKS)], sem.at[0])

        def start_row(r, c):
            zero_rows(r, 1).start()
            return c

        def wait_row(r, c):
            zero_rows(r, 1).wait()
            return c

        def per_expert(row_fn):
            def body(e, c):
                lo = meta_ref[META_START, e] + meta_ref[META_CNT, e]
                return lax.fori_loop(lo, meta_ref[META_END, e], row_fn, c)
            return body

        def start_tile(t, c):
            zero_rows(t * tm, tm).start()
            return c

        def wait_tile(t, c):
            zero_rows(t * tm, tm).wait()
            return c

        nact = meta_ref[META_NACT, 0]
        lax.fori_loop(0, N_EXPERTS, per_expert(start_row), 0)
        lax.fori_loop(nact, n_tiles, start_tile, 0)
        lax.fori_loop(0, N_EXPERTS, per_expert(wait_row), 0)
        lax.fori_loop(nact, n_tiles, wait_tile, 0)


def _dispatch(meta, pos, h2_p, h2_s, tt, tm, n_tiles):
    blocks_p = h2_p.shape[0] // (tt * ROW_CHUNKS)
    blocks_s = h2_s.shape[0] // (tt * ROW_CHUNKS)
    return pl.pallas_call(
        functools.partial(_dispatch_kernel, tt=tt, blocks_p=blocks_p, tm=tm, n_tiles=n_tiles),
        grid_spec=pltpu.PrefetchScalarGridSpec(
            num_scalar_prefetch=1,
            grid=(blocks_p + blocks_s,),
            in_specs=[pl.BlockSpec((2, tt), lambda i, m: (0, i), memory_space=pltpu.SMEM),
                      pl.BlockSpec((tt * ROW_CHUNKS, LANES), lambda i, m: (jnp.minimum(i, blocks_p - 1), 0)),
                      pl.BlockSpec((tt * ROW_CHUNKS, LANES), lambda i, m: (jnp.maximum(i - blocks_p, 0), 0))],
            out_specs=pl.BlockSpec(memory_space=pl.ANY),
            scratch_shapes=[pltpu.VMEM((tm * ROW_CHUNKS, LANES), F32), pltpu.SemaphoreType.DMA((2,))]),
        out_shape=jax.ShapeDtypeStruct((n_tiles * tm * ROW_CHUNKS, LANES), F32),
        compiler_params=pltpu.CompilerParams(dimension_semantics=("arbitrary",)),
        name="dispatch",
    )(meta, pos, h2_p, h2_s)


def _experts_kernel(te_ref, meta_ref, xs_ref, wg_ref, wu_ref, wd_ref, ys_ref, wg_b, wu_b, wd_b, *, tm):
    i = pl.program_id(0)
    active = i < meta_ref[META_NACT, 0]
    fresh = (i == 0) | (te_ref[0, i] != te_ref[0, jnp.maximum(i - 1, 0)])

    @pl.when(active & fresh)
    def _():
        wg_b[...] = wg_ref[...].astype(BF16)
        wu_b[...] = wu_ref[...].astype(BF16)
        wd_b[...] = wd_ref[...].astype(BF16)

    @pl.when(active)
    def _():
        x = _load_row_tiles(xs_ref, tm).astype(BF16)
        hid = (jax.nn.silu(_dot(x, wg_b[...])) * _dot(x, wu_b[...])).astype(BF16)
        _store_row_tiles(ys_ref, _dot(hid, wd_b[...]))

    @pl.when(jnp.logical_not(active))
    def _():
        ys_ref[...] = jnp.zeros(ys_ref.shape, F32)


def _experts(te, meta, xs, w_gate, w_up, w_down, tm, n_tiles):
    last_active = lambda i, m: jnp.minimum(i, m[META_NACT, 0] - 1)
    tile = pl.BlockSpec((tm * ROW_CHUNKS, LANES), lambda i, te, m: (last_active(i, m), 0))
    w_in_spec = pl.BlockSpec((None, D_MODEL, D_EXPERT), lambda i, te, m: (te[0, last_active(i, m)], 0, 0))
    w_out_spec = pl.BlockSpec((None, D_EXPERT, D_MODEL), lambda i, te, m: (te[0, last_active(i, m)], 0, 0))
    return pl.pallas_call(
        functools.partial(_experts_kernel, tm=tm),
        grid_spec=pltpu.PrefetchScalarGridSpec(
            num_scalar_prefetch=2,
            grid=(n_tiles,),
            in_specs=[tile, w_in_spec, w_in_spec, w_out_spec],
            out_specs=pl.BlockSpec((tm * ROW_CHUNKS, LANES), lambda i, te, m: (i, 0)),
            scratch_shapes=[pltpu.VMEM((D_MODEL, D_EXPERT), BF16), pltpu.VMEM((D_MODEL, D_EXPERT), BF16),
                            pltpu.VMEM((D_EXPERT, D_MODEL), BF16)]),
        out_shape=jax.ShapeDtypeStruct(xs.shape, F32),
        compiler_params=pltpu.CompilerParams(dimension_semantics=("arbitrary",), vmem_limit_bytes=VMEM_LIMIT),
        name="experts",
    )(te, meta, xs, w_gate, w_up, w_down)


def _combine_kernel(pos_ref, route_ref, x1_ref, g2_ref, gam_ref, bet_ref, ys_hbm, y_ref, buf0, buf1, sem,
                    *, tt, alpha):
    bufs = (buf0, buf1)

    def issue(j, _):
        for k in range(2):
            _row_copy(ys_hbm, pos_ref[k, j], bufs[k], j, sem.at[k]).start()
        return 0

    def drain(j, _):
        for k in range(2):
            _row_copy(ys_hbm, pos_ref[k, j], bufs[k], j, sem.at[k]).wait()
        return 0

    lax.fori_loop(0, tt, issue, 0, unroll=ROW_DMA_UNROLL)
    lax.fori_loop(0, tt, drain, 0, unroll=ROW_DMA_UNROLL)
    route = route_ref[...]
    eye = lax.broadcasted_iota(I32, (tt, tt), 0) == lax.broadcasted_iota(I32, (tt, tt), 1)
    as_col = lambda row: jnp.sum(jnp.where(eye, jnp.broadcast_to(row, (tt, tt)), 0.0), axis=1, keepdims=True)
    f = as_col(route[2:3, :]) * _load_row_tiles(buf0, tt) + as_col(route[3:4, :]) * _load_row_tiles(buf1, tt)
    y_ref[...] = _layer_norm_rows(alpha * x1_ref[...] + g2_ref[...] * f, gam_ref[...], bet_ref[...])


def _combine(pos, route_t, col_block0, x1, mod3, tiles_per_group, tt, gam, bet, ys, alpha):
    n = x1.shape[0]
    rows = lambda w: pl.BlockSpec((tt, w), lambda i: (i, 0))
    whole = lambda arr: pl.BlockSpec(arr.shape, lambda i: (0,) * arr.ndim)
    return pl.pallas_call(
        functools.partial(_combine_kernel, tt=tt, alpha=alpha),
        grid=(n // tt,),
        in_specs=[pl.BlockSpec((2, tt), lambda i: (0, i + col_block0), memory_space=pltpu.SMEM),
                  pl.BlockSpec((SUBLANES, tt), lambda i: (0, i + col_block0)),
                  rows(D_MODEL), _mod_spec(mod3, 5, tiles_per_group), whole(gam), whole(bet),
                  pl.BlockSpec(memory_space=pl.ANY)],
        out_specs=rows(D_MODEL),
        out_shape=jax.ShapeDtypeStruct((n, D_MODEL), F32),
        scratch_shapes=[pltpu.VMEM((tt * ROW_CHUNKS, LANES), F32), pltpu.VMEM((tt * ROW_CHUNKS, LANES), F32),
                        pltpu.SemaphoreType.DMA((2,))],
        compiler_params=pltpu.CompilerParams(dimension_semantics=("arbitrary",), vmem_limit_bytes=VMEM_LIMIT),
        name="combine",
    )(pos, route_t, x1, mod3, gam, bet, ys)


def kernel(x_prompt, x_sample, cache_k, cache_v, cache_kidx, page_table, c_prompt, c_sample, w_ada, b_ada, w_in,
           a_ln_g, a_ln_b, w_spatial, b_spatial, w_out, ln1_g, ln1_b, w_group_router, w_expert_router, w_gate,
           w_up, w_down, ln2_g, ln2_b):
    depth = w_ada.shape[0]
    assert depth == 1, "one trunk layer"
    alpha = (2.0 * depth) ** 0.25
    bsz, seq, d = x_prompt.shape
    db, ts, _ = x_sample.shape
    n_pages = page_table.shape[1]
    page = cache_k.shape[2]
    past = n_pages * page
    tile_p = min(PROMPT_TILE, seq)
    assert d == D_MODEL and seq % tile_p == 0 and tile_p % CHUNK == 0 and ts <= SUBLANES and page == LANES
    l = 0

    n_c = bsz + db
    n_c_pad = pl.cdiv(n_c, SUBLANES) * SUBLANES
    c_all = jnp.pad(jnp.concatenate([c_prompt, c_sample], axis=0), ((0, n_c_pad - n_c), (0, 0)))
    mod = _ada(c_all, w_ada[l], b_ada[l][None, :])
    mod_p = mod[:bsz]
    mod_s = mod[bsz:n_c]

    w_in_b = _pad_w_in(w_in[l])
    w_out_b = w_out[l].astype(BF16)
    mavg = _head_avg_matrix()
    gam_a = a_ln_g[l].reshape(1, A_WIDTH)
    bet_a = a_ln_b[l].reshape(1, A_WIDTH)
    w_router = jnp.concatenate([w_group_router[l].T, jnp.zeros((SUBLANES - N_GROUPS, d), F32),
                                w_expert_router[l].reshape(d, N_EXPERTS).T], axis=0)
    ln1 = (ln1_g[l][None, :], ln1_b[l][None, :])
    ln2 = (ln2_g[l][None, :], ln2_b[l][None, :])

    wsp = w_spatial[l].reshape(A_HEADS // 2, 2, CHUNK, CHUNK).transpose(0, 2, 1, 3).reshape(A_HEADS // 2, CHUNK, 2 * CHUNK)
    bsp = jnp.repeat(b_spatial[l].T, HEAD_DIM, axis=1)
    a_p, kt_p, vt32_p, kit_p, kb_p, kiwib_p, vt_p, qt_p, qit_p, wit_p = _front_prompt(
        x_prompt, mod_p.reshape(bsz, 6, d), w_in_b, _rot_tables(np.arange(seq)), mavg, gam_a, bet_a, wsp, bsp, tile_p)
    b_p = _attn_prompt(qit_p, wit_p, qt_p, kiwib_p, kb_p, vt_p)
    n_p = bsz * seq
    x1_p, h2_p, route_p = _mid(x_prompt.reshape(n_p, d), a_p.reshape(n_p, A_WIDTH), b_p.reshape(n_p, B_WIDTH),
                               mod_p.reshape(bsz, 1, 6 * d), seq // tile_p, tile_p, w_out_b, *ln1, w_router, alpha)

    r_s = ts * db
    x_tm = x_sample.transpose(1, 0, 2).reshape(r_s, d)
    rt_s = _rot_tables(np.repeat(past + np.arange(ts), db))
    w_small = w_spatial[l][:, :ts, :ts]
    wl = jnp.repeat(w_small.transpose(1, 2, 0).reshape(ts * ts, A_HEADS), HEAD_DIM, axis=1)
    bl = jnp.repeat(b_spatial[l][:, :ts].T, HEAD_DIM, axis=1)
    a_s, q_s, k_s, v_s, qi_s, kiwi_s, vg_s = _front_sample(x_tm, mod_s, w_in_b, rt_s, mavg, gam_a, bet_a, wl, bl, ts, db)

    def seq_major(a):
        return a.reshape(ts, db, a.shape[-1]).transpose(1, 0, 2)

    def new_t(a, heads):
        a = a.reshape(ts, db, heads, HEAD_DIM).transpose(1, 2, 3, 0)
        return jnp.pad(a, ((0, 0), (0, 0), (0, 0), (0, LANES - ts)))

    ga = min(IDX_SEQS, db)
    assert db % ga == 0 and (ga * ts) % SUBLANES == 0
    qi4 = qi_s.reshape(ts, db, IDX_HEADS, IDX_DIM).transpose(1, 2, 0, 3).reshape(db, IDX_HEADS * ts, IDX_DIM)
    wcol = kiwi_s[:, WI_LANE:WI_LANE + IDX_HEADS].reshape(ts, db, IDX_HEADS).transpose(1, 2, 0).reshape(db, IDX_HEADS * ts, 1)
    qpos = jnp.tile(past + jnp.arange(ts, dtype=I32), db).reshape(db * ts, 1)
    kinew_t = new_t(kiwi_s[:, :IDX_DIM], 1)[:, 0]
    kidx_t = jnp.transpose(cache_kidx[l], (0, 2, 1))
    k_t = jnp.transpose(cache_k[l], (0, 2, 3, 1))
    v_t = jnp.transpose(cache_v[l], (0, 2, 3, 1))
    bias_s = _idx_sample(page_table, qi4, wcol, qpos, kinew_t, kidx_t, ts, ga)
    b_s = _attn_sample(page_table, seq_major(q_s), bias_s, new_t(k_s, B_KV_HEADS), new_t(v_s, B_KV_HEADS), k_t, v_t)
    b_s_tm = b_s.transpose(1, 0, 2).reshape(r_s, B_WIDTH)
    x1_s, h2_s, route_s = _mid(x_tm, a_s, b_s_tm, mod_s.reshape(1, db, 6 * d), ts, db, w_out_b, *ln1, w_router, alpha)

    n_all = n_p + r_s
    tok_p = min(TOKEN_TILE, seq)
    rank_tile = min(RANK_TILE, n_all)
    assert n_all % rank_tile == 0 and n_p % tok_p == 0 and r_s % tok_p == 0 and n_p % db == 0
    route_all = jnp.concatenate([route_p, route_s], axis=1)
    pos, meta, te, n_tiles = _plan(route_all, rank_tile, MOE_TILE)
    xs = _dispatch(meta, pos, h2_p, h2_s, tok_p, MOE_TILE, n_tiles)
    ys = _experts(te, meta, xs, w_gate[l], w_up[l], w_down[l], MOE_TILE, n_tiles)
    y_p = _combine(pos, route_all, 0, x1_p, mod_p.reshape(bsz, 1, 6 * d), seq // tok_p, tok_p, *ln2, ys, alpha)
    y_s_tm = _combine(pos, route_all, n_p // db, x1_s, mod_s.reshape(1, db, 6 * d), ts, db, *ln2, ys, alpha)
    y_s = y_s_tm.reshape(ts, db, d).transpose(1, 0, 2)

    kv5 = lambda a, n, t: a.reshape(1, n, t, B_KV_HEADS, HEAD_DIM)
    kv5_t = lambda a: a.reshape(1, bsz, B_KV_HEADS, HEAD_DIM, seq).transpose(0, 1, 4, 2, 3)
    return (y_p.reshape(bsz, seq, d), y_s,
            kv5_t(kt_p), kv5_t(vt32_p), kit_p.transpose(0, 2, 1)[None],
            kv5(seq_major(k_s), db, ts), kv5(seq_major(v_s), db, ts), seq_major(kiwi_s)[..., :IDX_DIM][None],
            seq_major(vg_s).reshape(1, db, ts, A_HEADS, HEAD_DIM))
```

```python
import functools

import jax
import jax.numpy as jnp
import numpy as np
from jax import lax
from jax.experimental import pallas as pl
from jax.experimental.pallas import tpu as pltpu

F32 = jnp.float32
BF16 = jnp.bfloat16
I32 = jnp.int32

D_MODEL = 1024
HEAD_DIM = 64
A_HEADS = 8
A_WIDTH = A_HEADS * HEAD_DIM
CHUNK = 128
B_HEADS = 8
B_KV_HEADS = 4
B_GROUP = B_HEADS // B_KV_HEADS
B_WIDTH = B_HEADS * HEAD_DIM
KV_WIDTH = B_KV_HEADS * HEAD_DIM
IDX_HEADS = 4
IDX_DIM = 64
IDX_WIDTH = IDX_HEADS * IDX_DIM
TOPK_MAX = 256
ROPE_THETA = 500000.0
ROT_DIM = HEAD_DIM // 4
ROT_HALF = ROT_DIM // 2
ATTN_SCALE = HEAD_DIM ** -0.5
LOG2_E = 1.4426950408889634
Q_SCALE = ATTN_SCALE * LOG2_E
DENOM_ROWS = 16
N_GROUPS = 4
EXPERTS_PER_GROUP = 8
N_EXPERTS = N_GROUPS * EXPERTS_PER_GROUP
D_EXPERT = 512
LN_EPS = 1e-5

LANES = 128
SUBLANES = 8
ROW_CHUNKS = D_MODEL // LANES

C_AU, C_AV, C_Q, C_K, C_V, C_QI, C_KI = 0, 512, 1024, 1536, 1792, 2048, 2304
IN_WIDTH = C_KI + IDX_DIM + IDX_HEADS
IN_PAD = 2432
WI_LANE = IDX_DIM

INT_MIN = -(2 ** 31)
INT_MAX = 2 ** 31 - 1
VMEM_LIMIT = 48 * 1024 * 1024

PROMPT_TILE = 256
MOE_TILE = 256
TOKEN_TILE = 256
RANK_TILE = 512
IDX_SEQS = 16
STEPS_PER_CHECK = 4
ROW_DMA_UNROLL = 8


def _dot(a, b):
    return jnp.dot(a, b, preferred_element_type=F32)


def _dot_nt(a, b):
    return lax.dot_general(a, b, (((1,), (1,)), ((), ())), preferred_element_type=F32)


def _split_dot(x, m):
    hi = x.astype(BF16)
    lo = (x - hi.astype(F32)).astype(BF16)
    return _dot(hi, m) + _dot(lo, m)


def _ada_kernel(c_ref, w_ref, b_ref, o_ref):
    s = jax.nn.silu(c_ref[...]).astype(BF16)
    o_ref[...] = _dot(s, w_ref[...].astype(BF16)) + b_ref[...]


def _ada(c_all, w_ada, b_ada):
    rows = c_all.shape[0]
    n_out = w_ada.shape[1]
    tn = 1024
    return pl.pallas_call(
        _ada_kernel,
        grid=(n_out // tn,),
        in_specs=[
            pl.BlockSpec((rows, D_MODEL), lambda j: (0, 0)),
            pl.BlockSpec((D_MODEL, tn), lambda j: (0, j)),
            pl.BlockSpec((1, tn), lambda j: (0, j)),
        ],
        out_specs=pl.BlockSpec((rows, tn), lambda j: (0, j)),
        out_shape=jax.ShapeDtypeStruct((rows, n_out), F32),
        compiler_params=pltpu.CompilerParams(
            dimension_semantics=("arbitrary",), vmem_limit_bytes=VMEM_LIMIT),
        name="ada",
    )(c_all, w_ada, b_ada)


def _rotate(x, rt, blk):
    c = rt[:, 0:LANES]
    s_lo = rt[:, LANES:2 * LANES]
    s_hi = rt[:, 2 * LANES:3 * LANES]
    if blk == 1:
        head = lax.broadcasted_iota(I32, c.shape, 1) < IDX_DIM
        c = jnp.where(head, c, 1.0)
        s_lo = jnp.where(head, s_lo, 0.0)
        s_hi = jnp.where(head, s_hi, 0.0)
    outs = []
    for j in range(x.shape[1] // LANES):
        xb = x[:, j * LANES:(j + 1) * LANES]
        up = pltpu.roll(xb, LANES - ROT_HALF, 1)
        dn = pltpu.roll(xb, ROT_HALF, 1)
        outs.append(xb * c + up * s_lo + dn * s_hi)
    return outs[0] if len(outs) == 1 else jnp.concatenate(outs, axis=1)


def _head_ln(g, mavg, gam, bet):
    def seg_mean(x):
        parts = [_split_dot(x[:, j * 256:(j + 1) * 256], mavg) for j in range(A_WIDTH // 256)]
        return jnp.concatenate(parts, axis=1)
    mu = seg_mean(g)
    d = g - mu
    var = seg_mean(d * d)
    return d * lax.rsqrt(var + LN_EPS) * gam + bet


def _project(h, w_ref, rt, mavg, gam, bet):
    u = jax.nn.gelu(_dot(h, w_ref[:, C_AU:C_AV]))
    vg = _head_ln(jax.nn.gelu(_dot(h, w_ref[:, C_AV:C_Q])), mavg, gam, bet)
    q = _rotate(_dot(h, w_ref[:, C_Q:C_K]), rt, 0) * Q_SCALE
    k = _rotate(_dot(h, w_ref[:, C_K:C_V]), rt, 0)
    v = _dot(h, w_ref[:, C_V:C_QI])
    qi = _rotate(_dot(h, w_ref[:, C_QI:C_KI]), rt, 0)
    kiwi = _rotate(_dot(h, w_ref[:, C_KI:IN_PAD]), rt, 1)
    return u, vg, q, k, v, qi, kiwi


def _front_prompt_kernel(x_ref, mod_ref, w_ref, rt_ref, mavg_ref, gam_ref, bet_ref, wsp_ref, bsp_ref,
                         a_ref, kt_ref, vt_ref, kit_ref, kb_ref, kiwib_ref, vtb_ref, qt_ref, qit_ref, wit_ref):
    shift = mod_ref[0:1, :]
    scale = mod_ref[1:2, :]
    h = (x_ref[...] * (1.0 + scale) + shift).astype(BF16)
    u, vg, q, k, v, qi, kiwi = _project(h, w_ref, rt_ref[...], mavg_ref[...], gam_ref[...], bet_ref[...])
    v_t = jnp.transpose(v)
    kiwi_t = jnp.transpose(kiwi)
    kt_ref[...] = jnp.transpose(k)
    vt_ref[...] = v_t
    kit_ref[...] = kiwi_t[0:IDX_DIM, :]
    kb_ref[...] = k.astype(BF16)
    kiwib_ref[...] = kiwi.astype(BF16)
    vtb_ref[...] = v_t.astype(BF16)
    qt_ref[...] = jnp.transpose(q).astype(BF16)
    qit_ref[...] = jnp.transpose(qi).astype(BF16)
    wit_ref[...] = kiwi_t[WI_LANE:WI_LANE + SUBLANES, :]

    rows = lax.broadcasted_iota(I32, (CHUNK, 2 * CHUNK), 0)
    cols = lax.broadcasted_iota(I32, (CHUNK, 2 * CHUNK), 1) % CHUNK
    causal = cols <= rows
    lane = lax.broadcasted_iota(I32, (CHUNK, LANES), 1)
    tt = x_ref.shape[0]
    for cidx in range(tt // CHUNK):
        rs = slice(cidx * CHUNK, (cidx + 1) * CHUNK)
        blocks = []
        for p in range(A_HEADS // 2):
            wcat = jnp.where(causal, wsp_ref[p], 0.0).astype(BF16)
            vb = vg[rs, p * LANES:(p + 1) * LANES]
            rhs = jnp.concatenate([jnp.where(lane < HEAD_DIM, vb, 0.0),
                                   jnp.where(lane >= HEAD_DIM, vb, 0.0)], axis=0).astype(BF16)
            blocks.append(_dot(wcat, rhs))
        s = jnp.concatenate(blocks, axis=1) + bsp_ref[...]
        a_ref[rs, :] = (u[rs, :] * s).astype(BF16)


def _front_prompt(x, mod, w_in, rt, mavg, gam, bet, wsp, bsp, tt):
    b, t, _ = x.shape
    nc = t // tt
    tok = lambda w: pl.BlockSpec((None, tt, w), lambda i, j: (i, j, 0))
    tr = lambda r: pl.BlockSpec((None, None, r, tt), lambda i, j: (i, j, 0, 0))
    pos_minor = lambda r: pl.BlockSpec((None, r, tt), lambda i, j: (i, 0, j))
    const2 = lambda a: pl.BlockSpec(a.shape, lambda i, j: (0,) * a.ndim)
    out_shapes = (
        jax.ShapeDtypeStruct((b, t, A_WIDTH), BF16),
        jax.ShapeDtypeStruct((b, KV_WIDTH, t), F32),
        jax.ShapeDtypeStruct((b, KV_WIDTH, t), F32),
        jax.ShapeDtypeStruct((b, IDX_DIM, t), F32),
        jax.ShapeDtypeStruct((b, t, KV_WIDTH), BF16),
        jax.ShapeDtypeStruct((b, t, LANES), BF16),
        jax.ShapeDtypeStruct((b, nc, KV_WIDTH, tt), BF16),
        jax.ShapeDtypeStruct((b, nc, B_WIDTH, tt), BF16),
        jax.ShapeDtypeStruct((b, nc, IDX_WIDTH, tt), BF16),
        jax.ShapeDtypeStruct((b, nc, SUBLANES, tt), F32),
    )
    return pl.pallas_call(
        _front_prompt_kernel,
        grid=(b, nc),
        in_specs=[
            tok(D_MODEL),
            pl.BlockSpec((None, 6, D_MODEL), lambda i, j: (i, 0, 0)),
            const2(w_in),
            pl.BlockSpec((tt, 3 * LANES), lambda i, j: (j, 0)),
            const2(mavg), const2(gam), const2(bet), const2(wsp), const2(bsp),
        ],
        out_specs=[tok(A_WIDTH), pos_minor(KV_WIDTH), pos_minor(KV_WIDTH), pos_minor(IDX_DIM), tok(KV_WIDTH),
                   tok(LANES), tr(KV_WIDTH), tr(B_WIDTH), tr(IDX_WIDTH), tr(SUBLANES)],
        out_shape=out_shapes,
        compiler_params=pltpu.CompilerParams(
            dimension_semantics=("arbitrary", "arbitrary"), vmem_limit_bytes=VMEM_LIMIT),
        name="front_prompt",
    )(x, mod, w_in, rt, mavg, gam, bet, wsp, bsp)


def _order_key(score, kpos, idx_bits):
    offs = 1 << idx_bits
    bits = pltpu.bitcast(score, I32)
    key = bits ^ ((bits >> 31) & 0x7FFFFFFF)
    key = key + jnp.where(score > 0.0, offs, 0)
    return jnp.where(score == 0.0, offs - kpos, key)


def _select_threshold(count_ge, count_gt_eq_lt, shape, topk, idx_bits):
    offs = 1 << idx_bits
    first_candidates = (offs + 1, 1)
    max_steps = -(-(32 + len(first_candidates)) // STEPS_PER_CHECK) * STEPS_PER_CHECK

    def unresolved(carry):
        i, lo, hi, n_lo = carry
        pending = (n_lo != topk) & (hi - 1 > lo)
        return (i < max_steps) & (jnp.max(pending.astype(F32)) > 0.0)

    def bisect(carry):
        i, lo, hi, n_lo = carry
        for _ in range(STEPS_PER_CHECK):
            mid = (lo >> 1) + (hi >> 1) + (lo & hi & 1)
            for step, value in enumerate(first_candidates):
                forced = jnp.where(i == step, value, INT_MIN)
                mid = jnp.where((lo < forced) & (forced < hi), forced, mid)
            tot = count_ge(mid)
            take = tot >= topk
            lo = jnp.where(take, mid, lo)
            n_lo = jnp.where(take, tot, n_lo)
            hi = jnp.where(take, hi, mid)
            i = i + 1
        return i, lo, hi, n_lo

    never = jnp.full(shape, INT_MAX, I32)
    _, thr, _, n_lo = lax.while_loop(
        unresolved, bisect, (jnp.int32(0), jnp.full(shape, INT_MIN + 1, I32), jnp.full(shape, INT_MAX, I32), never))
    tied = (n_lo > topk) & (n_lo != INT_MAX)
    big = jnp.full(shape, offs, I32)

    def resolve_ties(_):
        n_gt, _ = count_gt_eq_lt(thr, big)
        need = topk - n_gt

        def idx_step(i, cut):
            cand = cut | lax.shift_left(jnp.int32(1), idx_bits - 1 - i)
            _, n_eq = count_gt_eq_lt(thr, cand)
            return jnp.where(n_eq <= need, cand, cut)

        cut = lax.fori_loop(0, idx_bits, idx_step, jnp.zeros(shape, I32))
        return jnp.where(tied, cut, big)

    any_tied = jnp.max(tied.astype(F32)) > 0.0
    cut = lax.cond(any_tied, resolve_ties, lambda _: big, 0)
    return thr, cut


def _fold_rows(x):
    acc = x[0:SUBLANES]
    for r in range(1, x.shape[0] // SUBLANES):
        acc = acc + x[r * SUBLANES:(r + 1) * SUBLANES]
    return acc


def _col_total(cnt8):
    return jnp.sum(cnt8.astype(F32), axis=0, keepdims=True).astype(I32)


def _attn_prompt_kernel(qit_ref, wit_ref, qt_ref, kiwib_ref, kb_ref, vt_ref, o_ref,
                        key_ref, w4_ref, wq_ref, s0_ref, s1_ref, bias_ref, m_ref, acc_ref, *, tq, topk, idx_bits):
    j = pl.program_id(1)
    kc = tq
    n_kc = j + 1

    @pl.when((pl.program_id(0) == 0) & (j == 0))
    def _():
        w4_ref[...] = jnp.zeros(w4_ref.shape, BF16)
        wq_ref[...] = jnp.zeros(wq_ref.shape, BF16)

    for h in range(IDX_HEADS):
        w4_ref[0:IDX_DIM, h * tq:(h + 1) * tq] = qit_ref[h * IDX_DIM:(h + 1) * IDX_DIM, :]
    for h in range(B_HEADS):
        n = h // B_GROUP
        wq_ref[h, n * HEAD_DIM:(n + 1) * HEAD_DIM, :] = qt_ref[h * HEAD_DIM:(h + 1) * HEAD_DIM, :]

    wit = wit_ref[...]
    krow = lax.broadcasted_iota(I32, (kc, tq), 0)
    qpos = j * tq + lax.broadcasted_iota(I32, (kc, tq), 1)

    def score_chunk(c, _):
        k0 = pl.multiple_of(c * kc, kc)
        s = _dot(kiwib_ref[pl.ds(k0, kc), :], w4_ref[...])
        tot = jnp.maximum(s[:, 0:tq], 0.0) * wit[0:1, :]
        for h in range(1, IDX_HEADS):
            tot = tot + jnp.maximum(s[:, h * tq:(h + 1) * tq], 0.0) * wit[h:h + 1, :]
        kpos = k0 + krow
        key_ref[c] = jnp.where(kpos <= qpos, _order_key(tot, kpos, idx_bits), INT_MIN)
        return 0

    lax.fori_loop(0, n_kc, score_chunk, 0)

    def count_ge(cand):
        def body(c, cnt):
            return cnt + _fold_rows(jnp.where(key_ref[c] >= cand, 1, 0))
        return _col_total(lax.fori_loop(0, n_kc, body, jnp.zeros((SUBLANES, tq), I32)))

    def count_gt_eq_lt(thr, pos):
        def body(c, carry):
            n_gt, n_eq = carry
            key = key_ref[c]
            n_gt = n_gt + _fold_rows(jnp.where(key > thr, 1, 0))
            n_eq = n_eq + _fold_rows(jnp.where((key == thr) & (c * kc + krow < pos), 1, 0))
            return n_gt, n_eq
        z = jnp.zeros((SUBLANES, tq), I32)
        n_gt, n_eq = lax.fori_loop(0, n_kc, body, (z, z))
        return _col_total(n_gt), _col_total(n_eq)

    thr, cut = _select_threshold(count_ge, count_gt_eq_lt, (1, tq), topk, idx_bits)

    m_ref[...] = jnp.full(m_ref.shape, jnp.finfo(F32).min, F32)
    acc_ref[...] = jnp.zeros(acc_ref.shape, F32)

    def key_block(c):
        return kb_ref[pl.ds(pl.multiple_of(c * kc, kc), kc), :]

    def selection_bias(c, open_bias):
        key = key_ref[c]
        sel = (key > thr) | ((key == thr) & (c * kc + krow < cut))
        bias_ref[...] = jnp.where(sel, open_bias, -jnp.inf)

    selection_bias(0, 0.0)
    kblk0 = key_block(0)
    for h in range(B_HEADS):
        s0_ref[h] = _dot(kblk0, wq_ref[h]) + bias_ref[...]

    def stage(c, src, dst, c_next, open_next):
        selection_bias(c_next, open_next)
        kblk_next = key_block(c_next)
        vtc = vt_ref[c]
        for h in range(B_HEADS):
            n = h // B_GROUP
            dst[h] = _dot(kblk_next, wq_ref[h]) + bias_ref[...]
            s = src[h]
            m_old = m_ref[h]
            m_new = jnp.maximum(m_old, jnp.max(s, axis=0, keepdims=True))
            p = jnp.exp2(s - m_new).astype(BF16)
            v_aug = jnp.concatenate([vtc[n * HEAD_DIM:(n + 1) * HEAD_DIM, :], ones_rows], axis=0)
            acc_ref[h] = jnp.exp2(m_old - m_new) * acc_ref[h] + _dot(v_aug, p)
            m_ref[h] = m_new

    ones_rows = jnp.where(lax.broadcasted_iota(I32, (DENOM_ROWS, kc), 0) == 0, 1.0, 0.0).astype(BF16)
    last = n_kc - 1

    def attend_pair(i, _):
        c0 = 2 * i
        c1 = jnp.minimum(c0 + 1, last)
        stage(c0, s0_ref, s1_ref, c1, jnp.where(c0 + 1 <= last, 0.0, -jnp.inf))
        stage(c1, s1_ref, s0_ref, jnp.minimum(c0 + 2, last), 0.0)
        return 0

    lax.fori_loop(0, (n_kc + 1) // 2, attend_pair, 0)
    out_t = jnp.concatenate([acc_ref[h, 0:HEAD_DIM, :] / acc_ref[h, HEAD_DIM:HEAD_DIM + 1, :]
                             for h in range(B_HEADS)], axis=0)
    o_ref[...] = jnp.transpose(out_t).astype(BF16)


def _attn_prompt(qit, wit, qt, kiwib, kb, vt):
    b, nc, _, tq = qt.shape
    t = nc * tq
    topk = min(TOPK_MAX, t // 4)
    idx_bits = max(1, (t - 1).bit_length())
    tr = lambda r: pl.BlockSpec((None, None, r, tq), lambda i, j: (i, j, 0, 0))
    full = lambda w: pl.BlockSpec((None, t, w), lambda i, j: (i, 0, 0))
    kernel = functools.partial(_attn_prompt_kernel, tq=tq, topk=topk, idx_bits=idx_bits)
    return pl.pallas_call(
        kernel,
        grid=(b, nc),
        in_specs=[tr(IDX_WIDTH), tr(SUBLANES), tr(B_WIDTH), full(LANES), full(KV_WIDTH),
                  pl.BlockSpec((None, nc, KV_WIDTH, tq), lambda i, j: (i, 0, 0, 0))],
        out_specs=pl.BlockSpec((None, tq, B_WIDTH), lambda i, j: (i, j, 0)),
        out_shape=jax.ShapeDtypeStruct((b, t, B_WIDTH), BF16),
        scratch_shapes=[
            pltpu.VMEM((nc, tq, tq), I32),
            pltpu.VMEM((LANES, IDX_HEADS * tq), BF16),
            pltpu.VMEM((B_HEADS, KV_WIDTH, tq), BF16),
            pltpu.VMEM((B_HEADS, tq, tq), F32),
            pltpu.VMEM((B_HEADS, tq, tq), F32),
            pltpu.VMEM((tq, tq), F32),
            pltpu.VMEM((B_HEADS, 1, tq), F32),
            pltpu.VMEM((B_HEADS, HEAD_DIM + DENOM_ROWS, tq), F32),
        ],
        compiler_params=pltpu.CompilerParams(
            dimension_semantics=("arbitrary", "arbitrary"), vmem_limit_bytes=VMEM_LIMIT),
        name="attn_prompt",
    )(qit, wit, qt, kiwib, kb, vt)


def _rot_tables(pos):
    r = pos.shape[0]
    inv_freq = np.float32(ROPE_THETA) ** (-np.arange(ROT_HALF, dtype=np.float32) * np.float32(2.0) / np.float32(ROT_DIM))
    ang = pos.astype(np.float32)[:, None] * inv_freq[None, :]
    cos, sin = np.cos(ang), np.sin(ang)
    rest = HEAD_DIM - ROT_DIM
    c64 = np.concatenate([cos, cos, np.ones((r, rest), np.float32)], axis=1)
    lo64 = np.concatenate([-sin, np.zeros((r, HEAD_DIM - ROT_HALF), np.float32)], axis=1)
    hi64 = np.concatenate([np.zeros((r, ROT_HALF), np.float32), sin, np.zeros((r, rest), np.float32)], axis=1)
    return jnp.asarray(np.concatenate([c64, c64, lo64, lo64, hi64, hi64], axis=1).astype(np.float32))


def _head_avg_matrix():
    return jnp.kron(jnp.eye(256 // HEAD_DIM, dtype=F32), jnp.full((HEAD_DIM, HEAD_DIM), 1.0 / HEAD_DIM, F32)).astype(BF16)


def _pad_w_in(w_in):
    return jnp.pad(w_in, ((0, 0), (0, IN_PAD - IN_WIDTH))).astype(BF16)


def _front_sample_kernel(x_ref, shift_ref, scale_ref, w_ref, rt_ref, mavg_ref, gam_ref, bet_ref, wl_ref, bl_ref,
                         a_ref, q_ref, k_ref, v_ref, qi_ref, kiwi_ref, vg_ref, *, ts, db):
    one_scale = 1.0 + scale_ref[...]
    shift = shift_ref[...]
    h = jnp.concatenate([x_ref[t * db:(t + 1) * db, :] * one_scale + shift for t in range(ts)], axis=0).astype(BF16)
    u, vg, q, k, v, qi, kiwi = _project(h, w_ref, rt_ref[...], mavg_ref[...], gam_ref[...], bet_ref[...])
    q_ref[...] = q.astype(BF16)
    k_ref[...] = k
    v_ref[...] = v
    qi_ref[...] = qi.astype(BF16)
    kiwi_ref[...] = kiwi
    vg_ref[...] = vg
    for t in range(ts):
        s = bl_ref[t:t + 1, :]
        for src in range(t + 1):
            s = s + wl_ref[t * ts + src:t * ts + src + 1, :] * vg[src * db:(src + 1) * db, :]
        a_ref[t * db:(t + 1) * db, :] = (u[t * db:(t + 1) * db, :] * s).astype(BF16)


def _front_sample(x_tm, mod_s, w_in, rt, mavg, gam, bet, wl, bl, ts, db):
    r = ts * db
    whole = lambda a: pl.BlockSpec(a.shape, lambda i: (0,) * a.ndim)
    out = lambda w, dt: jax.ShapeDtypeStruct((r, w), dt)
    outs = (out(A_WIDTH, BF16), out(B_WIDTH, BF16), out(KV_WIDTH, F32), out(KV_WIDTH, F32),
            out(IDX_WIDTH, BF16), out(LANES, F32), out(A_WIDTH, F32))
    return pl.pallas_call(
        functools.partial(_front_sample_kernel, ts=ts, db=db),
        grid=(1,),
        in_specs=[
            whole(x_tm),
            pl.BlockSpec((db, D_MODEL), lambda i: (0, 0)),
            pl.BlockSpec((db, D_MODEL), lambda i: (0, 1)),
            whole(w_in), whole(rt), whole(mavg), whole(gam), whole(bet), whole(wl), whole(bl),
        ],
        out_specs=[pl.BlockSpec((r, s.shape[1]), lambda i: (0, 0)) for s in outs],
        out_shape=outs,
        compiler_params=pltpu.CompilerParams(dimension_semantics=("arbitrary",), vmem_limit_bytes=VMEM_LIMIT),
        name="front_sample",
    )(x_tm, mod_s, mod_s, w_in, rt, mavg, gam, bet, wl, bl)


def _idx_sample_kernel(pt_ref, qi4_ref, wcol_ref, qpos_ref, kinew_ref, kidx_hbm, bias_ref,
                       ki_buf, key_ref, sem, *, ga, ts, n_pages, page, lpad, topk, idx_bits):
    i = pl.program_id(0)
    n_steps = pl.num_programs(0)
    past = n_pages * page
    rows = ga * ts
    slot = i % 2

    def page_copy(step, to_slot, g, p):
        phys = pt_ref[step * ga + g, p]
        return pltpu.make_async_copy(kidx_hbm.at[phys], ki_buf.at[to_slot, g, :, pl.ds(p * page, page)],
                                     sem.at[to_slot])

    def start_all(step, to_slot):
        for g in range(ga):
            for p in range(n_pages):
                page_copy(step, to_slot, g, p).start()

    @pl.when(i == 0)
    def _():
        start_all(0, 0)

    @pl.when(i + 1 < n_steps)
    def _():
        start_all(i + 1, 1 - slot)

    ki_buf[slot, :, :, pl.ds(past, LANES)] = kinew_ref[...]
    for g in range(ga):
        for p in range(n_pages):
            page_copy(i, slot, g, p).wait()

    kpos = lax.broadcasted_iota(I32, (ts, lpad), 1)
    for g in range(ga):
        s = _dot(qi4_ref[g], ki_buf[slot, g].astype(BF16))
        r = jnp.maximum(s, 0.0) * wcol_ref[g]
        tot = r[0:ts]
        for h in range(1, IDX_HEADS):
            tot = tot + r[h * ts:(h + 1) * ts]
        adm = (kpos <= qpos_ref[g * ts:(g + 1) * ts, :]) & (kpos < past + ts)
        key_ref[g * ts:(g + 1) * ts, :] = jnp.where(adm, _order_key(tot, kpos, idx_bits), INT_MIN)

    sub = lpad // LANES
    lane128 = lax.broadcasted_iota(I32, (rows, LANES), 1)

    def row_total(cnt):
        tot = jnp.sum(cnt.astype(F32), axis=1, keepdims=True)
        return jnp.broadcast_to(tot, cnt.shape).astype(I32)

    def count_ge(cand):
        cnt = jnp.zeros((rows, LANES), I32)
        for s_ in range(sub):
            cnt = cnt + jnp.where(key_ref[:, s_ * LANES:(s_ + 1) * LANES] >= cand, 1, 0)
        return row_total(cnt)

    def count_gt_eq_lt(thr, pos):
        n_gt = jnp.zeros((rows, LANES), I32)
        n_eq = jnp.zeros((rows, LANES), I32)
        for s_ in range(sub):
            kk = key_ref[:, s_ * LANES:(s_ + 1) * LANES]
            n_gt = n_gt + jnp.where(kk > thr, 1, 0)
            n_eq = n_eq + jnp.where((kk == thr) & (s_ * LANES + lane128 < pos), 1, 0)
        return row_total(n_gt), row_total(n_eq)

    thr, cut = _select_threshold(count_ge, count_gt_eq_lt, (rows, LANES), topk, idx_bits)
    for s_ in range(sub):
        kk = key_ref[:, s_ * LANES:(s_ + 1) * LANES]
        sel = (kk > thr) | ((kk == thr) & (s_ * LANES + lane128 < cut))
        bias = jnp.where(sel, 0.0, -jnp.inf)
        for g in range(ga):
            bias_ref[g, :, s_ * LANES:(s_ + 1) * LANES] = bias[g * ts:(g + 1) * ts]


def _idx_sample(page_table, qi4, wcol, qpos, kinew_t, kidx_t, ts, ga):
    db = qi4.shape[0]
    n_pages = page_table.shape[1]
    page = kidx_t.shape[2]
    past = n_pages * page
    lpad = past + LANES
    topk = min(TOPK_MAX, (past + ts) // 4)
    idx_bits = max(1, (lpad - 1).bit_length())
    kernel = functools.partial(_idx_sample_kernel, ga=ga, ts=ts, n_pages=n_pages, page=page, lpad=lpad,
                               topk=topk, idx_bits=idx_bits)
    return pl.pallas_call(
        kernel,
        grid_spec=pltpu.PrefetchScalarGridSpec(
            num_scalar_prefetch=1,
            grid=(db // ga,),
            in_specs=[pl.BlockSpec((ga, IDX_HEADS * ts, IDX_DIM), lambda i, pt: (i, 0, 0)),
                      pl.BlockSpec((ga, IDX_HEADS * ts, 1), lambda i, pt: (i, 0, 0)),
                      pl.BlockSpec((ga * ts, 1), lambda i, pt: (i, 0)),
                      pl.BlockSpec((ga, IDX_DIM, LANES), lambda i, pt: (i, 0, 0)),
                      pl.BlockSpec(memory_space=pl.ANY)],
            out_specs=pl.BlockSpec((ga, ts, lpad), lambda i, pt: (i, 0, 0)),
            scratch_shapes=[
                pltpu.VMEM((2, ga, IDX_DIM, lpad), F32),
                pltpu.VMEM((ga * ts, lpad), I32),
                pltpu.SemaphoreType.DMA((2,)),
            ]),
        out_shape=jax.ShapeDtypeStruct((db, ts, lpad), F32),
        compiler_params=pltpu.CompilerParams(dimension_semantics=("arbitrary",), vmem_limit_bytes=VMEM_LIMIT),
        name="idx_sample",
    )(page_table, qi4, wcol, qpos, kinew_t, kidx_t)


def _attn_sample_kernel(pt_ref, q_ref, bias_ref, knew_ref, vnew_ref, k_hbm, v_hbm, o_ref,
                        k_buf, v_buf, sem, *, ts, n_pages, page):
    b = pl.program_id(0)
    nb = pl.num_programs(0)
    past = n_pages * page
    slot = b % 2

    def page_copies(seq, to_slot, p):
        phys = pt_ref[seq, p]
        dst = pl.ds(p * page, page)
        return (pltpu.make_async_copy(k_hbm.at[phys], k_buf.at[to_slot, :, :, dst], sem.at[0, to_slot]),
                pltpu.make_async_copy(v_hbm.at[phys], v_buf.at[to_slot, :, :, dst], sem.at[1, to_slot]))

    def start_all(seq, to_slot):
        for p in range(n_pages):
            for cp in page_copies(seq, to_slot, p):
                cp.start()

    @pl.when(b == 0)
    def _():
        start_all(0, 0)

    @pl.when(b + 1 < nb)
    def _():
        start_all(b + 1, 1 - slot)

    k_buf[slot, :, :, pl.ds(past, LANES)] = knew_ref[...]
    v_buf[slot, :, :, pl.ds(past, LANES)] = vnew_ref[...]
    for p in range(n_pages):
        for cp in page_copies(b, slot, p):
            cp.wait()

    q = q_ref[...]
    bias = bias_ref[...]
    bias2 = jnp.concatenate([bias] * B_GROUP, axis=0)
    outs = [None] * B_HEADS
    for n in range(B_KV_HEADS):
        kt = k_buf[slot, n].astype(BF16)
        vt = v_buf[slot, n].astype(BF16)
        qs = jnp.concatenate([q[:, (n * B_GROUP + g) * HEAD_DIM:(n * B_GROUP + g + 1) * HEAD_DIM]
                              for g in range(B_GROUP)], axis=0)
        sc = _dot(qs, kt) + bias2
        m = jnp.max(sc, axis=1, keepdims=True)
        p_ = jnp.exp2(sc - m)
        o = _dot_nt(p_.astype(BF16), vt) / jnp.sum(p_, axis=1, keepdims=True)
        for g in range(B_GROUP):
            outs[n * B_GROUP + g] = o[g * ts:(g + 1) * ts]
    o_ref[...] = jnp.concatenate(outs, axis=1).astype(BF16)


def _attn_sample(page_table, q, bias, knew_t, vnew_t, k_t, v_t):
    db, ts, _ = q.shape
    n_pages = page_table.shape[1]
    page = k_t.shape[3]
    lpad = bias.shape[2]
    seq3 = lambda a: pl.BlockSpec((None,) + a.shape[1:], lambda b, pt: (b,) + (0,) * (a.ndim - 1))
    anyspec = pl.BlockSpec(memory_space=pl.ANY)
    kernel = functools.partial(_attn_sample_kernel, ts=ts, n_pages=n_pages, page=page)
    return pl.pallas_call(
        kernel,
        grid_spec=pltpu.PrefetchScalarGridSpec(
            num_scalar_prefetch=1,
            grid=(db,),
            in_specs=[seq3(q), seq3(bias), seq3(knew_t), seq3(vnew_t), anyspec, anyspec],
            out_specs=pl.BlockSpec((None, ts, B_WIDTH), lambda b, pt: (b, 0, 0)),
            scratch_shapes=[
                pltpu.VMEM((2, B_KV_HEADS, HEAD_DIM, lpad), F32),
                pltpu.VMEM((2, B_KV_HEADS, HEAD_DIM, lpad), F32),
                pltpu.SemaphoreType.DMA((2, 2)),
            ]),
        out_shape=jax.ShapeDtypeStruct((db, ts, B_WIDTH), BF16),
        compiler_params=pltpu.CompilerParams(dimension_semantics=("arbitrary",), vmem_limit_bytes=VMEM_LIMIT),
        name="attn_sample",
    )(page_table, q, bias, knew_t, vnew_t, k_t, v_t)


def _layer_norm_rows(y, gam, bet):
    mu = jnp.mean(y, axis=1, keepdims=True)
    d = y - mu
    var = jnp.mean(d * d, axis=1, keepdims=True)
    return d * lax.rsqrt(var + LN_EPS) * gam + bet


def _store_row_tiles(ref, val):
    r = val.shape[0]
    for c in range(ROW_CHUNKS):
        ref[pl.ds(c, r, stride=ROW_CHUNKS), :] = val[:, c * LANES:(c + 1) * LANES]


def _load_row_tiles(ref, r):
    return jnp.concatenate([ref[pl.ds(c, r, stride=ROW_CHUNKS), :] for c in range(ROW_CHUNKS)], axis=1)


ROUTER_ROWS = SUBLANES + N_EXPERTS


def _route(logits_t):
    r = logits_t.shape[1]
    far = float(LANES)

    def softmax_rows(x):
        e = jnp.exp(x - jnp.max(x, axis=0, keepdims=True))
        return e / jnp.sum(e, axis=0, keepdims=True)

    def first_max(p):
        rows = lax.broadcasted_iota(I32, p.shape, 0).astype(F32)
        best = jnp.max(p, axis=0, keepdims=True)
        return best, jnp.min(jnp.where(p == best, rows, far), axis=0, keepdims=True), rows

    g_w, g_sel, _ = first_max(softmax_rows(logits_t[0:N_GROUPS]))
    el = jnp.zeros((EXPERTS_PER_GROUP, r), F32)
    for g in range(N_GROUPS):
        lo = SUBLANES + g * EXPERTS_PER_GROUP
        el = jnp.where(g_sel == float(g), logits_t[lo:lo + EXPERTS_PER_GROUP], el)
    ep = softmax_rows(el)
    p1, i1, rows = first_max(ep)
    p2, i2, _ = first_max(jnp.where(rows == i1, -1.0, ep))
    denom = p1 + p2
    base = g_sel * float(EXPERTS_PER_GROUP)
    return jnp.concatenate([base + i1, base + i2, g_w * p1 / denom, g_w * p2 / denom,
                            jnp.zeros((SUBLANES - 4, r), F32)], axis=0)


def _split_bf16(x):
    hi = x.astype(BF16)
    return hi, (x - hi.astype(F32)).astype(BF16)


def _mid_kernel(x_ref, a_ref, b_ref, g1_ref, sh2_ref, sc2_ref, wo_ref, gam_ref, bet_ref, wr_ref,
                x1_ref, h2_ref, route_ref, *, alpha):
    mixed = _dot(a_ref[...], wo_ref[0:A_WIDTH, :]) + _dot(b_ref[...], wo_ref[A_WIDTH:A_WIDTH + B_WIDTH, :])
    x1 = _layer_norm_rows(alpha * x_ref[...] + g1_ref[...] * mixed, gam_ref[...], bet_ref[...])
    x1_ref[...] = x1
    h2 = x1 * (1.0 + sc2_ref[...]) + sh2_ref[...]
    _store_row_tiles(h2_ref, h2)
    w_hi, w_lo = _split_bf16(wr_ref[...])
    h_hi, h_lo = _split_bf16(h2)
    logits_t = _dot_nt(w_hi, h_hi) + _dot_nt(w_hi, h_lo) + _dot_nt(w_lo, h_hi)
    route_ref[...] = _route(logits_t)


def _mod_spec(mod3, comp, tiles_per_group):
    rm = mod3.shape[1]
    return pl.BlockSpec((None, rm, D_MODEL), lambda i: (i // tiles_per_group, 0, comp))


def _mid(x, a, b, mod3, tiles_per_group, tile, w_out, gam, bet, w_router, alpha):
    n = x.shape[0]
    rows = lambda w: pl.BlockSpec((tile, w), lambda i: (i, 0))
    whole = lambda arr: pl.BlockSpec(arr.shape, lambda i: (0,) * arr.ndim)
    return pl.pallas_call(
        functools.partial(_mid_kernel, alpha=alpha),
        grid=(n // tile,),
        in_specs=[rows(D_MODEL), rows(A_WIDTH), rows(B_WIDTH),
                  _mod_spec(mod3, 2, tiles_per_group), _mod_spec(mod3, 3, tiles_per_group),
                  _mod_spec(mod3, 4, tiles_per_group),
                  whole(w_out), whole(gam), whole(bet), whole(w_router)],
        out_specs=[rows(D_MODEL), pl.BlockSpec((tile * ROW_CHUNKS, LANES), lambda i: (i, 0)),
                   pl.BlockSpec((SUBLANES, tile), lambda i: (0, i))],
        out_shape=(jax.ShapeDtypeStruct((n, D_MODEL), F32),
                   jax.ShapeDtypeStruct((n * ROW_CHUNKS, LANES), F32),
                   jax.ShapeDtypeStruct((SUBLANES, n), F32)),
        compiler_params=pltpu.CompilerParams(dimension_semantics=("arbitrary",), vmem_limit_bytes=VMEM_LIMIT),
        name="mid",
    )(x, a, b, mod3, mod3, mod3, w_out, gam, bet, w_router)


META_CNT, META_START, META_END, META_NACT = 0, 1, 2, 3


def _plan_kernel(route_ref, pos_ref, meta_ref, te_ref, carry_ref, starts_ref, *, nt, tm):
    phase = pl.program_id(0)
    i = pl.program_id(1)

    @pl.when(i == 0)
    def _():
        carry_ref[...] = jnp.zeros(carry_ref.shape, F32)

    eid = route_ref[0:2, :].astype(I32)
    e_iota = lax.broadcasted_iota(I32, (N_EXPERTS, nt), 0)
    hit0 = eid[0:1, :] == e_iota
    hit1 = eid[1:2, :] == e_iota
    onehot = jnp.where(hit0 | hit1, 1.0, 0.0)

    @pl.when(phase == 0)
    def _():
        pos_ref[...] = jnp.zeros(pos_ref.shape, I32)
        carry_ref[...] = carry_ref[...] + jnp.sum(onehot, axis=1, keepdims=True)

        @pl.when(i == pl.num_programs(1) - 1)
        def _():
            cnt = carry_ref[...]
            tiles = jnp.floor((cnt + float(tm - 1)) * (1.0 / tm))
            r = lax.broadcasted_iota(I32, (N_EXPERTS, N_EXPERTS), 0)
            c = lax.broadcasted_iota(I32, (N_EXPERTS, N_EXPERTS), 1)
            lower = jnp.where(r > c, 1.0, 0.0).astype(BF16)
            tiles_before = _dot(lower, jnp.broadcast_to(tiles, (N_EXPERTS, LANES)).astype(BF16))[:, 0:1]
            tiles_end = tiles_before + tiles
            starts_ref[...] = tiles_before * float(tm)

            diag = (lax.broadcasted_iota(I32, (N_EXPERTS, LANES), 0)
                    == lax.broadcasted_iota(I32, (N_EXPERTS, LANES), 1))

            def as_row(col):
                return jnp.sum(jnp.where(diag, jnp.broadcast_to(col, (N_EXPERTS, LANES)), 0.0),
                               axis=0, keepdims=True)

            nact = jnp.broadcast_to(jnp.max(tiles_end, axis=0, keepdims=True), (1, LANES))
            meta_ref[...] = jnp.concatenate(
                [as_row(cnt), as_row(tiles_before * float(tm)), as_row(tiles_end * float(tm)), nact,
                 jnp.zeros((SUBLANES - 4, LANES), F32)], axis=0).astype(I32)
            tile_i = lax.broadcasted_iota(I32, (N_EXPERTS, te_ref.shape[1]), 1).astype(F32)
            owner = jnp.sum(jnp.where(tiles_end <= tile_i, 1.0, 0.0), axis=0, keepdims=True)
            te_ref[...] = jnp.minimum(owner, float(N_EXPERTS - 1)).astype(I32)

    @pl.when(phase == 1)
    def _():
        upper = (lax.broadcasted_iota(I32, (nt, nt), 0) <= lax.broadcasted_iota(I32, (nt, nt), 1))
        incl = _dot(onehot.astype(BF16), jnp.where(upper, 1.0, 0.0).astype(BF16))
        slot = starts_ref[...] + carry_ref[...] + incl - 1.0
        pos_ref[0:1, :] = jnp.sum(jnp.where(hit0, slot, 0.0), axis=0, keepdims=True).astype(I32)
        pos_ref[1:2, :] = jnp.sum(jnp.where(hit1, slot, 0.0), axis=0, keepdims=True).astype(I32)
        carry_ref[...] = carry_ref[...] + jnp.sum(onehot, axis=1, keepdims=True)


def _plan(route_t, nt, tm):
    n = route_t.shape[1]
    n_tiles = (2 * n) // tm + N_EXPERTS
    te_width = pl.cdiv(n_tiles, LANES) * LANES
    pos, meta, te = pl.pallas_call(
        functools.partial(_plan_kernel, nt=nt, tm=tm),
        grid=(2, n // nt),
        in_specs=[pl.BlockSpec((SUBLANES, nt), lambda p, i: (0, i))],
        out_specs=[pl.BlockSpec((2, nt), lambda p, i: (0, i * p)),
                   pl.BlockSpec((SUBLANES, LANES), lambda p, i: (0, 0)),
                   pl.BlockSpec((1, te_width), lambda p, i: (0, 0))],
        out_shape=(jax.ShapeDtypeStruct((2, n), I32), jax.ShapeDtypeStruct((SUBLANES, LANES), I32),
                   jax.ShapeDtypeStruct((1, te_width), I32)),
        scratch_shapes=[pltpu.VMEM((N_EXPERTS, 1), F32), pltpu.VMEM((N_EXPERTS, 1), F32)],
        compiler_params=pltpu.CompilerParams(dimension_semantics=("arbitrary", "arbitrary")),
        name="plan",
    )(route_t)
    return pos, meta, te, n_tiles


def _row_copy(src, src_row, dst, dst_row, sem):
    return pltpu.make_async_copy(src.at[pl.ds(src_row * ROW_CHUNKS, ROW_CHUNKS)],
                                 dst.at[pl.ds(dst_row * ROW_CHUNKS, ROW_CHUNKS)], sem)


def _dispatch_kernel(meta_ref, pos_ref, hp_ref, hs_ref, xs_out, zero_ref, sem, *, tt, blocks_p, tm, n_tiles):
    i = pl.program_id(0)

    def scatter(h_ref):
        def issue(j, _):
            for k in range(2):
                _row_copy(h_ref, j, xs_out, pos_ref[k, j], sem.at[k]).start()
            return 0

        def drain(j, _):
            for k in range(2):
                _row_copy(h_ref, j, xs_out, pos_ref[k, j], sem.at[k]).wait()
            return 0

        lax.fori_loop(0, tt, issue, 0, unroll=ROW_DMA_UNROLL)
        lax.fori_loop(0, tt, drain, 0, unroll=ROW_DMA_UNROLL)

    @pl.when(i < blocks_p)
    def _():
        scatter(hp_ref)

    @pl.when(i >= blocks_p)
    def _():
        scatter(hs_ref)

    @pl.when(i == pl.num_programs(0) - 1)
    def _():
        zero_ref[...] = jnp.zeros(zero_ref.shape, F32)

        def zero_rows(first_row, n_rows):
            return pltpu.make_async_copy(zero_ref.at[pl.ds(0, n_rows * ROW_CHUNKS)],
                                         xs_out.at[pl.ds(first_row * ROW_CHUNKS, n_rows * ROW_CHUNKS)], sem.at[0])

        def start_row(r, c):
            zero_rows(r, 1).start()
            return c

        def wait_row(r, c):
            zero_rows(r, 1).wait()
            return c

        def per_expert(row_fn):
            def body(e, c):
                lo = meta_ref[META_START, e] + meta_ref[META_CNT, e]
                return lax.fori_loop(lo, meta_ref[META_END, e], row_fn, c)
            return body

        def start_tile(t, c):
            zero_rows(t * tm, tm).start()
            return c

        def wait_tile(t, c):
            zero_rows(t * tm, tm).wait()
            return c

        nact = meta_ref[META_NACT, 0]
        lax.fori_loop(0, N_EXPERTS, per_expert(start_row), 0)
        lax.fori_loop(nact, n_tiles, start_tile, 0)
        lax.fori_loop(0, N_EXPERTS, per_expert(wait_row), 0)
        lax.fori_loop(nact, n_tiles, wait_tile, 0)


def _dispatch(meta, pos, h2_p, h2_s, tt, tm, n_tiles):
    blocks_p = h2_p.shape[0] // (tt * ROW_CHUNKS)
    blocks_s = h2_s.shape[0] // (tt * ROW_CHUNKS)
    return pl.pallas_call(
        functools.partial(_dispatch_kernel, tt=tt, blocks_p=blocks_p, tm=tm, n_tiles=n_tiles),
        grid_spec=pltpu.PrefetchScalarGridSpec(
            num_scalar_prefetch=1,
            grid=(blocks_p + blocks_s,),
            in_specs=[pl.BlockSpec((2, tt), lambda i, m: (0, i), memory_space=pltpu.SMEM),
                      pl.BlockSpec((tt * ROW_CHUNKS, LANES), lambda i, m: (jnp.minimum(i, blocks_p - 1), 0)),
                      pl.BlockSpec((tt * ROW_CHUNKS, LANES), lambda i, m: (jnp.maximum(i - blocks_p, 0), 0))],
            out_specs=pl.BlockSpec(memory_space=pl.ANY),
            scratch_shapes=[pltpu.VMEM((tm * ROW_CHUNKS, LANES), F32), pltpu.SemaphoreType.DMA((2,))]),
        out_shape=jax.ShapeDtypeStruct((n_tiles * tm * ROW_CHUNKS, LANES), F32),
        compiler_params=pltpu.CompilerParams(dimension_semantics=("arbitrary",)),
        name="dispatch",
    )(meta, pos, h2_p, h2_s)


def _experts_kernel(te_ref, meta_ref, xs_ref, wg_ref, wu_ref, wd_ref, ys_ref, wg_b, wu_b, wd_b, *, tm):
    i = pl.program_id(0)
    active = i < meta_ref[META_NACT, 0]
    fresh = (i == 0) | (te_ref[0, i] != te_ref[0, jnp.maximum(i - 1, 0)])

    @pl.when(active & fresh)
    def _():
        wg_b[...] = wg_ref[...].astype(BF16)
        wu_b[...] = wu_ref[...].astype(BF16)
        wd_b[...] = wd_ref[...].astype(BF16)

    @pl.when(active)
    def _():
        x = _load_row_tiles(xs_ref, tm).astype(BF16)
        hid = (jax.nn.silu(_dot(x, wg_b[...])) * _dot(x, wu_b[...])).astype(BF16)
        _store_row_tiles(ys_ref, _dot(hid, wd_b[...]))

    @pl.when(jnp.logical_not(active))
    def _():
        ys_ref[...] = jnp.zeros(ys_ref.shape, F32)


def _experts(te, meta, xs, w_gate, w_up, w_down, tm, n_tiles):
    last_active = lambda i, m: jnp.minimum(i, m[META_NACT, 0] - 1)
    tile = pl.BlockSpec((tm * ROW_CHUNKS, LANES), lambda i, te, m: (last_active(i, m), 0))
    w_in_spec = pl.BlockSpec((None, D_MODEL, D_EXPERT), lambda i, te, m: (te[0, last_active(i, m)], 0, 0))
    w_out_spec = pl.BlockSpec((None, D_EXPERT, D_MODEL), lambda i, te, m: (te[0, last_active(i, m)], 0, 0))
    return pl.pallas_call(
        functools.partial(_experts_kernel, tm=tm),
        grid_spec=pltpu.PrefetchScalarGridSpec(
            num_scalar_prefetch=2,
            grid=(n_tiles,),
            in_specs=[tile, w_in_spec, w_in_spec, w_out_spec],
            out_specs=pl.BlockSpec((tm * ROW_CHUNKS, LANES), lambda i, te, m: (i, 0)),
            scratch_shapes=[pltpu.VMEM((D_MODEL, D_EXPERT), BF16), pltpu.VMEM((D_MODEL, D_EXPERT), BF16),
                            pltpu.VMEM((D_EXPERT, D_MODEL), BF16)]),
        out_shape=jax.ShapeDtypeStruct(xs.shape, F32),
        compiler_params=pltpu.CompilerParams(dimension_semantics=("arbitrary",), vmem_limit_bytes=VMEM_LIMIT),
        name="experts",
    )(te, meta, xs, w_gate, w_up, w_down)


def _combine_kernel(pos_ref, route_ref, x1_ref, g2_ref, gam_ref, bet_ref, ys_hbm, y_ref, buf0, buf1, sem,
                    *, tt, alpha):
    bufs = (buf0, buf1)

    def issue(j, _):
        for k in range(2):
            _row_copy(ys_hbm, pos_ref[k, j], bufs[k], j, sem.at[k]).start()
        return 0

    def drain(j, _):
        for k in range(2):
            _row_copy(ys_hbm, pos_ref[k, j], bufs[k], j, sem.at[k]).wait()
        return 0

    lax.fori_loop(0, tt, issue, 0, unroll=ROW_DMA_UNROLL)
    lax.fori_loop(0, tt, drain, 0, unroll=ROW_DMA_UNROLL)
    route = route_ref[...]
    eye = lax.broadcasted_iota(I32, (tt, tt), 0) == lax.broadcasted_iota(I32, (tt, tt), 1)
    as_col = lambda row: jnp.sum(jnp.where(eye, jnp.broadcast_to(row, (tt, tt)), 0.0), axis=1, keepdims=True)
    f = as_col(route[2:3, :]) * _load_row_tiles(buf0, tt) + as_col(route[3:4, :]) * _load_row_tiles(buf1, tt)
    y_ref[...] = _layer_norm_rows(alpha * x1_ref[...] + g2_ref[...] * f, gam_ref[...], bet_ref[...])


def _combine(pos, route_t, col_block0, x1, mod3, tiles_per_group, tt, gam, bet, ys, alpha):
    n = x1.shape[0]
    rows = lambda w: pl.BlockSpec((tt, w), lambda i: (i, 0))
    whole = lambda arr: pl.BlockSpec(arr.shape, lambda i: (0,) * arr.ndim)
    return pl.pallas_call(
        functools.partial(_combine_kernel, tt=tt, alpha=alpha),
        grid=(n // tt,),
        in_specs=[pl.BlockSpec((2, tt), lambda i: (0, i + col_block0), memory_space=pltpu.SMEM),
                  pl.BlockSpec((SUBLANES, tt), lambda i: (0, i + col_block0)),
                  rows(D_MODEL), _mod_spec(mod3, 5, tiles_per_group), whole(gam), whole(bet),
                  pl.BlockSpec(memory_space=pl.ANY)],
        out_specs=rows(D_MODEL),
        out_shape=jax.ShapeDtypeStruct((n, D_MODEL), F32),
        scratch_shapes=[pltpu.VMEM((tt * ROW_CHUNKS, LANES), F32), pltpu.VMEM((tt * ROW_CHUNKS, LANES), F32),
                        pltpu.SemaphoreType.DMA((2,))],
        compiler_params=pltpu.CompilerParams(dimension_semantics=("arbitrary",), vmem_limit_bytes=VMEM_LIMIT),
        name="combine",
    )(pos, route_t, x1, mod3, gam, bet, ys)


def kernel(x_prompt, x_sample, cache_k, cache_v, cache_kidx, page_table, c_prompt, c_sample, w_ada, b_ada, w_in,
           a_ln_g, a_ln_b, w_spatial, b_spatial, w_out, ln1_g, ln1_b, w_group_router, w_expert_router, w_gate,
           w_up, w_down, ln2_g, ln2_b):
    depth = w_ada.shape[0]
    assert depth == 1, "one trunk layer"
    alpha = (2.0 * depth) ** 0.25
    bsz, seq, d = x_prompt.shape
    db, ts, _ = x_sample.shape
    n_pages = page_table.shape[1]
    page = cache_k.shape[2]
    past = n_pages * page
    tile_p = min(PROMPT_TILE, seq)
    assert d == D_MODEL and seq % tile_p == 0 and tile_p % CHUNK == 0 and ts <= SUBLANES and page == LANES
    l = 0

    n_c = bsz + db
    n_c_pad = pl.cdiv(n_c, SUBLANES) * SUBLANES
    c_all = jnp.pad(jnp.concatenate([c_prompt, c_sample], axis=0), ((0, n_c_pad - n_c), (0, 0)))
    mod = _ada(c_all, w_ada[l], b_ada[l][None, :])
    mod_p = mod[:bsz]
    mod_s = mod[bsz:n_c]

    w_in_b = _pad_w_in(w_in[l])
    w_out_b = w_out[l].astype(BF16)
    mavg = _head_avg_matrix()
    gam_a = a_ln_g[l].reshape(1, A_WIDTH)
    bet_a = a_ln_b[l].reshape(1, A_WIDTH)
    w_router = jnp.concatenate([w_group_router[l].T, jnp.zeros((SUBLANES - N_GROUPS, d), F32),
                                w_expert_router[l].reshape(d, N_EXPERTS).T], axis=0)
    ln1 = (ln1_g[l][None, :], ln1_b[l][None, :])
    ln2 = (ln2_g[l][None, :], ln2_b[l][None, :])

    wsp = w_spatial[l].reshape(A_HEADS // 2, 2, CHUNK, CHUNK).transpose(0, 2, 1, 3).reshape(A_HEADS // 2, CHUNK, 2 * CHUNK)
    bsp = jnp.repeat(b_spatial[l].T, HEAD_DIM, axis=1)
    a_p, kt_p, vt32_p, kit_p, kb_p, kiwib_p, vt_p, qt_p, qit_p, wit_p = _front_prompt(
        x_prompt, mod_p.reshape(bsz, 6, d), w_in_b, _rot_tables(np.arange(seq)), mavg, gam_a, bet_a, wsp, bsp, tile_p)
    b_p = _attn_prompt(qit_p, wit_p, qt_p, kiwib_p, kb_p, vt_p)
    n_p = bsz * seq
    x1_p, h2_p, route_p = _mid(x_prompt.reshape(n_p, d), a_p.reshape(n_p, A_WIDTH), b_p.reshape(n_p, B_WIDTH),
                               mod_p.reshape(bsz, 1, 6 * d), seq // tile_p, tile_p, w_out_b, *ln1, w_router, alpha)

    r_s = ts * db
    x_tm = x_sample.transpose(1, 0, 2).reshape(r_s, d)
    rt_s = _rot_tables(np.repeat(past + np.arange(ts), db))
    w_small = w_spatial[l][:, :ts, :ts]
    wl = jnp.repeat(w_small.transpose(1, 2, 0).reshape(ts * ts, A_HEADS), HEAD_DIM, axis=1)
    bl = jnp.repeat(b_spatial[l][:, :ts].T, HEAD_DIM, axis=1)
    a_s, q_s, k_s, v_s, qi_s, kiwi_s, vg_s = _front_sample(x_tm, mod_s, w_in_b, rt_s, mavg, gam_a, bet_a, wl, bl, ts, db)

    def seq_major(a):
        return a.reshape(ts, db, a.shape[-1]).transpose(1, 0, 2)

    def new_t(a, heads):
        a = a.reshape(ts, db, heads, HEAD_DIM).transpose(1, 2, 3, 0)
        return jnp.pad(a, ((0, 0), (0, 0), (0, 0), (0, LANES - ts)))

    ga = min(IDX_SEQS, db)
    assert db % ga == 0 and (ga * ts) % SUBLANES == 0
    qi4 = qi_s.reshape(ts, db, IDX_HEADS, IDX_DIM).transpose(1, 2, 0, 3).reshape(db, IDX_HEADS * ts, IDX_DIM)
    wcol = kiwi_s[:, WI_LANE:WI_LANE + IDX_HEADS].reshape(ts, db, IDX_HEADS).transpose(1, 2, 0).reshape(db, IDX_HEADS * ts, 1)
    qpos = jnp.tile(past + jnp.arange(ts, dtype=I32), db).reshape(db * ts, 1)
    kinew_t = new_t(kiwi_s[:, :IDX_DIM], 1)[:, 0]
    kidx_t = jnp.transpose(cache_kidx[l], (0, 2, 1))
    k_t = jnp.transpose(cache_k[l], (0, 2, 3, 1))
    v_t = jnp.transpose(cache_v[l], (0, 2, 3, 1))
    bias_s = _idx_sample(page_table, qi4, wcol, qpos, kinew_t, kidx_t, ts, ga)
    b_s = _attn_sample(page_table, seq_major(q_s), bias_s, new_t(k_s, B_KV_HEADS), new_t(v_s, B_KV_HEADS), k_t, v_t)
    b_s_tm = b_s.transpose(1, 0, 2).reshape(r_s, B_WIDTH)
    x1_s, h2_s, route_s = _mid(x_tm, a_s, b_s_tm, mod_s.reshape(1, db, 6 * d), ts, db, w_out_b, *ln1, w_router, alpha)

    n_all = n_p + r_s
    tok_p = min(TOKEN_TILE, seq)
    rank_tile = min(RANK_TILE, n_all)
    assert n_all % rank_tile == 0 and n_p % tok_p == 0 and r_s % tok_p == 0 and n_p % db == 0
    route_all = jnp.concatenate([route_p, route_s], axis=1)
    pos, meta, te, n_tiles = _plan(route_all, rank_tile, MOE_TILE)
    xs = _dispatch(meta, pos, h2_p, h2_s, tok_p, MOE_TILE, n_tiles)
    ys = _experts(te, meta, xs, w_gate[l], w_up[l], w_down[l], MOE_TILE, n_tiles)
    y_p = _combine(pos, route_all, 0, x1_p, mod_p.reshape(bsz, 1, 6 * d), seq // tok_p, tok_p, *ln2, ys, alpha)
    y_s_tm = _combine(pos, route_all, n_p // db, x1_s, mod_s.reshape(1, db, 6 * d), ts, db, *ln2, ys, alpha)
    y_s = y_s_tm.reshape(ts, db, d).transpose(1, 0, 2)

    kv5 = lambda a, n, t: a.reshape(1, n, t, B_KV_HEADS, HEAD_DIM)
    kv5_t = lambda a: a.reshape(1, bsz, B_KV_HEADS, HEAD_DIM, seq).transpose(0, 1, 4, 2, 3)
    return (y_p.reshape(bsz, seq, d), y_s,
            kv5_t(kt_p), kv5_t(vt32_p), kit_p.transpose(0, 2, 1)[None],
            kv5(seq_major(k_s), db, ts), kv5(seq_major(v_s), db, ts), seq_major(kiwi_s)[..., :IDX_DIM][None],
            seq_major(vg_s).reshape(1, db, ts, A_HEADS, HEAD_DIM))
```

```python
import functools

import jax
import jax.numpy as jnp
import numpy as np
from jax import lax
from jax.experimental import pallas as pl
from jax.experimental.pallas import tpu as pltpu

F32 = jnp.float32
BF16 = jnp.bfloat16
I32 = jnp.int32

D_MODEL = 1024
HEAD_DIM = 64
A_HEADS = 8
A_WIDTH = A_HEADS * HEAD_DIM
CHUNK = 128
B_HEADS = 8
B_KV_HEADS = 4
B_GROUP = B_HEADS // B_KV_HEADS
B_WIDTH = B_HEADS * HEAD_DIM
KV_WIDTH = B_KV_HEADS * HEAD_DIM
IDX_HEADS = 4
IDX_DIM = 64
IDX_WIDTH = IDX_HEADS * IDX_DIM
TOPK_MAX = 256
ROPE_THETA = 500000.0
ROT_DIM = HEAD_DIM // 4
ROT_HALF = ROT_DIM // 2
ATTN_SCALE = HEAD_DIM ** -0.5
LOG2_E = 1.4426950408889634
Q_SCALE = ATTN_SCALE * LOG2_E
DENOM_ROWS = 16
N_GROUPS = 4
EXPERTS_PER_GROUP = 8
N_EXPERTS = N_GROUPS * EXPERTS_PER_GROUP
D_EXPERT = 512
LN_EPS = 1e-5

LANES = 128
SUBLANES = 8
ROW_CHUNKS = D_MODEL // LANES

C_AU, C_AV, C_Q, C_K, C_V, C_QI, C_KI = 0, 512, 1024, 1536, 1792, 2048, 2304
IN_WIDTH = C_KI + IDX_DIM + IDX_HEADS
IN_PAD = 2432
WI_LANE = IDX_DIM

INT_MIN = -(2 ** 31)
INT_MAX = 2 ** 31 - 1
VMEM_LIMIT = 48 * 1024 * 1024

PROMPT_TILE = 256
MOE_TILE = 256
TOKEN_TILE = 256
RANK_TILE = 512
IDX_SEQS = 16
STEPS_PER_CHECK = 4
ROW_DMA_UNROLL = 8
MID_TILE = 512


def _dot(a, b):
    return jnp.dot(a, b, preferred_element_type=F32)


def _dot_nt(a, b):
    return lax.dot_general(a, b, (((1,), (1,)), ((), ())), preferred_element_type=F32)


def _split_dot(x, m):
    hi = x.astype(BF16)
    lo = (x - hi.astype(F32)).astype(BF16)
    return _dot(hi, m) + _dot(lo, m)


def _ada_kernel(c_ref, w_ref, b_ref, o_ref):
    s = jax.nn.silu(c_ref[...]).astype(BF16)
    o_ref[...] = _dot(s, w_ref[...].astype(BF16)) + b_ref[...]


def _ada(c_all, w_ada, b_ada):
    rows = c_all.shape[0]
    n_out = w_ada.shape[1]
    tn = 1024
    return pl.pallas_call(
        _ada_kernel,
        grid=(n_out // tn,),
        in_specs=[
            pl.BlockSpec((rows, D_MODEL), lambda j: (0, 0)),
            pl.BlockSpec((D_MODEL, tn), lambda j: (0, j)),
            pl.BlockSpec((1, tn), lambda j: (0, j)),
        ],
        out_specs=pl.BlockSpec((rows, tn), lambda j: (0, j)),
        out_shape=jax.ShapeDtypeStruct((rows, n_out), F32),
        compiler_params=pltpu.CompilerParams(
            dimension_semantics=("arbitrary",), vmem_limit_bytes=VMEM_LIMIT),
        name="ada",
    )(c_all, w_ada, b_ada)


def _rotate(x, rt, blk):
    c = rt[:, 0:LANES]
    s_lo = rt[:, LANES:2 * LANES]
    s_hi = rt[:, 2 * LANES:3 * LANES]
    if blk == 1:
        head = lax.broadcasted_iota(I32, c.shape, 1) < IDX_DIM
        c = jnp.where(head, c, 1.0)
        s_lo = jnp.where(head, s_lo, 0.0)
        s_hi = jnp.where(head, s_hi, 0.0)
    outs = []
    for j in range(x.shape[1] // LANES):
        xb = x[:, j * LANES:(j + 1) * LANES]
        up = pltpu.roll(xb, LANES - ROT_HALF, 1)
        dn = pltpu.roll(xb, ROT_HALF, 1)
        outs.append(xb * c + up * s_lo + dn * s_hi)
    return outs[0] if len(outs) == 1 else jnp.concatenate(outs, axis=1)


def _head_ln(g, mavg, gam, bet):
    def seg_mean(x):
        parts = [_split_dot(x[:, j * 256:(j + 1) * 256], mavg) for j in range(A_WIDTH // 256)]
        return jnp.concatenate(parts, axis=1)
    mu = seg_mean(g)
    d = g - mu
    var = seg_mean(d * d)
    return d * lax.rsqrt(var + LN_EPS) * gam + bet


def _project(h, w_ref, rt, mavg, gam, bet):
    u = jax.nn.gelu(_dot(h, w_ref[:, C_AU:C_AV]))
    vg = _head_ln(jax.nn.gelu(_dot(h, w_ref[:, C_AV:C_Q])), mavg, gam, bet)
    q = _rotate(_dot(h, w_ref[:, C_Q:C_K]), rt, 0) * Q_SCALE
    k = _rotate(_dot(h, w_ref[:, C_K:C_V]), rt, 0)
    v = _dot(h, w_ref[:, C_V:C_QI])
    qi = _rotate(_dot(h, w_ref[:, C_QI:C_KI]), rt, 0)
    kiwi = _rotate(_dot(h, w_ref[:, C_KI:IN_PAD]), rt, 1)
    return u, vg, q, k, v, qi, kiwi


def _front_prompt_kernel(x_ref, mod_ref, w_ref, rt_ref, mavg_ref, gam_ref, bet_ref, wsp_ref, bsp_ref,
                         a_ref, kt_ref, vt_ref, kit_ref, kb_ref, kiwib_ref, vtb_ref, qt_ref, qit_ref, wit_ref):
    shift = mod_ref[0:1, :]
    scale = mod_ref[1:2, :]
    h = (x_ref[...] * (1.0 + scale) + shift).astype(BF16)
    u, vg, q, k, v, qi, kiwi = _project(h, w_ref, rt_ref[...], mavg_ref[...], gam_ref[...], bet_ref[...])
    v_t = jnp.transpose(v)
    kiwi_t = jnp.transpose(kiwi)
    kt_ref[...] = jnp.transpose(k)
    vt_ref[...] = v_t
    kit_ref[...] = kiwi_t[0:IDX_DIM, :]
    kb_ref[...] = k.astype(BF16)
    kiwib_ref[...] = kiwi.astype(BF16)
    vtb_ref[...] = v_t.astype(BF16)
    qt_ref[...] = jnp.transpose(q).astype(BF16)
    qit_ref[...] = jnp.transpose(qi).astype(BF16)
    wit_ref[...] = kiwi_t[WI_LANE:WI_LANE + SUBLANES, :]

    rows = lax.broadcasted_iota(I32, (CHUNK, 2 * CHUNK), 0)
    cols = lax.broadcasted_iota(I32, (CHUNK, 2 * CHUNK), 1) % CHUNK
    causal = cols <= rows
    lane = lax.broadcasted_iota(I32, (CHUNK, LANES), 1)
    tt = x_ref.shape[0]
    for cidx in range(tt // CHUNK):
        rs = slice(cidx * CHUNK, (cidx + 1) * CHUNK)
        blocks = []
        for p in range(A_HEADS // 2):
            wcat = jnp.where(causal, wsp_ref[p], 0.0).astype(BF16)
            vb = vg[rs, p * LANES:(p + 1) * LANES]
            rhs = jnp.concatenate([jnp.where(lane < HEAD_DIM, vb, 0.0),
                                   jnp.where(lane >= HEAD_DIM, vb, 0.0)], axis=0).astype(BF16)
            blocks.append(_dot(wcat, rhs))
        s = jnp.concatenate(blocks, axis=1) + bsp_ref[...]
        a_ref[rs, :] = (u[rs, :] * s).astype(BF16)


def _front_prompt(x, mod, w_in, rt, mavg, gam, bet, wsp, bsp, tt):
    b, t, _ = x.shape
    nc = t // tt
    tok = lambda w: pl.BlockSpec((None, tt, w), lambda i, j: (i, j, 0))
    tr = lambda r: pl.BlockSpec((None, None, r, tt), lambda i, j: (i, j, 0, 0))
    pos_minor = lambda r: pl.BlockSpec((None, r, tt), lambda i, j: (i, 0, j))
    const2 = lambda a: pl.BlockSpec(a.shape, lambda i, j: (0,) * a.ndim)
    out_shapes = (
        jax.ShapeDtypeStruct((b, t, A_WIDTH), BF16),
        jax.ShapeDtypeStruct((b, KV_WIDTH, t), F32),
        jax.ShapeDtypeStruct((b, KV_WIDTH, t), F32),
        jax.ShapeDtypeStruct((b, IDX_DIM, t), F32),
        jax.ShapeDtypeStruct((b, t, KV_WIDTH), BF16),
        jax.ShapeDtypeStruct((b, t, LANES), BF16),
        jax.ShapeDtypeStruct((b, nc, KV_WIDTH, tt), BF16),
        jax.ShapeDtypeStruct((b, nc, B_WIDTH, tt), BF16),
        jax.ShapeDtypeStruct((b, nc, IDX_WIDTH, tt), BF16),
        jax.ShapeDtypeStruct((b, nc, SUBLANES, tt), F32),
    )
    return pl.pallas_call(
        _front_prompt_kernel,
        grid=(b, nc),
        in_specs=[
            tok(D_MODEL),
            pl.BlockSpec((None, 6, D_MODEL), lambda i, j: (i, 0, 0)),
            const2(w_in),
            pl.BlockSpec((tt, 3 * LANES), lambda i, j: (j, 0)),
            const2(mavg), const2(gam), const2(bet), const2(wsp), const2(bsp),
        ],
        out_specs=[tok(A_WIDTH), pos_minor(KV_WIDTH), pos_minor(KV_WIDTH), pos_minor(IDX_DIM), tok(KV_WIDTH),
                   tok(LANES), tr(KV_WIDTH), tr(B_WIDTH), tr(IDX_WIDTH), tr(SUBLANES)],
        out_shape=out_shapes,
        compiler_params=pltpu.CompilerParams(
            dimension_semantics=("arbitrary", "arbitrary"), vmem_limit_bytes=VMEM_LIMIT),
        name="front_prompt",
    )(x, mod, w_in, rt, mavg, gam, bet, wsp, bsp)


def _order_key(score, kpos, idx_bits):
    offs = 1 << idx_bits
    bits = pltpu.bitcast(score, I32)
    key = bits ^ ((bits >> 31) & 0x7FFFFFFF)
    key = key + jnp.where(score > 0.0, offs, 0)
    return jnp.where(score == 0.0, offs - kpos, key)


def _select_threshold(count_ge, count_gt_eq_lt, shape, topk, idx_bits):
    offs = 1 << idx_bits
    first_candidates = (offs + 1, 1)
    max_steps = -(-(32 + len(first_candidates)) // STEPS_PER_CHECK) * STEPS_PER_CHECK

    def unresolved(carry):
        i, lo, hi, n_lo = carry
        pending = (n_lo != topk) & (hi - 1 > lo)
        return (i < max_steps) & (jnp.max(pending.astype(F32)) > 0.0)

    def bisect(carry):
        i, lo, hi, n_lo = carry
        for _ in range(STEPS_PER_CHECK):
            mid = (lo >> 1) + (hi >> 1) + (lo & hi & 1)
            for step, value in enumerate(first_candidates):
                forced = jnp.where(i == step, value, INT_MIN)
                mid = jnp.where((lo < forced) & (forced < hi), forced, mid)
            tot = count_ge(mid)
            take = tot >= topk
            lo = jnp.where(take, mid, lo)
            n_lo = jnp.where(take, tot, n_lo)
            hi = jnp.where(take, hi, mid)
            i = i + 1
        return i, lo, hi, n_lo

    never = jnp.full(shape, INT_MAX, I32)
    _, thr, _, n_lo = lax.while_loop(
        unresolved, bisect, (jnp.int32(0), jnp.full(shape, INT_MIN + 1, I32), jnp.full(shape, INT_MAX, I32), never))
    tied = (n_lo > topk) & (n_lo != INT_MAX)
    big = jnp.full(shape, offs, I32)

    def resolve_ties(_):
        n_gt, _ = count_gt_eq_lt(thr, big)
        need = topk - n_gt

        def idx_step(i, cut):
            cand = cut | lax.shift_left(jnp.int32(1), idx_bits - 1 - i)
            _, n_eq = count_gt_eq_lt(thr, cand)
            return jnp.where(n_eq <= need, cand, cut)

        cut = lax.fori_loop(0, idx_bits, idx_step, jnp.zeros(shape, I32))
        return jnp.where(tied, cut, big)

    any_tied = jnp.max(tied.astype(F32)) > 0.0
    cut = lax.cond(any_tied, resolve_ties, lambda _: big, 0)
    return thr, cut


def _fold_rows(x):
    acc = x[0:SUBLANES]
    for r in range(1, x.shape[0] // SUBLANES):
        acc = acc + x[r * SUBLANES:(r + 1) * SUBLANES]
    return acc


def _col_total(cnt8):
    return jnp.sum(cnt8.astype(F32), axis=0, keepdims=True).astype(I32)


def _attn_prompt_kernel(qit_ref, wit_ref, qt_ref, kiwib_ref, kb_ref, vt_ref, o_ref,
                        key_ref, w4_ref, wq_ref, s0_ref, s1_ref, bias_ref, m_ref, acc_ref, *, tq, topk, idx_bits):
    j = pl.program_id(1)
    kc = tq
    n_kc = j + 1

    @pl.when((pl.program_id(0) == 0) & (j == 0))
    def _():
        w4_ref[...] = jnp.zeros(w4_ref.shape, BF16)
        wq_ref[...] = jnp.zeros(wq_ref.shape, BF16)

    for h in range(IDX_HEADS):
        w4_ref[0:IDX_DIM, h * tq:(h + 1) * tq] = qit_ref[h * IDX_DIM:(h + 1) * IDX_DIM, :]
    for h in range(B_HEADS):
        n = h // B_GROUP
        wq_ref[h, n * HEAD_DIM:(n + 1) * HEAD_DIM, :] = qt_ref[h * HEAD_DIM:(h + 1) * HEAD_DIM, :]

    wit = wit_ref[...]
    krow = lax.broadcasted_iota(I32, (kc, tq), 0)
    qpos = j * tq + lax.broadcasted_iota(I32, (kc, tq), 1)

    def score_chunk(c, _):
        k0 = pl.multiple_of(c * kc, kc)
        s = _dot(kiwib_ref[pl.ds(k0, kc), :], w4_ref[...])
        tot = jnp.maximum(s[:, 0:tq], 0.0) * wit[0:1, :]
        for h in range(1, IDX_HEADS):
            tot = tot + jnp.maximum(s[:, h * tq:(h + 1) * tq], 0.0) * wit[h:h + 1, :]
        kpos = k0 + krow
        key_ref[c] = jnp.where(kpos <= qpos, _order_key(tot, kpos, idx_bits), INT_MIN)
        return 0

    lax.fori_loop(0, n_kc, score_chunk, 0)
    key_ref[n_kc] = jnp.full((kc, tq), INT_MIN, I32)

    def count_ge(cand):
        def body(i, cnt):
            for c in (2 * i, 2 * i + 1):
                cnt = cnt + _fold_rows(jnp.where(key_ref[c] >= cand, 1, 0))
            return cnt
        return _col_total(lax.fori_loop(0, (n_kc + 1) // 2, body, jnp.zeros((SUBLANES, tq), I32)))

    def count_gt_eq_lt(thr, pos):
        def body(c, carry):
            n_gt, n_eq = carry
            key = key_ref[c]
            n_gt = n_gt + _fold_rows(jnp.where(key > thr, 1, 0))
            n_eq = n_eq + _fold_rows(jnp.where((key == thr) & (c * kc + krow < pos), 1, 0))
            return n_gt, n_eq
        z = jnp.zeros((SUBLANES, tq), I32)
        n_gt, n_eq = lax.fori_loop(0, n_kc, body, (z, z))
        return _col_total(n_gt), _col_total(n_eq)

    thr, cut = _select_threshold(count_ge, count_gt_eq_lt, (1, tq), topk, idx_bits)

    m_ref[...] = jnp.full(m_ref.shape, jnp.finfo(F32).min, F32)
    acc_ref[...] = jnp.zeros(acc_ref.shape, F32)

    def key_block(c):
        return kb_ref[pl.ds(pl.multiple_of(c * kc, kc), kc), :]

    def selection_bias(c, open_bias):
        key = key_ref[c]
        sel = (key > thr) | ((key == thr) & (c * kc + krow < cut))
        bias_ref[...] = jnp.where(sel, open_bias, -jnp.inf)

    selection_bias(0, 0.0)
    kblk0 = key_block(0)
    for h in range(B_HEADS):
        s0_ref[h] = _dot(kblk0, wq_ref[h]) + bias_ref[...]

    def stage(c, src, dst, c_next, open_next):
        selection_bias(c_next, open_next)
        kblk_next = key_block(c_next)
        vtc = vt_ref[c]
        for h in range(B_HEADS):
            n = h // B_GROUP
            dst[h] = _dot(kblk_next, wq_ref[h]) + bias_ref[...]
            s = src[h]
            m_old = m_ref[h]
            m_new = jnp.maximum(m_old, jnp.max(s, axis=0, keepdims=True))
            p = jnp.exp2(s - m_new).astype(BF16)
            v_aug = jnp.concatenate([vtc[n * HEAD_DIM:(n + 1) * HEAD_DIM, :], ones_rows], axis=0)
            acc_ref[h] = jnp.exp2(m_old - m_new) * acc_ref[h] + _dot(v_aug, p)
            m_ref[h] = m_new

    ones_rows = jnp.where(lax.broadcasted_iota(I32, (DENOM_ROWS, kc), 0) == 0, 1.0, 0.0).astype(BF16)
    last = n_kc - 1

    def attend_pair(i, _):
        c0 = 2 * i
        c1 = jnp.minimum(c0 + 1, last)
        stage(c0, s0_ref, s1_ref, c1, jnp.where(c0 + 1 <= last, 0.0, -jnp.inf))
        stage(c1, s1_ref, s0_ref, jnp.minimum(c0 + 2, last), 0.0)
        return 0

    lax.fori_loop(0, (n_kc + 1) // 2, attend_pair, 0)
    out_t = jnp.concatenate([acc_ref[h, 0:HEAD_DIM, :] / acc_ref[h, HEAD_DIM:HEAD_DIM + 1, :]
                             for h in range(B_HEADS)], axis=0)
    o_ref[...] = jnp.transpose(out_t).astype(BF16)


def _attn_prompt(qit, wit, qt, kiwib, kb, vt):
    b, nc, _, tq = qt.shape
    t = nc * tq
    topk = min(TOPK_MAX, t // 4)
    idx_bits = max(1, (t - 1).bit_length())
    tr = lambda r: pl.BlockSpec((None, None, r, tq), lambda i, j: (i, j, 0, 0))
    full = lambda w: pl.BlockSpec((None, t, w), lambda i, j: (i, 0, 0))
    kernel = functools.partial(_attn_prompt_kernel, tq=tq, topk=topk, idx_bits=idx_bits)
    return pl.pallas_call(
        kernel,
        grid=(b, nc),
        in_specs=[tr(IDX_WIDTH), tr(SUBLANES), tr(B_WIDTH), full(LANES), full(KV_WIDTH),
                  pl.BlockSpec((None, nc, KV_WIDTH, tq), lambda i, j: (i, 0, 0, 0))],
        out_specs=pl.BlockSpec((None, tq, B_WIDTH), lambda i, j: (i, j, 0)),
        out_shape=jax.ShapeDtypeStruct((b, t, B_WIDTH), BF16),
        scratch_shapes=[
            pltpu.VMEM((nc + 1, tq, tq), I32),
            pltpu.VMEM((LANES, IDX_HEADS * tq), BF16),
            pltpu.VMEM((B_HEADS, KV_WIDTH, tq), BF16),
            pltpu.VMEM((B_HEADS, tq, tq), F32),
            pltpu.VMEM((B_HEADS, tq, tq), F32),
            pltpu.VMEM((tq, tq), F32),
            pltpu.VMEM((B_HEADS, 1, tq), F32),
            pltpu.VMEM((B_HEADS, HEAD_DIM + DENOM_ROWS, tq), F32),
        ],
        compiler_params=pltpu.CompilerParams(
            dimension_semantics=("arbitrary", "arbitrary"), vmem_limit_bytes=VMEM_LIMIT),
        name="attn_prompt",
    )(qit, wit, qt, kiwib, kb, vt)


def _rot_tables(pos):
    r = pos.shape[0]
    inv_freq = np.float32(ROPE_THETA) ** (-np.arange(ROT_HALF, dtype=np.float32) * np.float32(2.0) / np.float32(ROT_DIM))
    ang = pos.astype(np.float32)[:, None] * inv_freq[None, :]
    cos, sin = np.cos(ang), np.sin(ang)
    rest = HEAD_DIM - ROT_DIM
    c64 = np.concatenate([cos, cos, np.ones((r, rest), np.float32)], axis=1)
    lo64 = np.concatenate([-sin, np.zeros((r, HEAD_DIM - ROT_HALF), np.float32)], axis=1)
    hi64 = np.concatenate([np.zeros((r, ROT_HALF), np.float32), sin, np.zeros((r, rest), np.float32)], axis=1)
    return jnp.asarray(np.concatenate([c64, c64, lo64, lo64, hi64, hi64], axis=1).astype(np.float32))


def _head_avg_matrix():
    return jnp.kron(jnp.eye(256 // HEAD_DIM, dtype=F32), jnp.full((HEAD_DIM, HEAD_DIM), 1.0 / HEAD_DIM, F32)).astype(BF16)


def _pad_w_in(w_in):
    return jnp.pad(w_in, ((0, 0), (0, IN_PAD - IN_WIDTH))).astype(BF16)


def _front_sample_kernel(x_ref, shift_ref, scale_ref, w_ref, rt_ref, mavg_ref, gam_ref, bet_ref, wl_ref, bl_ref,
                         a_ref, q_ref, k_ref, v_ref, qi_ref, kiwi_ref, vg_ref, *, ts, db):
    one_scale = 1.0 + scale_ref[...]
    shift = shift_ref[...]
    h = jnp.concatenate([x_ref[t * db:(t + 1) * db, :] * one_scale + shift for t in range(ts)], axis=0).astype(BF16)
    u, vg, q, k, v, qi, kiwi = _project(h, w_ref, rt_ref[...], mavg_ref[...], gam_ref[...], bet_ref[...])
    q_ref[...] = q.astype(BF16)
    k_ref[...] = k
    v_ref[...] = v
    qi_ref[...] = qi.astype(BF16)
    kiwi_ref[...] = kiwi
    vg_ref[...] = vg
    for t in range(ts):
        s = bl_ref[t:t + 1, :]
        for src in range(t + 1):
            s = s + wl_ref[t * ts + src:t * ts + src + 1, :] * vg[src * db:(src + 1) * db, :]
        a_ref[t * db:(t + 1) * db, :] = (u[t * db:(t + 1) * db, :] * s).astype(BF16)


def _front_sample(x_tm, mod_s, w_in, rt, mavg, gam, bet, wl, bl, ts, db):
    r = ts * db
    whole = lambda a: pl.BlockSpec(a.shape, lambda i: (0,) * a.ndim)
    out = lambda w, dt: jax.ShapeDtypeStruct((r, w), dt)
    outs = (out(A_WIDTH, BF16), out(B_WIDTH, BF16), out(KV_WIDTH, F32), out(KV_WIDTH, F32),
            out(IDX_WIDTH, BF16), out(LANES, F32), out(A_WIDTH, F32))
    return pl.pallas_call(
        functools.partial(_front_sample_kernel, ts=ts, db=db),
        grid=(1,),
        in_specs=[
            whole(x_tm),
            pl.BlockSpec((db, D_MODEL), lambda i: (0, 0)),
            pl.BlockSpec((db, D_MODEL), lambda i: (0, 1)),
            whole(w_in), whole(rt), whole(mavg), whole(gam), whole(bet), whole(wl), whole(bl),
        ],
        out_specs=[pl.BlockSpec((r, s.shape[1]), lambda i: (0, 0)) for s in outs],
        out_shape=outs,
        compiler_params=pltpu.CompilerParams(dimension_semantics=("arbitrary",), vmem_limit_bytes=VMEM_LIMIT),
        name="front_sample",
    )(x_tm, mod_s, mod_s, w_in, rt, mavg, gam, bet, wl, bl)


def _idx_sample_kernel(pt_ref, qi4_ref, wcol_ref, qpos_ref, kinew_ref, kidx_hbm, bias_ref,
                       ki_buf, key_ref, sem, *, ga, ts, n_pages, page, lpad, topk, idx_bits):
    i = pl.program_id(0)
    n_steps = pl.num_programs(0)
    past = n_pages * page
    rows = ga * ts
    slot = i % 2

    def page_copy(step, to_slot, g, p):
        phys = pt_ref[step * ga + g, p]
        return pltpu.make_async_copy(kidx_hbm.at[phys], ki_buf.at[to_slot, g, :, pl.ds(p * page, page)],
                                     sem.at[to_slot])

    def start_all(step, to_slot):
        for g in range(ga):
            for p in range(n_pages):
                page_copy(step, to_slot, g, p).start()

    @pl.when(i == 0)
    def _():
        start_all(0, 0)

    @pl.when(i + 1 < n_steps)
    def _():
        start_all(i + 1, 1 - slot)

    ki_buf[slot, :, :, pl.ds(past, LANES)] = kinew_ref[...]
    for g in range(ga):
        for p in range(n_pages):
            page_copy(i, slot, g, p).wait()

    kpos = lax.broadcasted_iota(I32, (ts, lpad), 1)
    for g in range(ga):
        s = _dot(qi4_ref[g], ki_buf[slot, g].astype(BF16))
        r = jnp.maximum(s, 0.0) * wcol_ref[g]
        tot = r[0:ts]
        for h in range(1, IDX_HEADS):
            tot = tot + r[h * ts:(h + 1) * ts]
        adm = (kpos <= qpos_ref[g * ts:(g + 1) * ts, :]) & (kpos < past + ts)
        key_ref[g * ts:(g + 1) * ts, :] = jnp.where(adm, _order_key(tot, kpos, idx_bits), INT_MIN)

    sub = lpad // LANES
    lane128 = lax.broadcasted_iota(I32, (rows, LANES), 1)

    def row_total(cnt):
        tot = jnp.sum(cnt.astype(F32), axis=1, keepdims=True)
        return jnp.broadcast_to(tot, cnt.shape).astype(I32)

    def count_ge(cand):
        cnt = jnp.zeros((rows, LANES), I32)
        for s_ in range(sub):
            cnt = cnt + jnp.where(key_ref[:, s_ * LANES:(s_ + 1) * LANES] >= cand, 1, 0)
        return row_total(cnt)

    def count_gt_eq_lt(thr, pos):
        n_gt = jnp.zeros((rows, LANES), I32)
        n_eq = jnp.zeros((rows, LANES), I32)
        for s_ in range(sub):
            kk = key_ref[:, s_ * LANES:(s_ + 1) * LANES]
            n_gt = n_gt + jnp.where(kk > thr, 1, 0)
            n_eq = n_eq + jnp.where((kk == thr) & (s_ * LANES + lane128 < pos), 1, 0)
        return row_total(n_gt), row_total(n_eq)

    thr, cut = _select_threshold(count_ge, count_gt_eq_lt, (rows, LANES), topk, idx_bits)
    for s_ in range(sub):
        kk = key_ref[:, s_ * LANES:(s_ + 1) * LANES]
        sel = (kk > thr) | ((kk == thr) & (s_ * LANES + lane128 < cut))
        bias = jnp.where(sel, 0.0, -jnp.inf)
        for g in range(ga):
            bias_ref[g, :, s_ * LANES:(s_ + 1) * LANES] = bias[g * ts:(g + 1) * ts]


def _idx_sample(page_table, qi4, wcol, qpos, kinew_t, kidx_t, ts, ga):
    db = qi4.shape[0]
    n_pages = page_table.shape[1]
    page = kidx_t.shape[2]
    past = n_pages * page
    lpad = past + LANES
    topk = min(TOPK_MAX, (past + ts) // 4)
    idx_bits = max(1, (lpad - 1).bit_length())
    kernel = functools.partial(_idx_sample_kernel, ga=ga, ts=ts, n_pages=n_pages, page=page, lpad=lpad,
                               topk=topk, idx_bits=idx_bits)
    return pl.pallas_call(
        kernel,
        grid_spec=pltpu.PrefetchScalarGridSpec(
            num_scalar_prefetch=1,
            grid=(db // ga,),
            in_specs=[pl.BlockSpec((ga, IDX_HEADS * ts, IDX_DIM), lambda i, pt: (i, 0, 0)),
                      pl.BlockSpec((ga, IDX_HEADS * ts, 1), lambda i, pt: (i, 0, 0)),
                      pl.BlockSpec((ga * ts, 1), lambda i, pt: (i, 0)),
                      pl.BlockSpec((ga, IDX_DIM, LANES), lambda i, pt: (i, 0, 0)),
                      pl.BlockSpec(memory_space=pl.ANY)],
            out_specs=pl.BlockSpec((ga, ts, lpad), lambda i, pt: (i, 0, 0)),
            scratch_shapes=[
                pltpu.VMEM((2, ga, IDX_DIM, lpad), F32),
                pltpu.VMEM((ga * ts, lpad), I32),
                pltpu.SemaphoreType.DMA((2,)),
            ]),
        out_shape=jax.ShapeDtypeStruct((db, ts, lpad), F32),
        compiler_params=pltpu.CompilerParams(dimension_semantics=("arbitrary",), vmem_limit_bytes=VMEM_LIMIT),
        name="idx_sample",
    )(page_table, qi4, wcol, qpos, kinew_t, kidx_t)


def _attn_sample_kernel(pt_ref, q_ref, bias_ref, knew_ref, vnew_ref, k_hbm, v_hbm, o_ref,
                        k_buf, v_buf, sem, *, ts, n_pages, page):
    b = pl.program_id(0)
    nb = pl.num_programs(0)
    past = n_pages * page
    slot = b % 2

    def page_copies(seq, to_slot, p):
        phys = pt_ref[seq, p]
        dst = pl.ds(p * page, page)
        return (pltpu.make_async_copy(k_hbm.at[phys], k_buf.at[to_slot, :, :, dst], sem.at[0, to_slot]),
                pltpu.make_async_copy(v_hbm.at[phys], v_buf.at[to_slot, :, :, dst], sem.at[1, to_slot]))

    def start_all(seq, to_slot):
        for p in range(n_pages):
            for cp in page_copies(seq, to_slot, p):
                cp.start()

    @pl.when(b == 0)
    def _():
        start_all(0, 0)

    @pl.when(b + 1 < nb)
    def _():
        start_all(b + 1, 1 - slot)

    k_buf[slot, :, :, pl.ds(past, LANES)] = knew_ref[...]
    v_buf[slot, :, :, pl.ds(past, LANES)] = vnew_ref[...]
    for p in range(n_pages):
        for cp in page_copies(b, slot, p):
            cp.wait()

    q = q_ref[...]
    bias = bias_ref[...]
    bias2 = jnp.concatenate([bias] * B_GROUP, axis=0)
    outs = [None] * B_HEADS
    for n in range(B_KV_HEADS):
        kt = k_buf[slot, n].astype(BF16)
        vt = v_buf[slot, n].astype(BF16)
        qs = jnp.concatenate([q[:, (n * B_GROUP + g) * HEAD_DIM:(n * B_GROUP + g + 1) * HEAD_DIM]
                              for g in range(B_GROUP)], axis=0)
        sc = _dot(qs, kt) + bias2
        m = jnp.max(sc, axis=1, keepdims=True)
        p_ = jnp.exp2(sc - m)
        o = _dot_nt(p_.astype(BF16), vt) / jnp.sum(p_, axis=1, keepdims=True)
        for g in range(B_GROUP):
            outs[n * B_GROUP + g] = o[g * ts:(g + 1) * ts]
    o_ref[...] = jnp.concatenate(outs, axis=1).astype(BF16)


def _attn_sample(page_table, q, bias, knew_t, vnew_t, k_t, v_t):
    db, ts, _ = q.shape
    n_pages = page_table.shape[1]
    page = k_t.shape[3]
    lpad = bias.shape[2]
    seq3 = lambda a: pl.BlockSpec((None,) + a.shape[1:], lambda b, pt: (b,) + (0,) * (a.ndim - 1))
    anyspec = pl.BlockSpec(memory_space=pl.ANY)
    kernel = functools.partial(_attn_sample_kernel, ts=ts, n_pages=n_pages, page=page)
    return pl.pallas_call(
        kernel,
        grid_spec=pltpu.PrefetchScalarGridSpec(
            num_scalar_prefetch=1,
            grid=(db,),
            in_specs=[seq3(q), seq3(bias), seq3(knew_t), seq3(vnew_t), anyspec, anyspec],
            out_specs=pl.BlockSpec((None, ts, B_WIDTH), lambda b, pt: (b, 0, 0)),
            scratch_shapes=[
                pltpu.VMEM((2, B_KV_HEADS, HEAD_DIM, lpad), F32),
                pltpu.VMEM((2, B_KV_HEADS, HEAD_DIM, lpad), F32),
                pltpu.SemaphoreType.DMA((2, 2)),
            ]),
        out_shape=jax.ShapeDtypeStruct((db, ts, B_WIDTH), BF16),
        compiler_params=pltpu.CompilerParams(dimension_semantics=("arbitrary",), vmem_limit_bytes=VMEM_LIMIT),
        name="attn_sample",
    )(page_table, q, bias, knew_t, vnew_t, k_t, v_t)


def _layer_norm_rows(y, gam, bet):
    mu = jnp.mean(y, axis=1, keepdims=True)
    d = y - mu
    var = jnp.mean(d * d, axis=1, keepdims=True)
    return d * lax.rsqrt(var + LN_EPS) * gam + bet


def _store_row_tiles(ref, val):
    r = val.shape[0]
    for c in range(ROW_CHUNKS):
        ref[pl.ds(c, r, stride=ROW_CHUNKS), :] = val[:, c * LANES:(c + 1) * LANES]


def _load_row_tiles(ref, r):
    return jnp.concatenate([ref[pl.ds(c, r, stride=ROW_CHUNKS), :] for c in range(ROW_CHUNKS)], axis=1)


ROUTER_ROWS = SUBLANES + N_EXPERTS


def _route(logits_t):
    r = logits_t.shape[1]
    far = float(LANES)

    def softmax_rows(x):
        e = jnp.exp(x - jnp.max(x, axis=0, keepdims=True))
        return e / jnp.sum(e, axis=0, keepdims=True)

    def first_max(p):
        rows = lax.broadcasted_iota(I32, p.shape, 0).astype(F32)
        best = jnp.max(p, axis=0, keepdims=True)
        return best, jnp.min(jnp.where(p == best, rows, far), axis=0, keepdims=True), rows

    g_w, g_sel, _ = first_max(softmax_rows(logits_t[0:N_GROUPS]))
    el = jnp.zeros((EXPERTS_PER_GROUP, r), F32)
    for g in range(N_GROUPS):
        lo = SUBLANES + g * EXPERTS_PER_GROUP
        el = jnp.where(g_sel == float(g), logits_t[lo:lo + EXPERTS_PER_GROUP], el)
    ep = softmax_rows(el)
    p1, i1, rows = first_max(ep)
    p2, i2, _ = first_max(jnp.where(rows == i1, -1.0, ep))
    denom = p1 + p2
    base = g_sel * float(EXPERTS_PER_GROUP)
    return jnp.concatenate([base + i1, base + i2, g_w * p1 / denom, g_w * p2 / denom,
                            jnp.zeros((SUBLANES - 4, r), F32)], axis=0)


def _split_bf16(x):
    hi = x.astype(BF16)
    return hi, (x - hi.astype(F32)).astype(BF16)


def _mid_kernel(x_ref, a_ref, b_ref, g1_ref, sh2_ref, sc2_ref, wo_ref, gam_ref, bet_ref, wr_ref,
                x1_ref, h2_ref, route_ref, *, alpha):
    mixed = _dot(a_ref[...], wo_ref[0:A_WIDTH, :]) + _dot(b_ref[...], wo_ref[A_WIDTH:A_WIDTH + B_WIDTH, :])
    x1 = _layer_norm_rows(alpha * x_ref[...] + g1_ref[...] * mixed, gam_ref[...], bet_ref[...])
    x1_ref[...] = x1
    h2 = x1 * (1.0 + sc2_ref[...]) + sh2_ref[...]
    _store_row_tiles(h2_ref, h2)
    w_hi, w_lo = _split_bf16(wr_ref[...])
    h_hi, h_lo = _split_bf16(h2)
    logits_t = _dot_nt(w_hi, h_hi) + _dot_nt(w_hi, h_lo) + _dot_nt(w_lo, h_hi)
    route_ref[...] = _route(logits_t)


def _mod_spec(mod3, comp, tiles_per_group):
    rm = mod3.shape[1]
    return pl.BlockSpec((None, rm, D_MODEL), lambda i: (i // tiles_per_group, 0, comp))


def _mid(x, a, b, mod3, tiles_per_group, tile, w_out, gam, bet, w_router, alpha):
    n = x.shape[0]
    rows = lambda w: pl.BlockSpec((tile, w), lambda i: (i, 0))
    whole = lambda arr: pl.BlockSpec(arr.shape, lambda i: (0,) * arr.ndim)
    return pl.pallas_call(
        functools.partial(_mid_kernel, alpha=alpha),
        grid=(n // tile,),
        in_specs=[rows(D_MODEL), rows(A_WIDTH), rows(B_WIDTH),
                  _mod_spec(mod3, 2, tiles_per_group), _mod_spec(mod3, 3, tiles_per_group),
                  _mod_spec(mod3, 4, tiles_per_group),
                  whole(w_out), whole(gam), whole(bet), whole(w_router)],
        out_specs=[rows(D_MODEL), pl.BlockSpec((tile * ROW_CHUNKS, LANES), lambda i: (i, 0)),
                   pl.BlockSpec((SUBLANES, tile), lambda i: (0, i))],
        out_shape=(jax.ShapeDtypeStruct((n, D_MODEL), F32),
                   jax.ShapeDtypeStruct((n * ROW_CHUNKS, LANES), F32),
                   jax.ShapeDtypeStruct((SUBLANES, n), F32)),
        compiler_params=pltpu.CompilerParams(dimension_semantics=("arbitrary",), vmem_limit_bytes=VMEM_LIMIT),
        name="mid",
    )(x, a, b, mod3, mod3, mod3, w_out, gam, bet, w_router)


META_CNT, META_START, META_END, META_NACT = 0, 1, 2, 3


def _plan_kernel(route_ref, pos_ref, meta_ref, te_ref, carry_ref, starts_ref, *, nt, tm):
    phase = pl.program_id(0)
    i = pl.program_id(1)

    @pl.when(i == 0)
    def _():
        carry_ref[...] = jnp.zeros(carry_ref.shape, F32)

    eid = route_ref[0:2, :].astype(I32)
    e_iota = lax.broadcasted_iota(I32, (N_EXPERTS, nt), 0)
    hit0 = eid[0:1, :] == e_iota
    hit1 = eid[1:2, :] == e_iota
    onehot = jnp.where(hit0 | hit1, 1.0, 0.0)

    @pl.when(phase == 0)
    def _():
        pos_ref[...] = jnp.zeros(pos_ref.shape, I32)
        carry_ref[...] = carry_ref[...] + jnp.sum(onehot, axis=1, keepdims=True)

        @pl.when(i == pl.num_programs(1) - 1)
        def _():
            cnt = carry_ref[...]
            tiles = jnp.floor((cnt + float(tm - 1)) * (1.0 / tm))
            r = lax.broadcasted_iota(I32, (N_EXPERTS, N_EXPERTS), 0)
            c = lax.broadcasted_iota(I32, (N_EXPERTS, N_EXPERTS), 1)
            lower = jnp.where(r > c, 1.0, 0.0).astype(BF16)
            tiles_before = _dot(lower, jnp.broadcast_to(tiles, (N_EXPERTS, LANES)).astype(BF16))[:, 0:1]
            tiles_end = tiles_before + tiles
            starts_ref[...] = tiles_before * float(tm)

            diag = (lax.broadcasted_iota(I32, (N_EXPERTS, LANES), 0)
                    == lax.broadcasted_iota(I32, (N_EXPERTS, LANES), 1))

            def as_row(col):
                return jnp.sum(jnp.where(diag, jnp.broadcast_to(col, (N_EXPERTS, LANES)), 0.0),
                               axis=0, keepdims=True)

            nact = jnp.broadcast_to(jnp.max(tiles_end, axis=0, keepdims=True), (1, LANES))
            meta_ref[...] = jnp.concatenate(
                [as_row(cnt), as_row(tiles_before * float(tm)), as_row(tiles_end * float(tm)), nact,
                 jnp.zeros((SUBLANES - 4, LANES), F32)], axis=0).astype(I32)
            tile_i = lax.broadcasted_iota(I32, (N_EXPERTS, te_ref.shape[1]), 1).astype(F32)
            owner = jnp.sum(jnp.where(tiles_end <= tile_i, 1.0, 0.0), axis=0, keepdims=True)
            te_ref[...] = jnp.minimum(owner, float(N_EXPERTS - 1)).astype(I32)

    @pl.when(phase == 1)
    def _():
        upper = (lax.broadcasted_iota(I32, (nt, nt), 0) <= lax.broadcasted_iota(I32, (nt, nt), 1))
        incl = _dot(onehot.astype(BF16), jnp.where(upper, 1.0, 0.0).astype(BF16))
        slot = starts_ref[...] + carry_ref[...] + incl - 1.0
        pos_ref[0:1, :] = jnp.sum(jnp.where(hit0, slot, 0.0), axis=0, keepdims=True).astype(I32)
        pos_ref[1:2, :] = jnp.sum(jnp.where(hit1, slot, 0.0), axis=0, keepdims=True).astype(I32)
        carry_ref[...] = carry_ref[...] + jnp.sum(onehot, axis=1, keepdims=True)


def _plan(route_t, nt, tm):
    n = route_t.shape[1]
    n_tiles = (2 * n) // tm + N_EXPERTS
    te_width = pl.cdiv(n_tiles, LANES) * LANES
    pos, meta, te = pl.pallas_call(
        functools.partial(_plan_kernel, nt=nt, tm=tm),
        grid=(2, n // nt),
        in_specs=[pl.BlockSpec((SUBLANES, nt), lambda p, i: (0, i))],
        out_specs=[pl.BlockSpec((2, nt), lambda p, i: (0, i * p)),
                   pl.BlockSpec((SUBLANES, LANES), lambda p, i: (0, 0)),
                   pl.BlockSpec((1, te_width), lambda p, i: (0, 0))],
        out_shape=(jax.ShapeDtypeStruct((2, n), I32), jax.ShapeDtypeStruct((SUBLANES, LANES), I32),
                   jax.ShapeDtypeStruct((1, te_width), I32)),
        scratch_shapes=[pltpu.VMEM((N_EXPERTS, 1), F32), pltpu.VMEM((N_EXPERTS, 1), F32)],
        compiler_params=pltpu.CompilerParams(dimension_semantics=("arbitrary", "arbitrary")),
        name="plan",
    )(route_t)
    return pos, meta, te, n_tiles


def _row_copy(src, src_row, dst, dst_row, sem):
    return pltpu.make_async_copy(src.at[pl.ds(src_row * ROW_CHUNKS, ROW_CHUNKS)],
                                 dst.at[pl.ds(dst_row * ROW_CHUNKS, ROW_CHUNKS)], sem)


def _dispatch_kernel(meta_ref, pos_ref, hp_ref, hs_ref, xs_out, zero_ref, sem, *, tt, blocks_p, tm, n_tiles):
    i = pl.program_id(0)

    def scatter(h_ref):
        def issue(j, _):
            for k in range(2):
                _row_copy(h_ref, j, xs_out, pos_ref[k, j], sem.at[k]).start()
            return 0

        def drain(j, _):
            for k in range(2):
                _row_copy(h_ref, j, xs_out, pos_ref[k, j], sem.at[k]).wait()
            return 0

        lax.fori_loop(0, tt, issue, 0, unroll=ROW_DMA_UNROLL)
        lax.fori_loop(0, tt, drain, 0, unroll=ROW_DMA_UNROLL)

    @pl.when(i < blocks_p)
    def _():
        scatter(hp_ref)

    @pl.when(i >= blocks_p)
    def _():
        scatter(hs_ref)

    @pl.when(i == pl.num_programs(0) - 1)
    def _():
        zero_ref[...] = jnp.zeros(zero_ref.shape, F32)

        def zero_rows(first_row, n_rows):
            return pltpu.make_async_copy(zero_ref.at[pl.ds(0, n_rows * ROW_CHUNKS)],
                                         xs_out.at[pl.ds(first_row * ROW_CHUNKS, n_rows * ROW_CHUNKS)], sem.at[0])

        def start_row(r, c):
            zero_rows(r, 1).start()
            return c

        def wait_row(r, c):
            zero_rows(r, 1).wait()
            return c

        def per_expert(row_fn):
            def body(e, c):
                lo = meta_ref[META_START, e] + meta_ref[META_CNT, e]
                return lax.fori_loop(lo, meta_ref[META_END, e], row_fn, c)
            return body

        def start_tile(t, c):
            zero_rows(t * tm, tm).start()
            return c

        def wait_tile(t, c):
            zero_rows(t * tm, tm).wait()
            return c

        nact = meta_ref[META_NACT, 0]
        lax.fori_loop(0, N_EXPERTS, per_expert(start_row), 0)
        lax.fori_loop(nact, n_tiles, start_tile, 0)
        lax.fori_loop(0, N_EXPERTS, per_expert(wait_row), 0)
        lax.fori_loop(nact, n_tiles, wait_tile, 0)


def _dispatch(meta, pos, h2_p, h2_s, tt, tm, n_tiles):
    blocks_p = h2_p.shape[0] // (tt * ROW_CHUNKS)
    blocks_s = h2_s.shape[0] // (tt * ROW_CHUNKS)
    return pl.pallas_call(
        functools.partial(_dispatch_kernel, tt=tt, blocks_p=blocks_p, tm=tm, n_tiles=n_tiles),
        grid_spec=pltpu.PrefetchScalarGridSpec(
            num_scalar_prefetch=1,
            grid=(blocks_p + blocks_s,),
            in_specs=[pl.BlockSpec((2, tt), lambda i, m: (0, i), memory_space=pltpu.SMEM),
                      pl.BlockSpec((tt * ROW_CHUNKS, LANES), lambda i, m: (jnp.minimum(i, blocks_p - 1), 0)),
                      pl.BlockSpec((tt * ROW_CHUNKS, LANES), lambda i, m: (jnp.maximum(i - blocks_p, 0), 0))],
            out_specs=pl.BlockSpec(memory_space=pl.ANY),
            scratch_shapes=[pltpu.VMEM((tm * ROW_CHUNKS, LANES), F32), pltpu.SemaphoreType.DMA((2,))]),
        out_shape=jax.ShapeDtypeStruct((n_tiles * tm * ROW_CHUNKS, LANES), F32),
        compiler_params=pltpu.CompilerParams(dimension_semantics=("arbitrary",)),
        name="dispatch",
    )(meta, pos, h2_p, h2_s)


def _experts_kernel(te_ref, meta_ref, xs_ref, wg_ref, wu_ref, wd_ref, ys_ref, wg_b, wu_b, wd_b, *, tm):
    i = pl.program_id(0)
    active = i < meta_ref[META_NACT, 0]
    fresh = (i == 0) | (te_ref[0, i] != te_ref[0, jnp.maximum(i - 1, 0)])

    @pl.when(active & fresh)
    def _():
        wg_b[...] = wg_ref[...].astype(BF16)
        wu_b[...] = wu_ref[...].astype(BF16)
        wd_b[...] = wd_ref[...].astype(BF16)

    @pl.when(active)
    def _():
        x = _load_row_tiles(xs_ref, tm).astype(BF16)
        hid = (jax.nn.silu(_dot(x, wg_b[...])) * _dot(x, wu_b[...])).astype(BF16)
        _store_row_tiles(ys_ref, _dot(hid, wd_b[...]))

    @pl.when(jnp.logical_not(active))
    def _():
        ys_ref[...] = jnp.zeros(ys_ref.shape, F32)


def _experts(te, meta, xs, w_gate, w_up, w_down, tm, n_tiles):
    last_active = lambda i, m: jnp.minimum(i, m[META_NACT, 0] - 1)
    tile = pl.BlockSpec((tm * ROW_CHUNKS, LANES), lambda i, te, m: (last_active(i, m), 0))
    w_in_spec = pl.BlockSpec((None, D_MODEL, D_EXPERT), lambda i, te, m: (te[0, last_active(i, m)], 0, 0))
    w_out_spec = pl.BlockSpec((None, D_EXPERT, D_MODEL), lambda i, te, m: (te[0, last_active(i, m)], 0, 0))
    return pl.pallas_call(
        functools.partial(_experts_kernel, tm=tm),
        grid_spec=pltpu.PrefetchScalarGridSpec(
            num_scalar_prefetch=2,
            grid=(n_tiles,),
            in_specs=[tile, w_in_spec, w_in_spec, w_out_spec],
            out_specs=pl.BlockSpec((tm * ROW_CHUNKS, LANES), lambda i, te, m: (i, 0)),
            scratch_shapes=[pltpu.VMEM((D_MODEL, D_EXPERT), BF16), pltpu.VMEM((D_MODEL, D_EXPERT), BF16),
                            pltpu.VMEM((D_EXPERT, D_MODEL), BF16)]),
        out_shape=jax.ShapeDtypeStruct(xs.shape, F32),
        compiler_params=pltpu.CompilerParams(dimension_semantics=("arbitrary",), vmem_limit_bytes=VMEM_LIMIT),
        name="experts",
    )(te, meta, xs, w_gate, w_up, w_down)


def _combine_kernel(pos_ref, route_ref, x1_ref, g2_ref, gam_ref, bet_ref, ys_hbm, y_ref, buf0, buf1, sem,
                    *, tt, alpha):
    bufs = (buf0, buf1)

    def issue(j, _):
        for k in range(2):
            _row_copy(ys_hbm, pos_ref[k, j], bufs[k], j, sem.at[k]).start()
        return 0

    def drain(j, _):
        for k in range(2):
            _row_copy(ys_hbm, pos_ref[k, j], bufs[k], j, sem.at[k]).wait()
        return 0

    lax.fori_loop(0, tt, issue, 0, unroll=ROW_DMA_UNROLL)
    lax.fori_loop(0, tt, drain, 0, unroll=ROW_DMA_UNROLL)
    route = route_ref[...]
    eye = lax.broadcasted_iota(I32, (tt, tt), 0) == lax.broadcasted_iota(I32, (tt, tt), 1)
    as_col = lambda row: jnp.sum(jnp.where(eye, jnp.broadcast_to(row, (tt, tt)), 0.0), axis=1, keepdims=True)
    f = as_col(route[2:3, :]) * _load_row_tiles(buf0, tt) + as_col(route[3:4, :]) * _load_row_tiles(buf1, tt)
    y_ref[...] = _layer_norm_rows(alpha * x1_ref[...] + g2_ref[...] * f, gam_ref[...], bet_ref[...])


def _combine(pos, route_t, col_block0, x1, mod3, tiles_per_group, tt, gam, bet, ys, alpha):
    n = x1.shape[0]
    rows = lambda w: pl.BlockSpec((tt, w), lambda i: (i, 0))
    whole = lambda arr: pl.BlockSpec(arr.shape, lambda i: (0,) * arr.ndim)
    return pl.pallas_call(
        functools.partial(_combine_kernel, tt=tt, alpha=alpha),
        grid=(n // tt,),
        in_specs=[pl.BlockSpec((2, tt), lambda i: (0, i + col_block0), memory_space=pltpu.SMEM),
                  pl.BlockSpec((SUBLANES, tt), lambda i: (0, i + col_block0)),
                  rows(D_MODEL), _mod_spec(mod3, 5, tiles_per_group), whole(gam), whole(bet),
                  pl.BlockSpec(memory_space=pl.ANY)],
        out_specs=rows(D_MODEL),
        out_shape=jax.ShapeDtypeStruct((n, D_MODEL), F32),
        scratch_shapes=[pltpu.VMEM((tt * ROW_CHUNKS, LANES), F32), pltpu.VMEM((tt * ROW_CHUNKS, LANES), F32),
                        pltpu.SemaphoreType.DMA((2,))],
        compiler_params=pltpu.CompilerParams(dimension_semantics=("arbitrary",), vmem_limit_bytes=VMEM_LIMIT),
        name="combine",
    )(pos, route_t, x1, mod3, gam, bet, ys)


def kernel(x_prompt, x_sample, cache_k, cache_v, cache_kidx, page_table, c_prompt, c_sample, w_ada, b_ada, w_in,
           a_ln_g, a_ln_b, w_spatial, b_spatial, w_out, ln1_g, ln1_b, w_group_router, w_expert_router, w_gate,
           w_up, w_down, ln2_g, ln2_b):
    depth = w_ada.shape[0]
    assert depth == 1, "one trunk layer"
    alpha = (2.0 * depth) ** 0.25
    bsz, seq, d = x_prompt.shape
    db, ts, _ = x_sample.shape
    n_pages = page_table.shape[1]
    page = cache_k.shape[2]
    past = n_pages * page
    tile_p = min(PROMPT_TILE, seq)
    assert d == D_MODEL and seq % tile_p == 0 and tile_p % CHUNK == 0 and ts <= SUBLANES and page == LANES
    l = 0

    n_c = bsz + db
    n_c_pad = pl.cdiv(n_c, SUBLANES) * SUBLANES
    c_all = jnp.pad(jnp.concatenate([c_prompt, c_sample], axis=0), ((0, n_c_pad - n_c), (0, 0)))
    mod = _ada(c_all, w_ada[l], b_ada[l][None, :])
    mod_p = mod[:bsz]
    mod_s = mod[bsz:n_c]

    w_in_b = _pad_w_in(w_in[l])
    w_out_b = w_out[l].astype(BF16)
    mavg = _head_avg_matrix()
    gam_a = a_ln_g[l].reshape(1, A_WIDTH)
    bet_a = a_ln_b[l].reshape(1, A_WIDTH)
    w_router = jnp.concatenate([w_group_router[l].T, jnp.zeros((SUBLANES - N_GROUPS, d), F32),
                                w_expert_router[l].reshape(d, N_EXPERTS).T], axis=0)
    ln1 = (ln1_g[l][None, :], ln1_b[l][None, :])
    ln2 = (ln2_g[l][None, :], ln2_b[l][None, :])

    wsp = w_spatial[l].reshape(A_HEADS // 2, 2, CHUNK, CHUNK).transpose(0, 2, 1, 3).reshape(A_HEADS // 2, CHUNK, 2 * CHUNK)
    bsp = jnp.repeat(b_spatial[l].T, HEAD_DIM, axis=1)
    a_p, kt_p, vt32_p, kit_p, kb_p, kiwib_p, vt_p, qt_p, qit_p, wit_p = _front_prompt(
        x_prompt, mod_p.reshape(bsz, 6, d), w_in_b, _rot_tables(np.arange(seq)), mavg, gam_a, bet_a, wsp, bsp, tile_p)
    b_p = _attn_prompt(qit_p, wit_p, qt_p, kiwib_p, kb_p, vt_p)
    n_p = bsz * seq
    tile_m = min(MID_TILE, seq)
    assert seq % tile_m == 0
    x1_p, h2_p, route_p = _mid(x_prompt.reshape(n_p, d), a_p.reshape(n_p, A_WIDTH), b_p.reshape(n_p, B_WIDTH),
                               mod_p.reshape(bsz, 1, 6 * d), seq // tile_m, tile_m, w_out_b, *ln1, w_router, alpha)

    r_s = ts * db
    x_tm = x_sample.transpose(1, 0, 2).reshape(r_s, d)
    rt_s = _rot_tables(np.repeat(past + np.arange(ts), db))
    w_small = w_spatial[l][:, :ts, :ts]
    wl = jnp.repeat(w_small.transpose(1, 2, 0).reshape(ts * ts, A_HEADS), HEAD_DIM, axis=1)
    bl = jnp.repeat(b_spatial[l][:, :ts].T, HEAD_DIM, axis=1)
    a_s, q_s, k_s, v_s, qi_s, kiwi_s, vg_s = _front_sample(x_tm, mod_s, w_in_b, rt_s, mavg, gam_a, bet_a, wl, bl, ts, db)

    def seq_major(a):
        return a.reshape(ts, db, a.shape[-1]).transpose(1, 0, 2)

    def new_t(a, heads):
        a = a.reshape(ts, db, heads, HEAD_DIM).transpose(1, 2, 3, 0)
        return jnp.pad(a, ((0, 0), (0, 0), (0, 0), (0, LANES - ts)))

    ga = min(IDX_SEQS, db)
    assert db % ga == 0 and (ga * ts) % SUBLANES == 0
    qi4 = qi_s.reshape(ts, db, IDX_HEADS, IDX_DIM).transpose(1, 2, 0, 3).reshape(db, IDX_HEADS * ts, IDX_DIM)
    wcol = kiwi_s[:, WI_LANE:WI_LANE + IDX_HEADS].reshape(ts, db, IDX_HEADS).transpose(1, 2, 0).reshape(db, IDX_HEADS * ts, 1)
    qpos = jnp.tile(past + jnp.arange(ts, dtype=I32), db).reshape(db * ts, 1)
    kinew_t = new_t(kiwi_s[:, :IDX_DIM], 1)[:, 0]
    kidx_t = jnp.transpose(cache_kidx[l], (0, 2, 1))
    k_t = jnp.transpose(cache_k[l], (0, 2, 3, 1))
    v_t = jnp.transpose(cache_v[l], (0, 2, 3, 1))
    bias_s = _idx_sample(page_table, qi4, wcol, qpos, kinew_t, kidx_t, ts, ga)
    b_s = _attn_sample(page_table, seq_major(q_s), bias_s, new_t(k_s, B_KV_HEADS), new_t(v_s, B_KV_HEADS), k_t, v_t)
    b_s_tm = b_s.transpose(1, 0, 2).reshape(r_s, B_WIDTH)
    x1_s, h2_s, route_s = _mid(x_tm, a_s, b_s_tm, mod_s.reshape(1, db, 6 * d), ts, db, w_out_b, *ln1, w_router, alpha)

    n_all = n_p + r_s
    tok_p = min(TOKEN_TILE, seq)
    rank_tile = min(RANK_TILE, n_all)
    assert n_all % rank_tile == 0 and n_p % tok_p == 0 and r_s % tok_p == 0 and n_p % db == 0
    route_all = jnp.concatenate([route_p, route_s], axis=1)
    pos, meta, te, n_tiles = _plan(route_all, rank_tile, MOE_TILE)
    xs = _dispatch(meta, pos, h2_p, h2_s, tok_p, MOE_TILE, n_tiles)
    ys = _experts(te, meta, xs, w_gate[l], w_up[l], w_down[l], MOE_TILE, n_tiles)
    y_p = _combine(pos, route_all, 0, x1_p, mod_p.reshape(bsz, 1, 6 * d), seq // tok_p, tok_p, *ln2, ys, alpha)
    y_s_tm = _combine(pos, route_all, n_p // db, x1_s, mod_s.reshape(1, db, 6 * d), ts, db, *ln2, ys, alpha)
    y_s = y_s_tm.reshape(ts, db, d).transpose(1, 0, 2)

    kv5 = lambda a, n, t: a.reshape(1, n, t, B_KV_HEADS, HEAD_DIM)
    kv5_t = lambda a: a.reshape(1, bsz, B_KV_HEADS, HEAD_DIM, seq).transpose(0, 1, 4, 2, 3)
    return (y_p.reshape(bsz, seq, d), y_s,
            kv5_t(kt_p), kv5_t(vt32_p), kit_p.transpose(0, 2, 1)[None],
            kv5(seq_major(k_s), db, ts), kv5(seq_major(v_s), db, ts), seq_major(kiwi_s)[..., :IDX_DIM][None],
            seq_major(vg_s).reshape(1, db, ts, A_HEADS, HEAD_DIM))
```

```python
import functools

import jax
import jax.numpy as jnp
import numpy as np
from jax import lax
from jax.experimental import pallas as pl
from jax.experimental.pallas import tpu as pltpu

F32 = jnp.float32
BF16 = jnp.bfloat16
I32 = jnp.int32

D_MODEL = 1024
HEAD_DIM = 64
A_HEADS = 8
A_WIDTH = A_HEADS * HEAD_DIM
CHUNK = 128
B_HEADS = 8
B_KV_HEADS = 4
B_GROUP = B_HEADS // B_KV_HEADS
B_WIDTH = B_HEADS * HEAD_DIM
KV_WIDTH = B_KV_HEADS * HEAD_DIM
IDX_HEADS = 4
IDX_DIM = 64
IDX_WIDTH = IDX_HEADS * IDX_DIM
TOPK_MAX = 256
ROPE_THETA = 500000.0
ROT_DIM = HEAD_DIM // 4
ROT_HALF = ROT_DIM // 2
ATTN_SCALE = HEAD_DIM ** -0.5
LOG2_E = 1.4426950408889634
Q_SCALE = ATTN_SCALE * LOG2_E
DENOM_ROWS = 16
N_GROUPS = 4
EXPERTS_PER_GROUP = 8
N_EXPERTS = N_GROUPS * EXPERTS_PER_GROUP
D_EXPERT = 512
LN_EPS = 1e-5

LANES = 128
SUBLANES = 8
ROW_CHUNKS = D_MODEL // LANES

C_AU, C_AV, C_Q, C_K, C_V, C_QI, C_KI = 0, 512, 1024, 1536, 1792, 2048, 2304
IN_WIDTH = C_KI + IDX_DIM + IDX_HEADS
IN_PAD = 2432
WI_LANE = IDX_DIM

INT_MIN = -(2 ** 31)
INT_MAX = 2 ** 31 - 1
VMEM_LIMIT = 48 * 1024 * 1024

PROMPT_TILE = 256
MOE_TILE = 256
TOKEN_TILE = 256
RANK_TILE = 512
IDX_SEQS = 16
STEPS_PER_CHECK = 4
ROW_DMA_UNROLL = 8
MID_TILE = 512


def _dot(a, b):
    return jnp.dot(a, b, preferred_element_type=F32)


def _dot_nt(a, b):
    return lax.dot_general(a, b, (((1,), (1,)), ((), ())), preferred_element_type=F32)


def _ada_kernel(c_ref, w_ref, b_ref, o_ref):
    s = jax.nn.silu(c_ref[...]).astype(BF16)
    o_ref[...] = _dot(s, w_ref[...].astype(BF16)) + b_ref[...]


def _ada(c_all, w_ada, b_ada):
    rows = c_all.shape[0]
    n_out = w_ada.shape[1]
    tn = 1024
    return pl.pallas_call(
        _ada_kernel,
        grid=(n_out // tn,),
        in_specs=[
            pl.BlockSpec((rows, D_MODEL), lambda j: (0, 0)),
            pl.BlockSpec((D_MODEL, tn), lambda j: (0, j)),
            pl.BlockSpec((1, tn), lambda j: (0, j)),
        ],
        out_specs=pl.BlockSpec((rows, tn), lambda j: (0, j)),
        out_shape=jax.ShapeDtypeStruct((rows, n_out), F32),
        compiler_params=pltpu.CompilerParams(
            dimension_semantics=("arbitrary",), vmem_limit_bytes=VMEM_LIMIT),
        name="ada",
    )(c_all, w_ada, b_ada)


def _rotate(x, rt, blk):
    c = rt[:, 0:LANES]
    s_lo = rt[:, LANES:2 * LANES]
    s_hi = rt[:, 2 * LANES:3 * LANES]
    if blk == 1:
        head = lax.broadcasted_iota(I32, c.shape, 1) < IDX_DIM
        c = jnp.where(head, c, 1.0)
        s_lo = jnp.where(head, s_lo, 0.0)
        s_hi = jnp.where(head, s_hi, 0.0)
    outs = []
    for j in range(x.shape[1] // LANES):
        xb = x[:, j * LANES:(j + 1) * LANES]
        up = pltpu.roll(xb, LANES - ROT_HALF, 1)
        dn = pltpu.roll(xb, ROT_HALF, 1)
        outs.append(xb * c + up * s_lo + dn * s_hi)
    return outs[0] if len(outs) == 1 else jnp.concatenate(outs, axis=1)


def _head_ln(g, mavg, gam, bet):
    def seg_mean(x):
        parts = [_dot(x[:, j * 256:(j + 1) * 256].astype(BF16), mavg) for j in range(A_WIDTH // 256)]
        return jnp.concatenate(parts, axis=1)
    mu = seg_mean(g)
    d = g - mu
    var = seg_mean(d * d)
    return d * lax.rsqrt(var + LN_EPS) * gam + bet


def _project(h, w_ref, rt, mavg, gam, bet):
    u = jax.nn.gelu(_dot(h, w_ref[:, C_AU:C_AV]))
    vg = _head_ln(jax.nn.gelu(_dot(h, w_ref[:, C_AV:C_Q])), mavg, gam, bet)
    q = _rotate(_dot(h, w_ref[:, C_Q:C_K]), rt, 0) * Q_SCALE
    k = _rotate(_dot(h, w_ref[:, C_K:C_V]), rt, 0)
    v = _dot(h, w_ref[:, C_V:C_QI])
    qi = _rotate(_dot(h, w_ref[:, C_QI:C_KI]), rt, 0)
    kiwi = _rotate(_dot(h, w_ref[:, C_KI:IN_PAD]), rt, 1)
    return u, vg, q, k, v, qi, kiwi


def _front_prompt_kernel(x_ref, mod_ref, w_ref, rt_ref, mavg_ref, gam_ref, bet_ref, wsp_ref, bsp_ref,
                         a_ref, kt_ref, vt_ref, kit_ref, kb_ref, kiwib_ref, vtb_ref, qt_ref, qit_ref, wit_ref):
    shift = mod_ref[0:1, :]
    scale = mod_ref[1:2, :]
    h = (x_ref[...] * (1.0 + scale) + shift).astype(BF16)
    u, vg, q, k, v, qi, kiwi = _project(h, w_ref, rt_ref[...], mavg_ref[...], gam_ref[...], bet_ref[...])
    v_t = jnp.transpose(v)
    kiwi_t = jnp.transpose(kiwi)
    kt_ref[...] = jnp.transpose(k)
    vt_ref[...] = v_t
    kit_ref[...] = kiwi_t[0:IDX_DIM, :]
    kb_ref[...] = k.astype(BF16)
    kiwib_ref[...] = kiwi.astype(BF16)
    vtb_ref[...] = v_t.astype(BF16)
    qt_ref[...] = jnp.transpose(q).astype(BF16)
    qit_ref[...] = jnp.transpose(qi).astype(BF16)
    wit_ref[...] = kiwi_t[WI_LANE:WI_LANE + SUBLANES, :]

    rows = lax.broadcasted_iota(I32, (CHUNK, 2 * CHUNK), 0)
    cols = lax.broadcasted_iota(I32, (CHUNK, 2 * CHUNK), 1) % CHUNK
    causal = cols <= rows
    lane = lax.broadcasted_iota(I32, (CHUNK, LANES), 1)
    tt = x_ref.shape[0]
    for cidx in range(tt // CHUNK):
        rs = slice(cidx * CHUNK, (cidx + 1) * CHUNK)
        blocks = []
        for p in range(A_HEADS // 2):
            wcat = jnp.where(causal, wsp_ref[p], 0.0).astype(BF16)
            vb = vg[rs, p * LANES:(p + 1) * LANES]
            rhs = jnp.concatenate([jnp.where(lane < HEAD_DIM, vb, 0.0),
                                   jnp.where(lane >= HEAD_DIM, vb, 0.0)], axis=0).astype(BF16)
            blocks.append(_dot(wcat, rhs))
        s = jnp.concatenate(blocks, axis=1) + bsp_ref[...]
        a_ref[rs, :] = (u[rs, :] * s).astype(BF16)


def _front_prompt(x, mod, w_in, rt, mavg, gam, bet, wsp, bsp, tt):
    b, t, _ = x.shape
    nc = t // tt
    tok = lambda w: pl.BlockSpec((None, tt, w), lambda i, j: (i, j, 0))
    tr = lambda r: pl.BlockSpec((None, None, r, tt), lambda i, j: (i, j, 0, 0))
    pos_minor = lambda r: pl.BlockSpec((None, r, tt), lambda i, j: (i, 0, j))
    const2 = lambda a: pl.BlockSpec(a.shape, lambda i, j: (0,) * a.ndim)
    out_shapes = (
        jax.ShapeDtypeStruct((b, t, A_WIDTH), BF16),
        jax.ShapeDtypeStruct((b, KV_WIDTH, t), F32),
        jax.ShapeDtypeStruct((b, KV_WIDTH, t), F32),
        jax.ShapeDtypeStruct((b, IDX_DIM, t), F32),
        jax.ShapeDtypeStruct((b, t, KV_WIDTH), BF16),
        jax.ShapeDtypeStruct((b, t, LANES), BF16),
        jax.ShapeDtypeStruct((b, nc, KV_WIDTH, tt), BF16),
        jax.ShapeDtypeStruct((b, nc, B_WIDTH, tt), BF16),
        jax.ShapeDtypeStruct((b, nc, IDX_WIDTH, tt), BF16),
        jax.ShapeDtypeStruct((b, nc, SUBLANES, tt), F32),
    )
    return pl.pallas_call(
        _front_prompt_kernel,
        grid=(b, nc),
        in_specs=[
            tok(D_MODEL),
            pl.BlockSpec((None, 6, D_MODEL), lambda i, j: (i, 0, 0)),
            const2(w_in),
            pl.BlockSpec((tt, 3 * LANES), lambda i, j: (j, 0)),
            const2(mavg), const2(gam), const2(bet), const2(wsp), const2(bsp),
        ],
        out_specs=[tok(A_WIDTH), pos_minor(KV_WIDTH), pos_minor(KV_WIDTH), pos_minor(IDX_DIM), tok(KV_WIDTH),
                   tok(LANES), tr(KV_WIDTH), tr(B_WIDTH), tr(IDX_WIDTH), tr(SUBLANES)],
        out_shape=out_shapes,
        compiler_params=pltpu.CompilerParams(
            dimension_semantics=("arbitrary", "arbitrary"), vmem_limit_bytes=VMEM_LIMIT),
        name="front_prompt",
    )(x, mod, w_in, rt, mavg, gam, bet, wsp, bsp)


def _order_key(score, kpos, idx_bits):
    offs = 1 << idx_bits
    bits = pltpu.bitcast(score, I32)
    key = bits ^ ((bits >> 31) & 0x7FFFFFFF)
    key = key + jnp.where(score > 0.0, offs, 0)
    return jnp.where(score == 0.0, offs - kpos, key)


def _select_threshold(count_ge, count_gt_eq_lt, shape, topk, idx_bits):
    offs = 1 << idx_bits
    first_candidates = (offs + 1, 1)
    max_steps = -(-(32 + len(first_candidates)) // STEPS_PER_CHECK) * STEPS_PER_CHECK

    def unresolved(carry):
        i, lo, hi, n_lo = carry
        pending = (n_lo != topk) & (hi - 1 > lo)
        return (i < max_steps) & (jnp.max(pending.astype(F32)) > 0.0)

    def bisect(carry):
        i, lo, hi, n_lo = carry
        for _ in range(STEPS_PER_CHECK):
            mid = (lo >> 1) + (hi >> 1) + (lo & hi & 1)
            for step, value in enumerate(first_candidates):
                forced = jnp.where(i == step, value, INT_MIN)
                mid = jnp.where((lo < forced) & (forced < hi), forced, mid)
            tot = count_ge(mid)
            take = tot >= topk
            lo = jnp.where(take, mid, lo)
            n_lo = jnp.where(take, tot, n_lo)
            hi = jnp.where(take, hi, mid)
            i = i + 1
        return i, lo, hi, n_lo

    never = jnp.full(shape, INT_MAX, I32)
    _, thr, _, n_lo = lax.while_loop(
        unresolved, bisect, (jnp.int32(0), jnp.full(shape, INT_MIN + 1, I32), jnp.full(shape, INT_MAX, I32), never))
    tied = (n_lo > topk) & (n_lo != INT_MAX)
    big = jnp.full(shape, offs, I32)

    def resolve_ties(_):
        n_gt, _ = count_gt_eq_lt(thr, big)
        need = topk - n_gt

        def idx_step(i, cut):
            cand = cut | lax.shift_left(jnp.int32(1), idx_bits - 1 - i)
            _, n_eq = count_gt_eq_lt(thr, cand)
            return jnp.where(n_eq <= need, cand, cut)

        cut = lax.fori_loop(0, idx_bits, idx_step, jnp.zeros(shape, I32))
        return jnp.where(tied, cut, big)

    any_tied = jnp.max(tied.astype(F32)) > 0.0
    cut = lax.cond(any_tied, resolve_ties, lambda _: big, 0)
    return thr, cut


def _fold_rows(x):
    acc = x[0:SUBLANES]
    for r in range(1, x.shape[0] // SUBLANES):
        acc = acc + x[r * SUBLANES:(r + 1) * SUBLANES]
    return acc


def _col_total(cnt8):
    return jnp.sum(cnt8.astype(F32), axis=0, keepdims=True).astype(I32)


def _attn_prompt_kernel(qit_ref, wit_ref, qt_ref, kiwib_ref, kb_ref, vt_ref, o_ref,
                        key_ref, w4_ref, wq_ref, s0_ref, s1_ref, bias_ref, m_ref, acc_ref, *, tq, topk, idx_bits):
    j = pl.program_id(1)
    kc = tq
    n_kc = j + 1

    @pl.when((pl.program_id(0) == 0) & (j == 0))
    def _():
        w4_ref[...] = jnp.zeros(w4_ref.shape, BF16)
        wq_ref[...] = jnp.zeros(wq_ref.shape, BF16)

    for h in range(IDX_HEADS):
        w4_ref[0:IDX_DIM, h * tq:(h + 1) * tq] = qit_ref[h * IDX_DIM:(h + 1) * IDX_DIM, :]
    for h in range(B_HEADS):
        n = h // B_GROUP
        wq_ref[h, n * HEAD_DIM:(n + 1) * HEAD_DIM, :] = qt_ref[h * HEAD_DIM:(h + 1) * HEAD_DIM, :]

    wit = wit_ref[...]
    krow = lax.broadcasted_iota(I32, (kc, tq), 0)
    qpos = j * tq + lax.broadcasted_iota(I32, (kc, tq), 1)

    n_chunks = key_ref.shape[0] - 1

    def score_pair(i, _):
        for c in (2 * i, 2 * i + 1):
            rows = pl.multiple_of(jnp.minimum(c, n_chunks - 1) * kc, kc)
            s = _dot(kiwib_ref[pl.ds(rows, kc), :], w4_ref[...])
            tot = jnp.maximum(s[:, 0:tq], 0.0) * wit[0:1, :]
            for h in range(1, IDX_HEADS):
                tot = tot + jnp.maximum(s[:, h * tq:(h + 1) * tq], 0.0) * wit[h:h + 1, :]
            kpos = c * kc + krow
            key_ref[c] = jnp.where(kpos <= qpos, _order_key(tot, kpos, idx_bits), INT_MIN)
        return 0

    lax.fori_loop(0, (n_kc + 1) // 2, score_pair, 0)

    def count_ge(cand):
        def body(i, cnt):
            for c in (2 * i, 2 * i + 1):
                cnt = cnt + _fold_rows(jnp.where(key_ref[c] >= cand, 1, 0))
            return cnt
        return _col_total(lax.fori_loop(0, (n_kc + 1) // 2, body, jnp.zeros((SUBLANES, tq), I32)))

    def count_gt_eq_lt(thr, pos):
        def body(c, carry):
            n_gt, n_eq = carry
            key = key_ref[c]
            n_gt = n_gt + _fold_rows(jnp.where(key > thr, 1, 0))
            n_eq = n_eq + _fold_rows(jnp.where((key == thr) & (c * kc + krow < pos), 1, 0))
            return n_gt, n_eq
        z = jnp.zeros((SUBLANES, tq), I32)
        n_gt, n_eq = lax.fori_loop(0, n_kc, body, (z, z))
        return _col_total(n_gt), _col_total(n_eq)

    thr, cut = _select_threshold(count_ge, count_gt_eq_lt, (1, tq), topk, idx_bits)

    m_ref[...] = jnp.full(m_ref.shape, jnp.finfo(F32).min, F32)
    acc_ref[...] = jnp.zeros(acc_ref.shape, F32)

    def key_block(c):
        return kb_ref[pl.ds(pl.multiple_of(c * kc, kc), kc), :]

    def selection_bias(c, open_bias):
        key = key_ref[c]
        sel = (key > thr) | ((key == thr) & (c * kc + krow < cut))
        bias_ref[...] = jnp.where(sel, open_bias, -jnp.inf)

    selection_bias(0, 0.0)
    kblk0 = key_block(0)
    for h in range(B_HEADS):
        s0_ref[h] = _dot(kblk0, wq_ref[h]) + bias_ref[...]

    def stage(c, src, dst, c_next, open_next):
        selection_bias(c_next, open_next)
        kblk_next = key_block(c_next)
        vtc = vt_ref[c]
        for h in range(B_HEADS):
            n = h // B_GROUP
            dst[h] = _dot(kblk_next, wq_ref[h]) + bias_ref[...]
            s = src[h]
            m_old = m_ref[h]
            m_new = jnp.maximum(m_old, jnp.max(s, axis=0, keepdims=True))
            p = jnp.exp2(s - m_new).astype(BF16)
            v_aug = jnp.concatenate([vtc[n * HEAD_DIM:(n + 1) * HEAD_DIM, :], ones_rows], axis=0)
            acc_ref[h] = jnp.exp2(m_old - m_new) * acc_ref[h] + _dot(v_aug, p)
            m_ref[h] = m_new

    ones_rows = jnp.where(lax.broadcasted_iota(I32, (DENOM_ROWS, kc), 0) == 0, 1.0, 0.0).astype(BF16)
    last = n_kc - 1

    def attend_pair(i, _):
        c0 = 2 * i
        c1 = jnp.minimum(c0 + 1, last)
        stage(c0, s0_ref, s1_ref, c1, jnp.where(c0 + 1 <= last, 0.0, -jnp.inf))
        stage(c1, s1_ref, s0_ref, jnp.minimum(c0 + 2, last), 0.0)
        return 0

    lax.fori_loop(0, (n_kc + 1) // 2, attend_pair, 0)
    out_t = jnp.concatenate([acc_ref[h, 0:HEAD_DIM, :] / acc_ref[h, HEAD_DIM:HEAD_DIM + 1, :]
                             for h in range(B_HEADS)], axis=0)
    o_ref[...] = jnp.transpose(out_t).astype(BF16)


def _attn_prompt(qit, wit, qt, kiwib, kb, vt):
    b, nc, _, tq = qt.shape
    t = nc * tq
    topk = min(TOPK_MAX, t // 4)
    idx_bits = max(1, (t - 1).bit_length())
    tr = lambda r: pl.BlockSpec((None, None, r, tq), lambda i, j: (i, j, 0, 0))
    full = lambda w: pl.BlockSpec((None, t, w), lambda i, j: (i, 0, 0))
    kernel = functools.partial(_attn_prompt_kernel, tq=tq, topk=topk, idx_bits=idx_bits)
    return pl.pallas_call(
        kernel,
        grid=(b, nc),
        in_specs=[tr(IDX_WIDTH), tr(SUBLANES), tr(B_WIDTH), full(LANES), full(KV_WIDTH),
                  pl.BlockSpec((None, nc, KV_WIDTH, tq), lambda i, j: (i, 0, 0, 0))],
        out_specs=pl.BlockSpec((None, tq, B_WIDTH), lambda i, j: (i, j, 0)),
        out_shape=jax.ShapeDtypeStruct((b, t, B_WIDTH), BF16),
        scratch_shapes=[
            pltpu.VMEM((nc + 1, tq, tq), I32),
            pltpu.VMEM((LANES, IDX_HEADS * tq), BF16),
            pltpu.VMEM((B_HEADS, KV_WIDTH, tq), BF16),
            pltpu.VMEM((B_HEADS, tq, tq), F32),
            pltpu.VMEM((B_HEADS, tq, tq), F32),
            pltpu.VMEM((tq, tq), F32),
            pltpu.VMEM((B_HEADS, 1, tq), F32),
            pltpu.VMEM((B_HEADS, HEAD_DIM + DENOM_ROWS, tq), F32),
        ],
        compiler_params=pltpu.CompilerParams(
            dimension_semantics=("arbitrary", "arbitrary"), vmem_limit_bytes=VMEM_LIMIT),
        name="attn_prompt",
    )(qit, wit, qt, kiwib, kb, vt)


def _rot_tables(pos):
    r = pos.shape[0]
    inv_freq = np.float32(ROPE_THETA) ** (-np.arange(ROT_HALF, dtype=np.float32) * np.float32(2.0) / np.float32(ROT_DIM))
    ang = pos.astype(np.float32)[:, None] * inv_freq[None, :]
    cos, sin = np.cos(ang), np.sin(ang)
    rest = HEAD_DIM - ROT_DIM
    c64 = np.concatenate([cos, cos, np.ones((r, rest), np.float32)], axis=1)
    lo64 = np.concatenate([-sin, np.zeros((r, HEAD_DIM - ROT_HALF), np.float32)], axis=1)
    hi64 = np.concatenate([np.zeros((r, ROT_HALF), np.float32), sin, np.zeros((r, rest), np.float32)], axis=1)
    return jnp.asarray(np.concatenate([c64, c64, lo64, lo64, hi64, hi64], axis=1).astype(np.float32))


def _head_avg_matrix():
    return jnp.kron(jnp.eye(256 // HEAD_DIM, dtype=F32), jnp.full((HEAD_DIM, HEAD_DIM), 1.0 / HEAD_DIM, F32)).astype(BF16)


def _pad_w_in(w_in):
    return jnp.pad(w_in, ((0, 0), (0, IN_PAD - IN_WIDTH))).astype(BF16)


def _front_sample_kernel(x_ref, shift_ref, scale_ref, w_ref, rt_ref, mavg_ref, gam_ref, bet_ref, wl_ref, bl_ref,
                         a_ref, q_ref, k_ref, v_ref, qi_ref, kiwi_ref, vg_ref, *, ts, db):
    one_scale = 1.0 + scale_ref[...]
    shift = shift_ref[...]
    h = jnp.concatenate([x_ref[t * db:(t + 1) * db, :] * one_scale + shift for t in range(ts)], axis=0).astype(BF16)
    u, vg, q, k, v, qi, kiwi = _project(h, w_ref, rt_ref[...], mavg_ref[...], gam_ref[...], bet_ref[...])
    q_ref[...] = q.astype(BF16)
    k_ref[...] = k
    v_ref[...] = v
    qi_ref[...] = qi.astype(BF16)
    kiwi_ref[...] = kiwi
    vg_ref[...] = vg
    for t in range(ts):
        s = bl_ref[t:t + 1, :]
        for src in range(t + 1):
            s = s + wl_ref[t * ts + src:t * ts + src + 1, :] * vg[src * db:(src + 1) * db, :]
        a_ref[t * db:(t + 1) * db, :] = (u[t * db:(t + 1) * db, :] * s).astype(BF16)


def _front_sample(x_tm, mod_s, w_in, rt, mavg, gam, bet, wl, bl, ts, db):
    r = ts * db
    whole = lambda a: pl.BlockSpec(a.shape, lambda i: (0,) * a.ndim)
    out = lambda w, dt: jax.ShapeDtypeStruct((r, w), dt)
    outs = (out(A_WIDTH, BF16), out(B_WIDTH, BF16), out(KV_WIDTH, F32), out(KV_WIDTH, F32),
            out(IDX_WIDTH, BF16), out(LANES, F32), out(A_WIDTH, F32))
    return pl.pallas_call(
        functools.partial(_front_sample_kernel, ts=ts, db=db),
        grid=(1,),
        in_specs=[
            whole(x_tm),
            pl.BlockSpec((db, D_MODEL), lambda i: (0, 0)),
            pl.BlockSpec((db, D_MODEL), lambda i: (0, 1)),
            whole(w_in), whole(rt), whole(mavg), whole(gam), whole(bet), whole(wl), whole(bl),
        ],
        out_specs=[pl.BlockSpec((r, s.shape[1]), lambda i: (0, 0)) for s in outs],
        out_shape=outs,
        compiler_params=pltpu.CompilerParams(dimension_semantics=("arbitrary",), vmem_limit_bytes=VMEM_LIMIT),
        name="front_sample",
    )(x_tm, mod_s, mod_s, w_in, rt, mavg, gam, bet, wl, bl)


def _idx_sample_kernel(pt_ref, qi4_ref, wcol_ref, qpos_ref, kinew_ref, kidx_hbm, bias_ref,
                       ki_buf, key_ref, sem, *, ga, ts, n_pages, page, lpad, topk, idx_bits):
    i = pl.program_id(0)
    n_steps = pl.num_programs(0)
    past = n_pages * page
    rows = ga * ts
    slot = i % 2

    def page_copy(step, to_slot, g, p):
        phys = pt_ref[step * ga + g, p]
        return pltpu.make_async_copy(kidx_hbm.at[phys], ki_buf.at[to_slot, g, :, pl.ds(p * page, page)],
                                     sem.at[to_slot])

    def start_all(step, to_slot):
        for g in range(ga):
            for p in range(n_pages):
                page_copy(step, to_slot, g, p).start()

    @pl.when(i == 0)
    def _():
        start_all(0, 0)

    @pl.when(i + 1 < n_steps)
    def _():
        start_all(i + 1, 1 - slot)

    ki_buf[slot, :, :, pl.ds(past, LANES)] = kinew_ref[...]
    for g in range(ga):
        for p in range(n_pages):
            page_copy(i, slot, g, p).wait()

    kpos = lax.broadcasted_iota(I32, (ts, lpad), 1)
    for g in range(ga):
        s = _dot(qi4_ref[g], ki_buf[slot, g].astype(BF16))
        r = jnp.maximum(s, 0.0) * wcol_ref[g]
        tot = r[0:ts]
        for h in range(1, IDX_HEADS):
            tot = tot + r[h * ts:(h + 1) * ts]
        adm = (kpos <= qpos_ref[g * ts:(g + 1) * ts, :]) & (kpos < past + ts)
        key_ref[g * ts:(g + 1) * ts, :] = jnp.where(adm, _order_key(tot, kpos, idx_bits), INT_MIN)

    sub = lpad // LANES
    lane128 = lax.broadcasted_iota(I32, (rows, LANES), 1)

    def row_total(cnt):
        tot = jnp.sum(cnt.astype(F32), axis=1, keepdims=True)
        return jnp.broadcast_to(tot, cnt.shape).astype(I32)

    def count_ge(cand):
        cnt = jnp.zeros((rows, LANES), I32)
        for s_ in range(sub):
            cnt = cnt + jnp.where(key_ref[:, s_ * LANES:(s_ + 1) * LANES] >= cand, 1, 0)
        return row_total(cnt)

    def count_gt_eq_lt(thr, pos):
        n_gt = jnp.zeros((rows, LANES), I32)
        n_eq = jnp.zeros((rows, LANES), I32)
        for s_ in range(sub):
            kk = key_ref[:, s_ * LANES:(s_ + 1) * LANES]
            n_gt = n_gt + jnp.where(kk > thr, 1, 0)
            n_eq = n_eq + jnp.where((kk == thr) & (s_ * LANES + lane128 < pos), 1, 0)
        return row_total(n_gt), row_total(n_eq)

    thr, cut = _select_threshold(count_ge, count_gt_eq_lt, (rows, LANES), topk, idx_bits)
    for s_ in range(sub):
        kk = key_ref[:, s_ * LANES:(s_ + 1) * LANES]
        sel = (kk > thr) | ((kk == thr) & (s_ * LANES + lane128 < cut))
        bias = jnp.where(sel, 0.0, -jnp.inf)
        for g in range(ga):
            bias_ref[g, :, s_ * LANES:(s_ + 1) * LANES] = bias[g * ts:(g + 1) * ts]


def _idx_sample(page_table, qi4, wcol, qpos, kinew_t, kidx_t, ts, ga):
    db = qi4.shape[0]
    n_pages = page_table.shape[1]
    page = kidx_t.shape[2]
    past = n_pages * page
    lpad = past + LANES
    topk = min(TOPK_MAX, (past + ts) // 4)
    idx_bits = max(1, (lpad - 1).bit_length())
    kernel = functools.partial(_idx_sample_kernel, ga=ga, ts=ts, n_pages=n_pages, page=page, lpad=lpad,
                               topk=topk, idx_bits=idx_bits)
    return pl.pallas_call(
        kernel,
        grid_spec=pltpu.PrefetchScalarGridSpec(
            num_scalar_prefetch=1,
            grid=(db // ga,),
            in_specs=[pl.BlockSpec((ga, IDX_HEADS * ts, IDX_DIM), lambda i, pt: (i, 0, 0)),
                      pl.BlockSpec((ga, IDX_HEADS * ts, 1), lambda i, pt: (i, 0, 0)),
                      pl.BlockSpec((ga * ts, 1), lambda i, pt: (i, 0)),
                      pl.BlockSpec((ga, IDX_DIM, LANES), lambda i, pt: (i, 0, 0)),
                      pl.BlockSpec(memory_space=pl.ANY)],
            out_specs=pl.BlockSpec((ga, ts, lpad), lambda i, pt: (i, 0, 0)),
            scratch_shapes=[
                pltpu.VMEM((2, ga, IDX_DIM, lpad), F32),
                pltpu.VMEM((ga * ts, lpad), I32),
                pltpu.SemaphoreType.DMA((2,)),
            ]),
        out_shape=jax.ShapeDtypeStruct((db, ts, lpad), F32),
        compiler_params=pltpu.CompilerParams(dimension_semantics=("arbitrary",), vmem_limit_bytes=VMEM_LIMIT),
        name="idx_sample",
    )(page_table, qi4, wcol, qpos, kinew_t, kidx_t)


def _attn_sample_kernel(pt_ref, q_ref, bias_ref, knew_ref, vnew_ref, k_hbm, v_hbm, o_ref,
                        k_buf, v_buf, sem, *, ts, n_pages, page):
    b = pl.program_id(0)
    nb = pl.num_programs(0)
    past = n_pages * page
    slot = b % 2

    def page_copies(seq, to_slot, p):
        phys = pt_ref[seq, p]
        dst = pl.ds(p * page, page)
        return (pltpu.make_async_copy(k_hbm.at[phys], k_buf.at[to_slot, :, :, dst], sem.at[0, to_slot]),
                pltpu.make_async_copy(v_hbm.at[phys], v_buf.at[to_slot, :, :, dst], sem.at[1, to_slot]))

    def start_all(seq, to_slot):
        for p in range(n_pages):
            for cp in page_copies(seq, to_slot, p):
                cp.start()

    @pl.when(b == 0)
    def _():
        start_all(0, 0)

    @pl.when(b + 1 < nb)
    def _():
        start_all(b + 1, 1 - slot)

    k_buf[slot, :, :, pl.ds(past, LANES)] = knew_ref[...]
    v_buf[slot, :, :, pl.ds(past, LANES)] = vnew_ref[...]
    for p in range(n_pages):
        for cp in page_copies(b, slot, p):
            cp.wait()

    q = q_ref[...]
    bias = bias_ref[...]
    bias2 = jnp.concatenate([bias] * B_GROUP, axis=0)
    outs = [None] * B_HEADS
    for n in range(B_KV_HEADS):
        kt = k_buf[slot, n].astype(BF16)
        vt = v_buf[slot, n].astype(BF16)
        qs = jnp.concatenate([q[:, (n * B_GROUP + g) * HEAD_DIM:(n * B_GROUP + g + 1) * HEAD_DIM]
                              for g in range(B_GROUP)], axis=0)
        sc = _dot(qs, kt) + bias2
        m = jnp.max(sc, axis=1, keepdims=True)
        p_ = jnp.exp2(sc - m)
        o = _dot_nt(p_.astype(BF16), vt) / jnp.sum(p_, axis=1, keepdims=True)
        for g in range(B_GROUP):
            outs[n * B_GROUP + g] = o[g * ts:(g + 1) * ts]
    o_ref[...] = jnp.concatenate(outs, axis=1).astype(BF16)


def _attn_sample(page_table, q, bias, knew_t, vnew_t, k_t, v_t):
    db, ts, _ = q.shape
    n_pages = page_table.shape[1]
    page = k_t.shape[3]
    lpad = bias.shape[2]
    seq3 = lambda a: pl.BlockSpec((None,) + a.shape[1:], lambda b, pt: (b,) + (0,) * (a.ndim - 1))
    anyspec = pl.BlockSpec(memory_space=pl.ANY)
    kernel = functools.partial(_attn_sample_kernel, ts=ts, n_pages=n_pages, page=page)
    return pl.pallas_call(
        kernel,
        grid_spec=pltpu.PrefetchScalarGridSpec(
            num_scalar_prefetch=1,
            grid=(db,),
            in_specs=[seq3(q), seq3(bias), seq3(knew_t), seq3(vnew_t), anyspec, anyspec],
            out_specs=pl.BlockSpec((None, ts, B_WIDTH), lambda b, pt: (b, 0, 0)),
            scratch_shapes=[
                pltpu.VMEM((2, B_KV_HEADS, HEAD_DIM, lpad), F32),
                pltpu.VMEM((2, B_KV_HEADS, HEAD_DIM, lpad), F32),
                pltpu.SemaphoreType.DMA((2, 2)),
            ]),
        out_shape=jax.ShapeDtypeStruct((db, ts, B_WIDTH), BF16),
        compiler_params=pltpu.CompilerParams(dimension_semantics=("arbitrary",), vmem_limit_bytes=VMEM_LIMIT),
        name="attn_sample",
    )(page_table, q, bias, knew_t, vnew_t, k_t, v_t)


def _layer_norm_rows(y, gam, bet):
    mu = jnp.mean(y, axis=1, keepdims=True)
    d = y - mu
    var = jnp.mean(d * d, axis=1, keepdims=True)
    return d * lax.rsqrt(var + LN_EPS) * gam + bet


def _store_row_tiles(ref, val):
    r = val.shape[0]
    for c in range(ROW_CHUNKS):
        ref[pl.ds(c, r, stride=ROW_CHUNKS), :] = val[:, c * LANES:(c + 1) * LANES]


def _load_row_tiles(ref, r):
    return jnp.concatenate([ref[pl.ds(c, r, stride=ROW_CHUNKS), :] for c in range(ROW_CHUNKS)], axis=1)


ROUTER_ROWS = SUBLANES + N_EXPERTS


def _route(logits_t):
    r = logits_t.shape[1]
    far = float(LANES)

    def softmax_rows(x):
        e = jnp.exp(x - jnp.max(x, axis=0, keepdims=True))
        return e / jnp.sum(e, axis=0, keepdims=True)

    def first_max(p):
        rows = lax.broadcasted_iota(I32, p.shape, 0).astype(F32)
        best = jnp.max(p, axis=0, keepdims=True)
        return best, jnp.min(jnp.where(p == best, rows, far), axis=0, keepdims=True), rows

    g_w, g_sel, _ = first_max(softmax_rows(logits_t[0:N_GROUPS]))
    el = jnp.zeros((EXPERTS_PER_GROUP, r), F32)
    for g in range(N_GROUPS):
        lo = SUBLANES + g * EXPERTS_PER_GROUP
        el = jnp.where(g_sel == float(g), logits_t[lo:lo + EXPERTS_PER_GROUP], el)
    ep = softmax_rows(el)
    p1, i1, rows = first_max(ep)
    p2, i2, _ = first_max(jnp.where(rows == i1, -1.0, ep))
    denom = p1 + p2
    base = g_sel * float(EXPERTS_PER_GROUP)
    return jnp.concatenate([base + i1, base + i2, g_w * p1 / denom, g_w * p2 / denom,
                            jnp.zeros((SUBLANES - 4, r), F32)], axis=0)


def _split_bf16(x):
    hi = x.astype(BF16)
    return hi, (x - hi.astype(F32)).astype(BF16)


def _mid_kernel(x_ref, a_ref, b_ref, g1_ref, sh2_ref, sc2_ref, wo_ref, gam_ref, bet_ref, wr_ref,
                x1_ref, h2_ref, route_ref, *, alpha):
    mixed = _dot(a_ref[...], wo_ref[0:A_WIDTH, :]) + _dot(b_ref[...], wo_ref[A_WIDTH:A_WIDTH + B_WIDTH, :])
    x1 = _layer_norm_rows(alpha * x_ref[...] + g1_ref[...] * mixed, gam_ref[...], bet_ref[...])
    x1_ref[...] = x1
    h2 = x1 * (1.0 + sc2_ref[...]) + sh2_ref[...]
    _store_row_tiles(h2_ref, h2)
    w_hi, w_lo = _split_bf16(wr_ref[...])
    h_hi, h_lo = _split_bf16(h2)
    logits_t = _dot_nt(w_hi, h_hi) + _dot_nt(w_hi, h_lo) + _dot_nt(w_lo, h_hi)
    route_ref[...] = _route(logits_t)


def _mod_spec(mod3, comp, tiles_per_group):
    rm = mod3.shape[1]
    return pl.BlockSpec((None, rm, D_MODEL), lambda i: (i // tiles_per_group, 0, comp))


def _mid(x, a, b, mod3, tiles_per_group, tile, w_out, gam, bet, w_router, alpha):
    n = x.shape[0]
    rows = lambda w: pl.BlockSpec((tile, w), lambda i: (i, 0))
    whole = lambda arr: pl.BlockSpec(arr.shape, lambda i: (0,) * arr.ndim)
    return pl.pallas_call(
        functools.partial(_mid_kernel, alpha=alpha),
        grid=(n // tile,),
        in_specs=[rows(D_MODEL), rows(A_WIDTH), rows(B_WIDTH),
                  _mod_spec(mod3, 2, tiles_per_group), _mod_spec(mod3, 3, tiles_per_group),
                  _mod_spec(mod3, 4, tiles_per_group),
                  whole(w_out), whole(gam), whole(bet), whole(w_router)],
        out_specs=[rows(D_MODEL), pl.BlockSpec((tile * ROW_CHUNKS, LANES), lambda i: (i, 0)),
                   pl.BlockSpec((SUBLANES, tile), lambda i: (0, i))],
        out_shape=(jax.ShapeDtypeStruct((n, D_MODEL), F32),
                   jax.ShapeDtypeStruct((n * ROW_CHUNKS, LANES), F32),
                   jax.ShapeDtypeStruct((SUBLANES, n), F32)),
        compiler_params=pltpu.CompilerParams(dimension_semantics=("arbitrary",), vmem_limit_bytes=VMEM_LIMIT),
        name="mid",
    )(x, a, b, mod3, mod3, mod3, w_out, gam, bet, w_router)


META_CNT, META_START, META_END, META_NACT = 0, 1, 2, 3


def _plan_kernel(route_ref, pos_ref, meta_ref, te_ref, carry_ref, starts_ref, *, nt, tm):
    phase = pl.program_id(0)
    i = pl.program_id(1)

    @pl.when(i == 0)
    def _():
        carry_ref[...] = jnp.zeros(carry_ref.shape, F32)

    eid = route_ref[0:2, :].astype(I32)
    e_iota = lax.broadcasted_iota(I32, (N_EXPERTS, nt), 0)
    hit0 = eid[0:1, :] == e_iota
    hit1 = eid[1:2, :] == e_iota
    onehot = jnp.where(hit0 | hit1, 1.0, 0.0)

    @pl.when(phase == 0)
    def _():
        pos_ref[...] = jnp.zeros(pos_ref.shape, I32)
        carry_ref[...] = carry_ref[...] + jnp.sum(onehot, axis=1, keepdims=True)

        @pl.when(i == pl.num_programs(1) - 1)
        def _():
            cnt = carry_ref[...]
            tiles = jnp.floor((cnt + float(tm - 1)) * (1.0 / tm))
            r = lax.broadcasted_iota(I32, (N_EXPERTS, N_EXPERTS), 0)
            c = lax.broadcasted_iota(I32, (N_EXPERTS, N_EXPERTS), 1)
            lower = jnp.where(r > c, 1.0, 0.0).astype(BF16)
            tiles_before = _dot(lower, jnp.broadcast_to(tiles, (N_EXPERTS, LANES)).astype(BF16))[:, 0:1]
            tiles_end = tiles_before + tiles
            starts_ref[...] = tiles_before * float(tm)

            diag = (lax.broadcasted_iota(I32, (N_EXPERTS, LANES), 0)
                    == lax.broadcasted_iota(I32, (N_EXPERTS, LANES), 1))

            def as_row(col):
                return jnp.sum(jnp.where(diag, jnp.broadcast_to(col, (N_EXPERTS, LANES)), 0.0),
                               axis=0, keepdims=True)

            nact = jnp.broadcast_to(jnp.max(tiles_end, axis=0, keepdims=True), (1, LANES))
            meta_ref[...] = jnp.concatenate(
                [as_row(cnt), as_row(tiles_before * float(tm)), as_row(tiles_end * float(tm)), nact,
                 jnp.zeros((SUBLANES - 4, LANES), F32)], axis=0).astype(I32)
            tile_i = lax.broadcasted_iota(I32, (N_EXPERTS, te_ref.shape[1]), 1).astype(F32)
            owner = jnp.sum(jnp.where(tiles_end <= tile_i, 1.0, 0.0), axis=0, keepdims=True)
            te_ref[...] = jnp.minimum(owner, float(N_EXPERTS - 1)).astype(I32)

    @pl.when(phase == 1)
    def _():
        upper = (lax.broadcasted_iota(I32, (nt, nt), 0) <= lax.broadcasted_iota(I32, (nt, nt), 1))
        incl = _dot(onehot.astype(BF16), jnp.where(upper, 1.0, 0.0).astype(BF16))
        slot = starts_ref[...] + carry_ref[...] + incl - 1.0
        pos_ref[0:1, :] = jnp.sum(jnp.where(hit0, slot, 0.0), axis=0, keepdims=True).astype(I32)
        pos_ref[1:2, :] = jnp.sum(jnp.where(hit1, slot, 0.0), axis=0, keepdims=True).astype(I32)
        carry_ref[...] = carry_ref[...] + jnp.sum(onehot, axis=1, keepdims=True)


def _plan(route_t, nt, tm):
    n = route_t.shape[1]
    n_tiles = (2 * n) // tm + N_EXPERTS
    te_width = pl.cdiv(n_tiles, LANES) * LANES
    pos, meta, te = pl.pallas_call(
        functools.partial(_plan_kernel, nt=nt, tm=tm),
        grid=(2, n // nt),
        in_specs=[pl.BlockSpec((SUBLANES, nt), lambda p, i: (0, i))],
        out_specs=[pl.BlockSpec((2, nt), lambda p, i: (0, i * p)),
                   pl.BlockSpec((SUBLANES, LANES), lambda p, i: (0, 0)),
                   pl.BlockSpec((1, te_width), lambda p, i: (0, 0))],
        out_shape=(jax.ShapeDtypeStruct((2, n), I32), jax.ShapeDtypeStruct((SUBLANES, LANES), I32),
                   jax.ShapeDtypeStruct((1, te_width), I32)),
        scratch_shapes=[pltpu.VMEM((N_EXPERTS, 1), F32), pltpu.VMEM((N_EXPERTS, 1), F32)],
        compiler_params=pltpu.CompilerParams(dimension_semantics=("arbitrary", "arbitrary")),
        name="plan",
    )(route_t)
    return pos, meta, te, n_tiles


def _row_copy(src, src_row, dst, dst_row, sem):
    return pltpu.make_async_copy(src.at[pl.ds(src_row * ROW_CHUNKS, ROW_CHUNKS)],
                                 dst.at[pl.ds(dst_row * ROW_CHUNKS, ROW_CHUNKS)], sem)


def _dispatch_kernel(meta_ref, pos_ref, hp_ref, hs_ref, xs_out, zero_ref, sem, *, tt, blocks_p, tm, n_tiles):
    i = pl.program_id(0)

    def scatter(h_ref):
        def issue(j, _):
            for k in range(2):
                _row_copy(h_ref, j, xs_out, pos_ref[k, j], sem.at[k]).start()
            return 0

        def drain(j, _):
            for k in range(2):
                _row_copy(h_ref, j, xs_out, pos_ref[k, j], sem.at[k]).wait()
            return 0

        lax.fori_loop(0, tt, issue, 0, unroll=ROW_DMA_UNROLL)
        lax.fori_loop(0, tt, drain, 0, unroll=ROW_DMA_UNROLL)

    @pl.when(i < blocks_p)
    def _():
        scatter(hp_ref)

    @pl.when(i >= blocks_p)
    def _():
        scatter(hs_ref)

    @pl.when(i == pl.num_programs(0) - 1)
    def _():
        zero_ref[...] = jnp.zeros(zero_ref.shape, F32)

        def zero_rows(first_row, n_rows):
            return pltpu.make_async_copy(zero_ref.at[pl.ds(0, n_rows * ROW_CHUNKS)],
                                         xs_out.at[pl.ds(first_row * ROW_CHUNKS, n_rows * ROW_CHUNKS)], sem.at[0])

        def start_row(r, c):
            zero_rows(r, 1).start()
            return c

        def wait_row(r, c):
            zero_rows(r, 1).wait()
            return c

        def per_expert(row_fn):
            def body(e, c):
                lo = meta_ref[META_START, e] + meta_ref[META_CNT, e]
                return lax.fori_loop(lo, meta_ref[META_END, e], row_fn, c)
            return body

        def start_tile(t, c):
            zero_rows(t * tm, tm).start()
            return c

        def wait_tile(t, c):
            zero_rows(t * tm, tm).wait()
            return c

        nact = meta_ref[META_NACT, 0]
        lax.fori_loop(0, N_EXPERTS, per_expert(start_row), 0)
        lax.fori_loop(nact, n_tiles, start_tile, 0)
        lax.fori_loop(0, N_EXPERTS, per_expert(wait_row), 0)
        lax.fori_loop(nact, n_tiles, wait_tile, 0)


def _dispatch(meta, pos, h2_p, h2_s, tt, tm, n_tiles):
    blocks_p = h2_p.shape[0] // (tt * ROW_CHUNKS)
    blocks_s = h2_s.shape[0] // (tt * ROW_CHUNKS)
    return pl.pallas_call(
        functools.partial(_dispatch_kernel, tt=tt, blocks_p=blocks_p, tm=tm, n_tiles=n_tiles),
        grid_spec=pltpu.PrefetchScalarGridSpec(
            num_scalar_prefetch=1,
            grid=(blocks_p + blocks_s,),
            in_specs=[pl.BlockSpec((2, tt), lambda i, m: (0, i), memory_space=pltpu.SMEM),
                      pl.BlockSpec((tt * ROW_CHUNKS, LANES), lambda i, m: (jnp.minimum(i, blocks_p - 1), 0)),
                      pl.BlockSpec((tt * ROW_CHUNKS, LANES), lambda i, m: (jnp.maximum(i - blocks_p, 0), 0))],
            out_specs=pl.BlockSpec(memory_space=pl.ANY),
            scratch_shapes=[pltpu.VMEM((tm * ROW_CHUNKS, LANES), F32), pltpu.SemaphoreType.DMA((2,))]),
        out_shape=jax.ShapeDtypeStruct((n_tiles * tm * ROW_CHUNKS, LANES), F32),
        compiler_params=pltpu.CompilerParams(dimension_semantics=("arbitrary",)),
        name="dispatch",
    )(meta, pos, h2_p, h2_s)


def _experts_kernel(te_ref, meta_ref, xs_ref, wg_ref, wu_ref, wd_ref, ys_ref, wg_b, wu_b, wd_b, *, tm):
    i = pl.program_id(0)
    active = i < meta_ref[META_NACT, 0]
    fresh = (i == 0) | (te_ref[0, i] != te_ref[0, jnp.maximum(i - 1, 0)])

    @pl.when(active & fresh)
    def _():
        wg_b[...] = wg_ref[...].astype(BF16)
        wu_b[...] = wu_ref[...].astype(BF16)
        wd_b[...] = wd_ref[...].astype(BF16)

    @pl.when(active)
    def _():
        x = _load_row_tiles(xs_ref, tm).astype(BF16)
        hid = (jax.nn.silu(_dot(x, wg_b[...])) * _dot(x, wu_b[...])).astype(BF16)
        _store_row_tiles(ys_ref, _dot(hid, wd_b[...]))

    @pl.when(jnp.logical_not(active))
    def _():
        ys_ref[...] = jnp.zeros(ys_ref.shape, F32)


def _experts(te, meta, xs, w_gate, w_up, w_down, tm, n_tiles):
    last_active = lambda i, m: jnp.minimum(i, m[META_NACT, 0] - 1)
    tile = pl.BlockSpec((tm * ROW_CHUNKS, LANES), lambda i, te, m: (last_active(i, m), 0))
    w_in_spec = pl.BlockSpec((None, D_MODEL, D_EXPERT), lambda i, te, m: (te[0, last_active(i, m)], 0, 0))
    w_out_spec = pl.BlockSpec((None, D_EXPERT, D_MODEL), lambda i, te, m: (te[0, last_active(i, m)], 0, 0))
    return pl.pallas_call(
        functools.partial(_experts_kernel, tm=tm),
        grid_spec=pltpu.PrefetchScalarGridSpec(
            num_scalar_prefetch=2,
            grid=(n_tiles,),
            in_specs=[tile, w_in_spec, w_in_spec, w_out_spec],
            out_specs=pl.BlockSpec((tm * ROW_CHUNKS, LANES), lambda i, te, m: (i, 0)),
            scratch_shapes=[pltpu.VMEM((D_MODEL, D_EXPERT), BF16), pltpu.VMEM((D_MODEL, D_EXPERT), BF16),
                            pltpu.VMEM((D_EXPERT, D_MODEL), BF16)]),
        out_shape=jax.ShapeDtypeStruct(xs.shape, F32),
        compiler_params=pltpu.CompilerParams(dimension_semantics=("arbitrary",), vmem_limit_bytes=VMEM_LIMIT),
        name="experts",
    )(te, meta, xs, w_gate, w_up, w_down)


def _combine_kernel(pos_ref, route_ref, x1_ref, g2_ref, gam_ref, bet_ref, ys_hbm, y_ref, buf0, buf1, sem,
                    *, tt, alpha):
    bufs = (buf0, buf1)

    def issue(j, _):
        for k in range(2):
            _row_copy(ys_hbm, pos_ref[k, j], bufs[k], j, sem.at[k]).start()
        return 0

    def drain(j, _):
        for k in range(2):
            _row_copy(ys_hbm, pos_ref[k, j], bufs[k], j, sem.at[k]).wait()
        return 0

    lax.fori_loop(0, tt, issue, 0, unroll=ROW_DMA_UNROLL)
    lax.fori_loop(0, tt, drain, 0, unroll=ROW_DMA_UNROLL)
    route = route_ref[...]
    eye = lax.broadcasted_iota(I32, (tt, tt), 0) == lax.broadcasted_iota(I32, (tt, tt), 1)
    as_col = lambda row: jnp.sum(jnp.where(eye, jnp.broadcast_to(row, (tt, tt)), 0.0), axis=1, keepdims=True)
    f = as_col(route[2:3, :]) * _load_row_tiles(buf0, tt) + as_col(route[3:4, :]) * _load_row_tiles(buf1, tt)
    y_ref[...] = _layer_norm_rows(alpha * x1_ref[...] + g2_ref[...] * f, gam_ref[...], bet_ref[...])


def _combine(pos, route_t, col_block0, x1, mod3, tiles_per_group, tt, gam, bet, ys, alpha):
    n = x1.shape[0]
    rows = lambda w: pl.BlockSpec((tt, w), lambda i: (i, 0))
    whole = lambda arr: pl.BlockSpec(arr.shape, lambda i: (0,) * arr.ndim)
    return pl.pallas_call(
        functools.partial(_combine_kernel, tt=tt, alpha=alpha),
        grid=(n // tt,),
        in_specs=[pl.BlockSpec((2, tt), lambda i: (0, i + col_block0), memory_space=pltpu.SMEM),
                  pl.BlockSpec((SUBLANES, tt), lambda i: (0, i + col_block0)),
                  rows(D_MODEL), _mod_spec(mod3, 5, tiles_per_group), whole(gam), whole(bet),
                  pl.BlockSpec(memory_space=pl.ANY)],
        out_specs=rows(D_MODEL),
        out_shape=jax.ShapeDtypeStruct((n, D_MODEL), F32),
        scratch_shapes=[pltpu.VMEM((tt * ROW_CHUNKS, LANES), F32), pltpu.VMEM((tt * ROW_CHUNKS, LANES), F32),
                        pltpu.SemaphoreType.DMA((2,))],
        compiler_params=pltpu.CompilerParams(dimension_semantics=("arbitrary",), vmem_limit_bytes=VMEM_LIMIT),
        name="combine",
    )(pos, route_t, x1, mod3, gam, bet, ys)


def kernel(x_prompt, x_sample, cache_k, cache_v, cache_kidx, page_table, c_prompt, c_sample, w_ada, b_ada, w_in,
           a_ln_g, a_ln_b, w_spatial, b_spatial, w_out, ln1_g, ln1_b, w_group_router, w_expert_router, w_gate,
           w_up, w_down, ln2_g, ln2_b):
    depth = w_ada.shape[0]
    assert depth == 1, "one trunk layer"
    alpha = (2.0 * depth) ** 0.25
    bsz, seq, d = x_prompt.shape
    db, ts, _ = x_sample.shape
    n_pages = page_table.shape[1]
    page = cache_k.shape[2]
    past = n_pages * page
    tile_p = min(PROMPT_TILE, seq)
    assert d == D_MODEL and seq % tile_p == 0 and tile_p % CHUNK == 0 and ts <= SUBLANES and page == LANES
    l = 0

    n_c = bsz + db
    n_c_pad = pl.cdiv(n_c, SUBLANES) * SUBLANES
    c_all = jnp.pad(jnp.concatenate([c_prompt, c_sample], axis=0), ((0, n_c_pad - n_c), (0, 0)))
    mod = _ada(c_all, w_ada[l], b_ada[l][None, :])
    mod_p = mod[:bsz]
    mod_s = mod[bsz:n_c]

    w_in_b = _pad_w_in(w_in[l])
    w_out_b = w_out[l].astype(BF16)
    mavg = _head_avg_matrix()
    gam_a = a_ln_g[l].reshape(1, A_WIDTH)
    bet_a = a_ln_b[l].reshape(1, A_WIDTH)
    w_router = jnp.concatenate([w_group_router[l].T, jnp.zeros((SUBLANES - N_GROUPS, d), F32),
                                w_expert_router[l].reshape(d, N_EXPERTS).T], axis=0)
    ln1 = (ln1_g[l][None, :], ln1_b[l][None, :])
    ln2 = (ln2_g[l][None, :], ln2_b[l][None, :])

    wsp = w_spatial[l].reshape(A_HEADS // 2, 2, CHUNK, CHUNK).transpose(0, 2, 1, 3).reshape(A_HEADS // 2, CHUNK, 2 * CHUNK)
    bsp = jnp.repeat(b_spatial[l].T, HEAD_DIM, axis=1)
    a_p, kt_p, vt32_p, kit_p, kb_p, kiwib_p, vt_p, qt_p, qit_p, wit_p = _front_prompt(
        x_prompt, mod_p.reshape(bsz, 6, d), w_in_b, _rot_tables(np.arange(seq)), mavg, gam_a, bet_a, wsp, bsp, tile_p)
    b_p = _attn_prompt(qit_p, wit_p, qt_p, kiwib_p, kb_p, vt_p)
    n_p = bsz * seq
    tile_m = min(MID_TILE, seq)
    assert seq % tile_m == 0
    x1_p, h2_p, route_p = _mid(x_prompt.reshape(n_p, d), a_p.reshape(n_p, A_WIDTH), b_p.reshape(n_p, B_WIDTH),
                               mod_p.reshape(bsz, 1, 6 * d), seq // tile_m, tile_m, w_out_b, *ln1, w_router, alpha)

    r_s = ts * db
    x_tm = x_sample.transpose(1, 0, 2).reshape(r_s, d)
    rt_s = _rot_tables(np.repeat(past + np.arange(ts), db))
    w_small = w_spatial[l][:, :ts, :ts]
    wl = jnp.repeat(w_small.transpose(1, 2, 0).reshape(ts * ts, A_HEADS), HEAD_DIM, axis=1)
    bl = jnp.repeat(b_spatial[l][:, :ts].T, HEAD_DIM, axis=1)
    a_s, q_s, k_s, v_s, qi_s, kiwi_s, vg_s = _front_sample(x_tm, mod_s, w_in_b, rt_s, mavg, gam_a, bet_a, wl, bl, ts, db)

    def seq_major(a):
        return a.reshape(ts, db, a.shape[-1]).transpose(1, 0, 2)

    def new_t(a, heads):
        a = a.reshape(ts, db, heads, HEAD_DIM).transpose(1, 2, 3, 0)
        return jnp.pad(a, ((0, 0), (0, 0), (0, 0), (0, LANES - ts)))

    ga = min(IDX_SEQS, db)
    assert db % ga == 0 and (ga * ts) % SUBLANES == 0
    qi4 = qi_s.reshape(ts, db, IDX_HEADS, IDX_DIM).transpose(1, 2, 0, 3).reshape(db, IDX_HEADS * ts, IDX_DIM)
    wcol = kiwi_s[:, WI_LANE:WI_LANE + IDX_HEADS].reshape(ts, db, IDX_HEADS).transpose(1, 2, 0).reshape(db, IDX_HEADS * ts, 1)
    qpos = jnp.tile(past + jnp.arange(ts, dtype=I32), db).reshape(db * ts, 1)
    kinew_t = new_t(kiwi_s[:, :IDX_DIM], 1)[:, 0]
    kidx_t = jnp.transpose(cache_kidx[l], (0, 2, 1))
    k_t = jnp.transpose(cache_k[l], (0, 2, 3, 1))
    v_t = jnp.transpose(cache_v[l], (0, 2, 3, 1))
    bias_s = _idx_sample(page_table, qi4, wcol, qpos, kinew_t, kidx_t, ts, ga)
    b_s = _attn_sample(page_table, seq_major(q_s), bias_s, new_t(k_s, B_KV_HEADS), new_t(v_s, B_KV_HEADS), k_t, v_t)
    b_s_tm = b_s.transpose(1, 0, 2).reshape(r_s, B_WIDTH)
    x1_s, h2_s, route_s = _mid(x_tm, a_s, b_s_tm, mod_s.reshape(1, db, 6 * d), ts, db, w_out_b, *ln1, w_router, alpha)

    n_all = n_p + r_s
    tok_p = min(TOKEN_TILE, seq)
    rank_tile = min(RANK_TILE, n_all)
    assert n_all % rank_tile == 0 and n_p % tok_p == 0 and r_s % tok_p == 0 and n_p % db == 0
    route_all = jnp.concatenate([route_p, route_s], axis=1)
    pos, meta, te, n_tiles = _plan(route_all, rank_tile, MOE_TILE)
    xs = _dispatch(meta, pos, h2_p, h2_s, tok_p, MOE_TILE, n_tiles)
    ys = _experts(te, meta, xs, w_gate[l], w_up[l], w_down[l], MOE_TILE, n_tiles)
    y_p = _combine(pos, route_all, 0, x1_p, mod_p.reshape(bsz, 1, 6 * d), seq // tok_p, tok_p, *ln2, ys, alpha)
    y_s_tm = _combine(pos, route_all, n_p // db, x1_s, mod_s.reshape(1, db, 6 * d), ts, db, *ln2, ys, alpha)
    y_s = y_s_tm.reshape(ts, db, d).transpose(1, 0, 2)

    kv5 = lambda a, n, t: a.reshape(1, n, t, B_KV_HEADS, HEAD_DIM)
    kv5_t = lambda a: a.reshape(1, bsz, B_KV_HEADS, HEAD_DIM, seq).transpose(0, 1, 4, 2, 3)
    return (y_p.reshape(bsz, seq, d), y_s,
            kv5_t(kt_p), kv5_t(vt32_p), kit_p.transpose(0, 2, 1)[None],
            kv5(seq_major(k_s), db, ts), kv5(seq_major(v_s), db, ts), seq_major(kiwi_s)[..., :IDX_DIM][None],
            seq_major(vg_s).reshape(1, db, ts, A_HEADS, HEAD_DIM))
```

```python
import functools

import jax
import jax.numpy as jnp
import numpy as np
from jax import lax
from jax.experimental import pallas as pl
from jax.experimental.pallas import tpu as pltpu

F32 = jnp.float32
BF16 = jnp.bfloat16
I32 = jnp.int32
I16 = jnp.int16

D_MODEL = 1024
HEAD_DIM = 64
A_HEADS = 8
A_WIDTH = A_HEADS * HEAD_DIM
CHUNK = 128
B_HEADS = 8
B_KV_HEADS = 4
B_GROUP = B_HEADS // B_KV_HEADS
B_WIDTH = B_HEADS * HEAD_DIM
KV_WIDTH = B_KV_HEADS * HEAD_DIM
IDX_HEADS = 4
IDX_DIM = 64
IDX_WIDTH = IDX_HEADS * IDX_DIM
TOPK_MAX = 256
ROPE_THETA = 500000.0
ROT_DIM = HEAD_DIM // 4
ROT_HALF = ROT_DIM // 2
ATTN_SCALE = HEAD_DIM ** -0.5
LOG2_E = 1.4426950408889634
Q_SCALE = ATTN_SCALE * LOG2_E
DENOM_ROWS = 16
N_GROUPS = 4
EXPERTS_PER_GROUP = 8
N_EXPERTS = N_GROUPS * EXPERTS_PER_GROUP
D_EXPERT = 512
LN_EPS = 1e-5

LANES = 128
SUBLANES = 8
ROW_CHUNKS = D_MODEL // LANES

C_AU, C_AV, C_Q, C_K, C_V, C_QI, C_KI = 0, 512, 1024, 1536, 1792, 2048, 2304
IN_WIDTH = C_KI + IDX_DIM + IDX_HEADS
IN_PAD = 2432
WI_LANE = IDX_DIM

INT_MIN = -(2 ** 31)
INT_MAX = 2 ** 31 - 1
VMEM_LIMIT = 48 * 1024 * 1024

PROMPT_TILE = 256
MOE_TILE = 256
TOKEN_TILE = 256
RANK_TILE = 512
IDX_SEQS = 16
STEPS_PER_CHECK = 4
ROW_DMA_UNROLL = 8
MID_TILE = 512


def _dot(a, b):
    return jnp.dot(a, b, preferred_element_type=F32)


def _dot_nt(a, b):
    return lax.dot_general(a, b, (((1,), (1,)), ((), ())), preferred_element_type=F32)


def _ada_kernel(c_ref, w_ref, b_ref, o_ref):
    s = jax.nn.silu(c_ref[...]).astype(BF16)
    o_ref[...] = _dot(s, w_ref[...].astype(BF16)) + b_ref[...]


def _ada(c_all, w_ada, b_ada):
    rows = c_all.shape[0]
    n_out = w_ada.shape[1]
    tn = 1024
    return pl.pallas_call(
        _ada_kernel,
        grid=(n_out // tn,),
        in_specs=[
            pl.BlockSpec((rows, D_MODEL), lambda j: (0, 0)),
            pl.BlockSpec((D_MODEL, tn), lambda j: (0, j)),
            pl.BlockSpec((1, tn), lambda j: (0, j)),
        ],
        out_specs=pl.BlockSpec((rows, tn), lambda j: (0, j)),
        out_shape=jax.ShapeDtypeStruct((rows, n_out), F32),
        compiler_params=pltpu.CompilerParams(
            dimension_semantics=("arbitrary",), vmem_limit_bytes=VMEM_LIMIT),
        name="ada",
    )(c_all, w_ada, b_ada)


def _rotate(x, rt, blk):
    c = rt[:, 0:LANES]
    s_lo = rt[:, LANES:2 * LANES]
    s_hi = rt[:, 2 * LANES:3 * LANES]
    if blk == 1:
        head = lax.broadcasted_iota(I32, c.shape, 1) < IDX_DIM
        c = jnp.where(head, c, 1.0)
        s_lo = jnp.where(head, s_lo, 0.0)
        s_hi = jnp.where(head, s_hi, 0.0)
    outs = []
    for j in range(x.shape[1] // LANES):
        xb = x[:, j * LANES:(j + 1) * LANES]
        up = pltpu.roll(xb, LANES - ROT_HALF, 1)
        dn = pltpu.roll(xb, ROT_HALF, 1)
        outs.append(xb * c + up * s_lo + dn * s_hi)
    return outs[0] if len(outs) == 1 else jnp.concatenate(outs, axis=1)


def _head_ln(g, mavg, gam, bet):
    def seg_mean(x):
        parts = [_dot(x[:, j * 256:(j + 1) * 256].astype(BF16), mavg) for j in range(A_WIDTH // 256)]
        return jnp.concatenate(parts, axis=1)
    mu = seg_mean(g)
    d = g - mu
    var = seg_mean(d * d)
    return d * lax.rsqrt(var + LN_EPS) * gam + bet


def _project(h, w_ref, rt, mavg, gam, bet):
    u = jax.nn.gelu(_dot(h, w_ref[:, C_AU:C_AV]))
    vg = _head_ln(jax.nn.gelu(_dot(h, w_ref[:, C_AV:C_Q])), mavg, gam, bet)
    q = _rotate(_dot(h, w_ref[:, C_Q:C_K]), rt, 0) * Q_SCALE
    k = _rotate(_dot(h, w_ref[:, C_K:C_V]), rt, 0)
    v = _dot(h, w_ref[:, C_V:C_QI])
    qi = _rotate(_dot(h, w_ref[:, C_QI:C_KI]), rt, 0)
    kiwi = _rotate(_dot(h, w_ref[:, C_KI:IN_PAD]), rt, 1)
    return u, vg, q, k, v, qi, kiwi


def _front_prompt_kernel(x_ref, mod_ref, w_ref, rt_ref, mavg_ref, gam_ref, bet_ref, wsp_ref, bsp_ref,
                         a_ref, kt_ref, vt_ref, kit_ref, kb_ref, kiwib_ref, vtb_ref, qt_ref, qit_ref, wit_ref):
    shift = mod_ref[0:1, :]
    scale = mod_ref[1:2, :]
    h = (x_ref[...] * (1.0 + scale) + shift).astype(BF16)
    u, vg, q, k, v, qi, kiwi = _project(h, w_ref, rt_ref[...], mavg_ref[...], gam_ref[...], bet_ref[...])
    v_t = jnp.transpose(v)
    kiwi_t = jnp.transpose(kiwi)
    kt_ref[...] = jnp.transpose(k)
    vt_ref[...] = v_t
    kit_ref[...] = kiwi_t[0:IDX_DIM, :]
    kb_ref[...] = k.astype(BF16)
    kiwib_ref[...] = kiwi.astype(BF16)
    vtb_ref[...] = v_t.astype(BF16)
    qt_ref[...] = jnp.transpose(q).astype(BF16)
    qit_ref[...] = jnp.transpose(qi).astype(BF16)
    wit_ref[...] = kiwi_t[WI_LANE:WI_LANE + SUBLANES, :]

    rows = lax.broadcasted_iota(I32, (CHUNK, 2 * CHUNK), 0)
    cols = lax.broadcasted_iota(I32, (CHUNK, 2 * CHUNK), 1) % CHUNK
    causal = cols <= rows
    lane = lax.broadcasted_iota(I32, (CHUNK, LANES), 1)
    tt = x_ref.shape[0]
    for cidx in range(tt // CHUNK):
        rs = slice(cidx * CHUNK, (cidx + 1) * CHUNK)
        blocks = []
        for p in range(A_HEADS // 2):
            wcat = jnp.where(causal, wsp_ref[p], 0.0).astype(BF16)
            vb = vg[rs, p * LANES:(p + 1) * LANES]
            rhs = jnp.concatenate([jnp.where(lane < HEAD_DIM, vb, 0.0),
                                   jnp.where(lane >= HEAD_DIM, vb, 0.0)], axis=0).astype(BF16)
            blocks.append(_dot(wcat, rhs))
        s = jnp.concatenate(blocks, axis=1) + bsp_ref[...]
        a_ref[rs, :] = (u[rs, :] * s).astype(BF16)


def _front_prompt(x, mod, w_in, rt, mavg, gam, bet, wsp, bsp, tt):
    b, t, _ = x.shape
    nc = t // tt
    tok = lambda w: pl.BlockSpec((None, tt, w), lambda i, j: (i, j, 0))
    tr = lambda r: pl.BlockSpec((None, None, r, tt), lambda i, j: (i, j, 0, 0))
    pos_minor = lambda r: pl.BlockSpec((None, r, tt), lambda i, j: (i, 0, j))
    const2 = lambda a: pl.BlockSpec(a.shape, lambda i, j: (0,) * a.ndim)
    out_shapes = (
        jax.ShapeDtypeStruct((b, t, A_WIDTH), BF16),
        jax.ShapeDtypeStruct((b, KV_WIDTH, t), F32),
        jax.ShapeDtypeStruct((b, KV_WIDTH, t), F32),
        jax.ShapeDtypeStruct((b, IDX_DIM, t), F32),
        jax.ShapeDtypeStruct((b, t, KV_WIDTH), BF16),
        jax.ShapeDtypeStruct((b, t, LANES), BF16),
        jax.ShapeDtypeStruct((b, nc, KV_WIDTH, tt), BF16),
        jax.ShapeDtypeStruct((b, nc, B_WIDTH, tt), BF16),
        jax.ShapeDtypeStruct((b, nc, IDX_WIDTH, tt), BF16),
        jax.ShapeDtypeStruct((b, nc, SUBLANES, tt), F32),
    )
    return pl.pallas_call(
        _front_prompt_kernel,
        grid=(b, nc),
        in_specs=[
            tok(D_MODEL),
            pl.BlockSpec((None, 6, D_MODEL), lambda i, j: (i, 0, 0)),
            const2(w_in),
            pl.BlockSpec((tt, 3 * LANES), lambda i, j: (j, 0)),
            const2(mavg), const2(gam), const2(bet), const2(wsp), const2(bsp),
        ],
        out_specs=[tok(A_WIDTH), pos_minor(KV_WIDTH), pos_minor(KV_WIDTH), pos_minor(IDX_DIM), tok(KV_WIDTH),
                   tok(LANES), tr(KV_WIDTH), tr(B_WIDTH), tr(IDX_WIDTH), tr(SUBLANES)],
        out_shape=out_shapes,
        compiler_params=pltpu.CompilerParams(
            dimension_semantics=("arbitrary", "arbitrary"), vmem_limit_bytes=VMEM_LIMIT),
        name="front_prompt",
    )(x, mod, w_in, rt, mavg, gam, bet, wsp, bsp)


def _order_key(score, kpos, idx_bits):
    offs = 1 << idx_bits
    bits = pltpu.bitcast(score, I32)
    key = bits ^ ((bits >> 31) & 0x7FFFFFFF)
    key = key + jnp.where(score > 0.0, offs, 0)
    return jnp.where(score == 0.0, offs - kpos, key)


def _select_threshold(count_ge, count_gt_eq_lt, shape, topk, idx_bits, count_ge_high=None):
    offs = 1 << idx_bits

    def narrow(lo, hi, n_lo, mid, tot, ok):
        take = ok & (tot >= topk)
        return jnp.where(take, mid, lo), jnp.where(ok & (tot < topk), mid, hi), jnp.where(take, tot, n_lo)

    def any_true(mask):
        return jnp.max(mask.astype(F32)) > 0.0

    lo = jnp.full(shape, INT_MIN + 1, I32)
    hi = jnp.full(shape, INT_MAX, I32)
    n_lo = jnp.full(shape, INT_MAX, I32)
    for value in (offs + 1, 1):
        mid = jnp.full(shape, value, I32)
        lo, hi, n_lo = narrow(lo, hi, n_lo, mid, count_ge(mid), (lo < mid) & (mid < hi))

    if count_ge_high is not None:
        def high_range(lo, hi):
            return (lo >> 16) + 1, (hi - 1) >> 16

        def coarse_pending(carry):
            i, lo, hi, n_lo = carry
            first, last = high_range(lo, hi)
            return (i < 16 + STEPS_PER_CHECK) & any_true((n_lo != topk) & (first <= last))

        def coarse_steps(carry):
            i, lo, hi, n_lo = carry
            for _ in range(STEPS_PER_CHECK):
                first, last = high_range(lo, hi)
                ok = first <= last
                mid16 = (first + last) >> 1
                tot = count_ge_high(jnp.where(ok, mid16, 2 ** 15 - 1))
                lo, hi, n_lo = narrow(lo, hi, n_lo, mid16 << 16, tot, ok)
            return i + STEPS_PER_CHECK, lo, hi, n_lo

        _, lo, hi, n_lo = lax.while_loop(coarse_pending, coarse_steps, (jnp.int32(0), lo, hi, n_lo))

    def pending(carry):
        i, lo, hi, n_lo = carry
        return (i < 32 + STEPS_PER_CHECK) & any_true((n_lo != topk) & (hi - 1 > lo))

    def steps(carry):
        i, lo, hi, n_lo = carry
        for _ in range(STEPS_PER_CHECK):
            mid = (lo >> 1) + (hi >> 1) + (lo & hi & 1)
            lo, hi, n_lo = narrow(lo, hi, n_lo, mid, count_ge(mid), hi - 1 > lo)
        return i + STEPS_PER_CHECK, lo, hi, n_lo

    _, thr, _, n_lo = lax.while_loop(pending, steps, (jnp.int32(0), lo, hi, n_lo))
    tied = (n_lo > topk) & (n_lo != INT_MAX)
    big = jnp.full(shape, offs, I32)

    def resolve_ties(_):
        n_gt, _ = count_gt_eq_lt(thr, big)
        need = topk - n_gt

        def idx_step(i, cut):
            cand = cut | lax.shift_left(jnp.int32(1), idx_bits - 1 - i)
            _, n_eq = count_gt_eq_lt(thr, cand)
            return jnp.where(n_eq <= need, cand, cut)

        cut = lax.fori_loop(0, idx_bits, idx_step, jnp.zeros(shape, I32))
        return jnp.where(tied, cut, big)

    any_tied = jnp.max(tied.astype(F32)) > 0.0
    cut = lax.cond(any_tied, resolve_ties, lambda _: big, 0)
    return thr, cut


def _fold_rows(x):
    acc = x[0:SUBLANES]
    for r in range(1, x.shape[0] // SUBLANES):
        acc = acc + x[r * SUBLANES:(r + 1) * SUBLANES]
    return acc


def _col_total(cnt8):
    return jnp.sum(cnt8.astype(F32), axis=0, keepdims=True).astype(I32)


def _attn_prompt_kernel(qit_ref, wit_ref, qt_ref, kiwib_ref, kb_ref, vt_ref, o_ref,
                        key_ref, khigh_ref, w4_ref, wq_ref, s0_ref, s1_ref, bias_ref, m_ref, acc_ref, *, tq, topk, idx_bits):
    j = pl.program_id(1)
    kc = tq
    n_kc = j + 1

    @pl.when((pl.program_id(0) == 0) & (j == 0))
    def _():
        w4_ref[...] = jnp.zeros(w4_ref.shape, BF16)
        wq_ref[...] = jnp.zeros(wq_ref.shape, BF16)

    for h in range(IDX_HEADS):
        w4_ref[0:IDX_DIM, h * tq:(h + 1) * tq] = qit_ref[h * IDX_DIM:(h + 1) * IDX_DIM, :]
    for h in range(B_HEADS):
        n = h // B_GROUP
        wq_ref[h, n * HEAD_DIM:(n + 1) * HEAD_DIM, :] = qt_ref[h * HEAD_DIM:(h + 1) * HEAD_DIM, :]

    wit = wit_ref[...]
    krow = lax.broadcasted_iota(I32, (kc, tq), 0)
    qpos = j * tq + lax.broadcasted_iota(I32, (kc, tq), 1)

    n_chunks = key_ref.shape[0] - 1

    def score_pair(i, _):
        for c in (2 * i, 2 * i + 1):
            rows = pl.multiple_of(jnp.minimum(c, n_chunks - 1) * kc, kc)
            s = _dot(kiwib_ref[pl.ds(rows, kc), :], w4_ref[...])
            tot = jnp.maximum(s[:, 0:tq], 0.0) * wit[0:1, :]
            for h in range(1, IDX_HEADS):
                tot = tot + jnp.maximum(s[:, h * tq:(h + 1) * tq], 0.0) * wit[h:h + 1, :]
            kpos = c * kc + krow
            key = jnp.where(kpos <= qpos, _order_key(tot, kpos, idx_bits), INT_MIN)
            key_ref[c] = key
            khigh_ref[c] = (key >> 16).astype(I16)
        return 0

    lax.fori_loop(0, (n_kc + 1) // 2, score_pair, 0)

    def count_ge(cand):
        def body(i, cnt):
            for c in (2 * i, 2 * i + 1):
                cnt = cnt + _fold_rows(jnp.where(key_ref[c] >= cand, 1, 0))
            return cnt
        return _col_total(lax.fori_loop(0, (n_kc + 1) // 2, body, jnp.zeros((SUBLANES, tq), I32)))

    def count_ge_high(cand16):
        cand16 = cand16.astype(I16)
        slab = 2 * SUBLANES

        def body(i, cnt):
            for c in (2 * i, 2 * i + 1):
                hit = jnp.where(khigh_ref[c] >= cand16, jnp.int16(1), jnp.int16(0))
                for r in range(kc // slab):
                    cnt = cnt + hit[r * slab:(r + 1) * slab]
            return cnt
        cnt = lax.fori_loop(0, (n_kc + 1) // 2, body, jnp.zeros((slab, tq), I16))
        return jnp.sum(cnt.astype(F32), axis=0, keepdims=True).astype(I32)

    def count_gt_eq_lt(thr, pos):
        def body(c, carry):
            n_gt, n_eq = carry
            key = key_ref[c]
            n_gt = n_gt + _fold_rows(jnp.where(key > thr, 1, 0))
            n_eq = n_eq + _fold_rows(jnp.where((key == thr) & (c * kc + krow < pos), 1, 0))
            return n_gt, n_eq
        z = jnp.zeros((SUBLANES, tq), I32)
        n_gt, n_eq = lax.fori_loop(0, n_kc, body, (z, z))
        return _col_total(n_gt), _col_total(n_eq)

    thr, cut = _select_threshold(count_ge, count_gt_eq_lt, (1, tq), topk, idx_bits, count_ge_high)

    m_ref[...] = jnp.full(m_ref.shape, jnp.finfo(F32).min, F32)
    acc_ref[...] = jnp.zeros(acc_ref.shape, F32)

    def key_block(c):
        return kb_ref[pl.ds(pl.multiple_of(c * kc, kc), kc), :]

    def selection_bias(c, open_bias):
        key = key_ref[c]
        sel = (key > thr) | ((key == thr) & (c * kc + krow < cut))
        bias_ref[...] = jnp.where(sel, open_bias, -jnp.inf)

    selection_bias(0, 0.0)
    kblk0 = key_block(0)
    for h in range(B_HEADS):
        s0_ref[h] = _dot(kblk0, wq_ref[h]) + bias_ref[...]

    def stage(c, src, dst, c_next, open_next):
        selection_bias(c_next, open_next)
        kblk_next = key_block(c_next)
        vtc = vt_ref[c]
        for h in range(B_HEADS):
            n = h // B_GROUP
            dst[h] = _dot(kblk_next, wq_ref[h]) + bias_ref[...]
            s = src[h]
            m_old = m_ref[h]
            m_new = jnp.maximum(m_old, jnp.max(s, axis=0, keepdims=True))
            p = jnp.exp2(s - m_new).astype(BF16)
            v_aug = jnp.concatenate([vtc[n * HEAD_DIM:(n + 1) * HEAD_DIM, :], ones_rows], axis=0)
            acc_ref[h] = jnp.exp2(m_old - m_new) * acc_ref[h] + _dot(v_aug, p)
            m_ref[h] = m_new

    ones_rows = jnp.where(lax.broadcasted_iota(I32, (DENOM_ROWS, kc), 0) == 0, 1.0, 0.0).astype(BF16)
    last = n_kc - 1

    def attend_pair(i, _):
        c0 = 2 * i
        c1 = jnp.minimum(c0 + 1, last)
        stage(c0, s0_ref, s1_ref, c1, jnp.where(c0 + 1 <= last, 0.0, -jnp.inf))
        stage(c1, s1_ref, s0_ref, jnp.minimum(c0 + 2, last), 0.0)
        return 0

    lax.fori_loop(0, (n_kc + 1) // 2, attend_pair, 0)
    out_t = jnp.concatenate([acc_ref[h, 0:HEAD_DIM, :] / acc_ref[h, HEAD_DIM:HEAD_DIM + 1, :]
                             for h in range(B_HEADS)], axis=0)
    o_ref[...] = jnp.transpose(out_t).astype(BF16)


def _attn_prompt(qit, wit, qt, kiwib, kb, vt):
    b, nc, _, tq = qt.shape
    t = nc * tq
    topk = min(TOPK_MAX, t // 4)
    idx_bits = max(1, (t - 1).bit_length())
    tr = lambda r: pl.BlockSpec((None, None, r, tq), lambda i, j: (i, j, 0, 0))
    full = lambda w: pl.BlockSpec((None, t, w), lambda i, j: (i, 0, 0))
    kernel = functools.partial(_attn_prompt_kernel, tq=tq, topk=topk, idx_bits=idx_bits)
    return pl.pallas_call(
        kernel,
        grid=(b, nc),
        in_specs=[tr(IDX_WIDTH), tr(SUBLANES), tr(B_WIDTH), full(LANES), full(KV_WIDTH),
                  pl.BlockSpec((None, nc, KV_WIDTH, tq), lambda i, j: (i, 0, 0, 0))],
        out_specs=pl.BlockSpec((None, tq, B_WIDTH), lambda i, j: (i, j, 0)),
        out_shape=jax.ShapeDtypeStruct((b, t, B_WIDTH), BF16),
        scratch_shapes=[
            pltpu.VMEM((nc + 1, tq, tq), I32),
            pltpu.VMEM((nc + 1, tq, tq), I16),
            pltpu.VMEM((LANES, IDX_HEADS * tq), BF16),
            pltpu.VMEM((B_HEADS, KV_WIDTH, tq), BF16),
            pltpu.VMEM((B_HEADS, tq, tq), F32),
            pltpu.VMEM((B_HEADS, tq, tq), F32),
            pltpu.VMEM((tq, tq), F32),
            pltpu.VMEM((B_HEADS, 1, tq), F32),
            pltpu.VMEM((B_HEADS, HEAD_DIM + DENOM_ROWS, tq), F32),
        ],
        compiler_params=pltpu.CompilerParams(
            dimension_semantics=("arbitrary", "arbitrary"), vmem_limit_bytes=VMEM_LIMIT),
        name="attn_prompt",
    )(qit, wit, qt, kiwib, kb, vt)


def _rot_tables(pos):
    r = pos.shape[0]
    inv_freq = np.float32(ROPE_THETA) ** (-np.arange(ROT_HALF, dtype=np.float32) * np.float32(2.0) / np.float32(ROT_DIM))
    ang = pos.astype(np.float32)[:, None] * inv_freq[None, :]
    cos, sin = np.cos(ang), np.sin(ang)
    rest = HEAD_DIM - ROT_DIM
    c64 = np.concatenate([cos, cos, np.ones((r, rest), np.float32)], axis=1)
    lo64 = np.concatenate([-sin, np.zeros((r, HEAD_DIM - ROT_HALF), np.float32)], axis=1)
    hi64 = np.concatenate([np.zeros((r, ROT_HALF), np.float32), sin, np.zeros((r, rest), np.float32)], axis=1)
    return jnp.asarray(np.concatenate([c64, c64, lo64, lo64, hi64, hi64], axis=1).astype(np.float32))


def _head_avg_matrix():
    return jnp.kron(jnp.eye(256 // HEAD_DIM, dtype=F32), jnp.full((HEAD_DIM, HEAD_DIM), 1.0 / HEAD_DIM, F32)).astype(BF16)


def _pad_w_in(w_in):
    return jnp.pad(w_in, ((0, 0), (0, IN_PAD - IN_WIDTH))).astype(BF16)


def _front_sample_kernel(x_ref, shift_ref, scale_ref, w_ref, rt_ref, mavg_ref, gam_ref, bet_ref, wl_ref, bl_ref,
                         a_ref, q_ref, k_ref, v_ref, qi_ref, kiwi_ref, vg_ref, *, ts, db):
    one_scale = 1.0 + scale_ref[...]
    shift = shift_ref[...]
    h = jnp.concatenate([x_ref[t * db:(t + 1) * db, :] * one_scale + shift for t in range(ts)], axis=0).astype(BF16)
    u, vg, q, k, v, qi, kiwi = _project(h, w_ref, rt_ref[...], mavg_ref[...], gam_ref[...], bet_ref[...])
    q_ref[...] = q.astype(BF16)
    k_ref[...] = k
    v_ref[...] = v
    qi_ref[...] = qi.astype(BF16)
    kiwi_ref[...] = kiwi
    vg_ref[...] = vg
    for t in range(ts):
        s = bl_ref[t:t + 1, :]
        for src in range(t + 1):
            s = s + wl_ref[t * ts + src:t * ts + src + 1, :] * vg[src * db:(src + 1) * db, :]
        a_ref[t * db:(t + 1) * db, :] = (u[t * db:(t + 1) * db, :] * s).astype(BF16)


def _front_sample(x_tm, mod_s, w_in, rt, mavg, gam, bet, wl, bl, ts, db):
    r = ts * db
    whole = lambda a: pl.BlockSpec(a.shape, lambda i: (0,) * a.ndim)
    out = lambda w, dt: jax.ShapeDtypeStruct((r, w), dt)
    outs = (out(A_WIDTH, BF16), out(B_WIDTH, BF16), out(KV_WIDTH, F32), out(KV_WIDTH, F32),
            out(IDX_WIDTH, BF16), out(LANES, F32), out(A_WIDTH, F32))
    return pl.pallas_call(
        functools.partial(_front_sample_kernel, ts=ts, db=db),
        grid=(1,),
        in_specs=[
            whole(x_tm),
            pl.BlockSpec((db, D_MODEL), lambda i: (0, 0)),
            pl.BlockSpec((db, D_MODEL), lambda i: (0, 1)),
            whole(w_in), whole(rt), whole(mavg), whole(gam), whole(bet), whole(wl), whole(bl),
        ],
        out_specs=[pl.BlockSpec((r, s.shape[1]), lambda i: (0, 0)) for s in outs],
        out_shape=outs,
        compiler_params=pltpu.CompilerParams(dimension_semantics=("arbitrary",), vmem_limit_bytes=VMEM_LIMIT),
        name="front_sample",
    )(x_tm, mod_s, mod_s, w_in, rt, mavg, gam, bet, wl, bl)


def _idx_sample_kernel(pt_ref, qi4_ref, wcol_ref, qpos_ref, kinew_ref, kidx_hbm, bias_ref,
                       ki_buf, key_ref, sem, *, ga, ts, n_pages, page, lpad, topk, idx_bits):
    i = pl.program_id(0)
    n_steps = pl.num_programs(0)
    past = n_pages * page
    rows = ga * ts
    slot = i % 2

    def page_copy(step, to_slot, g, p):
        phys = pt_ref[step * ga + g, p]
        return pltpu.make_async_copy(kidx_hbm.at[phys], ki_buf.at[to_slot, g, :, pl.ds(p * page, page)],
                                     sem.at[to_slot])

    def start_all(step, to_slot):
        for g in range(ga):
            for p in range(n_pages):
                page_copy(step, to_slot, g, p).start()

    @pl.when(i == 0)
    def _():
        start_all(0, 0)

    @pl.when(i + 1 < n_steps)
    def _():
        start_all(i + 1, 1 - slot)

    ki_buf[slot, :, :, pl.ds(past, LANES)] = kinew_ref[...]
    for g in range(ga):
        for p in range(n_pages):
            page_copy(i, slot, g, p).wait()

    kpos = lax.broadcasted_iota(I32, (ts, lpad), 1)
    for g in range(ga):
        s = _dot(qi4_ref[g], ki_buf[slot, g].astype(BF16))
        r = jnp.maximum(s, 0.0) * wcol_ref[g]
        tot = r[0:ts]
        for h in range(1, IDX_HEADS):
            tot = tot + r[h * ts:(h + 1) * ts]
        adm = (kpos <= qpos_ref[g * ts:(g + 1) * ts, :]) & (kpos < past + ts)
        key_ref[g * ts:(g + 1) * ts, :] = jnp.where(adm, _order_key(tot, kpos, idx_bits), INT_MIN)

    sub = lpad // LANES
    lane128 = lax.broadcasted_iota(I32, (rows, LANES), 1)

    def row_total(cnt):
        tot = jnp.sum(cnt.astype(F32), axis=1, keepdims=True)
        return jnp.broadcast_to(tot, cnt.shape).astype(I32)

    def count_ge(cand):
        cnt = jnp.zeros((rows, LANES), I32)
        for s_ in range(sub):
            cnt = cnt + jnp.where(key_ref[:, s_ * LANES:(s_ + 1) * LANES] >= cand, 1, 0)
        return row_total(cnt)

    def count_gt_eq_lt(thr, pos):
        n_gt = jnp.zeros((rows, LANES), I32)
        n_eq = jnp.zeros((rows, LANES), I32)
        for s_ in range(sub):
            kk = key_ref[:, s_ * LANES:(s_ + 1) * LANES]
            n_gt = n_gt + jnp.where(kk > thr, 1, 0)
            n_eq = n_eq + jnp.where((kk == thr) & (s_ * LANES + lane128 < pos), 1, 0)
        return row_total(n_gt), row_total(n_eq)

    thr, cut = _select_threshold(count_ge, count_gt_eq_lt, (rows, LANES), topk, idx_bits)
    for s_ in range(sub):
        kk = key_ref[:, s_ * LANES:(s_ + 1) * LANES]
        sel = (kk > thr) | ((kk == thr) & (s_ * LANES + lane128 < cut))
        bias = jnp.where(sel, 0.0, -jnp.inf)
        for g in range(ga):
            bias_ref[g, :, s_ * LANES:(s_ + 1) * LANES] = bias[g * ts:(g + 1) * ts]


def _idx_sample(page_table, qi4, wcol, qpos, kinew_t, kidx_t, ts, ga):
    db = qi4.shape[0]
    n_pages = page_table.shape[1]
    page = kidx_t.shape[2]
    past = n_pages * page
    lpad = past + LANES
    topk = min(TOPK_MAX, (past + ts) // 4)
    idx_bits = max(1, (lpad - 1).bit_length())
    kernel = functools.partial(_idx_sample_kernel, ga=ga, ts=ts, n_pages=n_pages, page=page, lpad=lpad,
                               topk=topk, idx_bits=idx_bits)
    return pl.pallas_call(
        kernel,
        grid_spec=pltpu.PrefetchScalarGridSpec(
            num_scalar_prefetch=1,
            grid=(db // ga,),
            in_specs=[pl.BlockSpec((ga, IDX_HEADS * ts, IDX_DIM), lambda i, pt: (i, 0, 0)),
                      pl.BlockSpec((ga, IDX_HEADS * ts, 1), lambda i, pt: (i, 0, 0)),
                      pl.BlockSpec((ga * ts, 1), lambda i, pt: (i, 0)),
                      pl.BlockSpec((ga, IDX_DIM, LANES), lambda i, pt: (i, 0, 0)),
                      pl.BlockSpec(memory_space=pl.ANY)],
            out_specs=pl.BlockSpec((ga, ts, lpad), lambda i, pt: (i, 0, 0)),
            scratch_shapes=[
                pltpu.VMEM((2, ga, IDX_DIM, lpad), F32),
                pltpu.VMEM((ga * ts, lpad), I32),
                pltpu.SemaphoreType.DMA((2,)),
            ]),
        out_shape=jax.ShapeDtypeStruct((db, ts, lpad), F32),
        compiler_params=pltpu.CompilerParams(dimension_semantics=("arbitrary",), vmem_limit_bytes=VMEM_LIMIT),
        name="idx_sample",
    )(page_table, qi4, wcol, qpos, kinew_t, kidx_t)


def _attn_sample_kernel(pt_ref, q_ref, bias_ref, knew_ref, vnew_ref, k_hbm, v_hbm, o_ref,
                        k_buf, v_buf, sem, *, ts, n_pages, page):
    b = pl.program_id(0)
    nb = pl.num_programs(0)
    past = n_pages * page
    slot = b % 2

    def page_copies(seq, to_slot, p):
        phys = pt_ref[seq, p]
        dst = pl.ds(p * page, page)
        return (pltpu.make_async_copy(k_hbm.at[phys], k_buf.at[to_slot, :, :, dst], sem.at[0, to_slot]),
                pltpu.make_async_copy(v_hbm.at[phys], v_buf.at[to_slot, :, :, dst], sem.at[1, to_slot]))

    def start_all(seq, to_slot):
        for p in range(n_pages):
            for cp in page_copies(seq, to_slot, p):
                cp.start()

    @pl.when(b == 0)
    def _():
        start_all(0, 0)

    @pl.when(b + 1 < nb)
    def _():
        start_all(b + 1, 1 - slot)

    k_buf[slot, :, :, pl.ds(past, LANES)] = knew_ref[...]
    v_buf[slot, :, :, pl.ds(past, LANES)] = vnew_ref[...]
    for p in range(n_pages):
        for cp in page_copies(b, slot, p):
            cp.wait()

    q = q_ref[...]
    bias = bias_ref[...]
    bias2 = jnp.concatenate([bias] * B_GROUP, axis=0)
    outs = [None] * B_HEADS
    for n in range(B_KV_HEADS):
        kt = k_buf[slot, n].astype(BF16)
        vt = v_buf[slot, n].astype(BF16)
        qs = jnp.concatenate([q[:, (n * B_GROUP + g) * HEAD_DIM:(n * B_GROUP + g + 1) * HEAD_DIM]
                              for g in range(B_GROUP)], axis=0)
        sc = _dot(qs, kt) + bias2
        m = jnp.max(sc, axis=1, keepdims=True)
        p_ = jnp.exp2(sc - m)
        o = _dot_nt(p_.astype(BF16), vt) / jnp.sum(p_, axis=1, keepdims=True)
        for g in range(B_GROUP):
            outs[n * B_GROUP + g] = o[g * ts:(g + 1) * ts]
    o_ref[...] = jnp.concatenate(outs, axis=1).astype(BF16)


def _attn_sample(page_table, q, bias, knew_t, vnew_t, k_t, v_t):
    db, ts, _ = q.shape
    n_pages = page_table.shape[1]
    page = k_t.shape[3]
    lpad = bias.shape[2]
    seq3 = lambda a: pl.BlockSpec((None,) + a.shape[1:], lambda b, pt: (b,) + (0,) * (a.ndim - 1))
    anyspec = pl.BlockSpec(memory_space=pl.ANY)
    kernel = functools.partial(_attn_sample_kernel, ts=ts, n_pages=n_pages, page=page)
    return pl.pallas_call(
        kernel,
        grid_spec=pltpu.PrefetchScalarGridSpec(
            num_scalar_prefetch=1,
            grid=(db,),
            in_specs=[seq3(q), seq3(bias), seq3(knew_t), seq3(vnew_t), anyspec, anyspec],
            out_specs=pl.BlockSpec((None, ts, B_WIDTH), lambda b, pt: (b, 0, 0)),
            scratch_shapes=[
                pltpu.VMEM((2, B_KV_HEADS, HEAD_DIM, lpad), F32),
                pltpu.VMEM((2, B_KV_HEADS, HEAD_DIM, lpad), F32),
                pltpu.SemaphoreType.DMA((2, 2)),
            ]),
        out_shape=jax.ShapeDtypeStruct((db, ts, B_WIDTH), BF16),
        compiler_params=pltpu.CompilerParams(dimension_semantics=("arbitrary",), vmem_limit_bytes=VMEM_LIMIT),
        name="attn_sample",
    )(page_table, q, bias, knew_t, vnew_t, k_t, v_t)


def _layer_norm_rows(y, gam, bet):
    mu = jnp.mean(y, axis=1, keepdims=True)
    d = y - mu
    var = jnp.mean(d * d, axis=1, keepdims=True)
    return d * lax.rsqrt(var + LN_EPS) * gam + bet


def _store_row_tiles(ref, val):
    r = val.shape[0]
    for c in range(ROW_CHUNKS):
        ref[pl.ds(c, r, stride=ROW_CHUNKS), :] = val[:, c * LANES:(c + 1) * LANES]


def _load_row_tiles(ref, r):
    return jnp.concatenate([ref[pl.ds(c, r, stride=ROW_CHUNKS), :] for c in range(ROW_CHUNKS)], axis=1)


ROUTER_ROWS = SUBLANES + N_EXPERTS


def _route(logits_t):
    r = logits_t.shape[1]
    far = float(LANES)

    def softmax_rows(x):
        e = jnp.exp(x - jnp.max(x, axis=0, keepdims=True))
        return e / jnp.sum(e, axis=0, keepdims=True)

    def first_max(p):
        rows = lax.broadcasted_iota(I32, p.shape, 0).astype(F32)
        best = jnp.max(p, axis=0, keepdims=True)
        return best, jnp.min(jnp.where(p == best, rows, far), axis=0, keepdims=True), rows

    g_w, g_sel, _ = first_max(softmax_rows(logits_t[0:N_GROUPS]))
    el = jnp.zeros((EXPERTS_PER_GROUP, r), F32)
    for g in range(N_GROUPS):
        lo = SUBLANES + g * EXPERTS_PER_GROUP
        el = jnp.where(g_sel == float(g), logits_t[lo:lo + EXPERTS_PER_GROUP], el)
    ep = softmax_rows(el)
    p1, i1, rows = first_max(ep)
    p2, i2, _ = first_max(jnp.where(rows == i1, -1.0, ep))
    denom = p1 + p2
    base = g_sel * float(EXPERTS_PER_GROUP)
    return jnp.concatenate([base + i1, base + i2, g_w * p1 / denom, g_w * p2 / denom,
                            jnp.zeros((SUBLANES - 4, r), F32)], axis=0)


def _split_bf16(x):
    hi = x.astype(BF16)
    return hi, (x - hi.astype(F32)).astype(BF16)


def _mid_kernel(x_ref, a_ref, b_ref, g1_ref, sh2_ref, sc2_ref, wo_ref, gam_ref, bet_ref, wr_ref,
                x1_ref, h2_ref, route_ref, *, alpha):
    mixed = _dot(a_ref[...], wo_ref[0:A_WIDTH, :]) + _dot(b_ref[...], wo_ref[A_WIDTH:A_WIDTH + B_WIDTH, :])
    x1 = _layer_norm_rows(alpha * x_ref[...] + g1_ref[...] * mixed, gam_ref[...], bet_ref[...])
    x1_ref[...] = x1
    h2 = x1 * (1.0 + sc2_ref[...]) + sh2_ref[...]
    _store_row_tiles(h2_ref, h2)
    w_hi, w_lo = _split_bf16(wr_ref[...])
    h_hi, h_lo = _split_bf16(h2)
    logits_t = _dot_nt(w_hi, h_hi) + _dot_nt(w_hi, h_lo) + _dot_nt(w_lo, h_hi)
    route_ref[...] = _route(logits_t)


def _mod_spec(mod3, comp, tiles_per_group):
    rm = mod3.shape[1]
    return pl.BlockSpec((None, rm, D_MODEL), lambda i: (i // tiles_per_group, 0, comp))


def _mid(x, a, b, mod3, tiles_per_group, tile, w_out, gam, bet, w_router, alpha):
    n = x.shape[0]
    rows = lambda w: pl.BlockSpec((tile, w), lambda i: (i, 0))
    whole = lambda arr: pl.BlockSpec(arr.shape, lambda i: (0,) * arr.ndim)
    return pl.pallas_call(
        functools.partial(_mid_kernel, alpha=alpha),
        grid=(n // tile,),
        in_specs=[rows(D_MODEL), rows(A_WIDTH), rows(B_WIDTH),
                  _mod_spec(mod3, 2, tiles_per_group), _mod_spec(mod3, 3, tiles_per_group),
                  _mod_spec(mod3, 4, tiles_per_group),
                  whole(w_out), whole(gam), whole(bet), whole(w_router)],
        out_specs=[rows(D_MODEL), pl.BlockSpec((tile * ROW_CHUNKS, LANES), lambda i: (i, 0)),
                   pl.BlockSpec((SUBLANES, tile), lambda i: (0, i))],
        out_shape=(jax.ShapeDtypeStruct((n, D_MODEL), F32),
                   jax.ShapeDtypeStruct((n * ROW_CHUNKS, LANES), F32),
                   jax.ShapeDtypeStruct((SUBLANES, n), F32)),
        compiler_params=pltpu.CompilerParams(dimension_semantics=("arbitrary",), vmem_limit_bytes=VMEM_LIMIT),
        name="mid",
    )(x, a, b, mod3, mod3, mod3, w_out, gam, bet, w_router)


META_CNT, META_START, META_END, META_NACT = 0, 1, 2, 3


def _plan_kernel(route_ref, pos_ref, meta_ref, te_ref, carry_ref, starts_ref, *, nt, tm):
    phase = pl.program_id(0)
    i = pl.program_id(1)

    @pl.when(i == 0)
    def _():
        carry_ref[...] = jnp.zeros(carry_ref.shape, F32)

    eid = route_ref[0:2, :].astype(I32)
    e_iota = lax.broadcasted_iota(I32, (N_EXPERTS, nt), 0)
    hit0 = eid[0:1, :] == e_iota
    hit1 = eid[1:2, :] == e_iota
    onehot = jnp.where(hit0 | hit1, 1.0, 0.0)

    @pl.when(phase == 0)
    def _():
        pos_ref[...] = jnp.zeros(pos_ref.shape, I32)
        carry_ref[...] = carry_ref[...] + jnp.sum(onehot, axis=1, keepdims=True)

        @pl.when(i == pl.num_programs(1) - 1)
        def _():
            cnt = carry_ref[...]
            tiles = jnp.floor((cnt + float(tm - 1)) * (1.0 / tm))
            r = lax.broadcasted_iota(I32, (N_EXPERTS, N_EXPERTS), 0)
            c = lax.broadcasted_iota(I32, (N_EXPERTS, N_EXPERTS), 1)
            lower = jnp.where(r > c, 1.0, 0.0).astype(BF16)
            tiles_before = _dot(lower, jnp.broadcast_to(tiles, (N_EXPERTS, LANES)).astype(BF16))[:, 0:1]
            tiles_end = tiles_before + tiles
            starts_ref[...] = tiles_before * float(tm)

            diag = (lax.broadcasted_iota(I32, (N_EXPERTS, LANES), 0)
                    == lax.broadcasted_iota(I32, (N_EXPERTS, LANES), 1))

            def as_row(col):
                return jnp.sum(jnp.where(diag, jnp.broadcast_to(col, (N_EXPERTS, LANES)), 0.0),
                               axis=0, keepdims=True)

            nact = jnp.broadcast_to(jnp.max(tiles_end, axis=0, keepdims=True), (1, LANES))
            meta_ref[...] = jnp.concatenate(
                [as_row(cnt), as_row(tiles_before * float(tm)), as_row(tiles_end * float(tm)), nact,
                 jnp.zeros((SUBLANES - 4, LANES), F32)], axis=0).astype(I32)
            tile_i = lax.broadcasted_iota(I32, (N_EXPERTS, te_ref.shape[1]), 1).astype(F32)
            owner = jnp.sum(jnp.where(tiles_end <= tile_i, 1.0, 0.0), axis=0, keepdims=True)
            te_ref[...] = jnp.minimum(owner, float(N_EXPERTS - 1)).astype(I32)

    @pl.when(phase == 1)
    def _():
        upper = (lax.broadcasted_iota(I32, (nt, nt), 0) <= lax.broadcasted_iota(I32, (nt, nt), 1))
        incl = _dot(onehot.astype(BF16), jnp.where(upper, 1.0, 0.0).astype(BF16))
        slot = starts_ref[...] + carry_ref[...] + incl - 1.0
        pos_ref[0:1, :] = jnp.sum(jnp.where(hit0, slot, 0.0), axis=0, keepdims=True).astype(I32)
        pos_ref[1:2, :] = jnp.sum(jnp.where(hit1, slot, 0.0), axis=0, keepdims=True).astype(I32)
        carry_ref[...] = carry_ref[...] + jnp.sum(onehot, axis=1, keepdims=True)


def _plan(route_t, nt, tm):
    n = route_t.shape[1]
    n_tiles = (2 * n) // tm + N_EXPERTS
    te_width = pl.cdiv(n_tiles, LANES) * LANES
    pos, meta, te = pl.pallas_call(
        functools.partial(_plan_kernel, nt=nt, tm=tm),
        grid=(2, n // nt),
        in_specs=[pl.BlockSpec((SUBLANES, nt), lambda p, i: (0, i))],
        out_specs=[pl.BlockSpec((2, nt), lambda p, i: (0, i * p)),
                   pl.BlockSpec((SUBLANES, LANES), lambda p, i: (0, 0)),
                   pl.BlockSpec((1, te_width), lambda p, i: (0, 0))],
        out_shape=(jax.ShapeDtypeStruct((2, n), I32), jax.ShapeDtypeStruct((SUBLANES, LANES), I32),
                   jax.ShapeDtypeStruct((1, te_width), I32)),
        scratch_shapes=[pltpu.VMEM((N_EXPERTS, 1), F32), pltpu.VMEM((N_EXPERTS, 1), F32)],
        compiler_params=pltpu.CompilerParams(dimension_semantics=("arbitrary", "arbitrary")),
        name="plan",
    )(route_t)
    return pos, meta, te, n_tiles


def _row_copy(src, src_row, dst, dst_row, sem):
    return pltpu.make_async_copy(src.at[pl.ds(src_row * ROW_CHUNKS, ROW_CHUNKS)],
                                 dst.at[pl.ds(dst_row * ROW_CHUNKS, ROW_CHUNKS)], sem)


def _dispatch_kernel(meta_ref, pos_ref, hp_ref, hs_ref, xs_out, zero_ref, sem, *, tt, blocks_p, tm, n_tiles):
    i = pl.program_id(0)

    def scatter(h_ref):
        def issue(j, _):
            for k in range(2):
                _row_copy(h_ref, j, xs_out, pos_ref[k, j], sem.at[k]).start()
            return 0

        def drain(j, _):
            for k in range(2):
                _row_copy(h_ref, j, xs_out, pos_ref[k, j], sem.at[k]).wait()
            return 0

        lax.fori_loop(0, tt, issue, 0, unroll=ROW_DMA_UNROLL)
        lax.fori_loop(0, tt, drain, 0, unroll=ROW_DMA_UNROLL)

    @pl.when(i < blocks_p)
    def _():
        scatter(hp_ref)

    @pl.when(i >= blocks_p)
    def _():
        scatter(hs_ref)

    @pl.when(i == pl.num_programs(0) - 1)
    def _():
        zero_ref[...] = jnp.zeros(zero_ref.shape, F32)

        def zero_rows(first_row, n_rows):
            return pltpu.make_async_copy(zero_ref.at[pl.ds(0, n_rows * ROW_CHUNKS)],
                                         xs_out.at[pl.ds(first_row * ROW_CHUNKS, n_rows * ROW_CHUNKS)], sem.at[0])

        def start_row(r, c):
            zero_rows(r, 1).start()
            return c

        def wait_row(r, c):
            zero_rows(r, 1).wait()
            return c

        def per_expert(row_fn):
            def body(e, c):
                lo = meta_ref[META_START, e] + meta_ref[META_CNT, e]
                return lax.fori_loop(lo, meta_ref[META_END, e], row_fn, c)
            return body

        def start_tile(t, c):
            zero_rows(t * tm, tm).start()
            return c

        def wait_tile(t, c):
            zero_rows(t * tm, tm).wait()
            return c

        nact = meta_ref[META_NACT, 0]
        lax.fori_loop(0, N_EXPERTS, per_expert(start_row), 0)
        lax.fori_loop(nact, n_tiles, start_tile, 0)
        lax.fori_loop(0, N_EXPERTS, per_expert(wait_row), 0)
        lax.fori_loop(nact, n_tiles, wait_tile, 0)


def _dispatch(meta, pos, h2_p, h2_s, tt, tm, n_tiles):
    blocks_p = h2_p.shape[0] // (tt * ROW_CHUNKS)
    blocks_s = h2_s.shape[0] // (tt * ROW_CHUNKS)
    return pl.pallas_call(
        functools.partial(_dispatch_kernel, tt=tt, blocks_p=blocks_p, tm=tm, n_tiles=n_tiles),
        grid_spec=pltpu.PrefetchScalarGridSpec(
            num_scalar_prefetch=1,
            grid=(blocks_p + blocks_s,),
            in_specs=[pl.BlockSpec((2, tt), lambda i, m: (0, i), memory_space=pltpu.SMEM),
                      pl.BlockSpec((tt * ROW_CHUNKS, LANES), lambda i, m: (jnp.minimum(i, blocks_p - 1), 0)),
                      pl.BlockSpec((tt * ROW_CHUNKS, LANES), lambda i, m: (jnp.maximum(i - blocks_p, 0), 0))],
            out_specs=pl.BlockSpec(memory_space=pl.ANY),
            scratch_shapes=[pltpu.VMEM((tm * ROW_CHUNKS, LANES), F32), pltpu.SemaphoreType.DMA((2,))]),
        out_shape=jax.ShapeDtypeStruct((n_tiles * tm * ROW_CHUNKS, LANES), F32),
        compiler_params=pltpu.CompilerParams(dimension_semantics=("arbitrary",)),
        name="dispatch",
    )(meta, pos, h2_p, h2_s)


def _experts_kernel(te_ref, meta_ref, xs_ref, wg_ref, wu_ref, wd_ref, ys_ref, wg_b, wu_b, wd_b, *, tm):
    i = pl.program_id(0)
    active = i < meta_ref[META_NACT, 0]
    fresh = (i == 0) | (te_ref[0, i] != te_ref[0, jnp.maximum(i - 1, 0)])

    @pl.when(active & fresh)
    def _():
        wg_b[...] = wg_ref[...].astype(BF16)
        wu_b[...] = wu_ref[...].astype(BF16)
        wd_b[...] = wd_ref[...].astype(BF16)

    @pl.when(active)
    def _():
        x = _load_row_tiles(xs_ref, tm).astype(BF16)
        hid = (jax.nn.silu(_dot(x, wg_b[...])) * _dot(x, wu_b[...])).astype(BF16)
        _store_row_tiles(ys_ref, _dot(hid, wd_b[...]))

    @pl.when(jnp.logical_not(active))
    def _():
        ys_ref[...] = jnp.zeros(ys_ref.shape, F32)


def _experts(te, meta, xs, w_gate, w_up, w_down, tm, n_tiles):
    last_active = lambda i, m: jnp.minimum(i, m[META_NACT, 0] - 1)
    tile = pl.BlockSpec((tm * ROW_CHUNKS, LANES), lambda i, te, m: (last_active(i, m), 0))
    w_in_spec = pl.BlockSpec((None, D_MODEL, D_EXPERT), lambda i, te, m: (te[0, last_active(i, m)], 0, 0))
    w_out_spec = pl.BlockSpec((None, D_EXPERT, D_MODEL), lambda i, te, m: (te[0, last_active(i, m)], 0, 0))
    return pl.pallas_call(
        functools.partial(_experts_kernel, tm=tm),
        grid_spec=pltpu.PrefetchScalarGridSpec(
            num_scalar_prefetch=2,
            grid=(n_tiles,),
            in_specs=[tile, w_in_spec, w_in_spec, w_out_spec],
            out_specs=pl.BlockSpec((tm * ROW_CHUNKS, LANES), lambda i, te, m: (i, 0)),
            scratch_shapes=[pltpu.VMEM((D_MODEL, D_EXPERT), BF16), pltpu.VMEM((D_MODEL, D_EXPERT), BF16),
                            pltpu.VMEM((D_EXPERT, D_MODEL), BF16)]),
        out_shape=jax.ShapeDtypeStruct(xs.shape, F32),
        compiler_params=pltpu.CompilerParams(dimension_semantics=("arbitrary",), vmem_limit_bytes=VMEM_LIMIT),
        name="experts",
    )(te, meta, xs, w_gate, w_up, w_down)


def _combine_kernel(pos_ref, route_ref, x1_ref, g2_ref, gam_ref, bet_ref, ys_hbm, y_ref, buf0, buf1, sem,
                    *, tt, alpha):
    bufs = (buf0, buf1)

    def issue(j, _):
        for k in range(2):
            _row_copy(ys_hbm, pos_ref[k, j], bufs[k], j, sem.at[k]).start()
        return 0

    def drain(j, _):
        for k in range(2):
            _row_copy(ys_hbm, pos_ref[k, j], bufs[k], j, sem.at[k]).wait()
        return 0

    lax.fori_loop(0, tt, issue, 0, unroll=ROW_DMA_UNROLL)
    lax.fori_loop(0, tt, drain, 0, unroll=ROW_DMA_UNROLL)
    route = route_ref[...]
    eye = lax.broadcasted_iota(I32, (tt, tt), 0) == lax.broadcasted_iota(I32, (tt, tt), 1)
    as_col = lambda row: jnp.sum(jnp.where(eye, jnp.broadcast_to(row, (tt, tt)), 0.0), axis=1, keepdims=True)
    f = as_col(route[2:3, :]) * _load_row_tiles(buf0, tt) + as_col(route[3:4, :]) * _load_row_tiles(buf1, tt)
    y_ref[...] = _layer_norm_rows(alpha * x1_ref[...] + g2_ref[...] * f, gam_ref[...], bet_ref[...])


def _combine(pos, route_t, col_block0, x1, mod3, tiles_per_group, tt, gam, bet, ys, alpha):
    n = x1.shape[0]
    rows = lambda w: pl.BlockSpec((tt, w), lambda i: (i, 0))
    whole = lambda arr: pl.BlockSpec(arr.shape, lambda i: (0,) * arr.ndim)
    return pl.pallas_call(
        functools.partial(_combine_kernel, tt=tt, alpha=alpha),
        grid=(n // tt,),
        in_specs=[pl.BlockSpec((2, tt), lambda i: (0, i + col_block0), memory_space=pltpu.SMEM),
                  pl.BlockSpec((SUBLANES, tt), lambda i: (0, i + col_block0)),
                  rows(D_MODEL), _mod_spec(mod3, 5, tiles_per_group), whole(gam), whole(bet),
                  pl.BlockSpec(memory_space=pl.ANY)],
        out_specs=rows(D_MODEL),
        out_shape=jax.ShapeDtypeStruct((n, D_MODEL), F32),
        scratch_shapes=[pltpu.VMEM((tt * ROW_CHUNKS, LANES), F32), pltpu.VMEM((tt * ROW_CHUNKS, LANES), F32),
                        pltpu.SemaphoreType.DMA((2,))],
        compiler_params=pltpu.CompilerParams(dimension_semantics=("arbitrary",), vmem_limit_bytes=VMEM_LIMIT),
        name="combine",
    )(pos, route_t, x1, mod3, gam, bet, ys)


def kernel(x_prompt, x_sample, cache_k, cache_v, cache_kidx, page_table, c_prompt, c_sample, w_ada, b_ada, w_in,
           a_ln_g, a_ln_b, w_spatial, b_spatial, w_out, ln1_g, ln1_b, w_group_router, w_expert_router, w_gate,
           w_up, w_down, ln2_g, ln2_b):
    depth = w_ada.shape[0]
    assert depth == 1, "one trunk layer"
    alpha = (2.0 * depth) ** 0.25
    bsz, seq, d = x_prompt.shape
    db, ts, _ = x_sample.shape
    n_pages = page_table.shape[1]
    page = cache_k.shape[2]
    past = n_pages * page
    tile_p = min(PROMPT_TILE, seq)
    assert d == D_MODEL and seq % tile_p == 0 and tile_p % CHUNK == 0 and ts <= SUBLANES and page == LANES
    l = 0

    n_c = bsz + db
    n_c_pad = pl.cdiv(n_c, SUBLANES) * SUBLANES
    c_all = jnp.pad(jnp.concatenate([c_prompt, c_sample], axis=0), ((0, n_c_pad - n_c), (0, 0)))
    mod = _ada(c_all, w_ada[l], b_ada[l][None, :])
    mod_p = mod[:bsz]
    mod_s = mod[bsz:n_c]

    w_in_b = _pad_w_in(w_in[l])
    w_out_b = w_out[l].astype(BF16)
    mavg = _head_avg_matrix()
    gam_a = a_ln_g[l].reshape(1, A_WIDTH)
    bet_a = a_ln_b[l].reshape(1, A_WIDTH)
    w_router = jnp.concatenate([w_group_router[l].T, jnp.zeros((SUBLANES - N_GROUPS, d), F32),
                                w_expert_router[l].reshape(d, N_EXPERTS).T], axis=0)
    ln1 = (ln1_g[l][None, :], ln1_b[l][None, :])
    ln2 = (ln2_g[l][None, :], ln2_b[l][None, :])

    wsp = w_spatial[l].reshape(A_HEADS // 2, 2, CHUNK, CHUNK).transpose(0, 2, 1, 3).reshape(A_HEADS // 2, CHUNK, 2 * CHUNK)
    bsp = jnp.repeat(b_spatial[l].T, HEAD_DIM, axis=1)
    a_p, kt_p, vt32_p, kit_p, kb_p, kiwib_p, vt_p, qt_p, qit_p, wit_p = _front_prompt(
        x_prompt, mod_p.reshape(bsz, 6, d), w_in_b, _rot_tables(np.arange(seq)), mavg, gam_a, bet_a, wsp, bsp, tile_p)
    b_p = _attn_prompt(qit_p, wit_p, qt_p, kiwib_p, kb_p, vt_p)
    n_p = bsz * seq
    tile_m = min(MID_TILE, seq)
    assert seq % tile_m == 0
    x1_p, h2_p, route_p = _mid(x_prompt.reshape(n_p, d), a_p.reshape(n_p, A_WIDTH), b_p.reshape(n_p, B_WIDTH),
                               mod_p.reshape(bsz, 1, 6 * d), seq // tile_m, tile_m, w_out_b, *ln1, w_router, alpha)

    r_s = ts * db
    x_tm = x_sample.transpose(1, 0, 2).reshape(r_s, d)
    rt_s = _rot_tables(np.repeat(past + np.arange(ts), db))
    w_small = w_spatial[l][:, :ts, :ts]
    wl = jnp.repeat(w_small.transpose(1, 2, 0).reshape(ts * ts, A_HEADS), HEAD_DIM, axis=1)
    bl = jnp.repeat(b_spatial[l][:, :ts].T, HEAD_DIM, axis=1)
    a_s, q_s, k_s, v_s, qi_s, kiwi_s, vg_s = _front_sample(x_tm, mod_s, w_in_b, rt_s, mavg, gam_a, bet_a, wl, bl, ts, db)

    def seq_major(a):
        return a.reshape(ts, db, a.shape[-1]).transpose(1, 0, 2)

    def new_t(a, heads):
        a = a.reshape(ts, db, heads, HEAD_DIM).transpose(1, 2, 3, 0)
        return jnp.pad(a, ((0, 0), (0, 0), (0, 0), (0, LANES - ts)))

    ga = min(IDX_SEQS, db)
    assert db % ga == 0 and (ga * ts) % SUBLANES == 0
    qi4 = qi_s.reshape(ts, db, IDX_HEADS, IDX_DIM).transpose(1, 2, 0, 3).reshape(db, IDX_HEADS * ts, IDX_DIM)
    wcol = kiwi_s[:, WI_LANE:WI_LANE + IDX_HEADS].reshape(ts, db, IDX_HEADS).transpose(1, 2, 0).reshape(db, IDX_HEADS * ts, 1)
    qpos = jnp.tile(past + jnp.arange(ts, dtype=I32), db).reshape(db * ts, 1)
    kinew_t = new_t(kiwi_s[:, :IDX_DIM], 1)[:, 0]
    kidx_t = jnp.transpose(cache_kidx[l], (0, 2, 1))
    k_t = jnp.transpose(cache_k[l], (0, 2, 3, 1))
    v_t = jnp.transpose(cache_v[l], (0, 2, 3, 1))
    bias_s = _idx_sample(page_table, qi4, wcol, qpos, kinew_t, kidx_t, ts, ga)
    b_s = _attn_sample(page_table, seq_major(q_s), bias_s, new_t(k_s, B_KV_HEADS), new_t(v_s, B_KV_HEADS), k_t, v_t)
    b_s_tm = b_s.transpose(1, 0, 2).reshape(r_s, B_WIDTH)
    x1_s, h2_s, route_s = _mid(x_tm, a_s, b_s_tm, mod_s.reshape(1, db, 6 * d), ts, db, w_out_b, *ln1, w_router, alpha)

    n_all = n_p + r_s
    tok_p = min(TOKEN_TILE, seq)
    rank_tile = min(RANK_TILE, n_all)
    assert n_all % rank_tile == 0 and n_p % tok_p == 0 and r_s % tok_p == 0 and n_p % db == 0
    route_all = jnp.concatenate([route_p, route_s], axis=1)
    pos, meta, te, n_tiles = _plan(route_all, rank_tile, MOE_TILE)
    xs = _dispatch(meta, pos, h2_p, h2_s, tok_p, MOE_TILE, n_tiles)
    ys = _experts(te, meta, xs, w_gate[l], w_up[l], w_down[l], MOE_TILE, n_tiles)
    y_p = _combine(pos, route_all, 0, x1_p, mod_p.reshape(bsz, 1, 6 * d), seq // tok_p, tok_p, *ln2, ys, alpha)
    y_s_tm = _combine(pos, route_all, n_p // db, x1_s, mod_s.reshape(1, db, 6 * d), ts, db, *ln2, ys, alpha)
    y_s = y_s_tm.reshape(ts, db, d).transpose(1, 0, 2)

    kv5 = lambda a, n, t: a.reshape(1, n, t, B_KV_HEADS, HEAD_DIM)
    kv5_t = lambda a: a.reshape(1, bsz, B_KV_HEADS, HEAD_DIM, seq).transpose(0, 1, 4, 2, 3)
    return (y_p.reshape(bsz, seq, d), y_s,
            kv5_t(kt_p), kv5_t(vt32_p), kit_p.transpose(0, 2, 1)[None],
            kv5(seq_major(k_s), db, ts), kv5(seq_major(v_s), db, ts), seq_major(kiwi_s)[..., :IDX_DIM][None],
            seq_major(vg_s).reshape(1, db, ts, A_HEADS, HEAD_DIM))
```

```python
import functools

import jax
import jax.numpy as jnp
import numpy as np
from jax import lax
from jax.experimental import pallas as pl
from jax.experimental.pallas import tpu as pltpu

F32 = jnp.float32
BF16 = jnp.bfloat16
I32 = jnp.int32

D_MODEL = 1024
HEAD_DIM = 64
A_HEADS = 8
A_WIDTH = A_HEADS * HEAD_DIM
CHUNK = 128
B_HEADS = 8
B_KV_HEADS = 4
B_GROUP = B_HEADS // B_KV_HEADS
B_WIDTH = B_HEADS * HEAD_DIM
KV_WIDTH = B_KV_HEADS * HEAD_DIM
IDX_HEADS = 4
IDX_DIM = 64
IDX_WIDTH = IDX_HEADS * IDX_DIM
TOPK_MAX = 256
ROPE_THETA = 500000.0
ROT_DIM = HEAD_DIM // 4
ROT_HALF = ROT_DIM // 2
ATTN_SCALE = HEAD_DIM ** -0.5
LOG2_E = 1.4426950408889634
Q_SCALE = ATTN_SCALE * LOG2_E
DENOM_ROWS = 16
N_GROUPS = 4
EXPERTS_PER_GROUP = 8
N_EXPERTS = N_GROUPS * EXPERTS_PER_GROUP
D_EXPERT = 512
LN_EPS = 1e-5

LANES = 128
SUBLANES = 8
ROW_CHUNKS = D_MODEL // LANES

C_AU, C_AV, C_Q, C_K, C_V, C_QI, C_KI = 0, 512, 1024, 1536, 1792, 2048, 2304
IN_WIDTH = C_KI + IDX_DIM + IDX_HEADS
IN_PAD = 2432
WI_LANE = IDX_DIM

INT_MIN = -(2 ** 31)
INT_MAX = 2 ** 31 - 1
VMEM_LIMIT = 48 * 1024 * 1024

PROMPT_TILE = 256
MOE_TILE = 256
TOKEN_TILE = 256
RANK_TILE = 512
IDX_SEQS = 16
STEPS_PER_CHECK = 4
ROW_DMA_UNROLL = 8
MID_TILE = 512


def _dot(a, b):
    return jnp.dot(a, b, preferred_element_type=F32)


def _dot_nt(a, b):
    return lax.dot_general(a, b, (((1,), (1,)), ((), ())), preferred_element_type=F32)


def _ada_kernel(c_ref, w_ref, b_ref, o_ref):
    s = jax.nn.silu(c_ref[...]).astype(BF16)
    o_ref[...] = _dot(s, w_ref[...].astype(BF16)) + b_ref[...]


def _ada(c_all, w_ada, b_ada):
    rows = c_all.shape[0]
    n_out = w_ada.shape[1]
    tn = 1024
    return pl.pallas_call(
        _ada_kernel,
        grid=(n_out // tn,),
        in_specs=[
            pl.BlockSpec((rows, D_MODEL), lambda j: (0, 0)),
            pl.BlockSpec((D_MODEL, tn), lambda j: (0, j)),
            pl.BlockSpec((1, tn), lambda j: (0, j)),
        ],
        out_specs=pl.BlockSpec((rows, tn), lambda j: (0, j)),
        out_shape=jax.ShapeDtypeStruct((rows, n_out), F32),
        compiler_params=pltpu.CompilerParams(
            dimension_semantics=("arbitrary",), vmem_limit_bytes=VMEM_LIMIT),
        name="ada",
    )(c_all, w_ada, b_ada)


def _rotate(x, rt, blk):
    c = rt[:, 0:LANES]
    s_lo = rt[:, LANES:2 * LANES]
    s_hi = rt[:, 2 * LANES:3 * LANES]
    if blk == 1:
        head = lax.broadcasted_iota(I32, c.shape, 1) < IDX_DIM
        c = jnp.where(head, c, 1.0)
        s_lo = jnp.where(head, s_lo, 0.0)
        s_hi = jnp.where(head, s_hi, 0.0)
    outs = []
    for j in range(x.shape[1] // LANES):
        xb = x[:, j * LANES:(j + 1) * LANES]
        up = pltpu.roll(xb, LANES - ROT_HALF, 1)
        dn = pltpu.roll(xb, ROT_HALF, 1)
        outs.append(xb * c + up * s_lo + dn * s_hi)
    return outs[0] if len(outs) == 1 else jnp.concatenate(outs, axis=1)


def _head_ln(g, mavg, gam, bet):
    def seg_mean(x):
        parts = [_dot(x[:, j * 256:(j + 1) * 256].astype(BF16), mavg) for j in range(A_WIDTH // 256)]
        return jnp.concatenate(parts, axis=1)
    mu = seg_mean(g)
    d = g - mu
    var = seg_mean(d * d)
    return d * lax.rsqrt(var + LN_EPS) * gam + bet


def _project(h, w_ref, rt, mavg, gam, bet):
    u = jax.nn.gelu(_dot(h, w_ref[:, C_AU:C_AV]))
    vg = _head_ln(jax.nn.gelu(_dot(h, w_ref[:, C_AV:C_Q])), mavg, gam, bet)
    q = _rotate(_dot(h, w_ref[:, C_Q:C_K]), rt, 0) * Q_SCALE
    k = _rotate(_dot(h, w_ref[:, C_K:C_V]), rt, 0)
    v = _dot(h, w_ref[:, C_V:C_QI])
    qi = _rotate(_dot(h, w_ref[:, C_QI:C_KI]), rt, 0)
    kiwi = _rotate(_dot(h, w_ref[:, C_KI:IN_PAD]), rt, 1)
    return u, vg, q, k, v, qi, kiwi


def _front_prompt_kernel(x_ref, mod_ref, w_ref, rt_ref, mavg_ref, gam_ref, bet_ref, wsp_ref, bsp_ref,
                         a_ref, kt_ref, vt_ref, kit_ref, kb_ref, kiwib_ref, vtb_ref, qt_ref, qit_ref, wit_ref):
    shift = mod_ref[0:1, :]
    scale = mod_ref[1:2, :]
    h = (x_ref[...] * (1.0 + scale) + shift).astype(BF16)
    u, vg, q, k, v, qi, kiwi = _project(h, w_ref, rt_ref[...], mavg_ref[...], gam_ref[...], bet_ref[...])
    v_t = jnp.transpose(v)
    kiwi_t = jnp.transpose(kiwi)
    kt_ref[...] = jnp.transpose(k)
    vt_ref[...] = v_t
    kit_ref[...] = kiwi_t[0:IDX_DIM, :]
    kb_ref[...] = k.astype(BF16)
    kiwib_ref[...] = kiwi.astype(BF16)
    vtb_ref[...] = v_t.astype(BF16)
    qt_ref[...] = jnp.transpose(q).astype(BF16)
    qit_ref[...] = jnp.transpose(qi).astype(BF16)
    wit_ref[...] = kiwi_t[WI_LANE:WI_LANE + SUBLANES, :]

    rows = lax.broadcasted_iota(I32, (CHUNK, 2 * CHUNK), 0)
    cols = lax.broadcasted_iota(I32, (CHUNK, 2 * CHUNK), 1) % CHUNK
    causal = cols <= rows
    lane = lax.broadcasted_iota(I32, (CHUNK, LANES), 1)
    tt = x_ref.shape[0]
    for cidx in range(tt // CHUNK):
        rs = slice(cidx * CHUNK, (cidx + 1) * CHUNK)
        blocks = []
        for p in range(A_HEADS // 2):
            wcat = jnp.where(causal, wsp_ref[p], 0.0).astype(BF16)
            vb = vg[rs, p * LANES:(p + 1) * LANES]
            rhs = jnp.concatenate([jnp.where(lane < HEAD_DIM, vb, 0.0),
                                   jnp.where(lane >= HEAD_DIM, vb, 0.0)], axis=0).astype(BF16)
            blocks.append(_dot(wcat, rhs))
        s = jnp.concatenate(blocks, axis=1) + bsp_ref[...]
        a_ref[rs, :] = (u[rs, :] * s).astype(BF16)


def _front_prompt(x, mod, w_in, rt, mavg, gam, bet, wsp, bsp, tt):
    b, t, _ = x.shape
    nc = t // tt
    tok = lambda w: pl.BlockSpec((None, tt, w), lambda i, j: (i, j, 0))
    tr = lambda r: pl.BlockSpec((None, None, r, tt), lambda i, j: (i, j, 0, 0))
    pos_minor = lambda r: pl.BlockSpec((None, r, tt), lambda i, j: (i, 0, j))
    const2 = lambda a: pl.BlockSpec(a.shape, lambda i, j: (0,) * a.ndim)
    out_shapes = (
        jax.ShapeDtypeStruct((b, t, A_WIDTH), BF16),
        jax.ShapeDtypeStruct((b, KV_WIDTH, t), F32),
        jax.ShapeDtypeStruct((b, KV_WIDTH, t), F32),
        jax.ShapeDtypeStruct((b, IDX_DIM, t), F32),
        jax.ShapeDtypeStruct((b, t, KV_WIDTH), BF16),
        jax.ShapeDtypeStruct((b, t, LANES), BF16),
        jax.ShapeDtypeStruct((b, nc, KV_WIDTH, tt), BF16),
        jax.ShapeDtypeStruct((b, nc, B_WIDTH, tt), BF16),
        jax.ShapeDtypeStruct((b, nc, IDX_WIDTH, tt), BF16),
        jax.ShapeDtypeStruct((b, nc, SUBLANES, tt), F32),
    )
    return pl.pallas_call(
        _front_prompt_kernel,
        grid=(b, nc),
        in_specs=[
            tok(D_MODEL),
            pl.BlockSpec((None, 6, D_MODEL), lambda i, j: (i, 0, 0)),
            const2(w_in),
            pl.BlockSpec((tt, 3 * LANES), lambda i, j: (j, 0)),
            const2(mavg), const2(gam), const2(bet), const2(wsp), const2(bsp),
        ],
        out_specs=[tok(A_WIDTH), pos_minor(KV_WIDTH), pos_minor(KV_WIDTH), pos_minor(IDX_DIM), tok(KV_WIDTH),
                   tok(LANES), tr(KV_WIDTH), tr(B_WIDTH), tr(IDX_WIDTH), tr(SUBLANES)],
        out_shape=out_shapes,
        compiler_params=pltpu.CompilerParams(
            dimension_semantics=("arbitrary", "arbitrary"), vmem_limit_bytes=VMEM_LIMIT),
        name="front_prompt",
    )(x, mod, w_in, rt, mavg, gam, bet, wsp, bsp)


def _order_key(score, kpos, idx_bits):
    offs = 1 << idx_bits
    bits = pltpu.bitcast(score, I32)
    key = bits ^ ((bits >> 31) & 0x7FFFFFFF)
    key = key + jnp.where(score > 0.0, offs, 0)
    return jnp.where(score == 0.0, offs - kpos, key)


def _select_threshold(count_ge, count_gt_eq_lt, shape, topk, idx_bits):
    offs = 1 << idx_bits
    first_candidates = (offs + 1, 1)
    max_steps = -(-(32 + len(first_candidates)) // STEPS_PER_CHECK) * STEPS_PER_CHECK

    def unresolved(carry):
        i, lo, hi, n_lo = carry
        pending = (n_lo != topk) & (hi - 1 > lo)
        return (i < max_steps) & (jnp.max(pending.astype(F32)) > 0.0)

    def bisect(carry):
        i, lo, hi, n_lo = carry
        for _ in range(STEPS_PER_CHECK):
            mid = (lo >> 1) + (hi >> 1) + (lo & hi & 1)
            for step, value in enumerate(first_candidates):
                forced = jnp.where(i == step, value, INT_MIN)
                mid = jnp.where((lo < forced) & (forced < hi), forced, mid)
            tot = count_ge(mid)
            take = tot >= topk
            lo = jnp.where(take, mid, lo)
            n_lo = jnp.where(take, tot, n_lo)
            hi = jnp.where(take, hi, mid)
            i = i + 1
        return i, lo, hi, n_lo

    never = jnp.full(shape, INT_MAX, I32)
    _, thr, _, n_lo = lax.while_loop(
        unresolved, bisect, (jnp.int32(0), jnp.full(shape, INT_MIN + 1, I32), jnp.full(shape, INT_MAX, I32), never))
    tied = (n_lo > topk) & (n_lo != INT_MAX)
    big = jnp.full(shape, offs, I32)

    def resolve_ties(_):
        n_gt, _ = count_gt_eq_lt(thr, big)
        need = topk - n_gt

        def idx_step(i, cut):
            cand = cut | lax.shift_left(jnp.int32(1), idx_bits - 1 - i)
            _, n_eq = count_gt_eq_lt(thr, cand)
            return jnp.where(n_eq <= need, cand, cut)

        cut = lax.fori_loop(0, idx_bits, idx_step, jnp.zeros(shape, I32))
        return jnp.where(tied, cut, big)

    any_tied = jnp.max(tied.astype(F32)) > 0.0
    cut = lax.cond(any_tied, resolve_ties, lambda _: big, 0)
    return thr, cut


def _fold_rows(x):
    acc = x[0:SUBLANES]
    for r in range(1, x.shape[0] // SUBLANES):
        acc = acc + x[r * SUBLANES:(r + 1) * SUBLANES]
    return acc


def _col_total(cnt8):
    return jnp.sum(cnt8.astype(F32), axis=0, keepdims=True).astype(I32)


def _attn_prompt_kernel(qit_ref, wit_ref, qt_ref, kiwib_ref, kb_ref, vt_ref, o_ref,
                        key_ref, w4_ref, wq_ref, s0_ref, s1_ref, bias_ref, m_ref, acc_ref, *, tq, topk, idx_bits):
    j = pl.program_id(1)
    kc = tq
    n_kc = j + 1

    @pl.when((pl.program_id(0) == 0) & (j == 0))
    def _():
        w4_ref[...] = jnp.zeros(w4_ref.shape, BF16)
        wq_ref[...] = jnp.zeros(wq_ref.shape, BF16)

    for h in range(IDX_HEADS):
        w4_ref[0:IDX_DIM, h * tq:(h + 1) * tq] = qit_ref[h * IDX_DIM:(h + 1) * IDX_DIM, :]
    for h in range(B_HEADS):
        n = h // B_GROUP
        wq_ref[h, n * HEAD_DIM:(n + 1) * HEAD_DIM, :] = qt_ref[h * HEAD_DIM:(h + 1) * HEAD_DIM, :]

    wit = wit_ref[...]
    krow = lax.broadcasted_iota(I32, (kc, tq), 0)
    qpos = j * tq + lax.broadcasted_iota(I32, (kc, tq), 1)

    n_chunks = key_ref.shape[0] - 1

    def score_pair(i, _):
        for c in (2 * i, 2 * i + 1):
            rows = pl.multiple_of(jnp.minimum(c, n_chunks - 1) * kc, kc)
            s = _dot(kiwib_ref[pl.ds(rows, kc), :], w4_ref[...])
            tot = jnp.maximum(s[:, 0:tq], 0.0) * wit[0:1, :]
            for h in range(1, IDX_HEADS):
                tot = tot + jnp.maximum(s[:, h * tq:(h + 1) * tq], 0.0) * wit[h:h + 1, :]
            kpos = c * kc + krow
            key_ref[c] = jnp.where(kpos <= qpos, _order_key(tot, kpos, idx_bits), INT_MIN)
        return 0

    lax.fori_loop(0, (n_kc + 1) // 2, score_pair, 0)

    def count_ge(cand):
        def body(i, cnt):
            for c in (2 * i, 2 * i + 1):
                cnt = cnt + _fold_rows(jnp.where(key_ref[c] >= cand, 1, 0))
            return cnt
        return _col_total(lax.fori_loop(0, (n_kc + 1) // 2, body, jnp.zeros((SUBLANES, tq), I32)))

    def count_gt_eq_lt(thr, pos):
        def body(c, carry):
            n_gt, n_eq = carry
            key = key_ref[c]
            n_gt = n_gt + _fold_rows(jnp.where(key > thr, 1, 0))
            n_eq = n_eq + _fold_rows(jnp.where((key == thr) & (c * kc + krow < pos), 1, 0))
            return n_gt, n_eq
        z = jnp.zeros((SUBLANES, tq), I32)
        n_gt, n_eq = lax.fori_loop(0, n_kc, body, (z, z))
        return _col_total(n_gt), _col_total(n_eq)

    thr, cut = _select_threshold(count_ge, count_gt_eq_lt, (1, tq), topk, idx_bits)

    m_ref[...] = jnp.full(m_ref.shape, jnp.finfo(F32).min, F32)
    acc_ref[...] = jnp.zeros(acc_ref.shape, F32)

    def key_block(c):
        return kb_ref[pl.ds(pl.multiple_of(c * kc, kc), kc), :]

    def selection_bias(c, open_bias):
        key = key_ref[c]
        sel = (key > thr) | ((key == thr) & (c * kc + krow < cut))
        bias_ref[...] = jnp.where(sel, open_bias, -jnp.inf)

    selection_bias(0, 0.0)
    kblk0 = key_block(0)
    for h in range(B_HEADS):
        s0_ref[h] = _dot(kblk0, wq_ref[h]) + bias_ref[...]

    def stage(c, src, dst, c_next, open_next):
        selection_bias(c_next, open_next)
        kblk_next = key_block(c_next)
        vtc = vt_ref[c]
        for h in range(B_HEADS):
            n = h // B_GROUP
            dst[h] = _dot(kblk_next, wq_ref[h]) + bias_ref[...]
            s = src[h]
            m_old = m_ref[h]
            m_new = jnp.maximum(m_old, jnp.max(s, axis=0, keepdims=True))
            p = jnp.exp2(s - m_new).astype(BF16)
            v_aug = jnp.concatenate([vtc[n * HEAD_DIM:(n + 1) * HEAD_DIM, :], ones_rows], axis=0)
            acc_ref[h] = jnp.exp2(m_old - m_new) * acc_ref[h] + _dot(v_aug, p)
            m_ref[h] = m_new

    ones_rows = jnp.where(lax.broadcasted_iota(I32, (DENOM_ROWS, kc), 0) == 0, 1.0, 0.0).astype(BF16)
    last = n_kc - 1

    def attend_pair(i, _):
        c0 = 2 * i
        c1 = jnp.minimum(c0 + 1, last)
        stage(c0, s0_ref, s1_ref, c1, jnp.where(c0 + 1 <= last, 0.0, -jnp.inf))
        stage(c1, s1_ref, s0_ref, jnp.minimum(c0 + 2, last), 0.0)
        return 0

    lax.fori_loop(0, (n_kc + 1) // 2, attend_pair, 0)
    out_t = jnp.concatenate([acc_ref[h, 0:HEAD_DIM, :] / acc_ref[h, HEAD_DIM:HEAD_DIM + 1, :]
                             for h in range(B_HEADS)], axis=0)
    o_ref[...] = jnp.transpose(out_t).astype(BF16)


def _attn_prompt(qit, wit, qt, kiwib, kb, vt):
    b, nc, _, tq = qt.shape
    t = nc * tq
    topk = min(TOPK_MAX, t // 4)
    idx_bits = max(1, (t - 1).bit_length())
    tr = lambda r: pl.BlockSpec((None, None, r, tq), lambda i, j: (i, j, 0, 0))
    full = lambda w: pl.BlockSpec((None, t, w), lambda i, j: (i, 0, 0))
    kernel = functools.partial(_attn_prompt_kernel, tq=tq, topk=topk, idx_bits=idx_bits)
    return pl.pallas_call(
        kernel,
        grid=(b, nc),
        in_specs=[tr(IDX_WIDTH), tr(SUBLANES), tr(B_WIDTH), full(LANES), full(KV_WIDTH),
                  pl.BlockSpec((None, nc, KV_WIDTH, tq), lambda i, j: (i, 0, 0, 0))],
        out_specs=pl.BlockSpec((None, tq, B_WIDTH), lambda i, j: (i, j, 0)),
        out_shape=jax.ShapeDtypeStruct((b, t, B_WIDTH), BF16),
        scratch_shapes=[
            pltpu.VMEM((nc + 1, tq, tq), I32),
            pltpu.VMEM((LANES, IDX_HEADS * tq), BF16),
            pltpu.VMEM((B_HEADS, KV_WIDTH, tq), BF16),
            pltpu.VMEM((B_HEADS, tq, tq), F32),
            pltpu.VMEM((B_HEADS, tq, tq), F32),
            pltpu.VMEM((tq, tq), F32),
            pltpu.VMEM((B_HEADS, 1, tq), F32),
            pltpu.VMEM((B_HEADS, HEAD_DIM + DENOM_ROWS, tq), F32),
        ],
        compiler_params=pltpu.CompilerParams(
            dimension_semantics=("arbitrary", "arbitrary"), vmem_limit_bytes=VMEM_LIMIT),
        name="attn_prompt",
    )(qit, wit, qt, kiwib, kb, vt)


def _rot_tables(pos):
    r = pos.shape[0]
    inv_freq = np.float32(ROPE_THETA) ** (-np.arange(ROT_HALF, dtype=np.float32) * np.float32(2.0) / np.float32(ROT_DIM))
    ang = pos.astype(np.float32)[:, None] * inv_freq[None, :]
    cos, sin = np.cos(ang), np.sin(ang)
    rest = HEAD_DIM - ROT_DIM
    c64 = np.concatenate([cos, cos, np.ones((r, rest), np.float32)], axis=1)
    lo64 = np.concatenate([-sin, np.zeros((r, HEAD_DIM - ROT_HALF), np.float32)], axis=1)
    hi64 = np.concatenate([np.zeros((r, ROT_HALF), np.float32), sin, np.zeros((r, rest), np.float32)], axis=1)
    return jnp.asarray(np.concatenate([c64, c64, lo64, lo64, hi64, hi64], axis=1).astype(np.float32))


def _head_avg_matrix():
    return jnp.kron(jnp.eye(256 // HEAD_DIM, dtype=F32), jnp.full((HEAD_DIM, HEAD_DIM), 1.0 / HEAD_DIM, F32)).astype(BF16)


def _pad_w_in(w_in):
    return jnp.pad(w_in, ((0, 0), (0, IN_PAD - IN_WIDTH))).astype(BF16)


def _front_sample_kernel(x_ref, shift_ref, scale_ref, w_ref, rt_ref, mavg_ref, gam_ref, bet_ref, wl_ref, bl_ref,
                         a_ref, q_ref, k_ref, v_ref, qi_ref, kiwi_ref, vg_ref, *, ts, db):
    one_scale = 1.0 + scale_ref[...]
    shift = shift_ref[...]
    h = jnp.concatenate([x_ref[t * db:(t + 1) * db, :] * one_scale + shift for t in range(ts)], axis=0).astype(BF16)
    u, vg, q, k, v, qi, kiwi = _project(h, w_ref, rt_ref[...], mavg_ref[...], gam_ref[...], bet_ref[...])
    q_ref[...] = q.astype(BF16)
    k_ref[...] = k
    v_ref[...] = v
    qi_ref[...] = qi.astype(BF16)
    kiwi_ref[...] = kiwi
    vg_ref[...] = vg
    for t in range(ts):
        s = bl_ref[t:t + 1, :]
        for src in range(t + 1):
            s = s + wl_ref[t * ts + src:t * ts + src + 1, :] * vg[src * db:(src + 1) * db, :]
        a_ref[t * db:(t + 1) * db, :] = (u[t * db:(t + 1) * db, :] * s).astype(BF16)


def _front_sample(x_tm, mod_s, w_in, rt, mavg, gam, bet, wl, bl, ts, db):
    r = ts * db
    whole = lambda a: pl.BlockSpec(a.shape, lambda i: (0,) * a.ndim)
    out = lambda w, dt: jax.ShapeDtypeStruct((r, w), dt)
    outs = (out(A_WIDTH, BF16), out(B_WIDTH, BF16), out(KV_WIDTH, F32), out(KV_WIDTH, F32),
            out(IDX_WIDTH, BF16), out(LANES, F32), out(A_WIDTH, F32))
    return pl.pallas_call(
        functools.partial(_front_sample_kernel, ts=ts, db=db),
        grid=(1,),
        in_specs=[
            whole(x_tm),
            pl.BlockSpec((db, D_MODEL), lambda i: (0, 0)),
            pl.BlockSpec((db, D_MODEL), lambda i: (0, 1)),
            whole(w_in), whole(rt), whole(mavg), whole(gam), whole(bet), whole(wl), whole(bl),
        ],
        out_specs=[pl.BlockSpec((r, s.shape[1]), lambda i: (0, 0)) for s in outs],
        out_shape=outs,
        compiler_params=pltpu.CompilerParams(dimension_semantics=("arbitrary",), vmem_limit_bytes=VMEM_LIMIT),
        name="front_sample",
    )(x_tm, mod_s, mod_s, w_in, rt, mavg, gam, bet, wl, bl)


def _idx_sample_kernel(pt_ref, qi4_ref, wcol_ref, qpos_ref, kinew_ref, kidx_hbm, bias_ref,
                       ki_buf, key_ref, sem, *, ga, ts, n_pages, page, lpad, topk, idx_bits):
    i = pl.program_id(0)
    n_steps = pl.num_programs(0)
    past = n_pages * page
    rows = ga * ts
    slot = i % 2

    def page_copy(step, to_slot, g, p):
        phys = pt_ref[step * ga + g, p]
        return pltpu.make_async_copy(kidx_hbm.at[phys], ki_buf.at[to_slot, g, :, pl.ds(p * page, page)],
                                     sem.at[to_slot])

    def start_all(step, to_slot):
        for g in range(ga):
            for p in range(n_pages):
                page_copy(step, to_slot, g, p).start()

    @pl.when(i == 0)
    def _():
        start_all(0, 0)

    @pl.when(i + 1 < n_steps)
    def _():
        start_all(i + 1, 1 - slot)

    ki_buf[slot, :, :, pl.ds(past, LANES)] = kinew_ref[...]
    for g in range(ga):
        for p in range(n_pages):
            page_copy(i, slot, g, p).wait()

    kpos = lax.broadcasted_iota(I32, (ts, lpad), 1)
    for g in range(ga):
        s = _dot(qi4_ref[g], ki_buf[slot, g].astype(BF16))
        r = jnp.maximum(s, 0.0) * wcol_ref[g]
        tot = r[0:ts]
        for h in range(1, IDX_HEADS):
            tot = tot + r[h * ts:(h + 1) * ts]
        adm = (kpos <= qpos_ref[g * ts:(g + 1) * ts, :]) & (kpos < past + ts)
        key_ref[g * ts:(g + 1) * ts, :] = jnp.where(adm, _order_key(tot, kpos, idx_bits), INT_MIN)

    sub = lpad // LANES
    lane128 = lax.broadcasted_iota(I32, (rows, LANES), 1)

    def row_total(cnt):
        tot = jnp.sum(cnt.astype(F32), axis=1, keepdims=True)
        return jnp.broadcast_to(tot, cnt.shape).astype(I32)

    def count_ge(cand):
        cnt = jnp.zeros((rows, LANES), I32)
        for s_ in range(sub):
            cnt = cnt + jnp.where(key_ref[:, s_ * LANES:(s_ + 1) * LANES] >= cand, 1, 0)
        return row_total(cnt)

    def count_gt_eq_lt(thr, pos):
        n_gt = jnp.zeros((rows, LANES), I32)
        n_eq = jnp.zeros((rows, LANES), I32)
        for s_ in range(sub):
            kk = key_ref[:, s_ * LANES:(s_ + 1) * LANES]
            n_gt = n_gt + jnp.where(kk > thr, 1, 0)
            n_eq = n_eq + jnp.where((kk == thr) & (s_ * LANES + lane128 < pos), 1, 0)
        return row_total(n_gt), row_total(n_eq)

    thr, cut = _select_threshold(count_ge, count_gt_eq_lt, (rows, LANES), topk, idx_bits)
    for s_ in range(sub):
        kk = key_ref[:, s_ * LANES:(s_ + 1) * LANES]
        sel = (kk > thr) | ((kk == thr) & (s_ * LANES + lane128 < cut))
        bias = jnp.where(sel, 0.0, -jnp.inf)
        for g in range(ga):
            bias_ref[g, :, s_ * LANES:(s_ + 1) * LANES] = bias[g * ts:(g + 1) * ts]


def _idx_sample(page_table, qi4, wcol, qpos, kinew_t, kidx_t, ts, ga):
    db = qi4.shape[0]
    n_pages = page_table.shape[1]
    page = kidx_t.shape[2]
    past = n_pages * page
    lpad = past + LANES
    topk = min(TOPK_MAX, (past + ts) // 4)
    idx_bits = max(1, (lpad - 1).bit_length())
    kernel = functools.partial(_idx_sample_kernel, ga=ga, ts=ts, n_pages=n_pages, page=page, lpad=lpad,
                               topk=topk, idx_bits=idx_bits)
    return pl.pallas_call(
        kernel,
        grid_spec=pltpu.PrefetchScalarGridSpec(
            num_scalar_prefetch=1,
            grid=(db // ga,),
            in_specs=[pl.BlockSpec((ga, IDX_HEADS * ts, IDX_DIM), lambda i, pt: (i, 0, 0)),
                      pl.BlockSpec((ga, IDX_HEADS * ts, 1), lambda i, pt: (i, 0, 0)),
                      pl.BlockSpec((ga * ts, 1), lambda i, pt: (i, 0)),
                      pl.BlockSpec((ga, IDX_DIM, LANES), lambda i, pt: (i, 0, 0)),
                      pl.BlockSpec(memory_space=pl.ANY)],
            out_specs=pl.BlockSpec((ga, ts, lpad), lambda i, pt: (i, 0, 0)),
            scratch_shapes=[
                pltpu.VMEM((2, ga, IDX_DIM, lpad), F32),
                pltpu.VMEM((ga * ts, lpad), I32),
                pltpu.SemaphoreType.DMA((2,)),
            ]),
        out_shape=jax.ShapeDtypeStruct((db, ts, lpad), F32),
        compiler_params=pltpu.CompilerParams(dimension_semantics=("arbitrary",), vmem_limit_bytes=VMEM_LIMIT),
        name="idx_sample",
    )(page_table, qi4, wcol, qpos, kinew_t, kidx_t)


def _attn_sample_kernel(pt_ref, q_ref, bias_ref, knew_ref, vnew_ref, k_hbm, v_hbm, o_ref,
                        k_buf, v_buf, sem, *, ts, n_pages, page):
    b = pl.program_id(0)
    nb = pl.num_programs(0)
    past = n_pages * page
    slot = b % 2

    def page_copies(seq, to_slot, p):
        phys = pt_ref[seq, p]
        dst = pl.ds(p * page, page)
        return (pltpu.make_async_copy(k_hbm.at[phys], k_buf.at[to_slot, :, :, dst], sem.at[0, to_slot]),
                pltpu.make_async_copy(v_hbm.at[phys], v_buf.at[to_slot, :, :, dst], sem.at[1, to_slot]))

    def start_all(seq, to_slot):
        for p in range(n_pages):
            for cp in page_copies(seq, to_slot, p):
                cp.start()

    @pl.when(b == 0)
    def _():
        start_all(0, 0)

    @pl.when(b + 1 < nb)
    def _():
        start_all(b + 1, 1 - slot)

    k_buf[slot, :, :, pl.ds(past, LANES)] = knew_ref[...]
    v_buf[slot, :, :, pl.ds(past, LANES)] = vnew_ref[...]
    for p in range(n_pages):
        for cp in page_copies(b, slot, p):
            cp.wait()

    q = q_ref[...]
    bias = bias_ref[...]
    bias2 = jnp.concatenate([bias] * B_GROUP, axis=0)
    outs = [None] * B_HEADS
    for n in range(B_KV_HEADS):
        kt = k_buf[slot, n].astype(BF16)
        vt = v_buf[slot, n].astype(BF16)
        qs = jnp.concatenate([q[:, (n * B_GROUP + g) * HEAD_DIM:(n * B_GROUP + g + 1) * HEAD_DIM]
                              for g in range(B_GROUP)], axis=0)
        sc = _dot(qs, kt) + bias2
        m = jnp.max(sc, axis=1, keepdims=True)
        p_ = jnp.exp2(sc - m)
        o = _dot_nt(p_.astype(BF16), vt) / jnp.sum(p_, axis=1, keepdims=True)
        for g in range(B_GROUP):
            outs[n * B_GROUP + g] = o[g * ts:(g + 1) * ts]
    o_ref[...] = jnp.concatenate(outs, axis=1).astype(BF16)


def _attn_sample(page_table, q, bias, knew_t, vnew_t, k_t, v_t):
    db, ts, _ = q.shape
    n_pages = page_table.shape[1]
    page = k_t.shape[3]
    lpad = bias.shape[2]
    seq3 = lambda a: pl.BlockSpec((None,) + a.shape[1:], lambda b, pt: (b,) + (0,) * (a.ndim - 1))
    anyspec = pl.BlockSpec(memory_space=pl.ANY)
    kernel = functools.partial(_attn_sample_kernel, ts=ts, n_pages=n_pages, page=page)
    return pl.pallas_call(
        kernel,
        grid_spec=pltpu.PrefetchScalarGridSpec(
            num_scalar_prefetch=1,
            grid=(db,),
            in_specs=[seq3(q), seq3(bias), seq3(knew_t), seq3(vnew_t), anyspec, anyspec],
            out_specs=pl.BlockSpec((None, ts, B_WIDTH), lambda b, pt: (b, 0, 0)),
            scratch_shapes=[
                pltpu.VMEM((2, B_KV_HEADS, HEAD_DIM, lpad), F32),
                pltpu.VMEM((2, B_KV_HEADS, HEAD_DIM, lpad), F32),
                pltpu.SemaphoreType.DMA((2, 2)),
            ]),
        out_shape=jax.ShapeDtypeStruct((db, ts, B_WIDTH), BF16),
        compiler_params=pltpu.CompilerParams(dimension_semantics=("arbitrary",), vmem_limit_bytes=VMEM_LIMIT),
        name="attn_sample",
    )(page_table, q, bias, knew_t, vnew_t, k_t, v_t)


def _layer_norm_rows(y, gam, bet):
    mu = jnp.mean(y, axis=1, keepdims=True)
    d = y - mu
    var = jnp.mean(d * d, axis=1, keepdims=True)
    return d * lax.rsqrt(var + LN_EPS) * gam + bet


def _store_row_tiles(ref, val):
    r = val.shape[0]
    for c in range(ROW_CHUNKS):
        ref[pl.ds(c, r, stride=ROW_CHUNKS), :] = val[:, c * LANES:(c + 1) * LANES]


def _load_row_tiles(ref, r):
    return jnp.concatenate([ref[pl.ds(c, r, stride=ROW_CHUNKS), :] for c in range(ROW_CHUNKS)], axis=1)


ROUTER_ROWS = SUBLANES + N_EXPERTS


def _route(logits_t):
    r = logits_t.shape[1]
    far = float(LANES)

    def softmax_rows(x):
        e = jnp.exp(x - jnp.max(x, axis=0, keepdims=True))
        return e / jnp.sum(e, axis=0, keepdims=True)

    def first_max(p):
        rows = lax.broadcasted_iota(I32, p.shape, 0).astype(F32)
        best = jnp.max(p, axis=0, keepdims=True)
        return best, jnp.min(jnp.where(p == best, rows, far), axis=0, keepdims=True), rows

    g_w, g_sel, _ = first_max(softmax_rows(logits_t[0:N_GROUPS]))
    el = jnp.zeros((EXPERTS_PER_GROUP, r), F32)
    for g in range(N_GROUPS):
        lo = SUBLANES + g * EXPERTS_PER_GROUP
        el = jnp.where(g_sel == float(g), logits_t[lo:lo + EXPERTS_PER_GROUP], el)
    ep = softmax_rows(el)
    p1, i1, rows = first_max(ep)
    p2, i2, _ = first_max(jnp.where(rows == i1, -1.0, ep))
    denom = p1 + p2
    base = g_sel * float(EXPERTS_PER_GROUP)
    return jnp.concatenate([base + i1, base + i2, g_w * p1 / denom, g_w * p2 / denom,
                            jnp.zeros((SUBLANES - 4, r), F32)], axis=0)


def _split_bf16(x):
    hi = x.astype(BF16)
    return hi, (x - hi.astype(F32)).astype(BF16)


def _mid_kernel(x_ref, a_ref, b_ref, g1_ref, sh2_ref, sc2_ref, wo_ref, gam_ref, bet_ref, wr_ref,
                x1_ref, h2_ref, route_ref, *, alpha):
    mixed = _dot(a_ref[...], wo_ref[0:A_WIDTH, :]) + _dot(b_ref[...], wo_ref[A_WIDTH:A_WIDTH + B_WIDTH, :])
    x1 = _layer_norm_rows(alpha * x_ref[...] + g1_ref[...] * mixed, gam_ref[...], bet_ref[...])
    x1_ref[...] = x1
    h2 = x1 * (1.0 + sc2_ref[...]) + sh2_ref[...]
    _store_row_tiles(h2_ref, h2)
    w_hi, w_lo = _split_bf16(wr_ref[...])
    h_hi, h_lo = _split_bf16(h2)
    logits_t = _dot_nt(w_hi, h_hi) + _dot_nt(w_hi, h_lo) + _dot_nt(w_lo, h_hi)
    route_ref[...] = _route(logits_t)


def _mod_spec(mod3, comp, tiles_per_group):
    rm = mod3.shape[1]
    return pl.BlockSpec((None, rm, D_MODEL), lambda i: (i // tiles_per_group, 0, comp))


def _mid(x, a, b, mod3, tiles_per_group, tile, w_out, gam, bet, w_router, alpha):
    n = x.shape[0]
    rows = lambda w: pl.BlockSpec((tile, w), lambda i: (i, 0))
    whole = lambda arr: pl.BlockSpec(arr.shape, lambda i: (0,) * arr.ndim)
    return pl.pallas_call(
        functools.partial(_mid_kernel, alpha=alpha),
        grid=(n // tile,),
        in_specs=[rows(D_MODEL), rows(A_WIDTH), rows(B_WIDTH),
                  _mod_spec(mod3, 2, tiles_per_group), _mod_spec(mod3, 3, tiles_per_group),
                  _mod_spec(mod3, 4, tiles_per_group),
                  whole(w_out), whole(gam), whole(bet), whole(w_router)],
        out_specs=[rows(D_MODEL), pl.BlockSpec((tile * ROW_CHUNKS, LANES), lambda i: (i, 0)),
                   pl.BlockSpec((SUBLANES, tile), lambda i: (0, i))],
        out_shape=(jax.ShapeDtypeStruct((n, D_MODEL), F32),
                   jax.ShapeDtypeStruct((n * ROW_CHUNKS, LANES), F32),
                   jax.ShapeDtypeStruct((SUBLANES, n), F32)),
        compiler_params=pltpu.CompilerParams(dimension_semantics=("arbitrary",), vmem_limit_bytes=VMEM_LIMIT),
        name="mid",
    )(x, a, b, mod3, mod3, mod3, w_out, gam, bet, w_router)


META_CNT, META_START, META_END, META_NACT = 0, 1, 2, 3


def _plan_kernel(route_ref, pos_ref, meta_ref, te_ref, carry_ref, starts_ref, *, nt, tm):
    phase = pl.program_id(0)
    i = pl.program_id(1)

    @pl.when(i == 0)
    def _():
        carry_ref[...] = jnp.zeros(carry_ref.shape, F32)

    eid = route_ref[0:2, :].astype(I32)
    e_iota = lax.broadcasted_iota(I32, (N_EXPERTS, nt), 0)
    hit0 = eid[0:1, :] == e_iota
    hit1 = eid[1:2, :] == e_iota
    onehot = jnp.where(hit0 | hit1, 1.0, 0.0)

    @pl.when(phase == 0)
    def _():
        pos_ref[...] = jnp.zeros(pos_ref.shape, I32)
        carry_ref[...] = carry_ref[...] + jnp.sum(onehot, axis=1, keepdims=True)

        @pl.when(i == pl.num_programs(1) - 1)
        def _():
            cnt = carry_ref[...]
            tiles = jnp.floor((cnt + float(tm - 1)) * (1.0 / tm))
            r = lax.broadcasted_iota(I32, (N_EXPERTS, N_EXPERTS), 0)
            c = lax.broadcasted_iota(I32, (N_EXPERTS, N_EXPERTS), 1)
            lower = jnp.where(r > c, 1.0, 0.0).astype(BF16)
            tiles_before = _dot(lower, jnp.broadcast_to(tiles, (N_EXPERTS, LANES)).astype(BF16))[:, 0:1]
            tiles_end = tiles_before + tiles
            starts_ref[...] = tiles_before * float(tm)

            diag = (lax.broadcasted_iota(I32, (N_EXPERTS, LANES), 0)
                    == lax.broadcasted_iota(I32, (N_EXPERTS, LANES), 1))

            def as_row(col):
                return jnp.sum(jnp.where(diag, jnp.broadcast_to(col, (N_EXPERTS, LANES)), 0.0),
                               axis=0, keepdims=True)

            nact = jnp.broadcast_to(jnp.max(tiles_end, axis=0, keepdims=True), (1, LANES))
            meta_ref[...] = jnp.concatenate(
                [as_row(cnt), as_row(tiles_before * float(tm)), as_row(tiles_end * float(tm)), nact,
                 jnp.zeros((SUBLANES - 4, LANES), F32)], axis=0).astype(I32)
            tile_i = lax.broadcasted_iota(I32, (N_EXPERTS, te_ref.shape[1]), 1).astype(F32)
            owner = jnp.sum(jnp.where(tiles_end <= tile_i, 1.0, 0.0), axis=0, keepdims=True)
            te_ref[...] = jnp.minimum(owner, float(N_EXPERTS - 1)).astype(I32)

    @pl.when(phase == 1)
    def _():
        upper = (lax.broadcasted_iota(I32, (nt, nt), 0) <= lax.broadcasted_iota(I32, (nt, nt), 1))
        incl = _dot(onehot.astype(BF16), jnp.where(upper, 1.0, 0.0).astype(BF16))
        slot = starts_ref[...] + carry_ref[...] + incl - 1.0
        pos_ref[0:1, :] = jnp.sum(jnp.where(hit0, slot, 0.0), axis=0, keepdims=True).astype(I32)
        pos_ref[1:2, :] = jnp.sum(jnp.where(hit1, slot, 0.0), axis=0, keepdims=True).astype(I32)
        carry_ref[...] = carry_ref[...] + jnp.sum(onehot, axis=1, keepdims=True)


def _plan(route_t, nt, tm):
    n = route_t.shape[1]
    n_tiles = (2 * n) // tm + N_EXPERTS
    te_width = pl.cdiv(n_tiles, LANES) * LANES
    pos, meta, te = pl.pallas_call(
        functools.partial(_plan_kernel, nt=nt, tm=tm),
        grid=(2, n // nt),
        in_specs=[pl.BlockSpec((SUBLANES, nt), lambda p, i: (0, i))],
        out_specs=[pl.BlockSpec((2, nt), lambda p, i: (0, i * p)),
                   pl.BlockSpec((SUBLANES, LANES), lambda p, i: (0, 0)),
                   pl.BlockSpec((1, te_width), lambda p, i: (0, 0))],
        out_shape=(jax.ShapeDtypeStruct((2, n), I32), jax.ShapeDtypeStruct((SUBLANES, LANES), I32),
                   jax.ShapeDtypeStruct((1, te_width), I32)),
        scratch_shapes=[pltpu.VMEM((N_EXPERTS, 1), F32), pltpu.VMEM((N_EXPERTS, 1), F32)],
        compiler_params=pltpu.CompilerParams(dimension_semantics=("arbitrary", "arbitrary")),
        name="plan",
    )(route_t)
    return pos, meta, te, n_tiles


def _row_copy(src, src_row, dst, dst_row, sem):
    return pltpu.make_async_copy(src.at[pl.ds(src_row * ROW_CHUNKS, ROW_CHUNKS)],
                                 dst.at[pl.ds(dst_row * ROW_CHUNKS, ROW_CHUNKS)], sem)


def _dispatch_kernel(meta_ref, pos_ref, hp_ref, hs_ref, xs_out, zero_ref, sem, *, tt, blocks_p, tm, n_tiles):
    i = pl.program_id(0)

    def scatter(h_ref):
        def issue(j, _):
            for k in range(2):
                _row_copy(h_ref, j, xs_out, pos_ref[k, j], sem.at[k]).start()
            return 0

        def drain(j, _):
            for k in range(2):
                _row_copy(h_ref, j, xs_out, pos_ref[k, j], sem.at[k]).wait()
            return 0

        lax.fori_loop(0, tt, issue, 0, unroll=ROW_DMA_UNROLL)
        lax.fori_loop(0, tt, drain, 0, unroll=ROW_DMA_UNROLL)

    @pl.when(i < blocks_p)
    def _():
        scatter(hp_ref)

    @pl.when(i >= blocks_p)
    def _():
        scatter(hs_ref)

    @pl.when(i == pl.num_programs(0) - 1)
    def _():
        zero_ref[...] = jnp.zeros(zero_ref.shape, F32)

        def zero_rows(first_row, n_rows):
            return pltpu.make_async_copy(zero_ref.at[pl.ds(0, n_rows * ROW_CHUNKS)],
                                         xs_out.at[pl.ds(first_row * ROW_CHUNKS, n_rows * ROW_CHUNKS)], sem.at[0])

        def start_row(r, c):
            zero_rows(r, 1).start()
            return c

        def wait_row(r, c):
            zero_rows(r, 1).wait()
            return c

        def per_expert(row_fn):
            def body(e, c):
                lo = meta_ref[META_START, e] + meta_ref[META_CNT, e]
                return lax.fori_loop(lo, meta_ref[META_END, e], row_fn, c)
            return body

        def start_tile(t, c):
            zero_rows(t * tm, tm).start()
            return c

        def wait_tile(t, c):
            zero_rows(t * tm, tm).wait()
            return c

        nact = meta_ref[META_NACT, 0]
        lax.fori_loop(0, N_EXPERTS, per_expert(start_row), 0)
        lax.fori_loop(nact, n_tiles, start_tile, 0)
        lax.fori_loop(0, N_EXPERTS, per_expert(wait_row), 0)
        lax.fori_loop(nact, n_tiles, wait_tile, 0)


def _dispatch(meta, pos, h2_p, h2_s, tt, tm, n_tiles):
    blocks_p = h2_p.shape[0] // (tt * ROW_CHUNKS)
    blocks_s = h2_s.shape[0] // (tt * ROW_CHUNKS)
    return pl.pallas_call(
        functools.partial(_dispatch_kernel, tt=tt, blocks_p=blocks_p, tm=tm, n_tiles=n_tiles),
        grid_spec=pltpu.PrefetchScalarGridSpec(
            num_scalar_prefetch=1,
            grid=(blocks_p + blocks_s,),
            in_specs=[pl.BlockSpec((2, tt), lambda i, m: (0, i), memory_space=pltpu.SMEM),
                      pl.BlockSpec((tt * ROW_CHUNKS, LANES), lambda i, m: (jnp.minimum(i, blocks_p - 1), 0)),
                      pl.BlockSpec((tt * ROW_CHUNKS, LANES), lambda i, m: (jnp.maximum(i - blocks_p, 0), 0))],
            out_specs=pl.BlockSpec(memory_space=pl.ANY),
            scratch_shapes=[pltpu.VMEM((tm * ROW_CHUNKS, LANES), F32), pltpu.SemaphoreType.DMA((2,))]),
        out_shape=jax.ShapeDtypeStruct((n_tiles * tm * ROW_CHUNKS, LANES), F32),
        compiler_params=pltpu.CompilerParams(dimension_semantics=("arbitrary",)),
        name="dispatch",
    )(meta, pos, h2_p, h2_s)


def _experts_kernel(te_ref, meta_ref, xs_ref, wg_ref, wu_ref, wd_ref, ys_ref, wg_b, wu_b, wd_b, *, tm):
    i = pl.program_id(0)
    active = i < meta_ref[META_NACT, 0]
    fresh = (i == 0) | (te_ref[0, i] != te_ref[0, jnp.maximum(i - 1, 0)])

    @pl.when(active & fresh)
    def _():
        wg_b[...] = wg_ref[...].astype(BF16)
        wu_b[...] = wu_ref[...].astype(BF16)
        wd_b[...] = wd_ref[...].astype(BF16)

    @pl.when(active)
    def _():
        x = _load_row_tiles(xs_ref, tm).astype(BF16)
        hid = (jax.nn.silu(_dot(x, wg_b[...])) * _dot(x, wu_b[...])).astype(BF16)
        _store_row_tiles(ys_ref, _dot(hid, wd_b[...]))

    @pl.when(jnp.logical_not(active))
    def _():
        ys_ref[...] = jnp.zeros(ys_ref.shape, F32)


def _experts(te, meta, xs, w_gate, w_up, w_down, tm, n_tiles):
    last_active = lambda i, m: jnp.minimum(i, m[META_NACT, 0] - 1)
    tile = pl.BlockSpec((tm * ROW_CHUNKS, LANES), lambda i, te, m: (last_active(i, m), 0))
    w_in_spec = pl.BlockSpec((None, D_MODEL, D_EXPERT), lambda i, te, m: (te[0, last_active(i, m)], 0, 0))
    w_out_spec = pl.BlockSpec((None, D_EXPERT, D_MODEL), lambda i, te, m: (te[0, last_active(i, m)], 0, 0))
    return pl.pallas_call(
        functools.partial(_experts_kernel, tm=tm),
        grid_spec=pltpu.PrefetchScalarGridSpec(
            num_scalar_prefetch=2,
            grid=(n_tiles,),
            in_specs=[tile, w_in_spec, w_in_spec, w_out_spec],
            out_specs=pl.BlockSpec((tm * ROW_CHUNKS, LANES), lambda i, te, m: (i, 0)),
            scratch_shapes=[pltpu.VMEM((D_MODEL, D_EXPERT), BF16), pltpu.VMEM((D_MODEL, D_EXPERT), BF16),
                            pltpu.VMEM((D_EXPERT, D_MODEL), BF16)]),
        out_shape=jax.ShapeDtypeStruct(xs.shape, F32),
        compiler_params=pltpu.CompilerParams(dimension_semantics=("arbitrary",), vmem_limit_bytes=VMEM_LIMIT),
        name="experts",
    )(te, meta, xs, w_gate, w_up, w_down)


def _combine_kernel(pos_ref, pos_next_ref, route_ref, x1_ref, g2_ref, gam_ref, bet_ref, ys_hbm, y_ref,
                    a0, a1, b0, b1, sem, *, tt, alpha):
    i = pl.program_id(0)
    sets = ((a0, a1), (b0, b1))

    def gather(pos, which, wait):
        def body(j, c):
            for k in range(2):
                cp = _row_copy(ys_hbm, pos[k, j], sets[which][k], j, sem.at[which, k])
                cp.wait() if wait else cp.start()
            return c
        lax.fori_loop(0, tt, body, 0, unroll=ROW_DMA_UNROLL)

    @pl.when(i == 0)
    def _():
        gather(pos_ref, 0, wait=False)

    def step(which):
        @pl.when(i + 1 < pl.num_programs(0))
        def _():
            gather(pos_next_ref, 1 - which, wait=False)

        gather(pos_ref, which, wait=True)
        route = route_ref[...]
        eye = lax.broadcasted_iota(I32, (tt, tt), 0) == lax.broadcasted_iota(I32, (tt, tt), 1)
        as_col = lambda row: jnp.sum(jnp.where(eye, jnp.broadcast_to(row, (tt, tt)), 0.0), axis=1, keepdims=True)
        buf0, buf1 = sets[which]
        f = as_col(route[2:3, :]) * _load_row_tiles(buf0, tt) + as_col(route[3:4, :]) * _load_row_tiles(buf1, tt)
        y_ref[...] = _layer_norm_rows(alpha * x1_ref[...] + g2_ref[...] * f, gam_ref[...], bet_ref[...])

    for which in range(2):
        pl.when(i % 2 == which)(functools.partial(step, which))


def _combine(pos, route_t, col_block0, x1, mod3, tiles_per_group, tt, gam, bet, ys, alpha):
    n = x1.shape[0]
    rows = lambda w: pl.BlockSpec((tt, w), lambda i: (i, 0))
    whole = lambda arr: pl.BlockSpec(arr.shape, lambda i: (0,) * arr.ndim)
    steps = n // tt
    buf = pltpu.VMEM((tt * ROW_CHUNKS, LANES), F32)
    return pl.pallas_call(
        functools.partial(_combine_kernel, tt=tt, alpha=alpha),
        grid=(steps,),
        in_specs=[pl.BlockSpec((2, tt), lambda i: (0, i + col_block0), memory_space=pltpu.SMEM),
                  pl.BlockSpec((2, tt), lambda i: (0, jnp.minimum(i + 1, steps - 1) + col_block0),
                               memory_space=pltpu.SMEM),
                  pl.BlockSpec((SUBLANES, tt), lambda i: (0, i + col_block0)),
                  rows(D_MODEL), _mod_spec(mod3, 5, tiles_per_group), whole(gam), whole(bet),
                  pl.BlockSpec(memory_space=pl.ANY)],
        out_specs=rows(D_MODEL),
        out_shape=jax.ShapeDtypeStruct((n, D_MODEL), F32),
        scratch_shapes=[buf, buf, buf, buf, pltpu.SemaphoreType.DMA((2, 2))],
        compiler_params=pltpu.CompilerParams(dimension_semantics=("arbitrary",), vmem_limit_bytes=VMEM_LIMIT),
        name="combine",
    )(pos, pos, route_t, x1, mod3, gam, bet, ys)


def kernel(x_prompt, x_sample, cache_k, cache_v, cache_kidx, page_table, c_prompt, c_sample, w_ada, b_ada, w_in,
           a_ln_g, a_ln_b, w_spatial, b_spatial, w_out, ln1_g, ln1_b, w_group_router, w_expert_router, w_gate,
           w_up, w_down, ln2_g, ln2_b):
    depth = w_ada.shape[0]
    assert depth == 1, "one trunk layer"
    alpha = (2.0 * depth) ** 0.25
    bsz, seq, d = x_prompt.shape
    db, ts, _ = x_sample.shape
    n_pages = page_table.shape[1]
    page = cache_k.shape[2]
    past = n_pages * page
    tile_p = min(PROMPT_TILE, seq)
    assert d == D_MODEL and seq % tile_p == 0 and tile_p % CHUNK == 0 and ts <= SUBLANES and page == LANES
    l = 0

    n_c = bsz + db
    n_c_pad = pl.cdiv(n_c, SUBLANES) * SUBLANES
    c_all = jnp.pad(jnp.concatenate([c_prompt, c_sample], axis=0), ((0, n_c_pad - n_c), (0, 0)))
    mod = _ada(c_all, w_ada[l], b_ada[l][None, :])
    mod_p = mod[:bsz]
    mod_s = mod[bsz:n_c]

    w_in_b = _pad_w_in(w_in[l])
    w_out_b = w_out[l].astype(BF16)
    mavg = _head_avg_matrix()
    gam_a = a_ln_g[l].reshape(1, A_WIDTH)
    bet_a = a_ln_b[l].reshape(1, A_WIDTH)
    w_router = jnp.concatenate([w_group_router[l].T, jnp.zeros((SUBLANES - N_GROUPS, d), F32),
                                w_expert_router[l].reshape(d, N_EXPERTS).T], axis=0)
    ln1 = (ln1_g[l][None, :], ln1_b[l][None, :])
    ln2 = (ln2_g[l][None, :], ln2_b[l][None, :])

    wsp = w_spatial[l].reshape(A_HEADS // 2, 2, CHUNK, CHUNK).transpose(0, 2, 1, 3).reshape(A_HEADS // 2, CHUNK, 2 * CHUNK)
    bsp = jnp.repeat(b_spatial[l].T, HEAD_DIM, axis=1)
    a_p, kt_p, vt32_p, kit_p, kb_p, kiwib_p, vt_p, qt_p, qit_p, wit_p = _front_prompt(
        x_prompt, mod_p.reshape(bsz, 6, d), w_in_b, _rot_tables(np.arange(seq)), mavg, gam_a, bet_a, wsp, bsp, tile_p)
    b_p = _attn_prompt(qit_p, wit_p, qt_p, kiwib_p, kb_p, vt_p)
    n_p = bsz * seq
    tile_m = min(MID_TILE, seq)
    assert seq % tile_m == 0
    x1_p, h2_p, route_p = _mid(x_prompt.reshape(n_p, d), a_p.reshape(n_p, A_WIDTH), b_p.reshape(n_p, B_WIDTH),
                               mod_p.reshape(bsz, 1, 6 * d), seq // tile_m, tile_m, w_out_b, *ln1, w_router, alpha)

    r_s = ts * db
    x_tm = x_sample.transpose(1, 0, 2).reshape(r_s, d)
    rt_s = _rot_tables(np.repeat(past + np.arange(ts), db))
    w_small = w_spatial[l][:, :ts, :ts]
    wl = jnp.repeat(w_small.transpose(1, 2, 0).reshape(ts * ts, A_HEADS), HEAD_DIM, axis=1)
    bl = jnp.repeat(b_spatial[l][:, :ts].T, HEAD_DIM, axis=1)
    a_s, q_s, k_s, v_s, qi_s, kiwi_s, vg_s = _front_sample(x_tm, mod_s, w_in_b, rt_s, mavg, gam_a, bet_a, wl, bl, ts, db)

    def seq_major(a):
        return a.reshape(ts, db, a.shape[-1]).transpose(1, 0, 2)

    def new_t(a, heads):
        a = a.reshape(ts, db, heads, HEAD_DIM).transpose(1, 2, 3, 0)
        return jnp.pad(a, ((0, 0), (0, 0), (0, 0), (0, LANES - ts)))

    ga = min(IDX_SEQS, db)
    assert db % ga == 0 and (ga * ts) % SUBLANES == 0
    qi4 = qi_s.reshape(ts, db, IDX_HEADS, IDX_DIM).transpose(1, 2, 0, 3).reshape(db, IDX_HEADS * ts, IDX_DIM)
    wcol = kiwi_s[:, WI_LANE:WI_LANE + IDX_HEADS].reshape(ts, db, IDX_HEADS).transpose(1, 2, 0).reshape(db, IDX_HEADS * ts, 1)
    qpos = jnp.tile(past + jnp.arange(ts, dtype=I32), db).reshape(db * ts, 1)
    kinew_t = new_t(kiwi_s[:, :IDX_DIM], 1)[:, 0]
    kidx_t = jnp.transpose(cache_kidx[l], (0, 2, 1))
    k_t = jnp.transpose(cache_k[l], (0, 2, 3, 1))
    v_t = jnp.transpose(cache_v[l], (0, 2, 3, 1))
    bias_s = _idx_sample(page_table, qi4, wcol, qpos, kinew_t, kidx_t, ts, ga)
    b_s = _attn_sample(page_table, seq_major(q_s), bias_s, new_t(k_s, B_KV_HEADS), new_t(v_s, B_KV_HEADS), k_t, v_t)
    b_s_tm = b_s.transpose(1, 0, 2).reshape(r_s, B_WIDTH)
    x1_s, h2_s, route_s = _mid(x_tm, a_s, b_s_tm, mod_s.reshape(1, db, 6 * d), ts, db, w_out_b, *ln1, w_router, alpha)

    n_all = n_p + r_s
    tok_p = min(TOKEN_TILE, seq)
    rank_tile = min(RANK_TILE, n_all)
    assert n_all % rank_tile == 0 and n_p % tok_p == 0 and r_s % tok_p == 0 and n_p % db == 0
    route_all = jnp.concatenate([route_p, route_s], axis=1)
    pos, meta, te, n_tiles = _plan(route_all, rank_tile, MOE_TILE)
    xs = _dispatch(meta, pos, h2_p, h2_s, tok_p, MOE_TILE, n_tiles)
    ys = _experts(te, meta, xs, w_gate[l], w_up[l], w_down[l], MOE_TILE, n_tiles)
    y_p = _combine(pos, route_all, 0, x1_p, mod_p.reshape(bsz, 1, 6 * d), seq // tok_p, tok_p, *ln2, ys, alpha)
    y_s_tm = _combine(pos, route_all, n_p // db, x1_s, mod_s.reshape(1, db, 6 * d), ts, db, *ln2, ys, alpha)
    y_s = y_s_tm.reshape(ts, db, d).transpose(1, 0, 2)

    kv5 = lambda a, n, t: a.reshape(1, n, t, B_KV_HEADS, HEAD_DIM)
    kv5_t = lambda a: a.reshape(1, bsz, B_KV_HEADS, HEAD_DIM, seq).transpose(0, 1, 4, 2, 3)
    return (y_p.reshape(bsz, seq, d), y_s,
            kv5_t(kt_p), kv5_t(vt32_p), kit_p.transpose(0, 2, 1)[None],
            kv5(seq_major(k_s), db, ts), kv5(seq_major(v_s), db, ts), seq_major(kiwi_s)[..., :IDX_DIM][None],
            seq_major(vg_s).reshape(1, db, ts, A_HEADS, HEAD_DIM))
```

```python
import functools

import jax
import jax.numpy as jnp
import numpy as np
from jax import lax
from jax.experimental import pallas as pl
from jax.experimental.pallas import tpu as pltpu

F32 = jnp.float32
BF16 = jnp.bfloat16
I32 = jnp.int32

D_MODEL = 1024
HEAD_DIM = 64
A_HEADS = 8
A_WIDTH = A_HEADS * HEAD_DIM
CHUNK = 128
B_HEADS = 8
B_KV_HEADS = 4
B_GROUP = B_HEADS // B_KV_HEADS
B_WIDTH = B_HEADS * HEAD_DIM
KV_WIDTH = B_KV_HEADS * HEAD_DIM
IDX_HEADS = 4
IDX_DIM = 64
IDX_WIDTH = IDX_HEADS * IDX_DIM
TOPK_MAX = 256
ROPE_THETA = 500000.0
ROT_DIM = HEAD_DIM // 4
ROT_HALF = ROT_DIM // 2
ATTN_SCALE = HEAD_DIM ** -0.5
LOG2_E = 1.4426950408889634
Q_SCALE = ATTN_SCALE * LOG2_E
DENOM_ROWS = 16
N_GROUPS = 4
EXPERTS_PER_GROUP = 8
N_EXPERTS = N_GROUPS * EXPERTS_PER_GROUP
D_EXPERT = 512
LN_EPS = 1e-5

LANES = 128
SUBLANES = 8
ROW_CHUNKS = D_MODEL // LANES

C_AU, C_AV, C_Q, C_K, C_V, C_QI, C_KI = 0, 512, 1024, 1536, 1792, 2048, 2304
IN_WIDTH = C_KI + IDX_DIM + IDX_HEADS
IN_PAD = 2432
WI_LANE = IDX_DIM

INT_MIN = -(2 ** 31)
INT_MAX = 2 ** 31 - 1
VMEM_LIMIT = 48 * 1024 * 1024

PROMPT_TILE = 256
MOE_TILE = 256
TOKEN_TILE = 256
RANK_TILE = 512
IDX_SEQS = 16
STEPS_PER_CHECK = 4
ROW_DMA_UNROLL = 8
MID_TILE = 512


def _dot(a, b):
    return jnp.dot(a, b, preferred_element_type=F32)


def _dot_nt(a, b):
    return lax.dot_general(a, b, (((1,), (1,)), ((), ())), preferred_element_type=F32)


def _ada_kernel(c_ref, w_ref, b_ref, o_ref):
    s = jax.nn.silu(c_ref[...]).astype(BF16)
    o_ref[...] = _dot(s, w_ref[...].astype(BF16)) + b_ref[...]


def _ada(c_all, w_ada, b_ada):
    rows = c_all.shape[0]
    n_out = w_ada.shape[1]
    tn = 1024
    return pl.pallas_call(
        _ada_kernel,
        grid=(n_out // tn,),
        in_specs=[
            pl.BlockSpec((rows, D_MODEL), lambda j: (0, 0)),
            pl.BlockSpec((D_MODEL, tn), lambda j: (0, j)),
            pl.BlockSpec((1, tn), lambda j: (0, j)),
        ],
        out_specs=pl.BlockSpec((rows, tn), lambda j: (0, j)),
        out_shape=jax.ShapeDtypeStruct((rows, n_out), F32),
        compiler_params=pltpu.CompilerParams(
            dimension_semantics=("arbitrary",), vmem_limit_bytes=VMEM_LIMIT),
        name="ada",
    )(c_all, w_ada, b_ada)


def _rotate(x, rt, blk):
    c = rt[:, 0:LANES]
    s_lo = rt[:, LANES:2 * LANES]
    s_hi = rt[:, 2 * LANES:3 * LANES]
    if blk == 1:
        head = lax.broadcasted_iota(I32, c.shape, 1) < IDX_DIM
        c = jnp.where(head, c, 1.0)
        s_lo = jnp.where(head, s_lo, 0.0)
        s_hi = jnp.where(head, s_hi, 0.0)
    outs = []
    for j in range(x.shape[1] // LANES):
        xb = x[:, j * LANES:(j + 1) * LANES]
        up = pltpu.roll(xb, LANES - ROT_HALF, 1)
        dn = pltpu.roll(xb, ROT_HALF, 1)
        outs.append(xb * c + up * s_lo + dn * s_hi)
    return outs[0] if len(outs) == 1 else jnp.concatenate(outs, axis=1)


def _head_ln(g, mavg, gam, bet):
    def seg_mean(x):
        parts = [_dot(x[:, j * 256:(j + 1) * 256].astype(BF16), mavg) for j in range(A_WIDTH // 256)]
        return jnp.concatenate(parts, axis=1)
    mu = seg_mean(g)
    d = g - mu
    var = seg_mean(d * d)
    return d * lax.rsqrt(var + LN_EPS) * gam + bet


def _project(h, w_ref, rt, mavg, gam, bet):
    u = jax.nn.gelu(_dot(h, w_ref[:, C_AU:C_AV]))
    vg = _head_ln(jax.nn.gelu(_dot(h, w_ref[:, C_AV:C_Q])), mavg, gam, bet)
    q = _rotate(_dot(h, w_ref[:, C_Q:C_K]), rt, 0) * Q_SCALE
    k = _rotate(_dot(h, w_ref[:, C_K:C_V]), rt, 0)
    v = _dot(h, w_ref[:, C_V:C_QI])
    qi = _rotate(_dot(h, w_ref[:, C_QI:C_KI]), rt, 0)
    kiwi = _rotate(_dot(h, w_ref[:, C_KI:IN_PAD]), rt, 1)
    return u, vg, q, k, v, qi, kiwi


def _front_prompt_kernel(x_ref, mod_ref, w_ref, rt_ref, mavg_ref, gam_ref, bet_ref, wsp_ref, bsp_ref,
                         a_ref, kt_ref, vt_ref, kit_ref, kb_ref, kiwib_ref, vtb_ref, qt_ref, qit_ref, wit_ref):
    shift = mod_ref[0:1, :]
    scale = mod_ref[1:2, :]
    h = (x_ref[...] * (1.0 + scale) + shift).astype(BF16)
    u, vg, q, k, v, qi, kiwi = _project(h, w_ref, rt_ref[...], mavg_ref[...], gam_ref[...], bet_ref[...])
    v_t = jnp.transpose(v)
    kiwi_t = jnp.transpose(kiwi)
    kt_ref[...] = jnp.transpose(k)
    vt_ref[...] = v_t
    kit_ref[...] = kiwi_t[0:IDX_DIM, :]
    kb_ref[...] = k.astype(BF16)
    kiwib_ref[...] = kiwi.astype(BF16)
    vtb_ref[...] = v_t.astype(BF16)
    qt_ref[...] = jnp.transpose(q).astype(BF16)
    qit_ref[...] = jnp.transpose(qi).astype(BF16)
    wit_ref[...] = kiwi_t[WI_LANE:WI_LANE + SUBLANES, :]

    rows = lax.broadcasted_iota(I32, (CHUNK, 2 * CHUNK), 0)
    cols = lax.broadcasted_iota(I32, (CHUNK, 2 * CHUNK), 1) % CHUNK
    causal = cols <= rows
    lane = lax.broadcasted_iota(I32, (CHUNK, LANES), 1)
    tt = x_ref.shape[0]
    for cidx in range(tt // CHUNK):
        rs = slice(cidx * CHUNK, (cidx + 1) * CHUNK)
        blocks = []
        for p in range(A_HEADS // 2):
            wcat = jnp.where(causal, wsp_ref[p], 0.0).astype(BF16)
            vb = vg[rs, p * LANES:(p + 1) * LANES]
            rhs = jnp.concatenate([jnp.where(lane < HEAD_DIM, vb, 0.0),
                                   jnp.where(lane >= HEAD_DIM, vb, 0.0)], axis=0).astype(BF16)
            blocks.append(_dot(wcat, rhs))
        s = jnp.concatenate(blocks, axis=1) + bsp_ref[...]
        a_ref[rs, :] = (u[rs, :] * s).astype(BF16)


def _front_prompt(x, mod, w_in, rt, mavg, gam, bet, wsp, bsp, tt):
    b, t, _ = x.shape
    nc = t // tt
    tok = lambda w: pl.BlockSpec((None, tt, w), lambda i, j: (i, j, 0))
    tr = lambda r: pl.BlockSpec((None, None, r, tt), lambda i, j: (i, j, 0, 0))
    pos_minor = lambda r: pl.BlockSpec((None, r, tt), lambda i, j: (i, 0, j))
    const2 = lambda a: pl.BlockSpec(a.shape, lambda i, j: (0,) * a.ndim)
    out_shapes = (
        jax.ShapeDtypeStruct((b, t, A_WIDTH), BF16),
        jax.ShapeDtypeStruct((b, KV_WIDTH, t), F32),
        jax.ShapeDtypeStruct((b, KV_WIDTH, t), F32),
        jax.ShapeDtypeStruct((b, IDX_DIM, t), F32),
        jax.ShapeDtypeStruct((b, t, KV_WIDTH), BF16),
        jax.ShapeDtypeStruct((b, t, LANES), BF16),
        jax.ShapeDtypeStruct((b, nc, KV_WIDTH, tt), BF16),
        jax.ShapeDtypeStruct((b, nc, B_WIDTH, tt), BF16),
        jax.ShapeDtypeStruct((b, nc, IDX_WIDTH, tt), BF16),
        jax.ShapeDtypeStruct((b, nc, SUBLANES, tt), F32),
    )
    return pl.pallas_call(
        _front_prompt_kernel,
        grid=(b, nc),
        in_specs=[
            tok(D_MODEL),
            pl.BlockSpec((None, 6, D_MODEL), lambda i, j: (i, 0, 0)),
            const2(w_in),
            pl.BlockSpec((tt, 3 * LANES), lambda i, j: (j, 0)),
            const2(mavg), const2(gam), const2(bet), const2(wsp), const2(bsp),
        ],
        out_specs=[tok(A_WIDTH), pos_minor(KV_WIDTH), pos_minor(KV_WIDTH), pos_minor(IDX_DIM), tok(KV_WIDTH),
                   tok(LANES), tr(KV_WIDTH), tr(B_WIDTH), tr(IDX_WIDTH), tr(SUBLANES)],
        out_shape=out_shapes,
        compiler_params=pltpu.CompilerParams(
            dimension_semantics=("arbitrary", "arbitrary"), vmem_limit_bytes=VMEM_LIMIT),
        name="front_prompt",
    )(x, mod, w_in, rt, mavg, gam, bet, wsp, bsp)


def _order_key(score, kpos, idx_bits):
    offs = 1 << idx_bits
    bits = pltpu.bitcast(score, I32)
    key = bits ^ ((bits >> 31) & 0x7FFFFFFF)
    key = key + jnp.where(score > 0.0, offs, 0)
    return jnp.where(score == 0.0, offs - kpos, key)


def _select_threshold(count_ge, count_gt_eq_lt, shape, topk, idx_bits):
    offs = 1 << idx_bits
    first_candidates = (offs + 1, 1)
    max_steps = -(-(32 + len(first_candidates)) // STEPS_PER_CHECK) * STEPS_PER_CHECK

    def unresolved(carry):
        i, lo, hi, n_lo = carry
        pending = (n_lo != topk) & (hi - 1 > lo)
        return (i < max_steps) & (jnp.max(pending.astype(F32)) > 0.0)

    def bisect(carry):
        i, lo, hi, n_lo = carry
        for _ in range(STEPS_PER_CHECK):
            mid = (lo >> 1) + (hi >> 1) + (lo & hi & 1)
            for step, value in enumerate(first_candidates):
                forced = jnp.where(i == step, value, INT_MIN)
                mid = jnp.where((lo < forced) & (forced < hi), forced, mid)
            tot = count_ge(mid)
            take = tot >= topk
            lo = jnp.where(take, mid, lo)
            n_lo = jnp.where(take, tot, n_lo)
            hi = jnp.where(take, hi, mid)
            i = i + 1
        return i, lo, hi, n_lo

    never = jnp.full(shape, INT_MAX, I32)
    _, thr, _, n_lo = lax.while_loop(
        unresolved, bisect, (jnp.int32(0), jnp.full(shape, INT_MIN + 1, I32), jnp.full(shape, INT_MAX, I32), never))
    tied = (n_lo > topk) & (n_lo != INT_MAX)
    big = jnp.full(shape, offs, I32)

    def resolve_ties(_):
        n_gt, _ = count_gt_eq_lt(thr, big)
        need = topk - n_gt

        def idx_step(i, cut):
            cand = cut | lax.shift_left(jnp.int32(1), idx_bits - 1 - i)
            _, n_eq = count_gt_eq_lt(thr, cand)
            return jnp.where(n_eq <= need, cand, cut)

        cut = lax.fori_loop(0, idx_bits, idx_step, jnp.zeros(shape, I32))
        return jnp.where(tied, cut, big)

    any_tied = jnp.max(tied.astype(F32)) > 0.0
    cut = lax.cond(any_tied, resolve_ties, lambda _: big, 0)
    return thr, cut


def _fold_rows(x):
    acc = x[0:SUBLANES]
    for r in range(1, x.shape[0] // SUBLANES):
        acc = acc + x[r * SUBLANES:(r + 1) * SUBLANES]
    return acc


def _col_total(cnt8):
    return jnp.sum(cnt8.astype(F32), axis=0, keepdims=True).astype(I32)


def _attn_prompt_kernel(qit_ref, wit_ref, qt_ref, kiwib_ref, kb_ref, vt_ref, o_ref,
                        key_ref, w4_ref, wq_ref, s0_ref, s1_ref, bias_ref, m_ref, acc_ref, *, tq, topk, idx_bits):
    j = pl.program_id(1)
    kc = tq
    n_kc = j + 1

    @pl.when((pl.program_id(0) == 0) & (j == 0))
    def _():
        w4_ref[...] = jnp.zeros(w4_ref.shape, BF16)
        wq_ref[...] = jnp.zeros(wq_ref.shape, BF16)

    for h in range(IDX_HEADS):
        w4_ref[0:IDX_DIM, h * tq:(h + 1) * tq] = qit_ref[h * IDX_DIM:(h + 1) * IDX_DIM, :]
    for h in range(B_HEADS):
        n = h // B_GROUP
        wq_ref[h, n * HEAD_DIM:(n + 1) * HEAD_DIM, :] = qt_ref[h * HEAD_DIM:(h + 1) * HEAD_DIM, :]

    wit = wit_ref[...]
    krow = lax.broadcasted_iota(I32, (kc, tq), 0)
    qpos = j * tq + lax.broadcasted_iota(I32, (kc, tq), 1)

    n_chunks = key_ref.shape[0] - 1

    def score_pair(i, _):
        for c in (2 * i, 2 * i + 1):
            rows = pl.multiple_of(jnp.minimum(c, n_chunks - 1) * kc, kc)
            s = _dot(kiwib_ref[pl.ds(rows, kc), :], w4_ref[...])
            tot = jnp.maximum(s[:, 0:tq], 0.0) * wit[0:1, :]
            for h in range(1, IDX_HEADS):
                tot = tot + jnp.maximum(s[:, h * tq:(h + 1) * tq], 0.0) * wit[h:h + 1, :]
            kpos = c * kc + krow
            key_ref[c] = jnp.where(kpos <= qpos, _order_key(tot, kpos, idx_bits), INT_MIN)
        return 0

    lax.fori_loop(0, (n_kc + 1) // 2, score_pair, 0)

    def count_ge(cand):
        def body(i, cnt):
            for c in (2 * i, 2 * i + 1):
                cnt = cnt + _fold_rows(jnp.where(key_ref[c] >= cand, 1, 0))
            return cnt
        return _col_total(lax.fori_loop(0, (n_kc + 1) // 2, body, jnp.zeros((SUBLANES, tq), I32)))

    def count_gt_eq_lt(thr, pos):
        def body(c, carry):
            n_gt, n_eq = carry
            key = key_ref[c]
            n_gt = n_gt + _fold_rows(jnp.where(key > thr, 1, 0))
            n_eq = n_eq + _fold_rows(jnp.where((key == thr) & (c * kc + krow < pos), 1, 0))
            return n_gt, n_eq
        z = jnp.zeros((SUBLANES, tq), I32)
        n_gt, n_eq = lax.fori_loop(0, n_kc, body, (z, z))
        return _col_total(n_gt), _col_total(n_eq)

    thr, cut = _select_threshold(count_ge, count_gt_eq_lt, (1, tq), topk, idx_bits)

    m_ref[...] = jnp.full(m_ref.shape, jnp.finfo(F32).min, F32)
    acc_ref[...] = jnp.zeros(acc_ref.shape, F32)

    def key_block(c):
        return kb_ref[pl.ds(pl.multiple_of(c * kc, kc), kc), :]

    def selection_bias(c, open_bias):
        key = key_ref[c]
        sel = (key > thr) | ((key == thr) & (c * kc + krow < cut))
        bias_ref[...] = jnp.where(sel, open_bias, -jnp.inf)

    selection_bias(0, 0.0)
    kblk0 = key_block(0)
    for h in range(B_HEADS):
        s0_ref[h] = _dot(kblk0, wq_ref[h]) + bias_ref[...]

    def stage(c, src, dst, c_next, open_next):
        selection_bias(c_next, open_next)
        kblk_next = key_block(c_next)
        vtc = vt_ref[c]
        for h in range(B_HEADS):
            n = h // B_GROUP
            dst[h] = _dot(kblk_next, wq_ref[h]) + bias_ref[...]
            s = src[h]
            m_old = m_ref[h]
            m_new = jnp.maximum(m_old, jnp.max(s, axis=0, keepdims=True))
            p = jnp.exp2(s - m_new).astype(BF16)
            v_aug = jnp.concatenate([vtc[n * HEAD_DIM:(n + 1) * HEAD_DIM, :], ones_rows], axis=0)
            acc_ref[h] = jnp.exp2(m_old - m_new) * acc_ref[h] + _dot(v_aug, p)
            m_ref[h] = m_new

    ones_rows = jnp.where(lax.broadcasted_iota(I32, (DENOM_ROWS, kc), 0) == 0, 1.0, 0.0).astype(BF16)
    last = n_kc - 1

    def attend_pair(i, _):
        c0 = 2 * i
        c1 = jnp.minimum(c0 + 1, last)
        stage(c0, s0_ref, s1_ref, c1, jnp.where(c0 + 1 <= last, 0.0, -jnp.inf))
        stage(c1, s1_ref, s0_ref, jnp.minimum(c0 + 2, last), 0.0)
        return 0

    lax.fori_loop(0, (n_kc + 1) // 2, attend_pair, 0)
    out_t = jnp.concatenate([acc_ref[h, 0:HEAD_DIM, :] / acc_ref[h, HEAD_DIM:HEAD_DIM + 1, :]
                             for h in range(B_HEADS)], axis=0)
    o_ref[...] = jnp.transpose(out_t).astype(BF16)


def _attn_prompt(qit, wit, qt, kiwib, kb, vt):
    b, nc, _, tq = qt.shape
    t = nc * tq
    topk = min(TOPK_MAX, t // 4)
    idx_bits = max(1, (t - 1).bit_length())
    tr = lambda r: pl.BlockSpec((None, None, r, tq), lambda i, j: (i, j, 0, 0))
    full = lambda w: pl.BlockSpec((None, t, w), lambda i, j: (i, 0, 0))
    kernel = functools.partial(_attn_prompt_kernel, tq=tq, topk=topk, idx_bits=idx_bits)
    return pl.pallas_call(
        kernel,
        grid=(b, nc),
        in_specs=[tr(IDX_WIDTH), tr(SUBLANES), tr(B_WIDTH), full(LANES), full(KV_WIDTH),
                  pl.BlockSpec((None, nc, KV_WIDTH, tq), lambda i, j: (i, 0, 0, 0))],
        out_specs=pl.BlockSpec((None, tq, B_WIDTH), lambda i, j: (i, j, 0)),
        out_shape=jax.ShapeDtypeStruct((b, t, B_WIDTH), BF16),
        scratch_shapes=[
            pltpu.VMEM((nc + 1, tq, tq), I32),
            pltpu.VMEM((LANES, IDX_HEADS * tq), BF16),
            pltpu.VMEM((B_HEADS, KV_WIDTH, tq), BF16),
            pltpu.VMEM((B_HEADS, tq, tq), F32),
            pltpu.VMEM((B_HEADS, tq, tq), F32),
            pltpu.VMEM((tq, tq), F32),
            pltpu.VMEM((B_HEADS, 1, tq), F32),
            pltpu.VMEM((B_HEADS, HEAD_DIM + DENOM_ROWS, tq), F32),
        ],
        compiler_params=pltpu.CompilerParams(
            dimension_semantics=("arbitrary", "arbitrary"), vmem_limit_bytes=VMEM_LIMIT),
        name="attn_prompt",
    )(qit, wit, qt, kiwib, kb, vt)


def _rot_tables(pos):
    r = pos.shape[0]
    inv_freq = np.float32(ROPE_THETA) ** (-np.arange(ROT_HALF, dtype=np.float32) * np.float32(2.0) / np.float32(ROT_DIM))
    ang = pos.astype(np.float32)[:, None] * inv_freq[None, :]
    cos, sin = np.cos(ang), np.sin(ang)
    rest = HEAD_DIM - ROT_DIM
    c64 = np.concatenate([cos, cos, np.ones((r, rest), np.float32)], axis=1)
    lo64 = np.concatenate([-sin, np.zeros((r, HEAD_DIM - ROT_HALF), np.float32)], axis=1)
    hi64 = np.concatenate([np.zeros((r, ROT_HALF), np.float32), sin, np.zeros((r, rest), np.float32)], axis=1)
    return jnp.asarray(np.concatenate([c64, c64, lo64, lo64, hi64, hi64], axis=1).astype(np.float32))


def _head_avg_matrix():
    return jnp.kron(jnp.eye(256 // HEAD_DIM, dtype=F32), jnp.full((HEAD_DIM, HEAD_DIM), 1.0 / HEAD_DIM, F32)).astype(BF16)


def _pad_w_in(w_in):
    return jnp.pad(w_in, ((0, 0), (0, IN_PAD - IN_WIDTH))).astype(BF16)


def _front_sample_kernel(x_ref, shift_ref, scale_ref, w_ref, rt_ref, mavg_ref, gam_ref, bet_ref, wl_ref, bl_ref,
                         a_ref, q_ref, k_ref, v_ref, qi_ref, kiwi_ref, vg_ref, *, ts, db):
    one_scale = 1.0 + scale_ref[...]
    shift = shift_ref[...]
    h = jnp.concatenate([x_ref[t * db:(t + 1) * db, :] * one_scale + shift for t in range(ts)], axis=0).astype(BF16)
    u, vg, q, k, v, qi, kiwi = _project(h, w_ref, rt_ref[...], mavg_ref[...], gam_ref[...], bet_ref[...])
    q_ref[...] = q.astype(BF16)
    k_ref[...] = k
    v_ref[...] = v
    qi_ref[...] = qi.astype(BF16)
    kiwi_ref[...] = kiwi
    vg_ref[...] = vg
    for t in range(ts):
        s = bl_ref[t:t + 1, :]
        for src in range(t + 1):
            s = s + wl_ref[t * ts + src:t * ts + src + 1, :] * vg[src * db:(src + 1) * db, :]
        a_ref[t * db:(t + 1) * db, :] = (u[t * db:(t + 1) * db, :] * s).astype(BF16)


def _front_sample(x_tm, mod_s, w_in, rt, mavg, gam, bet, wl, bl, ts, db):
    r = ts * db
    whole = lambda a: pl.BlockSpec(a.shape, lambda i: (0,) * a.ndim)
    out = lambda w, dt: jax.ShapeDtypeStruct((r, w), dt)
    outs = (out(A_WIDTH, BF16), out(B_WIDTH, BF16), out(KV_WIDTH, F32), out(KV_WIDTH, F32),
            out(IDX_WIDTH, BF16), out(LANES, F32), out(A_WIDTH, F32))
    return pl.pallas_call(
        functools.partial(_front_sample_kernel, ts=ts, db=db),
        grid=(1,),
        in_specs=[
            whole(x_tm),
            pl.BlockSpec((db, D_MODEL), lambda i: (0, 0)),
            pl.BlockSpec((db, D_MODEL), lambda i: (0, 1)),
            whole(w_in), whole(rt), whole(mavg), whole(gam), whole(bet), whole(wl), whole(bl),
        ],
        out_specs=[pl.BlockSpec((r, s.shape[1]), lambda i: (0, 0)) for s in outs],
        out_shape=outs,
        compiler_params=pltpu.CompilerParams(dimension_semantics=("arbitrary",), vmem_limit_bytes=VMEM_LIMIT),
        name="front_sample",
    )(x_tm, mod_s, mod_s, w_in, rt, mavg, gam, bet, wl, bl)


def _idx_sample_kernel(pt_ref, qi4_ref, wcol_ref, qpos_ref, kinew_ref, kidx_hbm, bias_ref,
                       ki_buf, key_ref, sem, *, ga, ts, n_pages, page, lpad, topk, idx_bits):
    i = pl.program_id(0)
    n_steps = pl.num_programs(0)
    past = n_pages * page
    rows = ga * ts
    slot = i % 2

    def page_copy(step, to_slot, g, p):
        phys = pt_ref[step * ga + g, p]
        return pltpu.make_async_copy(kidx_hbm.at[phys], ki_buf.at[to_slot, g, :, pl.ds(p * page, page)],
                                     sem.at[to_slot])

    def start_all(step, to_slot):
        for g in range(ga):
            for p in range(n_pages):
                page_copy(step, to_slot, g, p).start()

    @pl.when(i == 0)
    def _():
        start_all(0, 0)

    @pl.when(i + 1 < n_steps)
    def _():
        start_all(i + 1, 1 - slot)

    ki_buf[slot, :, :, pl.ds(past, LANES)] = kinew_ref[...]
    for g in range(ga):
        for p in range(n_pages):
            page_copy(i, slot, g, p).wait()

    kpos = lax.broadcasted_iota(I32, (ts, lpad), 1)
    for g in range(ga):
        s = _dot(qi4_ref[g], ki_buf[slot, g].astype(BF16))
        r = jnp.maximum(s, 0.0) * wcol_ref[g]
        tot = r[0:ts]
        for h in range(1, IDX_HEADS):
            tot = tot + r[h * ts:(h + 1) * ts]
        adm = (kpos <= qpos_ref[g * ts:(g + 1) * ts, :]) & (kpos < past + ts)
        key_ref[g * ts:(g + 1) * ts, :] = jnp.where(adm, _order_key(tot, kpos, idx_bits), INT_MIN)

    sub = lpad // LANES
    lane128 = lax.broadcasted_iota(I32, (rows, LANES), 1)

    def row_total(cnt):
        tot = jnp.sum(cnt.astype(F32), axis=1, keepdims=True)
        return jnp.broadcast_to(tot, cnt.shape).astype(I32)

    def count_ge(cand):
        cnt = jnp.zeros((rows, LANES), I32)
        for s_ in range(sub):
            cnt = cnt + jnp.where(key_ref[:, s_ * LANES:(s_ + 1) * LANES] >= cand, 1, 0)
        return row_total(cnt)

    def count_gt_eq_lt(thr, pos):
        n_gt = jnp.zeros((rows, LANES), I32)
        n_eq = jnp.zeros((rows, LANES), I32)
        for s_ in range(sub):
            kk = key_ref[:, s_ * LANES:(s_ + 1) * LANES]
            n_gt = n_gt + jnp.where(kk > thr, 1, 0)
            n_eq = n_eq + jnp.where((kk == thr) & (s_ * LANES + lane128 < pos), 1, 0)
        return row_total(n_gt), row_total(n_eq)

    thr, cut = _select_threshold(count_ge, count_gt_eq_lt, (rows, LANES), topk, idx_bits)
    for s_ in range(sub):
        kk = key_ref[:, s_ * LANES:(s_ + 1) * LANES]
        sel = (kk > thr) | ((kk == thr) & (s_ * LANES + lane128 < cut))
        bias = jnp.where(sel, 0.0, -jnp.inf)
        for g in range(ga):
            bias_ref[g, :, s_ * LANES:(s_ + 1) * LANES] = bias[g * ts:(g + 1) * ts]


def _idx_sample(page_table, qi4, wcol, qpos, kinew_t, kidx_t, ts, ga):
    db = qi4.shape[0]
    n_pages = page_table.shape[1]
    page = kidx_t.shape[2]
    past = n_pages * page
    lpad = past + LANES
    topk = min(TOPK_MAX, (past + ts) // 4)
    idx_bits = max(1, (lpad - 1).bit_length())
    kernel = functools.partial(_idx_sample_kernel, ga=ga, ts=ts, n_pages=n_pages, page=page, lpad=lpad,
                               topk=topk, idx_bits=idx_bits)
    return pl.pallas_call(
        kernel,
        grid_spec=pltpu.PrefetchScalarGridSpec(
            num_scalar_prefetch=1,
            grid=(db // ga,),
            in_specs=[pl.BlockSpec((ga, IDX_HEADS * ts, IDX_DIM), lambda i, pt: (i, 0, 0)),
                      pl.BlockSpec((ga, IDX_HEADS * ts, 1), lambda i, pt: (i, 0, 0)),
                      pl.BlockSpec((ga * ts, 1), lambda i, pt: (i, 0)),
                      pl.BlockSpec((ga, IDX_DIM, LANES), lambda i, pt: (i, 0, 0)),
                      pl.BlockSpec(memory_space=pl.ANY)],
            out_specs=pl.BlockSpec((ga, ts, lpad), lambda i, pt: (i, 0, 0)),
            scratch_shapes=[
                pltpu.VMEM((2, ga, IDX_DIM, lpad), F32),
                pltpu.VMEM((ga * ts, lpad), I32),
                pltpu.SemaphoreType.DMA((2,)),
            ]),
        out_shape=jax.ShapeDtypeStruct((db, ts, lpad), F32),
        compiler_params=pltpu.CompilerParams(dimension_semantics=("arbitrary",), vmem_limit_bytes=VMEM_LIMIT),
        name="idx_sample",
    )(page_table, qi4, wcol, qpos, kinew_t, kidx_t)


def _attn_sample_kernel(pt_ref, q_ref, bias_ref, knew_ref, vnew_ref, k_hbm, v_hbm, o_ref,
                        k_buf, v_buf, sem, *, ts, n_pages, page):
    b = pl.program_id(0)
    nb = pl.num_programs(0)
    past = n_pages * page
    slot = b % 2

    def page_copies(seq, to_slot, p):
        phys = pt_ref[seq, p]
        dst = pl.ds(p * page, page)
        return (pltpu.make_async_copy(k_hbm.at[phys], k_buf.at[to_slot, :, :, dst], sem.at[0, to_slot]),
                pltpu.make_async_copy(v_hbm.at[phys], v_buf.at[to_slot, :, :, dst], sem.at[1, to_slot]))

    def start_all(seq, to_slot):
        for p in range(n_pages):
            for cp in page_copies(seq, to_slot, p):
                cp.start()

    @pl.when(b == 0)
    def _():
        start_all(0, 0)

    @pl.when(b + 1 < nb)
    def _():
        start_all(b + 1, 1 - slot)

    steps8, lanes = lax.broadcasted_iota(I32, (SUBLANES, LANES), 0), lax.broadcasted_iota(I32, (SUBLANES, LANES), 1)
    place = jnp.where(steps8 == lanes, 1.0, 0.0).astype(BF16)
    for new_ref, buf in ((knew_ref, k_buf), (vnew_ref, v_buf)):
        new = new_ref[...].astype(BF16)
        for n in range(B_KV_HEADS):
            buf[slot, n, :, pl.ds(past, LANES)] = lax.dot_general(
                new[:, n * HEAD_DIM:(n + 1) * HEAD_DIM], place, (((0,), (0,)), ((), ())), preferred_element_type=F32)
    for p in range(n_pages):
        for cp in page_copies(b, slot, p):
            cp.wait()

    q = q_ref[...]
    bias = bias_ref[...]
    bias2 = jnp.concatenate([bias] * B_GROUP, axis=0)
    outs = [None] * B_HEADS
    for n in range(B_KV_HEADS):
        kt = k_buf[slot, n].astype(BF16)
        vt = v_buf[slot, n].astype(BF16)
        qs = jnp.concatenate([q[:, (n * B_GROUP + g) * HEAD_DIM:(n * B_GROUP + g + 1) * HEAD_DIM]
                              for g in range(B_GROUP)], axis=0)
        sc = _dot(qs, kt) + bias2
        m = jnp.max(sc, axis=1, keepdims=True)
        p_ = jnp.exp2(sc - m)
        o = _dot_nt(p_.astype(BF16), vt) / jnp.sum(p_, axis=1, keepdims=True)
        for g in range(B_GROUP):
            outs[n * B_GROUP + g] = o[g * ts:(g + 1) * ts]
    o_ref[...] = jnp.concatenate(outs, axis=1).astype(BF16)


def _attn_sample(page_table, q, bias, knew_t, vnew_t, k_t, v_t):
    db, ts, _ = q.shape
    n_pages = page_table.shape[1]
    page = k_t.shape[3]
    lpad = bias.shape[2]
    seq3 = lambda a: pl.BlockSpec((None,) + a.shape[1:], lambda b, pt: (b,) + (0,) * (a.ndim - 1))
    anyspec = pl.BlockSpec(memory_space=pl.ANY)
    kernel = functools.partial(_attn_sample_kernel, ts=ts, n_pages=n_pages, page=page)
    return pl.pallas_call(
        kernel,
        grid_spec=pltpu.PrefetchScalarGridSpec(
            num_scalar_prefetch=1,
            grid=(db,),
            in_specs=[seq3(q), seq3(bias), seq3(knew_t), seq3(vnew_t), anyspec, anyspec],
            out_specs=pl.BlockSpec((None, ts, B_WIDTH), lambda b, pt: (b, 0, 0)),
            scratch_shapes=[
                pltpu.VMEM((2, B_KV_HEADS, HEAD_DIM, lpad), F32),
                pltpu.VMEM((2, B_KV_HEADS, HEAD_DIM, lpad), F32),
                pltpu.SemaphoreType.DMA((2, 2)),
            ]),
        out_shape=jax.ShapeDtypeStruct((db, ts, B_WIDTH), BF16),
        compiler_params=pltpu.CompilerParams(dimension_semantics=("arbitrary",), vmem_limit_bytes=VMEM_LIMIT),
        name="attn_sample",
    )(page_table, q, bias, knew_t, vnew_t, k_t, v_t)


def _layer_norm_rows(y, gam, bet):
    mu = jnp.mean(y, axis=1, keepdims=True)
    d = y - mu
    var = jnp.mean(d * d, axis=1, keepdims=True)
    return d * lax.rsqrt(var + LN_EPS) * gam + bet


def _store_row_tiles(ref, val):
    r = val.shape[0]
    for c in range(ROW_CHUNKS):
        ref[pl.ds(c, r, stride=ROW_CHUNKS), :] = val[:, c * LANES:(c + 1) * LANES]


def _load_row_tiles(ref, r):
    return jnp.concatenate([ref[pl.ds(c, r, stride=ROW_CHUNKS), :] for c in range(ROW_CHUNKS)], axis=1)


ROUTER_ROWS = SUBLANES + N_EXPERTS


def _route(logits_t):
    r = logits_t.shape[1]
    far = float(LANES)

    def softmax_rows(x):
        e = jnp.exp(x - jnp.max(x, axis=0, keepdims=True))
        return e / jnp.sum(e, axis=0, keepdims=True)

    def first_max(p):
        rows = lax.broadcasted_iota(I32, p.shape, 0).astype(F32)
        best = jnp.max(p, axis=0, keepdims=True)
        return best, jnp.min(jnp.where(p == best, rows, far), axis=0, keepdims=True), rows

    g_w, g_sel, _ = first_max(softmax_rows(logits_t[0:N_GROUPS]))
    el = jnp.zeros((EXPERTS_PER_GROUP, r), F32)
    for g in range(N_GROUPS):
        lo = SUBLANES + g * EXPERTS_PER_GROUP
        el = jnp.where(g_sel == float(g), logits_t[lo:lo + EXPERTS_PER_GROUP], el)
    ep = softmax_rows(el)
    p1, i1, rows = first_max(ep)
    p2, i2, _ = first_max(jnp.where(rows == i1, -1.0, ep))
    denom = p1 + p2
    base = g_sel * float(EXPERTS_PER_GROUP)
    return jnp.concatenate([base + i1, base + i2, g_w * p1 / denom, g_w * p2 / denom,
                            jnp.zeros((SUBLANES - 4, r), F32)], axis=0)


def _split_bf16(x):
    hi = x.astype(BF16)
    return hi, (x - hi.astype(F32)).astype(BF16)


def _mid_kernel(x_ref, a_ref, b_ref, g1_ref, sh2_ref, sc2_ref, wo_ref, gam_ref, bet_ref, wr_ref,
                x1_ref, h2_ref, route_ref, *, alpha):
    mixed = _dot(a_ref[...], wo_ref[0:A_WIDTH, :]) + _dot(b_ref[...], wo_ref[A_WIDTH:A_WIDTH + B_WIDTH, :])
    x1 = _layer_norm_rows(alpha * x_ref[...] + g1_ref[...] * mixed, gam_ref[...], bet_ref[...])
    x1_ref[...] = x1
    h2 = x1 * (1.0 + sc2_ref[...]) + sh2_ref[...]
    _store_row_tiles(h2_ref, h2)
    w_hi, w_lo = _split_bf16(wr_ref[...])
    h_hi, h_lo = _split_bf16(h2)
    logits_t = _dot_nt(w_hi, h_hi) + _dot_nt(w_hi, h_lo) + _dot_nt(w_lo, h_hi)
    route_ref[...] = _route(logits_t)


def _mod_spec(mod3, comp, tiles_per_group):
    rm = mod3.shape[1]
    return pl.BlockSpec((None, rm, D_MODEL), lambda i: (i // tiles_per_group, 0, comp))


def _mid(x, a, b, mod3, tiles_per_group, tile, w_out, gam, bet, w_router, alpha):
    n = x.shape[0]
    rows = lambda w: pl.BlockSpec((tile, w), lambda i: (i, 0))
    whole = lambda arr: pl.BlockSpec(arr.shape, lambda i: (0,) * arr.ndim)
    return pl.pallas_call(
        functools.partial(_mid_kernel, alpha=alpha),
        grid=(n // tile,),
        in_specs=[rows(D_MODEL), rows(A_WIDTH), rows(B_WIDTH),
                  _mod_spec(mod3, 2, tiles_per_group), _mod_spec(mod3, 3, tiles_per_group),
                  _mod_spec(mod3, 4, tiles_per_group),
                  whole(w_out), whole(gam), whole(bet), whole(w_router)],
        out_specs=[rows(D_MODEL), pl.BlockSpec((tile * ROW_CHUNKS, LANES), lambda i: (i, 0)),
                   pl.BlockSpec((SUBLANES, tile), lambda i: (0, i))],
        out_shape=(jax.ShapeDtypeStruct((n, D_MODEL), F32),
                   jax.ShapeDtypeStruct((n * ROW_CHUNKS, LANES), F32),
                   jax.ShapeDtypeStruct((SUBLANES, n), F32)),
        compiler_params=pltpu.CompilerParams(dimension_semantics=("arbitrary",), vmem_limit_bytes=VMEM_LIMIT),
        name="mid",
    )(x, a, b, mod3, mod3, mod3, w_out, gam, bet, w_router)


META_CNT, META_START, META_END, META_NACT = 0, 1, 2, 3


def _plan_kernel(route_ref, pos_ref, meta_ref, te_ref, carry_ref, starts_ref, *, nt, tm):
    phase = pl.program_id(0)
    i = pl.program_id(1)

    @pl.when(i == 0)
    def _():
        carry_ref[...] = jnp.zeros(carry_ref.shape, F32)

    eid = route_ref[0:2, :].astype(I32)
    e_iota = lax.broadcasted_iota(I32, (N_EXPERTS, nt), 0)
    hit0 = eid[0:1, :] == e_iota
    hit1 = eid[1:2, :] == e_iota
    onehot = jnp.where(hit0 | hit1, 1.0, 0.0)

    @pl.when(phase == 0)
    def _():
        pos_ref[...] = jnp.zeros(pos_ref.shape, I32)
        carry_ref[...] = carry_ref[...] + jnp.sum(onehot, axis=1, keepdims=True)

        @pl.when(i == pl.num_programs(1) - 1)
        def _():
            cnt = carry_ref[...]
            tiles = jnp.floor((cnt + float(tm - 1)) * (1.0 / tm))
            r = lax.broadcasted_iota(I32, (N_EXPERTS, N_EXPERTS), 0)
            c = lax.broadcasted_iota(I32, (N_EXPERTS, N_EXPERTS), 1)
            lower = jnp.where(r > c, 1.0, 0.0).astype(BF16)
            tiles_before = _dot(lower, jnp.broadcast_to(tiles, (N_EXPERTS, LANES)).astype(BF16))[:, 0:1]
            tiles_end = tiles_before + tiles
            starts_ref[...] = tiles_before * float(tm)

            diag = (lax.broadcasted_iota(I32, (N_EXPERTS, LANES), 0)
                    == lax.broadcasted_iota(I32, (N_EXPERTS, LANES), 1))

            def as_row(col):
                return jnp.sum(jnp.where(diag, jnp.broadcast_to(col, (N_EXPERTS, LANES)), 0.0),
                               axis=0, keepdims=True)

            nact = jnp.broadcast_to(jnp.max(tiles_end, axis=0, keepdims=True), (1, LANES))
            meta_ref[...] = jnp.concatenate(
                [as_row(cnt), as_row(tiles_before * float(tm)), as_row(tiles_end * float(tm)), nact,
                 jnp.zeros((SUBLANES - 4, LANES), F32)], axis=0).astype(I32)
            tile_i = lax.broadcasted_iota(I32, (N_EXPERTS, te_ref.shape[1]), 1).astype(F32)
            owner = jnp.sum(jnp.where(tiles_end <= tile_i, 1.0, 0.0), axis=0, keepdims=True)
            te_ref[...] = jnp.minimum(owner, float(N_EXPERTS - 1)).astype(I32)

    @pl.when(phase == 1)
    def _():
        upper = (lax.broadcasted_iota(I32, (nt, nt), 0) <= lax.broadcasted_iota(I32, (nt, nt), 1))
        incl = _dot(onehot.astype(BF16), jnp.where(upper, 1.0, 0.0).astype(BF16))
        slot = starts_ref[...] + carry_ref[...] + incl - 1.0
        pos_ref[0:1, :] = jnp.sum(jnp.where(hit0, slot, 0.0), axis=0, keepdims=True).astype(I32)
        pos_ref[1:2, :] = jnp.sum(jnp.where(hit1, slot, 0.0), axis=0, keepdims=True).astype(I32)
        carry_ref[...] = carry_ref[...] + jnp.sum(onehot, axis=1, keepdims=True)


def _plan(route_t, nt, tm):
    n = route_t.shape[1]
    n_tiles = (2 * n) // tm + N_EXPERTS
    te_width = pl.cdiv(n_tiles, LANES) * LANES
    pos, meta, te = pl.pallas_call(
        functools.partial(_plan_kernel, nt=nt, tm=tm),
        grid=(2, n // nt),
        in_specs=[pl.BlockSpec((SUBLANES, nt), lambda p, i: (0, i))],
        out_specs=[pl.BlockSpec((2, nt), lambda p, i: (0, i * p)),
                   pl.BlockSpec((SUBLANES, LANES), lambda p, i: (0, 0)),
                   pl.BlockSpec((1, te_width), lambda p, i: (0, 0))],
        out_shape=(jax.ShapeDtypeStruct((2, n), I32), jax.ShapeDtypeStruct((SUBLANES, LANES), I32),
                   jax.ShapeDtypeStruct((1, te_width), I32)),
        scratch_shapes=[pltpu.VMEM((N_EXPERTS, 1), F32), pltpu.VMEM((N_EXPERTS, 1), F32)],
        compiler_params=pltpu.CompilerParams(dimension_semantics=("arbitrary", "arbitrary")),
        name="plan",
    )(route_t)
    return pos, meta, te, n_tiles


def _row_copy(src, src_row, dst, dst_row, sem):
    return pltpu.make_async_copy(src.at[pl.ds(src_row * ROW_CHUNKS, ROW_CHUNKS)],
                                 dst.at[pl.ds(dst_row * ROW_CHUNKS, ROW_CHUNKS)], sem)


def _dispatch_kernel(meta_ref, pos_ref, hp_ref, hs_ref, xs_out, zero_ref, sem, *, tt, blocks_p, tm, n_tiles):
    i = pl.program_id(0)

    def scatter(h_ref):
        def issue(j, _):
            for k in range(2):
                _row_copy(h_ref, j, xs_out, pos_ref[k, j], sem.at[k]).start()
            return 0

        def drain(j, _):
            for k in range(2):
                _row_copy(h_ref, j, xs_out, pos_ref[k, j], sem.at[k]).wait()
            return 0

        lax.fori_loop(0, tt, issue, 0, unroll=ROW_DMA_UNROLL)
        lax.fori_loop(0, tt, drain, 0, unroll=ROW_DMA_UNROLL)

    @pl.when(i < blocks_p)
    def _():
        scatter(hp_ref)

    @pl.when(i >= blocks_p)
    def _():
        scatter(hs_ref)

    @pl.when(i == pl.num_programs(0) - 1)
    def _():
        zero_ref[...] = jnp.zeros(zero_ref.shape, F32)

        def zero_rows(first_row, n_rows):
            return pltpu.make_async_copy(zero_ref.at[pl.ds(0, n_rows * ROW_CHUNKS)],
                                         xs_out.at[pl.ds(first_row * ROW_CHUNKS, n_rows * ROW_CHUNKS)], sem.at[0])

        def start_row(r, c):
            zero_rows(r, 1).start()
            return c

        def wait_row(r, c):
            zero_rows(r, 1).wait()
            return c

        def per_expert(row_fn):
            def body(e, c):
                lo = meta_ref[META_START, e] + meta_ref[META_CNT, e]
                return lax.fori_loop(lo, meta_ref[META_END, e], row_fn, c)
            return body

        def start_tile(t, c):
            zero_rows(t * tm, tm).start()
            return c

        def wait_tile(t, c):
            zero_rows(t * tm, tm).wait()
            return c

        nact = meta_ref[META_NACT, 0]
        lax.fori_loop(0, N_EXPERTS, per_expert(start_row), 0)
        lax.fori_loop(nact, n_tiles, start_tile, 0)
        lax.fori_loop(0, N_EXPERTS, per_expert(wait_row), 0)
        lax.fori_loop(nact, n_tiles, wait_tile, 0)


def _dispatch(meta, pos, h2_p, h2_s, tt, tm, n_tiles):
    blocks_p = h2_p.shape[0] // (tt * ROW_CHUNKS)
    blocks_s = h2_s.shape[0] // (tt * ROW_CHUNKS)
    return pl.pallas_call(
        functools.partial(_dispatch_kernel, tt=tt, blocks_p=blocks_p, tm=tm, n_tiles=n_tiles),
        grid_spec=pltpu.PrefetchScalarGridSpec(
            num_scalar_prefetch=1,
            grid=(blocks_p + blocks_s,),
            in_specs=[pl.BlockSpec((2, tt), lambda i, m: (0, i), memory_space=pltpu.SMEM),
                      pl.BlockSpec((tt * ROW_CHUNKS, LANES), lambda i, m: (jnp.minimum(i, blocks_p - 1), 0)),
                      pl.BlockSpec((tt * ROW_CHUNKS, LANES), lambda i, m: (jnp.maximum(i - blocks_p, 0), 0))],
            out_specs=pl.BlockSpec(memory_space=pl.ANY),
            scratch_shapes=[pltpu.VMEM((tm * ROW_CHUNKS, LANES), F32), pltpu.SemaphoreType.DMA((2,))]),
        out_shape=jax.ShapeDtypeStruct((n_tiles * tm * ROW_CHUNKS, LANES), F32),
        compiler_params=pltpu.CompilerParams(dimension_semantics=("arbitrary",)),
        name="dispatch",
    )(meta, pos, h2_p, h2_s)


def _experts_kernel(te_ref, meta_ref, xs_ref, wg_ref, wu_ref, wd_ref, ys_ref, wg_b, wu_b, wd_b, *, tm):
    i = pl.program_id(0)
    active = i < meta_ref[META_NACT, 0]
    fresh = (i == 0) | (te_ref[0, i] != te_ref[0, jnp.maximum(i - 1, 0)])

    @pl.when(active & fresh)
    def _():
        wg_b[...] = wg_ref[...].astype(BF16)
        wu_b[...] = wu_ref[...].astype(BF16)
        wd_b[...] = wd_ref[...].astype(BF16)

    @pl.when(active)
    def _():
        x = _load_row_tiles(xs_ref, tm).astype(BF16)
        hid = (jax.nn.silu(_dot(x, wg_b[...])) * _dot(x, wu_b[...])).astype(BF16)
        _store_row_tiles(ys_ref, _dot(hid, wd_b[...]))

    @pl.when(jnp.logical_not(active))
    def _():
        ys_ref[...] = jnp.zeros(ys_ref.shape, F32)


def _experts(te, meta, xs, w_gate, w_up, w_down, tm, n_tiles):
    last_active = lambda i, m: jnp.minimum(i, m[META_NACT, 0] - 1)
    tile = pl.BlockSpec((tm * ROW_CHUNKS, LANES), lambda i, te, m: (last_active(i, m), 0))
    w_in_spec = pl.BlockSpec((None, D_MODEL, D_EXPERT), lambda i, te, m: (te[0, last_active(i, m)], 0, 0))
    w_out_spec = pl.BlockSpec((None, D_EXPERT, D_MODEL), lambda i, te, m: (te[0, last_active(i, m)], 0, 0))
    return pl.pallas_call(
        functools.partial(_experts_kernel, tm=tm),
        grid_spec=pltpu.PrefetchScalarGridSpec(
            num_scalar_prefetch=2,
            grid=(n_tiles,),
            in_specs=[tile, w_in_spec, w_in_spec, w_out_spec],
            out_specs=pl.BlockSpec((tm * ROW_CHUNKS, LANES), lambda i, te, m: (i, 0)),
            scratch_shapes=[pltpu.VMEM((D_MODEL, D_EXPERT), BF16), pltpu.VMEM((D_MODEL, D_EXPERT), BF16),
                            pltpu.VMEM((D_EXPERT, D_MODEL), BF16)]),
        out_shape=jax.ShapeDtypeStruct(xs.shape, F32),
        compiler_params=pltpu.CompilerParams(dimension_semantics=("arbitrary",), vmem_limit_bytes=VMEM_LIMIT),
        name="experts",
    )(te, meta, xs, w_gate, w_up, w_down)


def _combine_kernel(pos_ref, pos_next_ref, route_ref, x1_ref, g2_ref, gam_ref, bet_ref, ys_hbm, y_ref,
                    a0, a1, b0, b1, sem, *, tt, alpha):
    i = pl.program_id(0)
    sets = ((a0, a1), (b0, b1))

    def gather(pos, which, wait):
        def body(j, c):
            for k in range(2):
                cp = _row_copy(ys_hbm, pos[k, j], sets[which][k], j, sem.at[which, k])
                cp.wait() if wait else cp.start()
            return c
        lax.fori_loop(0, tt, body, 0, unroll=ROW_DMA_UNROLL)

    @pl.when(i == 0)
    def _():
        gather(pos_ref, 0, wait=False)

    def step(which):
        @pl.when(i + 1 < pl.num_programs(0))
        def _():
            gather(pos_next_ref, 1 - which, wait=False)

        gather(pos_ref, which, wait=True)
        route = route_ref[...]
        eye = lax.broadcasted_iota(I32, (tt, tt), 0) == lax.broadcasted_iota(I32, (tt, tt), 1)
        as_col = lambda row: jnp.sum(jnp.where(eye, jnp.broadcast_to(row, (tt, tt)), 0.0), axis=1, keepdims=True)
        buf0, buf1 = sets[which]
        f = as_col(route[2:3, :]) * _load_row_tiles(buf0, tt) + as_col(route[3:4, :]) * _load_row_tiles(buf1, tt)
        y_ref[...] = _layer_norm_rows(alpha * x1_ref[...] + g2_ref[...] * f, gam_ref[...], bet_ref[...])

    for which in range(2):
        pl.when(i % 2 == which)(functools.partial(step, which))


def _combine(pos, route_t, col_block0, x1, mod3, tiles_per_group, tt, gam, bet, ys, alpha):
    n = x1.shape[0]
    rows = lambda w: pl.BlockSpec((tt, w), lambda i: (i, 0))
    whole = lambda arr: pl.BlockSpec(arr.shape, lambda i: (0,) * arr.ndim)
    steps = n // tt
    buf = pltpu.VMEM((tt * ROW_CHUNKS, LANES), F32)
    return pl.pallas_call(
        functools.partial(_combine_kernel, tt=tt, alpha=alpha),
        grid=(steps,),
        in_specs=[pl.BlockSpec((2, tt), lambda i: (0, i + col_block0), memory_space=pltpu.SMEM),
                  pl.BlockSpec((2, tt), lambda i: (0, jnp.minimum(i + 1, steps - 1) + col_block0),
                               memory_space=pltpu.SMEM),
                  pl.BlockSpec((SUBLANES, tt), lambda i: (0, i + col_block0)),
                  rows(D_MODEL), _mod_spec(mod3, 5, tiles_per_group), whole(gam), whole(bet),
                  pl.BlockSpec(memory_space=pl.ANY)],
        out_specs=rows(D_MODEL),
        out_shape=jax.ShapeDtypeStruct((n, D_MODEL), F32),
        scratch_shapes=[buf, buf, buf, buf, pltpu.SemaphoreType.DMA((2, 2))],
        compiler_params=pltpu.CompilerParams(dimension_semantics=("arbitrary",), vmem_limit_bytes=VMEM_LIMIT),
        name="combine",
    )(pos, pos, route_t, x1, mod3, gam, bet, ys)


def kernel(x_prompt, x_sample, cache_k, cache_v, cache_kidx, page_table, c_prompt, c_sample, w_ada, b_ada, w_in,
           a_ln_g, a_ln_b, w_spatial, b_spatial, w_out, ln1_g, ln1_b, w_group_router, w_expert_router, w_gate,
           w_up, w_down, ln2_g, ln2_b):
    depth = w_ada.shape[0]
    assert depth == 1, "one trunk layer"
    alpha = (2.0 * depth) ** 0.25
    bsz, seq, d = x_prompt.shape
    db, ts, _ = x_sample.shape
    n_pages = page_table.shape[1]
    page = cache_k.shape[2]
    past = n_pages * page
    tile_p = min(PROMPT_TILE, seq)
    assert d == D_MODEL and seq % tile_p == 0 and tile_p % CHUNK == 0 and ts <= SUBLANES and page == LANES
    l = 0

    n_c = bsz + db
    n_c_pad = pl.cdiv(n_c, SUBLANES) * SUBLANES
    c_all = jnp.pad(jnp.concatenate([c_prompt, c_sample], axis=0), ((0, n_c_pad - n_c), (0, 0)))
    mod = _ada(c_all, w_ada[l], b_ada[l][None, :])
    mod_p = mod[:bsz]
    mod_s = mod[bsz:n_c]

    w_in_b = _pad_w_in(w_in[l])
    w_out_b = w_out[l].astype(BF16)
    mavg = _head_avg_matrix()
    gam_a = a_ln_g[l].reshape(1, A_WIDTH)
    bet_a = a_ln_b[l].reshape(1, A_WIDTH)
    w_router = jnp.concatenate([w_group_router[l].T, jnp.zeros((SUBLANES - N_GROUPS, d), F32),
                                w_expert_router[l].reshape(d, N_EXPERTS).T], axis=0)
    ln1 = (ln1_g[l][None, :], ln1_b[l][None, :])
    ln2 = (ln2_g[l][None, :], ln2_b[l][None, :])

    wsp = w_spatial[l].reshape(A_HEADS // 2, 2, CHUNK, CHUNK).transpose(0, 2, 1, 3).reshape(A_HEADS // 2, CHUNK, 2 * CHUNK)
    bsp = jnp.repeat(b_spatial[l].T, HEAD_DIM, axis=1)
    a_p, kt_p, vt32_p, kit_p, kb_p, kiwib_p, vt_p, qt_p, qit_p, wit_p = _front_prompt(
        x_prompt, mod_p.reshape(bsz, 6, d), w_in_b, _rot_tables(np.arange(seq)), mavg, gam_a, bet_a, wsp, bsp, tile_p)
    b_p = _attn_prompt(qit_p, wit_p, qt_p, kiwib_p, kb_p, vt_p)
    n_p = bsz * seq
    tile_m = min(MID_TILE, seq)
    assert seq % tile_m == 0
    x1_p, h2_p, route_p = _mid(x_prompt.reshape(n_p, d), a_p.reshape(n_p, A_WIDTH), b_p.reshape(n_p, B_WIDTH),
                               mod_p.reshape(bsz, 1, 6 * d), seq // tile_m, tile_m, w_out_b, *ln1, w_router, alpha)

    r_s = ts * db
    x_tm = x_sample.transpose(1, 0, 2).reshape(r_s, d)
    rt_s = _rot_tables(np.repeat(past + np.arange(ts), db))
    w_small = w_spatial[l][:, :ts, :ts]
    wl = jnp.repeat(w_small.transpose(1, 2, 0).reshape(ts * ts, A_HEADS), HEAD_DIM, axis=1)
    bl = jnp.repeat(b_spatial[l][:, :ts].T, HEAD_DIM, axis=1)
    a_s, q_s, k_s, v_s, qi_s, kiwi_s, vg_s = _front_sample(x_tm, mod_s, w_in_b, rt_s, mavg, gam_a, bet_a, wl, bl, ts, db)

    def seq_major(a):
        return a.reshape(ts, db, a.shape[-1]).transpose(1, 0, 2)

    def new_t(a, heads):
        a = a.reshape(ts, db, heads, HEAD_DIM).transpose(1, 2, 3, 0)
        return jnp.pad(a, ((0, 0), (0, 0), (0, 0), (0, LANES - ts)))

    ga = min(IDX_SEQS, db)
    assert db % ga == 0 and (ga * ts) % SUBLANES == 0
    qi4 = qi_s.reshape(ts, db, IDX_HEADS, IDX_DIM).transpose(1, 2, 0, 3).reshape(db, IDX_HEADS * ts, IDX_DIM)
    wcol = kiwi_s[:, WI_LANE:WI_LANE + IDX_HEADS].reshape(ts, db, IDX_HEADS).transpose(1, 2, 0).reshape(db, IDX_HEADS * ts, 1)
    qpos = jnp.tile(past + jnp.arange(ts, dtype=I32), db).reshape(db * ts, 1)
    kinew_t = new_t(kiwi_s[:, :IDX_DIM], 1)[:, 0]
    kidx_t = jnp.transpose(cache_kidx[l], (0, 2, 1))
    k_t = jnp.transpose(cache_k[l], (0, 2, 3, 1))
    v_t = jnp.transpose(cache_v[l], (0, 2, 3, 1))
    bias_s = _idx_sample(page_table, qi4, wcol, qpos, kinew_t, kidx_t, ts, ga)
    steps8 = lambda a: jnp.pad(seq_major(a), ((0, 0), (0, SUBLANES - ts), (0, 0)))
    b_s = _attn_sample(page_table, seq_major(q_s), bias_s, steps8(k_s), steps8(v_s), k_t, v_t)
    b_s_tm = b_s.transpose(1, 0, 2).reshape(r_s, B_WIDTH)
    x1_s, h2_s, route_s = _mid(x_tm, a_s, b_s_tm, mod_s.reshape(1, db, 6 * d), ts, db, w_out_b, *ln1, w_router, alpha)

    n_all = n_p + r_s
    tok_p = min(TOKEN_TILE, seq)
    rank_tile = min(RANK_TILE, n_all)
    assert n_all % rank_tile == 0 and n_p % tok_p == 0 and r_s % tok_p == 0 and n_p % db == 0
    route_all = jnp.concatenate([route_p, route_s], axis=1)
    pos, meta, te, n_tiles = _plan(route_all, rank_tile, MOE_TILE)
    xs = _dispatch(meta, pos, h2_p, h2_s, tok_p, MOE_TILE, n_tiles)
    ys = _experts(te, meta, xs, w_gate[l], w_up[l], w_down[l], MOE_TILE, n_tiles)
    y_p = _combine(pos, route_all, 0, x1_p, mod_p.reshape(bsz, 1, 6 * d), seq // tok_p, tok_p, *ln2, ys, alpha)
    y_s_tm = _combine(pos, route_all, n_p // db, x1_s, mod_s.reshape(1, db, 6 * d), ts, db, *ln2, ys, alpha)
    y_s = y_s_tm.reshape(ts, db, d).transpose(1, 0, 2)

    kv5 = lambda a, n, t: a.reshape(1, n, t, B_KV_HEADS, HEAD_DIM)
    kv5_t = lambda a: a.reshape(1, bsz, B_KV_HEADS, HEAD_DIM, seq).transpose(0, 1, 4, 2, 3)
    return (y_p.reshape(bsz, seq, d), y_s,
            kv5_t(kt_p), kv5_t(vt32_p), kit_p.transpose(0, 2, 1)[None],
            kv5(seq_major(k_s), db, ts), kv5(seq_major(v_s), db, ts), seq_major(kiwi_s)[..., :IDX_DIM][None],
            seq_major(vg_s).reshape(1, db, ts, A_HEADS, HEAD_DIM))
```

```python
import functools

import jax
import jax.numpy as jnp
import numpy as np
from jax import lax
from jax.experimental import pallas as pl
from jax.experimental.pallas import tpu as pltpu

F32 = jnp.float32
BF16 = jnp.bfloat16
I32 = jnp.int32

D_MODEL = 1024
HEAD_DIM = 64
A_HEADS = 8
A_WIDTH = A_HEADS * HEAD_DIM
CHUNK = 128
B_HEADS = 8
B_KV_HEADS = 4
B_GROUP = B_HEADS // B_KV_HEADS
B_WIDTH = B_HEADS * HEAD_DIM
KV_WIDTH = B_KV_HEADS * HEAD_DIM
IDX_HEADS = 4
IDX_DIM = 64
IDX_WIDTH = IDX_HEADS * IDX_DIM
TOPK_MAX = 256
ROPE_THETA = 500000.0
ROT_DIM = HEAD_DIM // 4
ROT_HALF = ROT_DIM // 2
ATTN_SCALE = HEAD_DIM ** -0.5
LOG2_E = 1.4426950408889634
Q_SCALE = ATTN_SCALE * LOG2_E
DENOM_ROWS = 16
N_GROUPS = 4
EXPERTS_PER_GROUP = 8
N_EXPERTS = N_GROUPS * EXPERTS_PER_GROUP
D_EXPERT = 512
LN_EPS = 1e-5

LANES = 128
SUBLANES = 8
ROW_CHUNKS = D_MODEL // LANES

C_AU, C_AV, C_Q, C_K, C_V, C_QI, C_KI = 0, 512, 1024, 1536, 1792, 2048, 2304
IN_WIDTH = C_KI + IDX_DIM + IDX_HEADS
IN_PAD = 2432
WI_LANE = IDX_DIM

INT_MIN = -(2 ** 31)
INT_MAX = 2 ** 31 - 1
VMEM_LIMIT = 48 * 1024 * 1024

PROMPT_TILE = 256
MOE_TILE = 256
TOKEN_TILE = 256
RANK_TILE = 512
IDX_SEQS = 16
STEPS_PER_CHECK = 4
ROW_DMA_UNROLL = 8
MID_TILE = 512


def _dot(a, b):
    return jnp.dot(a, b, preferred_element_type=F32)


def _dot_nt(a, b):
    return lax.dot_general(a, b, (((1,), (1,)), ((), ())), preferred_element_type=F32)


def _ada_kernel(c_ref, w_ref, b_ref, o_ref):
    s = jax.nn.silu(c_ref[...]).astype(BF16)
    o_ref[...] = _dot(s, w_ref[...].astype(BF16)) + b_ref[...]


def _ada(c_all, w_ada, b_ada):
    rows = c_all.shape[0]
    n_out = w_ada.shape[1]
    tn = 1024
    return pl.pallas_call(
        _ada_kernel,
        grid=(n_out // tn,),
        in_specs=[
            pl.BlockSpec((rows, D_MODEL), lambda j: (0, 0)),
            pl.BlockSpec((D_MODEL, tn), lambda j: (0, j)),
            pl.BlockSpec((1, tn), lambda j: (0, j)),
        ],
        out_specs=pl.BlockSpec((rows, tn), lambda j: (0, j)),
        out_shape=jax.ShapeDtypeStruct((rows, n_out), F32),
        compiler_params=pltpu.CompilerParams(
            dimension_semantics=("arbitrary",), vmem_limit_bytes=VMEM_LIMIT),
        name="ada",
    )(c_all, w_ada, b_ada)


def _rotate(x, rt, blk):
    c = rt[:, 0:LANES]
    s_lo = rt[:, LANES:2 * LANES]
    s_hi = rt[:, 2 * LANES:3 * LANES]
    if blk == 1:
        head = lax.broadcasted_iota(I32, c.shape, 1) < IDX_DIM
        c = jnp.where(head, c, 1.0)
        s_lo = jnp.where(head, s_lo, 0.0)
        s_hi = jnp.where(head, s_hi, 0.0)
    outs = []
    for j in range(x.shape[1] // LANES):
        xb = x[:, j * LANES:(j + 1) * LANES]
        up = pltpu.roll(xb, LANES - ROT_HALF, 1)
        dn = pltpu.roll(xb, ROT_HALF, 1)
        outs.append(xb * c + up * s_lo + dn * s_hi)
    return outs[0] if len(outs) == 1 else jnp.concatenate(outs, axis=1)


def _head_ln(g, mavg, gam, bet):
    def seg_mean(x):
        parts = [_dot(x[:, j * 256:(j + 1) * 256].astype(BF16), mavg) for j in range(A_WIDTH // 256)]
        return jnp.concatenate(parts, axis=1)
    mu = seg_mean(g)
    d = g - mu
    var = seg_mean(d * d)
    return d * lax.rsqrt(var + LN_EPS) * gam + bet


def _project(h, w_ref, rt, mavg, gam, bet):
    u = jax.nn.gelu(_dot(h, w_ref[:, C_AU:C_AV]))
    vg = _head_ln(jax.nn.gelu(_dot(h, w_ref[:, C_AV:C_Q])), mavg, gam, bet)
    q = _rotate(_dot(h, w_ref[:, C_Q:C_K]), rt, 0) * Q_SCALE
    k = _rotate(_dot(h, w_ref[:, C_K:C_V]), rt, 0)
    v = _dot(h, w_ref[:, C_V:C_QI])
    qi = _rotate(_dot(h, w_ref[:, C_QI:C_KI]), rt, 0)
    kiwi = _rotate(_dot(h, w_ref[:, C_KI:IN_PAD]), rt, 1)
    return u, vg, q, k, v, qi, kiwi


def _front_prompt_kernel(x_ref, mod_ref, w_ref, rt_ref, mavg_ref, gam_ref, bet_ref, wsp_ref, bsp_ref,
                         a_ref, kt_ref, vt_ref, kit_ref, kb_ref, kiwib_ref, vtb_ref, qt_ref, qit_ref, wit_ref,
                         wb_ref):
    shift = mod_ref[0:1, :]
    scale = mod_ref[1:2, :]
    h = (x_ref[...] * (1.0 + scale) + shift).astype(BF16)
    @pl.when((pl.program_id(0) == 0) & (pl.program_id(1) == 0))
    def _():
        wb_ref[...] = w_ref[...].astype(BF16)

    u, vg, q, k, v, qi, kiwi = _project(h, wb_ref, rt_ref[...], mavg_ref[...], gam_ref[...], bet_ref[...])
    v_t = jnp.transpose(v)
    kiwi_t = jnp.transpose(kiwi)
    kt_ref[...] = jnp.transpose(k)
    vt_ref[...] = v_t
    kit_ref[...] = kiwi_t[0:IDX_DIM, :]
    kb_ref[...] = k.astype(BF16)
    kiwib_ref[...] = kiwi.astype(BF16)
    vtb_ref[...] = v_t.astype(BF16)
    qt_ref[...] = jnp.transpose(q).astype(BF16)
    qit_ref[...] = jnp.transpose(qi).astype(BF16)
    wit_ref[...] = kiwi_t[WI_LANE:WI_LANE + SUBLANES, :]

    rows = lax.broadcasted_iota(I32, (CHUNK, 2 * CHUNK), 0)
    cols = lax.broadcasted_iota(I32, (CHUNK, 2 * CHUNK), 1) % CHUNK
    causal = cols <= rows
    lane = lax.broadcasted_iota(I32, (CHUNK, LANES), 1)
    tt = x_ref.shape[0]
    for cidx in range(tt // CHUNK):
        rs = slice(cidx * CHUNK, (cidx + 1) * CHUNK)
        blocks = []
        for p in range(A_HEADS // 2):
            wcat = jnp.where(causal, wsp_ref[p], 0.0).astype(BF16)
            vb = vg[rs, p * LANES:(p + 1) * LANES]
            rhs = jnp.concatenate([jnp.where(lane < HEAD_DIM, vb, 0.0),
                                   jnp.where(lane >= HEAD_DIM, vb, 0.0)], axis=0).astype(BF16)
            blocks.append(_dot(wcat, rhs))
        s = jnp.concatenate(blocks, axis=1) + bsp_ref[...]
        a_ref[rs, :] = (u[rs, :] * s).astype(BF16)


def _front_prompt(x, mod, w_in, rt, mavg, gam, bet, wsp, bsp, tt):
    b, t, _ = x.shape
    nc = t // tt
    tok = lambda w: pl.BlockSpec((None, tt, w), lambda i, j: (i, j, 0))
    tr = lambda r: pl.BlockSpec((None, None, r, tt), lambda i, j: (i, j, 0, 0))
    pos_minor = lambda r: pl.BlockSpec((None, r, tt), lambda i, j: (i, 0, j))
    const2 = lambda a: pl.BlockSpec(a.shape, lambda i, j: (0,) * a.ndim)
    out_shapes = (
        jax.ShapeDtypeStruct((b, t, A_WIDTH), BF16),
        jax.ShapeDtypeStruct((b, KV_WIDTH, t), F32),
        jax.ShapeDtypeStruct((b, KV_WIDTH, t), F32),
        jax.ShapeDtypeStruct((b, IDX_DIM, t), F32),
        jax.ShapeDtypeStruct((b, t, KV_WIDTH), BF16),
        jax.ShapeDtypeStruct((b, t, LANES), BF16),
        jax.ShapeDtypeStruct((b, nc, KV_WIDTH, tt), BF16),
        jax.ShapeDtypeStruct((b, nc, B_WIDTH, tt), BF16),
        jax.ShapeDtypeStruct((b, nc, IDX_WIDTH, tt), BF16),
        jax.ShapeDtypeStruct((b, nc, SUBLANES, tt), F32),
    )
    return pl.pallas_call(
        _front_prompt_kernel,
        grid=(b, nc),
        in_specs=[
            tok(D_MODEL),
            pl.BlockSpec((None, 6, D_MODEL), lambda i, j: (i, 0, 0)),
            const2(w_in),
            pl.BlockSpec((tt, 3 * LANES), lambda i, j: (j, 0)),
            const2(mavg), const2(gam), const2(bet), const2(wsp), const2(bsp),
        ],
        out_specs=[tok(A_WIDTH), pos_minor(KV_WIDTH), pos_minor(KV_WIDTH), pos_minor(IDX_DIM), tok(KV_WIDTH),
                   tok(LANES), tr(KV_WIDTH), tr(B_WIDTH), tr(IDX_WIDTH), tr(SUBLANES)],
        out_shape=out_shapes,
        scratch_shapes=[pltpu.VMEM(w_in.shape, BF16)],
        compiler_params=pltpu.CompilerParams(
            dimension_semantics=("arbitrary", "arbitrary"), vmem_limit_bytes=VMEM_LIMIT),
        name="front_prompt",
    )(x, mod, w_in, rt, mavg, gam, bet, wsp, bsp)


def _order_key(score, kpos, idx_bits):
    offs = 1 << idx_bits
    bits = pltpu.bitcast(score, I32)
    key = bits ^ ((bits >> 31) & 0x7FFFFFFF)
    key = key + jnp.where(score > 0.0, offs, 0)
    return jnp.where(score == 0.0, offs - kpos, key)


def _select_threshold(count_ge, count_gt_eq_lt, shape, topk, idx_bits):
    offs = 1 << idx_bits
    first_candidates = (offs + 1, 1)
    max_steps = -(-(32 + len(first_candidates)) // STEPS_PER_CHECK) * STEPS_PER_CHECK

    def unresolved(carry):
        i, lo, hi, n_lo = carry
        pending = (n_lo != topk) & (hi - 1 > lo)
        return (i < max_steps) & (jnp.max(pending.astype(F32)) > 0.0)

    def bisect(carry):
        i, lo, hi, n_lo = carry
        for _ in range(STEPS_PER_CHECK):
            mid = (lo >> 1) + (hi >> 1) + (lo & hi & 1)
            for step, value in enumerate(first_candidates):
                forced = jnp.where(i == step, value, INT_MIN)
                mid = jnp.where((lo < forced) & (forced < hi), forced, mid)
            tot = count_ge(mid)
            take = tot >= topk
            lo = jnp.where(take, mid, lo)
            n_lo = jnp.where(take, tot, n_lo)
            hi = jnp.where(take, hi, mid)
            i = i + 1
        return i, lo, hi, n_lo

    never = jnp.full(shape, INT_MAX, I32)
    _, thr, _, n_lo = lax.while_loop(
        unresolved, bisect, (jnp.int32(0), jnp.full(shape, INT_MIN + 1, I32), jnp.full(shape, INT_MAX, I32), never))
    tied = (n_lo > topk) & (n_lo != INT_MAX)
    big = jnp.full(shape, offs, I32)

    def resolve_ties(_):
        n_gt, _ = count_gt_eq_lt(thr, big)
        need = topk - n_gt

        def idx_step(i, cut):
            cand = cut | lax.shift_left(jnp.int32(1), idx_bits - 1 - i)
            _, n_eq = count_gt_eq_lt(thr, cand)
            return jnp.where(n_eq <= need, cand, cut)

        cut = lax.fori_loop(0, idx_bits, idx_step, jnp.zeros(shape, I32))
        return jnp.where(tied, cut, big)

    any_tied = jnp.max(tied.astype(F32)) > 0.0
    cut = lax.cond(any_tied, resolve_ties, lambda _: big, 0)
    return thr, cut


def _fold_rows(x):
    acc = x[0:SUBLANES]
    for r in range(1, x.shape[0] // SUBLANES):
        acc = acc + x[r * SUBLANES:(r + 1) * SUBLANES]
    return acc


def _col_total(cnt8):
    return jnp.sum(cnt8.astype(F32), axis=0, keepdims=True).astype(I32)


def _attn_prompt_kernel(qit_ref, wit_ref, qt_ref, kiwib_ref, kb_ref, vt_ref, o_ref,
                        key_ref, w4_ref, wq_ref, s0_ref, s1_ref, bias_ref, m_ref, acc_ref, *, tq, topk, idx_bits):
    j = pl.program_id(1)
    kc = tq
    n_kc = j + 1

    @pl.when((pl.program_id(0) == 0) & (j == 0))
    def _():
        w4_ref[...] = jnp.zeros(w4_ref.shape, BF16)
        wq_ref[...] = jnp.zeros(wq_ref.shape, BF16)

    for h in range(IDX_HEADS):
        w4_ref[0:IDX_DIM, h * tq:(h + 1) * tq] = qit_ref[h * IDX_DIM:(h + 1) * IDX_DIM, :]
    for h in range(B_HEADS):
        n = h // B_GROUP
        wq_ref[h, n * HEAD_DIM:(n + 1) * HEAD_DIM, :] = qt_ref[h * HEAD_DIM:(h + 1) * HEAD_DIM, :]

    wit = wit_ref[...]
    krow = lax.broadcasted_iota(I32, (kc, tq), 0)
    qpos = j * tq + lax.broadcasted_iota(I32, (kc, tq), 1)

    n_chunks = key_ref.shape[0] - 1

    def score_pair(i, _):
        for c in (2 * i, 2 * i + 1):
            rows = pl.multiple_of(jnp.minimum(c, n_chunks - 1) * kc, kc)
            s = _dot(kiwib_ref[pl.ds(rows, kc), :], w4_ref[...])
            tot = jnp.maximum(s[:, 0:tq], 0.0) * wit[0:1, :]
            for h in range(1, IDX_HEADS):
                tot = tot + jnp.maximum(s[:, h * tq:(h + 1) * tq], 0.0) * wit[h:h + 1, :]
            kpos = c * kc + krow
            key_ref[c] = jnp.where(kpos <= qpos, _order_key(tot, kpos, idx_bits), INT_MIN)
        return 0

    lax.fori_loop(0, (n_kc + 1) // 2, score_pair, 0)

    def count_ge(cand):
        def body(i, cnt):
            for c in (2 * i, 2 * i + 1):
                cnt = cnt + _fold_rows(jnp.where(key_ref[c] >= cand, 1, 0))
            return cnt
        return _col_total(lax.fori_loop(0, (n_kc + 1) // 2, body, jnp.zeros((SUBLANES, tq), I32)))

    def count_gt_eq_lt(thr, pos):
        def body(c, carry):
            n_gt, n_eq = carry
            key = key_ref[c]
            n_gt = n_gt + _fold_rows(jnp.where(key > thr, 1, 0))
            n_eq = n_eq + _fold_rows(jnp.where((key == thr) & (c * kc + krow < pos), 1, 0))
            return n_gt, n_eq
        z = jnp.zeros((SUBLANES, tq), I32)
        n_gt, n_eq = lax.fori_loop(0, n_kc, body, (z, z))
        return _col_total(n_gt), _col_total(n_eq)

    thr, cut = _select_threshold(count_ge, count_gt_eq_lt, (1, tq), topk, idx_bits)

    m_ref[...] = jnp.full(m_ref.shape, jnp.finfo(F32).min, F32)
    acc_ref[...] = jnp.zeros(acc_ref.shape, F32)

    def key_block(c):
        return kb_ref[pl.ds(pl.multiple_of(c * kc, kc), kc), :]

    def selection_bias(c, open_bias):
        key = key_ref[c]
        sel = (key > thr) | ((key == thr) & (c * kc + krow < cut))
        bias_ref[...] = jnp.where(sel, open_bias, -jnp.inf)

    selection_bias(0, 0.0)
    kblk0 = key_block(0)
    for h in range(B_HEADS):
        s0_ref[h] = _dot(kblk0, wq_ref[h]) + bias_ref[...]

    def stage(c, src, dst, c_next, open_next):
        selection_bias(c_next, open_next)
        kblk_next = key_block(c_next)
        vtc = vt_ref[c]
        for h in range(B_HEADS):
            n = h // B_GROUP
            dst[h] = _dot(kblk_next, wq_ref[h]) + bias_ref[...]
            s = src[h]
            m_old = m_ref[h]
            m_new = jnp.maximum(m_old, jnp.max(s, axis=0, keepdims=True))
            p = jnp.exp2(s - m_new).astype(BF16)
            v_aug = jnp.concatenate([vtc[n * HEAD_DIM:(n + 1) * HEAD_DIM, :], ones_rows], axis=0)
            acc_ref[h] = jnp.exp2(m_old - m_new) * acc_ref[h] + _dot(v_aug, p)
            m_ref[h] = m_new

    ones_rows = jnp.where(lax.broadcasted_iota(I32, (DENOM_ROWS, kc), 0) == 0, 1.0, 0.0).astype(BF16)
    last = n_kc - 1

    def attend_pair(i, _):
        c0 = 2 * i
        c1 = jnp.minimum(c0 + 1, last)
        stage(c0, s0_ref, s1_ref, c1, jnp.where(c0 + 1 <= last, 0.0, -jnp.inf))
        stage(c1, s1_ref, s0_ref, jnp.minimum(c0 + 2, last), 0.0)
        return 0

    lax.fori_loop(0, (n_kc + 1) // 2, attend_pair, 0)
    out_t = jnp.concatenate([acc_ref[h, 0:HEAD_DIM, :] / acc_ref[h, HEAD_DIM:HEAD_DIM + 1, :]
                             for h in range(B_HEADS)], axis=0)
    o_ref[...] = jnp.transpose(out_t).astype(BF16)


def _attn_prompt(qit, wit, qt, kiwib, kb, vt):
    b, nc, _, tq = qt.shape
    t = nc * tq
    topk = min(TOPK_MAX, t // 4)
    idx_bits = max(1, (t - 1).bit_length())
    tr = lambda r: pl.BlockSpec((None, None, r, tq), lambda i, j: (i, j, 0, 0))
    full = lambda w: pl.BlockSpec((None, t, w), lambda i, j: (i, 0, 0))
    kernel = functools.partial(_attn_prompt_kernel, tq=tq, topk=topk, idx_bits=idx_bits)
    return pl.pallas_call(
        kernel,
        grid=(b, nc),
        in_specs=[tr(IDX_WIDTH), tr(SUBLANES), tr(B_WIDTH), full(LANES), full(KV_WIDTH),
                  pl.BlockSpec((None, nc, KV_WIDTH, tq), lambda i, j: (i, 0, 0, 0))],
        out_specs=pl.BlockSpec((None, tq, B_WIDTH), lambda i, j: (i, j, 0)),
        out_shape=jax.ShapeDtypeStruct((b, t, B_WIDTH), BF16),
        scratch_shapes=[
            pltpu.VMEM((nc + 1, tq, tq), I32),
            pltpu.VMEM((LANES, IDX_HEADS * tq), BF16),
            pltpu.VMEM((B_HEADS, KV_WIDTH, tq), BF16),
            pltpu.VMEM((B_HEADS, tq, tq), F32),
            pltpu.VMEM((B_HEADS, tq, tq), F32),
            pltpu.VMEM((tq, tq), F32),
            pltpu.VMEM((B_HEADS, 1, tq), F32),
            pltpu.VMEM((B_HEADS, HEAD_DIM + DENOM_ROWS, tq), F32),
        ],
        compiler_params=pltpu.CompilerParams(
            dimension_semantics=("arbitrary", "arbitrary"), vmem_limit_bytes=VMEM_LIMIT),
        name="attn_prompt",
    )(qit, wit, qt, kiwib, kb, vt)


def _rot_tables(pos):
    r = pos.shape[0]
    inv_freq = np.float32(ROPE_THETA) ** (-np.arange(ROT_HALF, dtype=np.float32) * np.float32(2.0) / np.float32(ROT_DIM))
    ang = pos.astype(np.float32)[:, None] * inv_freq[None, :]
    cos, sin = np.cos(ang), np.sin(ang)
    rest = HEAD_DIM - ROT_DIM
    c64 = np.concatenate([cos, cos, np.ones((r, rest), np.float32)], axis=1)
    lo64 = np.concatenate([-sin, np.zeros((r, HEAD_DIM - ROT_HALF), np.float32)], axis=1)
    hi64 = np.concatenate([np.zeros((r, ROT_HALF), np.float32), sin, np.zeros((r, rest), np.float32)], axis=1)
    return jnp.asarray(np.concatenate([c64, c64, lo64, lo64, hi64, hi64], axis=1).astype(np.float32))


def _head_avg_matrix():
    return jnp.kron(jnp.eye(256 // HEAD_DIM, dtype=F32), jnp.full((HEAD_DIM, HEAD_DIM), 1.0 / HEAD_DIM, F32)).astype(BF16)


def _pad_w_in(w_in):
    return jnp.pad(w_in, ((0, 0), (0, IN_PAD - IN_WIDTH)))


def _front_sample_kernel(x_ref, shift_ref, scale_ref, w_ref, rt_ref, mavg_ref, gam_ref, bet_ref, wl_ref, bl_ref,
                         a_ref, q_ref, k_ref, v_ref, qi_ref, kiwi_ref, vg_ref, wb_ref, *, ts, db):
    wb_ref[...] = w_ref[...].astype(BF16)
    one_scale = 1.0 + scale_ref[...]
    shift = shift_ref[...]
    h = jnp.concatenate([x_ref[t * db:(t + 1) * db, :] * one_scale + shift for t in range(ts)], axis=0).astype(BF16)
    u, vg, q, k, v, qi, kiwi = _project(h, wb_ref, rt_ref[...], mavg_ref[...], gam_ref[...], bet_ref[...])
    q_ref[...] = q.astype(BF16)
    k_ref[...] = k
    v_ref[...] = v
    qi_ref[...] = qi.astype(BF16)
    kiwi_ref[...] = kiwi
    vg_ref[...] = vg
    for t in range(ts):
        s = bl_ref[t:t + 1, :]
        for src in range(t + 1):
            s = s + wl_ref[t * ts + src:t * ts + src + 1, :] * vg[src * db:(src + 1) * db, :]
        a_ref[t * db:(t + 1) * db, :] = (u[t * db:(t + 1) * db, :] * s).astype(BF16)


def _front_sample(x_tm, mod_s, w_in, rt, mavg, gam, bet, wl, bl, ts, db):
    r = ts * db
    whole = lambda a: pl.BlockSpec(a.shape, lambda i: (0,) * a.ndim)
    out = lambda w, dt: jax.ShapeDtypeStruct((r, w), dt)
    outs = (out(A_WIDTH, BF16), out(B_WIDTH, BF16), out(KV_WIDTH, F32), out(KV_WIDTH, F32),
            out(IDX_WIDTH, BF16), out(LANES, F32), out(A_WIDTH, F32))
    return pl.pallas_call(
        functools.partial(_front_sample_kernel, ts=ts, db=db),
        grid=(1,),
        in_specs=[
            whole(x_tm),
            pl.BlockSpec((db, D_MODEL), lambda i: (0, 0)),
            pl.BlockSpec((db, D_MODEL), lambda i: (0, 1)),
            whole(w_in), whole(rt), whole(mavg), whole(gam), whole(bet), whole(wl), whole(bl),
        ],
        out_specs=[pl.BlockSpec((r, s.shape[1]), lambda i: (0, 0)) for s in outs],
        out_shape=outs,
        scratch_shapes=[pltpu.VMEM(w_in.shape, BF16)],
        compiler_params=pltpu.CompilerParams(dimension_semantics=("arbitrary",), vmem_limit_bytes=VMEM_LIMIT),
        name="front_sample",
    )(x_tm, mod_s, mod_s, w_in, rt, mavg, gam, bet, wl, bl)


def _idx_sample_kernel(pt_ref, qi4_ref, wcol_ref, qpos_ref, kinew_ref, kidx_hbm, bias_ref,
                       ki_buf, key_ref, sem, *, ga, ts, n_pages, page, lpad, topk, idx_bits):
    i = pl.program_id(0)
    n_steps = pl.num_programs(0)
    past = n_pages * page
    rows = ga * ts
    slot = i % 2

    def page_copy(step, to_slot, g, p):
        phys = pt_ref[step * ga + g, p]
        return pltpu.make_async_copy(kidx_hbm.at[phys], ki_buf.at[to_slot, g, :, pl.ds(p * page, page)],
                                     sem.at[to_slot])

    def start_all(step, to_slot):
        for g in range(ga):
            for p in range(n_pages):
                page_copy(step, to_slot, g, p).start()

    @pl.when(i == 0)
    def _():
        start_all(0, 0)

    @pl.when(i + 1 < n_steps)
    def _():
        start_all(i + 1, 1 - slot)

    ki_buf[slot, :, :, pl.ds(past, LANES)] = kinew_ref[...]
    for g in range(ga):
        for p in range(n_pages):
            page_copy(i, slot, g, p).wait()

    kpos = lax.broadcasted_iota(I32, (ts, lpad), 1)
    for g in range(ga):
        s = _dot(qi4_ref[g], ki_buf[slot, g].astype(BF16))
        r = jnp.maximum(s, 0.0) * wcol_ref[g]
        tot = r[0:ts]
        for h in range(1, IDX_HEADS):
            tot = tot + r[h * ts:(h + 1) * ts]
        adm = (kpos <= qpos_ref[g * ts:(g + 1) * ts, :]) & (kpos < past + ts)
        key_ref[g * ts:(g + 1) * ts, :] = jnp.where(adm, _order_key(tot, kpos, idx_bits), INT_MIN)

    sub = lpad // LANES
    lane128 = lax.broadcasted_iota(I32, (rows, LANES), 1)

    def row_total(cnt):
        tot = jnp.sum(cnt.astype(F32), axis=1, keepdims=True)
        return jnp.broadcast_to(tot, cnt.shape).astype(I32)

    def count_ge(cand):
        cnt = jnp.zeros((rows, LANES), I32)
        for s_ in range(sub):
            cnt = cnt + jnp.where(key_ref[:, s_ * LANES:(s_ + 1) * LANES] >= cand, 1, 0)
        return row_total(cnt)

    def count_gt_eq_lt(thr, pos):
        n_gt = jnp.zeros((rows, LANES), I32)
        n_eq = jnp.zeros((rows, LANES), I32)
        for s_ in range(sub):
            kk = key_ref[:, s_ * LANES:(s_ + 1) * LANES]
            n_gt = n_gt + jnp.where(kk > thr, 1, 0)
            n_eq = n_eq + jnp.where((kk == thr) & (s_ * LANES + lane128 < pos), 1, 0)
        return row_total(n_gt), row_total(n_eq)

    thr, cut = _select_threshold(count_ge, count_gt_eq_lt, (rows, LANES), topk, idx_bits)
    for s_ in range(sub):
        kk = key_ref[:, s_ * LANES:(s_ + 1) * LANES]
        sel = (kk > thr) | ((kk == thr) & (s_ * LANES + lane128 < cut))
        bias = jnp.where(sel, 0.0, -jnp.inf)
        for g in range(ga):
            bias_ref[g, :, s_ * LANES:(s_ + 1) * LANES] = bias[g * ts:(g + 1) * ts]


def _idx_sample(page_table, qi4, wcol, qpos, kinew_t, kidx_t, ts, ga):
    db = qi4.shape[0]
    n_pages = page_table.shape[1]
    page = kidx_t.shape[2]
    past = n_pages * page
    lpad = past + LANES
    topk = min(TOPK_MAX, (past + ts) // 4)
    idx_bits = max(1, (lpad - 1).bit_length())
    kernel = functools.partial(_idx_sample_kernel, ga=ga, ts=ts, n_pages=n_pages, page=page, lpad=lpad,
                               topk=topk, idx_bits=idx_bits)
    return pl.pallas_call(
        kernel,
        grid_spec=pltpu.PrefetchScalarGridSpec(
            num_scalar_prefetch=1,
            grid=(db // ga,),
            in_specs=[pl.BlockSpec((ga, IDX_HEADS * ts, IDX_DIM), lambda i, pt: (i, 0, 0)),
                      pl.BlockSpec((ga, IDX_HEADS * ts, 1), lambda i, pt: (i, 0, 0)),
                      pl.BlockSpec((ga * ts, 1), lambda i, pt: (i, 0)),
                      pl.BlockSpec((ga, IDX_DIM, LANES), lambda i, pt: (i, 0, 0)),
                      pl.BlockSpec(memory_space=pl.ANY)],
            out_specs=pl.BlockSpec((ga, ts, lpad), lambda i, pt: (i, 0, 0)),
            scratch_shapes=[
                pltpu.VMEM((2, ga, IDX_DIM, lpad), F32),
                pltpu.VMEM((ga * ts, lpad), I32),
                pltpu.SemaphoreType.DMA((2,)),
            ]),
        out_shape=jax.ShapeDtypeStruct((db, ts, lpad), F32),
        compiler_params=pltpu.CompilerParams(dimension_semantics=("arbitrary",), vmem_limit_bytes=VMEM_LIMIT),
        name="idx_sample",
    )(page_table, qi4, wcol, qpos, kinew_t, kidx_t)


def _attn_sample_kernel(pt_ref, q_ref, bias_ref, knew_ref, vnew_ref, k_hbm, v_hbm, o_ref,
                        k_buf, v_buf, sem, *, ts, n_pages, page):
    b = pl.program_id(0)
    nb = pl.num_programs(0)
    past = n_pages * page
    slot = b % 2

    def page_copies(seq, to_slot, p):
        phys = pt_ref[seq, p]
        dst = pl.ds(p * page, page)
        return (pltpu.make_async_copy(k_hbm.at[phys], k_buf.at[to_slot, :, :, dst], sem.at[0, to_slot]),
                pltpu.make_async_copy(v_hbm.at[phys], v_buf.at[to_slot, :, :, dst], sem.at[1, to_slot]))

    def start_all(seq, to_slot):
        for p in range(n_pages):
            for cp in page_copies(seq, to_slot, p):
                cp.start()

    @pl.when(b == 0)
    def _():
        start_all(0, 0)

    @pl.when(b + 1 < nb)
    def _():
        start_all(b + 1, 1 - slot)

    steps8, lanes = lax.broadcasted_iota(I32, (SUBLANES, LANES), 0), lax.broadcasted_iota(I32, (SUBLANES, LANES), 1)
    place = jnp.where(steps8 == lanes, 1.0, 0.0).astype(BF16)
    for new_ref, buf in ((knew_ref, k_buf), (vnew_ref, v_buf)):
        new = new_ref[...].astype(BF16)
        for n in range(B_KV_HEADS):
            buf[slot, n, :, pl.ds(past, LANES)] = lax.dot_general(
                new[:, n * HEAD_DIM:(n + 1) * HEAD_DIM], place, (((0,), (0,)), ((), ())), preferred_element_type=F32)
    for p in range(n_pages):
        for cp in page_copies(b, slot, p):
            cp.wait()

    q = q_ref[...]
    bias = bias_ref[...]
    bias2 = jnp.concatenate([bias] * B_GROUP, axis=0)
    outs = [None] * B_HEADS
    for n in range(B_KV_HEADS):
        kt = k_buf[slot, n].astype(BF16)
        vt = v_buf[slot, n].astype(BF16)
        qs = jnp.concatenate([q[:, (n * B_GROUP + g) * HEAD_DIM:(n * B_GROUP + g + 1) * HEAD_DIM]
                              for g in range(B_GROUP)], axis=0)
        sc = _dot(qs, kt) + bias2
        m = jnp.max(sc, axis=1, keepdims=True)
        p_ = jnp.exp2(sc - m)
        o = _dot_nt(p_.astype(BF16), vt) / jnp.sum(p_, axis=1, keepdims=True)
        for g in range(B_GROUP):
            outs[n * B_GROUP + g] = o[g * ts:(g + 1) * ts]
    o_ref[...] = jnp.concatenate(outs, axis=1).astype(BF16)


def _attn_sample(page_table, q, bias, knew_t, vnew_t, k_t, v_t):
    db, ts, _ = q.shape
    n_pages = page_table.shape[1]
    page = k_t.shape[3]
    lpad = bias.shape[2]
    seq3 = lambda a: pl.BlockSpec((None,) + a.shape[1:], lambda b, pt: (b,) + (0,) * (a.ndim - 1))
    anyspec = pl.BlockSpec(memory_space=pl.ANY)
    kernel = functools.partial(_attn_sample_kernel, ts=ts, n_pages=n_pages, page=page)
    return pl.pallas_call(
        kernel,
        grid_spec=pltpu.PrefetchScalarGridSpec(
            num_scalar_prefetch=1,
            grid=(db,),
            in_specs=[seq3(q), seq3(bias), seq3(knew_t), seq3(vnew_t), anyspec, anyspec],
            out_specs=pl.BlockSpec((None, ts, B_WIDTH), lambda b, pt: (b, 0, 0)),
            scratch_shapes=[
                pltpu.VMEM((2, B_KV_HEADS, HEAD_DIM, lpad), F32),
                pltpu.VMEM((2, B_KV_HEADS, HEAD_DIM, lpad), F32),
                pltpu.SemaphoreType.DMA((2, 2)),
            ]),
        out_shape=jax.ShapeDtypeStruct((db, ts, B_WIDTH), BF16),
        compiler_params=pltpu.CompilerParams(dimension_semantics=("arbitrary",), vmem_limit_bytes=VMEM_LIMIT),
        name="attn_sample",
    )(page_table, q, bias, knew_t, vnew_t, k_t, v_t)


def _layer_norm_rows(y, gam, bet):
    mu = jnp.mean(y, axis=1, keepdims=True)
    d = y - mu
    var = jnp.mean(d * d, axis=1, keepdims=True)
    return d * lax.rsqrt(var + LN_EPS) * gam + bet


def _store_row_tiles(ref, val):
    r = val.shape[0]
    for c in range(ROW_CHUNKS):
        ref[pl.ds(c, r, stride=ROW_CHUNKS), :] = val[:, c * LANES:(c + 1) * LANES]


def _load_row_tiles(ref, r):
    return jnp.concatenate([ref[pl.ds(c, r, stride=ROW_CHUNKS), :] for c in range(ROW_CHUNKS)], axis=1)


def _route(logits_t):
    r = logits_t.shape[1]
    far = float(LANES)

    def softmax_rows(x):
        e = jnp.exp(x - jnp.max(x, axis=0, keepdims=True))
        return e / jnp.sum(e, axis=0, keepdims=True)

    def first_max(p):
        rows = lax.broadcasted_iota(I32, p.shape, 0).astype(F32)
        best = jnp.max(p, axis=0, keepdims=True)
        return best, jnp.min(jnp.where(p == best, rows, far), axis=0, keepdims=True), rows

    g_w, g_sel, _ = first_max(softmax_rows(logits_t[0:N_GROUPS]))
    el = jnp.zeros((EXPERTS_PER_GROUP, r), F32)
    for g in range(N_GROUPS):
        lo = SUBLANES + g * EXPERTS_PER_GROUP
        el = jnp.where(g_sel == float(g), logits_t[lo:lo + EXPERTS_PER_GROUP], el)
    ep = softmax_rows(el)
    p1, i1, rows = first_max(ep)
    p2, i2, _ = first_max(jnp.where(rows == i1, -1.0, ep))
    denom = p1 + p2
    base = g_sel * float(EXPERTS_PER_GROUP)
    return jnp.concatenate([base + i1, base + i2, g_w * p1 / denom, g_w * p2 / denom,
                            jnp.zeros((SUBLANES - 4, r), F32)], axis=0)


def _split_bf16(x):
    hi = x.astype(BF16)
    return hi, (x - hi.astype(F32)).astype(BF16)


def _mid_kernel(x_ref, a_ref, b_ref, g1_ref, sh2_ref, sc2_ref, wo_ref, gam_ref, bet_ref, wr_ref,
                x1_ref, h2_ref, route_ref, *, alpha):
    mixed = _dot(a_ref[...], wo_ref[0:A_WIDTH, :]) + _dot(b_ref[...], wo_ref[A_WIDTH:A_WIDTH + B_WIDTH, :])
    x1 = _layer_norm_rows(alpha * x_ref[...] + g1_ref[...] * mixed, gam_ref[...], bet_ref[...])
    x1_ref[...] = x1
    h2 = x1 * (1.0 + sc2_ref[...]) + sh2_ref[...]
    _store_row_tiles(h2_ref, h2)
    w_hi, w_lo = _split_bf16(wr_ref[...])
    h_hi, h_lo = _split_bf16(h2)
    logits_t = _dot_nt(w_hi, h_hi) + _dot_nt(w_hi, h_lo) + _dot_nt(w_lo, h_hi)
    route_ref[...] = _route(logits_t)


def _mod_spec(mod3, comp, tiles_per_group):
    rm = mod3.shape[1]
    return pl.BlockSpec((None, rm, D_MODEL), lambda i: (i // tiles_per_group, 0, comp))


def _mid(x, a, b, mod3, tiles_per_group, tile, w_out, gam, bet, w_router, alpha):
    n = x.shape[0]
    rows = lambda w: pl.BlockSpec((tile, w), lambda i: (i, 0))
    whole = lambda arr: pl.BlockSpec(arr.shape, lambda i: (0,) * arr.ndim)
    return pl.pallas_call(
        functools.partial(_mid_kernel, alpha=alpha),
        grid=(n // tile,),
        in_specs=[rows(D_MODEL), rows(A_WIDTH), rows(B_WIDTH),
                  _mod_spec(mod3, 2, tiles_per_group), _mod_spec(mod3, 3, tiles_per_group),
                  _mod_spec(mod3, 4, tiles_per_group),
                  whole(w_out), whole(gam), whole(bet), whole(w_router)],
        out_specs=[rows(D_MODEL), pl.BlockSpec((tile * ROW_CHUNKS, LANES), lambda i: (i, 0)),
                   pl.BlockSpec((SUBLANES, tile), lambda i: (0, i))],
        out_shape=(jax.ShapeDtypeStruct((n, D_MODEL), F32),
                   jax.ShapeDtypeStruct((n * ROW_CHUNKS, LANES), F32),
                   jax.ShapeDtypeStruct((SUBLANES, n), F32)),
        compiler_params=pltpu.CompilerParams(dimension_semantics=("arbitrary",), vmem_limit_bytes=VMEM_LIMIT),
        name="mid",
    )(x, a, b, mod3, mod3, mod3, w_out, gam, bet, w_router)


META_CNT, META_START, META_END, META_NACT = 0, 1, 2, 3


def _plan_kernel(route_ref, pos_ref, meta_ref, te_ref, carry_ref, starts_ref, *, nt, tm):
    phase = pl.program_id(0)
    i = pl.program_id(1)

    @pl.when(i == 0)
    def _():
        carry_ref[...] = jnp.zeros(carry_ref.shape, F32)

    eid = route_ref[0:2, :].astype(I32)
    e_iota = lax.broadcasted_iota(I32, (N_EXPERTS, nt), 0)
    hit0 = eid[0:1, :] == e_iota
    hit1 = eid[1:2, :] == e_iota
    onehot = jnp.where(hit0 | hit1, 1.0, 0.0)

    @pl.when(phase == 0)
    def _():
        pos_ref[...] = jnp.zeros(pos_ref.shape, I32)
        carry_ref[...] = carry_ref[...] + jnp.sum(onehot, axis=1, keepdims=True)

        @pl.when(i == pl.num_programs(1) - 1)
        def _():
            cnt = carry_ref[...]
            tiles = jnp.floor((cnt + float(tm - 1)) * (1.0 / tm))
            r = lax.broadcasted_iota(I32, (N_EXPERTS, N_EXPERTS), 0)
            c = lax.broadcasted_iota(I32, (N_EXPERTS, N_EXPERTS), 1)
            lower = jnp.where(r > c, 1.0, 0.0).astype(BF16)
            tiles_before = _dot(lower, jnp.broadcast_to(tiles, (N_EXPERTS, LANES)).astype(BF16))[:, 0:1]
            tiles_end = tiles_before + tiles
            starts_ref[...] = tiles_before * float(tm)

            diag = (lax.broadcasted_iota(I32, (N_EXPERTS, LANES), 0)
                    == lax.broadcasted_iota(I32, (N_EXPERTS, LANES), 1))

            def as_row(col):
                return jnp.sum(jnp.where(diag, jnp.broadcast_to(col, (N_EXPERTS, LANES)), 0.0),
                               axis=0, keepdims=True)

            nact = jnp.broadcast_to(jnp.max(tiles_end, axis=0, keepdims=True), (1, LANES))
            meta_ref[...] = jnp.concatenate(
                [as_row(cnt), as_row(tiles_before * float(tm)), as_row(tiles_end * float(tm)), nact,
                 jnp.zeros((SUBLANES - 4, LANES), F32)], axis=0).astype(I32)
            tile_i = lax.broadcasted_iota(I32, (N_EXPERTS, te_ref.shape[1]), 1).astype(F32)
            owner = jnp.sum(jnp.where(tiles_end <= tile_i, 1.0, 0.0), axis=0, keepdims=True)
            te_ref[...] = jnp.minimum(owner, float(N_EXPERTS - 1)).astype(I32)

    @pl.when(phase == 1)
    def _():
        upper = (lax.broadcasted_iota(I32, (nt, nt), 0) <= lax.broadcasted_iota(I32, (nt, nt), 1))
        incl = _dot(onehot.astype(BF16), jnp.where(upper, 1.0, 0.0).astype(BF16))
        slot = starts_ref[...] + carry_ref[...] + incl - 1.0
        pos_ref[0:1, :] = jnp.sum(jnp.where(hit0, slot, 0.0), axis=0, keepdims=True).astype(I32)
        pos_ref[1:2, :] = jnp.sum(jnp.where(hit1, slot, 0.0), axis=0, keepdims=True).astype(I32)
        carry_ref[...] = carry_ref[...] + jnp.sum(onehot, axis=1, keepdims=True)


def _plan(route_t, nt, tm):
    n = route_t.shape[1]
    n_tiles = (2 * n) // tm + N_EXPERTS
    te_width = pl.cdiv(n_tiles, LANES) * LANES
    pos, meta, te = pl.pallas_call(
        functools.partial(_plan_kernel, nt=nt, tm=tm),
        grid=(2, n // nt),
        in_specs=[pl.BlockSpec((SUBLANES, nt), lambda p, i: (0, i))],
        out_specs=[pl.BlockSpec((2, nt), lambda p, i: (0, i * p)),
                   pl.BlockSpec((SUBLANES, LANES), lambda p, i: (0, 0)),
                   pl.BlockSpec((1, te_width), lambda p, i: (0, 0))],
        out_shape=(jax.ShapeDtypeStruct((2, n), I32), jax.ShapeDtypeStruct((SUBLANES, LANES), I32),
                   jax.ShapeDtypeStruct((1, te_width), I32)),
        scratch_shapes=[pltpu.VMEM((N_EXPERTS, 1), F32), pltpu.VMEM((N_EXPERTS, 1), F32)],
        compiler_params=pltpu.CompilerParams(dimension_semantics=("arbitrary", "arbitrary")),
        name="plan",
    )(route_t)
    return pos, meta, te, n_tiles


def _row_copy(src, src_row, dst, dst_row, sem):
    return pltpu.make_async_copy(src.at[pl.ds(src_row * ROW_CHUNKS, ROW_CHUNKS)],
                                 dst.at[pl.ds(dst_row * ROW_CHUNKS, ROW_CHUNKS)], sem)


def _dispatch_kernel(meta_ref, pos_ref, hp_ref, hs_ref, xs_out, zero_ref, sem, *, tt, blocks_p, tm, n_tiles):
    i = pl.program_id(0)

    def scatter(h_ref):
        def issue(j, _):
            for k in range(2):
                _row_copy(h_ref, j, xs_out, pos_ref[k, j], sem.at[k]).start()
            return 0

        def drain(j, _):
            for k in range(2):
                _row_copy(h_ref, j, xs_out, pos_ref[k, j], sem.at[k]).wait()
            return 0

        lax.fori_loop(0, tt, issue, 0, unroll=ROW_DMA_UNROLL)
        lax.fori_loop(0, tt, drain, 0, unroll=ROW_DMA_UNROLL)

    @pl.when(i < blocks_p)
    def _():
        scatter(hp_ref)

    @pl.when(i >= blocks_p)
    def _():
        scatter(hs_ref)

    @pl.when(i == pl.num_programs(0) - 1)
    def _():
        zero_ref[...] = jnp.zeros(zero_ref.shape, F32)

        def zero_rows(first_row, n_rows):
            return pltpu.make_async_copy(zero_ref.at[pl.ds(0, n_rows * ROW_CHUNKS)],
                                         xs_out.at[pl.ds(first_row * ROW_CHUNKS, n_rows * ROW_CHUNKS)], sem.at[0])

        def start_row(r, c):
            zero_rows(r, 1).start()
            return c

        def wait_row(r, c):
            zero_rows(r, 1).wait()
            return c

        def per_expert(row_fn):
            def body(e, c):
                lo = meta_ref[META_START, e] + meta_ref[META_CNT, e]
                return lax.fori_loop(lo, meta_ref[META_END, e], row_fn, c)
            return body

        def start_tile(t, c):
            zero_rows(t * tm, tm).start()
            return c

        def wait_tile(t, c):
            zero_rows(t * tm, tm).wait()
            return c

        nact = meta_ref[META_NACT, 0]
        lax.fori_loop(0, N_EXPERTS, per_expert(start_row), 0)
        lax.fori_loop(nact, n_tiles, start_tile, 0)
        lax.fori_loop(0, N_EXPERTS, per_expert(wait_row), 0)
        lax.fori_loop(nact, n_tiles, wait_tile, 0)


def _dispatch(meta, pos, h2_p, h2_s, tt, tm, n_tiles):
    blocks_p = h2_p.shape[0] // (tt * ROW_CHUNKS)
    blocks_s = h2_s.shape[0] // (tt * ROW_CHUNKS)
    return pl.pallas_call(
        functools.partial(_dispatch_kernel, tt=tt, blocks_p=blocks_p, tm=tm, n_tiles=n_tiles),
        grid_spec=pltpu.PrefetchScalarGridSpec(
            num_scalar_prefetch=1,
            grid=(blocks_p + blocks_s,),
            in_specs=[pl.BlockSpec((2, tt), lambda i, m: (0, i), memory_space=pltpu.SMEM),
                      pl.BlockSpec((tt * ROW_CHUNKS, LANES), lambda i, m: (jnp.minimum(i, blocks_p - 1), 0)),
                      pl.BlockSpec((tt * ROW_CHUNKS, LANES), lambda i, m: (jnp.maximum(i - blocks_p, 0), 0))],
            out_specs=pl.BlockSpec(memory_space=pl.ANY),
            scratch_shapes=[pltpu.VMEM((tm * ROW_CHUNKS, LANES), F32), pltpu.SemaphoreType.DMA((2,))]),
        out_shape=jax.ShapeDtypeStruct((n_tiles * tm * ROW_CHUNKS, LANES), F32),
        compiler_params=pltpu.CompilerParams(dimension_semantics=("arbitrary",)),
        name="dispatch",
    )(meta, pos, h2_p, h2_s)


def _experts_kernel(te_ref, meta_ref, xs_ref, wg_ref, wu_ref, wd_ref, ys_ref, wg_b, wu_b, wd_b, *, tm):
    i = pl.program_id(0)
    active = i < meta_ref[META_NACT, 0]
    fresh = (i == 0) | (te_ref[0, i] != te_ref[0, jnp.maximum(i - 1, 0)])

    @pl.when(active & fresh)
    def _():
        wg_b[...] = wg_ref[...].astype(BF16)
        wu_b[...] = wu_ref[...].astype(BF16)
        wd_b[...] = wd_ref[...].astype(BF16)

    @pl.when(active)
    def _():
        x = _load_row_tiles(xs_ref, tm).astype(BF16)
        hid = (jax.nn.silu(_dot(x, wg_b[...])) * _dot(x, wu_b[...])).astype(BF16)
        _store_row_tiles(ys_ref, _dot(hid, wd_b[...]))

    @pl.when(jnp.logical_not(active))
    def _():
        ys_ref[...] = jnp.zeros(ys_ref.shape, F32)


def _experts(te, meta, xs, w_gate, w_up, w_down, tm, n_tiles):
    last_active = lambda i, m: jnp.minimum(i, m[META_NACT, 0] - 1)
    tile = pl.BlockSpec((tm * ROW_CHUNKS, LANES), lambda i, te, m: (last_active(i, m), 0))
    w_in_spec = pl.BlockSpec((None, D_MODEL, D_EXPERT), lambda i, te, m: (te[0, last_active(i, m)], 0, 0))
    w_out_spec = pl.BlockSpec((None, D_EXPERT, D_MODEL), lambda i, te, m: (te[0, last_active(i, m)], 0, 0))
    return pl.pallas_call(
        functools.partial(_experts_kernel, tm=tm),
        grid_spec=pltpu.PrefetchScalarGridSpec(
            num_scalar_prefetch=2,
            grid=(n_tiles,),
            in_specs=[tile, w_in_spec, w_in_spec, w_out_spec],
            out_specs=pl.BlockSpec((tm * ROW_CHUNKS, LANES), lambda i, te, m: (i, 0)),
            scratch_shapes=[pltpu.VMEM((D_MODEL, D_EXPERT), BF16), pltpu.VMEM((D_MODEL, D_EXPERT), BF16),
                            pltpu.VMEM((D_EXPERT, D_MODEL), BF16)]),
        out_shape=jax.ShapeDtypeStruct(xs.shape, F32),
        compiler_params=pltpu.CompilerParams(dimension_semantics=("arbitrary",), vmem_limit_bytes=VMEM_LIMIT),
        name="experts",
    )(te, meta, xs, w_gate, w_up, w_down)


def _combine_kernel(pos_ref, pos_next_ref, route_ref, x1_ref, g2_ref, gam_ref, bet_ref, ys_hbm, y_ref,
                    a0, a1, b0, b1, sem, *, tt, alpha):
    i = pl.program_id(0)
    sets = ((a0, a1), (b0, b1))

    def gather(pos, which, wait):
        def body(j, c):
            for k in range(2):
                cp = _row_copy(ys_hbm, pos[k, j], sets[which][k], j, sem.at[which, k])
                cp.wait() if wait else cp.start()
            return c
        lax.fori_loop(0, tt, body, 0, unroll=ROW_DMA_UNROLL)

    @pl.when(i == 0)
    def _():
        gather(pos_ref, 0, wait=False)

    def step(which):
        @pl.when(i + 1 < pl.num_programs(0))
        def _():
            gather(pos_next_ref, 1 - which, wait=False)

        gather(pos_ref, which, wait=True)
        route = route_ref[...]
        eye = lax.broadcasted_iota(I32, (tt, tt), 0) == lax.broadcasted_iota(I32, (tt, tt), 1)
        as_col = lambda row: jnp.sum(jnp.where(eye, jnp.broadcast_to(row, (tt, tt)), 0.0), axis=1, keepdims=True)
        buf0, buf1 = sets[which]
        f = as_col(route[2:3, :]) * _load_row_tiles(buf0, tt) + as_col(route[3:4, :]) * _load_row_tiles(buf1, tt)
        y_ref[...] = _layer_norm_rows(alpha * x1_ref[...] + g2_ref[...] * f, gam_ref[...], bet_ref[...])

    for which in range(2):
        pl.when(i % 2 == which)(functools.partial(step, which))


def _combine(pos, route_t, col_block0, x1, mod3, tiles_per_group, tt, gam, bet, ys, alpha):
    n = x1.shape[0]
    rows = lambda w: pl.BlockSpec((tt, w), lambda i: (i, 0))
    whole = lambda arr: pl.BlockSpec(arr.shape, lambda i: (0,) * arr.ndim)
    steps = n // tt
    buf = pltpu.VMEM((tt * ROW_CHUNKS, LANES), F32)
    return pl.pallas_call(
        functools.partial(_combine_kernel, tt=tt, alpha=alpha),
        grid=(steps,),
        in_specs=[pl.BlockSpec((2, tt), lambda i: (0, i + col_block0), memory_space=pltpu.SMEM),
                  pl.BlockSpec((2, tt), lambda i: (0, jnp.minimum(i + 1, steps - 1) + col_block0),
                               memory_space=pltpu.SMEM),
                  pl.BlockSpec((SUBLANES, tt), lambda i: (0, i + col_block0)),
                  rows(D_MODEL), _mod_spec(mod3, 5, tiles_per_group), whole(gam), whole(bet),
                  pl.BlockSpec(memory_space=pl.ANY)],
        out_specs=rows(D_MODEL),
        out_shape=jax.ShapeDtypeStruct((n, D_MODEL), F32),
        scratch_shapes=[buf, buf, buf, buf, pltpu.SemaphoreType.DMA((2, 2))],
        compiler_params=pltpu.CompilerParams(dimension_semantics=("arbitrary",), vmem_limit_bytes=VMEM_LIMIT),
        name="combine",
    )(pos, pos, route_t, x1, mod3, gam, bet, ys)


def kernel(x_prompt, x_sample, cache_k, cache_v, cache_kidx, page_table, c_prompt, c_sample, w_ada, b_ada, w_in,
           a_ln_g, a_ln_b, w_spatial, b_spatial, w_out, ln1_g, ln1_b, w_group_router, w_expert_router, w_gate,
           w_up, w_down, ln2_g, ln2_b):
    depth = w_ada.shape[0]
    assert depth == 1, "one trunk layer"
    alpha = (2.0 * depth) ** 0.25
    bsz, seq, d = x_prompt.shape
    db, ts, _ = x_sample.shape
    n_pages = page_table.shape[1]
    page = cache_k.shape[2]
    past = n_pages * page
    tile_p = min(PROMPT_TILE, seq)
    assert d == D_MODEL and seq % tile_p == 0 and tile_p % CHUNK == 0 and ts <= SUBLANES and page == LANES
    l = 0

    n_c = bsz + db
    n_c_pad = pl.cdiv(n_c, SUBLANES) * SUBLANES
    c_all = jnp.pad(jnp.concatenate([c_prompt, c_sample], axis=0), ((0, n_c_pad - n_c), (0, 0)))
    mod = _ada(c_all, w_ada[l], b_ada[l][None, :])
    mod_p = mod[:bsz]
    mod_s = mod[bsz:n_c]

    w_in_b = _pad_w_in(w_in[l])
    w_out_b = w_out[l].astype(BF16)
    mavg = _head_avg_matrix()
    gam_a = a_ln_g[l].reshape(1, A_WIDTH)
    bet_a = a_ln_b[l].reshape(1, A_WIDTH)
    w_router = jnp.concatenate([w_group_router[l].T, jnp.zeros((SUBLANES - N_GROUPS, d), F32),
                                w_expert_router[l].reshape(d, N_EXPERTS).T], axis=0)
    ln1 = (ln1_g[l][None, :], ln1_b[l][None, :])
    ln2 = (ln2_g[l][None, :], ln2_b[l][None, :])

    wsp = w_spatial[l].reshape(A_HEADS // 2, 2, CHUNK, CHUNK).transpose(0, 2, 1, 3).reshape(A_HEADS // 2, CHUNK, 2 * CHUNK)
    bsp = jnp.repeat(b_spatial[l].T, HEAD_DIM, axis=1)
    a_p, kt_p, vt32_p, kit_p, kb_p, kiwib_p, vt_p, qt_p, qit_p, wit_p = _front_prompt(
        x_prompt, mod_p.reshape(bsz, 6, d), w_in_b, _rot_tables(np.arange(seq)), mavg, gam_a, bet_a, wsp, bsp, tile_p)
    b_p = _attn_prompt(qit_p, wit_p, qt_p, kiwib_p, kb_p, vt_p)
    n_p = bsz * seq
    tile_m = min(MID_TILE, seq)
    assert seq % tile_m == 0
    x1_p, h2_p, route_p = _mid(x_prompt.reshape(n_p, d), a_p.reshape(n_p, A_WIDTH), b_p.reshape(n_p, B_WIDTH),
                               mod_p.reshape(bsz, 1, 6 * d), seq // tile_m, tile_m, w_out_b, *ln1, w_router, alpha)

    r_s = ts * db
    x_tm = x_sample.transpose(1, 0, 2).reshape(r_s, d)
    rt_s = _rot_tables(np.repeat(past + np.arange(ts), db))
    w_small = w_spatial[l][:, :ts, :ts]
    wl = jnp.repeat(w_small.transpose(1, 2, 0).reshape(ts * ts, A_HEADS), HEAD_DIM, axis=1)
    bl = jnp.repeat(b_spatial[l][:, :ts].T, HEAD_DIM, axis=1)
    a_s, q_s, k_s, v_s, qi_s, kiwi_s, vg_s = _front_sample(x_tm, mod_s, w_in_b, rt_s, mavg, gam_a, bet_a, wl, bl, ts, db)

    def seq_major(a):
        return a.reshape(ts, db, a.shape[-1]).transpose(1, 0, 2)

    def new_t(a, heads):
        a = a.reshape(ts, db, heads, HEAD_DIM).transpose(1, 2, 3, 0)
        return jnp.pad(a, ((0, 0), (0, 0), (0, 0), (0, LANES - ts)))

    ga = min(IDX_SEQS, db)
    assert db % ga == 0 and (ga * ts) % SUBLANES == 0
    qi4 = qi_s.reshape(ts, db, IDX_HEADS, IDX_DIM).transpose(1, 2, 0, 3).reshape(db, IDX_HEADS * ts, IDX_DIM)
    wcol = kiwi_s[:, WI_LANE:WI_LANE + IDX_HEADS].reshape(ts, db, IDX_HEADS).transpose(1, 2, 0).reshape(db, IDX_HEADS * ts, 1)
    qpos = jnp.tile(past + jnp.arange(ts, dtype=I32), db).reshape(db * ts, 1)
    kinew_t = new_t(kiwi_s[:, :IDX_DIM], 1)[:, 0]
    kidx_t = jnp.transpose(cache_kidx[l], (0, 2, 1))
    k_t = jnp.transpose(cache_k[l], (0, 2, 3, 1))
    v_t = jnp.transpose(cache_v[l], (0, 2, 3, 1))
    bias_s = _idx_sample(page_table, qi4, wcol, qpos, kinew_t, kidx_t, ts, ga)
    steps8 = lambda a: jnp.pad(seq_major(a), ((0, 0), (0, SUBLANES - ts), (0, 0)))
    b_s = _attn_sample(page_table, seq_major(q_s), bias_s, steps8(k_s), steps8(v_s), k_t, v_t)
    b_s_tm = b_s.transpose(1, 0, 2).reshape(r_s, B_WIDTH)
    x1_s, h2_s, route_s = _mid(x_tm, a_s, b_s_tm, mod_s.reshape(1, db, 6 * d), ts, db, w_out_b, *ln1, w_router, alpha)

    n_all = n_p + r_s
    tok_p = min(TOKEN_TILE, seq)
    rank_tile = min(RANK_TILE, n_all)
    assert n_all % rank_tile == 0 and n_p % tok_p == 0 and r_s % tok_p == 0 and n_p % db == 0
    route_all = jnp.concatenate([route_p, route_s], axis=1)
    pos, meta, te, n_tiles = _plan(route_all, rank_tile, MOE_TILE)
    xs = _dispatch(meta, pos, h2_p, h2_s, tok_p, MOE_TILE, n_tiles)
    ys = _experts(te, meta, xs, w_gate[l], w_up[l], w_down[l], MOE_TILE, n_tiles)
    y_p = _combine(pos, route_all, 0, x1_p, mod_p.reshape(bsz, 1, 6 * d), seq // tok_p, tok_p, *ln2, ys, alpha)
    y_s_tm = _combine(pos, route_all, n_p // db, x1_s, mod_s.reshape(1, db, 6 * d), ts, db, *ln2, ys, alpha)
    y_s = y_s_tm.reshape(ts, db, d).transpose(1, 0, 2)

    kv5 = lambda a, n, t: a.reshape(1, n, t, B_KV_HEADS, HEAD_DIM)
    kv5_t = lambda a: a.reshape(1, bsz, B_KV_HEADS, HEAD_DIM, seq).transpose(0, 1, 4, 2, 3)
    return (y_p.reshape(bsz, seq, d), y_s,
            kv5_t(kt_p), kv5_t(vt32_p), kit_p.transpose(0, 2, 1)[None],
            kv5(seq_major(k_s), db, ts), kv5(seq_major(v_s), db, ts), seq_major(kiwi_s)[..., :IDX_DIM][None],
            seq_major(vg_s).reshape(1, db, ts, A_HEADS, HEAD_DIM))
```

```python
import functools

import jax
import jax.numpy as jnp
import numpy as np
from jax import lax
from jax.experimental import pallas as pl
from jax.experimental.pallas import tpu as pltpu

F32 = jnp.float32
BF16 = jnp.bfloat16
I32 = jnp.int32

D_MODEL = 1024
HEAD_DIM = 64
A_HEADS = 8
A_WIDTH = A_HEADS * HEAD_DIM
CHUNK = 128
B_HEADS = 8
B_KV_HEADS = 4
B_GROUP = B_HEADS // B_KV_HEADS
B_WIDTH = B_HEADS * HEAD_DIM
KV_WIDTH = B_KV_HEADS * HEAD_DIM
IDX_HEADS = 4
IDX_DIM = 64
IDX_WIDTH = IDX_HEADS * IDX_DIM
TOPK_MAX = 256
ROPE_THETA = 500000.0
ROT_DIM = HEAD_DIM // 4
ROT_HALF = ROT_DIM // 2
ATTN_SCALE = HEAD_DIM ** -0.5
LOG2_E = 1.4426950408889634
Q_SCALE = ATTN_SCALE * LOG2_E
DENOM_ROWS = 16
N_GROUPS = 4
EXPERTS_PER_GROUP = 8
N_EXPERTS = N_GROUPS * EXPERTS_PER_GROUP
D_EXPERT = 512
LN_EPS = 1e-5

LANES = 128
SUBLANES = 8
ROW_CHUNKS = D_MODEL // LANES

C_AU, C_AV, C_Q, C_K, C_V, C_QI, C_KI = 0, 512, 1024, 1536, 1792, 2048, 2304
IN_WIDTH = C_KI + IDX_DIM + IDX_HEADS
IN_PAD = 2432
WI_LANE = IDX_DIM

INT_MIN = -(2 ** 31)
INT_MAX = 2 ** 31 - 1
VMEM_LIMIT = 48 * 1024 * 1024

PROMPT_TILE = 256
MOE_TILE = 256
TOKEN_TILE = 256
RANK_TILE = 1536
IDX_SEQS = 32
STEPS_PER_CHECK = 4
ROW_DMA_UNROLL = 8
MID_TILE = 512


def _dot(a, b):
    return jnp.dot(a, b, preferred_element_type=F32)


def _dot_nt(a, b):
    return lax.dot_general(a, b, (((1,), (1,)), ((), ())), preferred_element_type=F32)


def _ada_kernel(c_ref, w_ref, b_ref, o_ref):
    s = jax.nn.silu(c_ref[...]).astype(BF16)
    o_ref[...] = _dot(s, w_ref[...].astype(BF16)) + b_ref[...]


def _ada(c_all, w_ada, b_ada):
    rows = c_all.shape[0]
    n_out = w_ada.shape[1]
    tn = 1024
    return pl.pallas_call(
        _ada_kernel,
        grid=(n_out // tn,),
        in_specs=[
            pl.BlockSpec((rows, D_MODEL), lambda j: (0, 0)),
            pl.BlockSpec((D_MODEL, tn), lambda j: (0, j)),
            pl.BlockSpec((1, tn), lambda j: (0, j)),
        ],
        out_specs=pl.BlockSpec((rows, tn), lambda j: (0, j)),
        out_shape=jax.ShapeDtypeStruct((rows, n_out), F32),
        compiler_params=pltpu.CompilerParams(
            dimension_semantics=("arbitrary",), vmem_limit_bytes=VMEM_LIMIT),
        name="ada",
    )(c_all, w_ada, b_ada)


def _rotate(x, rt, blk):
    c = rt[:, 0:LANES]
    s_lo = rt[:, LANES:2 * LANES]
    s_hi = rt[:, 2 * LANES:3 * LANES]
    if blk == 1:
        head = lax.broadcasted_iota(I32, c.shape, 1) < IDX_DIM
        c = jnp.where(head, c, 1.0)
        s_lo = jnp.where(head, s_lo, 0.0)
        s_hi = jnp.where(head, s_hi, 0.0)
    outs = []
    for j in range(x.shape[1] // LANES):
        xb = x[:, j * LANES:(j + 1) * LANES]
        up = pltpu.roll(xb, LANES - ROT_HALF, 1)
        dn = pltpu.roll(xb, ROT_HALF, 1)
        outs.append(xb * c + up * s_lo + dn * s_hi)
    return outs[0] if len(outs) == 1 else jnp.concatenate(outs, axis=1)


def _head_ln(g, mavg, gam, bet):
    def seg_mean(x):
        parts = [_dot(x[:, j * 256:(j + 1) * 256].astype(BF16), mavg) for j in range(A_WIDTH // 256)]
        return jnp.concatenate(parts, axis=1)
    mu = seg_mean(g)
    d = g - mu
    var = seg_mean(d * d)
    return d * lax.rsqrt(var + LN_EPS) * gam + bet


def _project(h, w_ref, rt, mavg, gam, bet):
    u = jax.nn.gelu(_dot(h, w_ref[:, C_AU:C_AV]))
    vg = _head_ln(jax.nn.gelu(_dot(h, w_ref[:, C_AV:C_Q])), mavg, gam, bet)
    q = _rotate(_dot(h, w_ref[:, C_Q:C_K]), rt, 0) * Q_SCALE
    k = _rotate(_dot(h, w_ref[:, C_K:C_V]), rt, 0)
    v = _dot(h, w_ref[:, C_V:C_QI])
    qi = _rotate(_dot(h, w_ref[:, C_QI:C_KI]), rt, 0)
    kiwi = _rotate(_dot(h, w_ref[:, C_KI:IN_PAD]), rt, 1)
    return u, vg, q, k, v, qi, kiwi


def _front_prompt_kernel(x_ref, mod_ref, w_ref, rt_ref, mavg_ref, gam_ref, bet_ref, wsp_ref, bsp_ref,
                         a_ref, kt_ref, vt_ref, kit_ref, kb_ref, kiwib_ref, vtb_ref, qt_ref, qit_ref, wit_ref):
    shift = mod_ref[0:1, :]
    scale = mod_ref[1:2, :]
    h = (x_ref[...] * (1.0 + scale) + shift).astype(BF16)
    u, vg, q, k, v, qi, kiwi = _project(h, w_ref, rt_ref[...], mavg_ref[...], gam_ref[...], bet_ref[...])
    v_t = jnp.transpose(v)
    kiwi_t = jnp.transpose(kiwi)
    kt_ref[...] = jnp.transpose(k)
    vt_ref[...] = v_t
    kit_ref[...] = kiwi_t[0:IDX_DIM, :]
    kb_ref[...] = k.astype(BF16)
    kiwib_ref[...] = kiwi.astype(BF16)
    vtb_ref[...] = v_t.astype(BF16)
    qt_ref[...] = jnp.transpose(q).astype(BF16)
    qit_ref[...] = jnp.transpose(qi).astype(BF16)
    wit_ref[...] = kiwi_t[WI_LANE:WI_LANE + SUBLANES, :]

    rows = lax.broadcasted_iota(I32, (CHUNK, 2 * CHUNK), 0)
    cols = lax.broadcasted_iota(I32, (CHUNK, 2 * CHUNK), 1) % CHUNK
    causal = cols <= rows
    lane = lax.broadcasted_iota(I32, (CHUNK, LANES), 1)
    tt = x_ref.shape[0]
    for cidx in range(tt // CHUNK):
        rs = slice(cidx * CHUNK, (cidx + 1) * CHUNK)
        blocks = []
        for p in range(A_HEADS // 2):
            wcat = jnp.where(causal, wsp_ref[p], 0.0).astype(BF16)
            vb = vg[rs, p * LANES:(p + 1) * LANES]
            rhs = jnp.concatenate([jnp.where(lane < HEAD_DIM, vb, 0.0),
                                   jnp.where(lane >= HEAD_DIM, vb, 0.0)], axis=0).astype(BF16)
            blocks.append(_dot(wcat, rhs))
        s = jnp.concatenate(blocks, axis=1) + bsp_ref[...]
        a_ref[rs, :] = (u[rs, :] * s).astype(BF16)


def _front_prompt(x, mod, w_in, rt, mavg, gam, bet, wsp, bsp, tt):
    b, t, _ = x.shape
    nc = t // tt
    tok = lambda w: pl.BlockSpec((None, tt, w), lambda i, j: (i, j, 0))
    tr = lambda r: pl.BlockSpec((None, None, r, tt), lambda i, j: (i, j, 0, 0))
    pos_minor = lambda r: pl.BlockSpec((None, r, tt), lambda i, j: (i, 0, j))
    const2 = lambda a: pl.BlockSpec(a.shape, lambda i, j: (0,) * a.ndim)
    out_shapes = (
        jax.ShapeDtypeStruct((b, t, A_WIDTH), BF16),
        jax.ShapeDtypeStruct((b, KV_WIDTH, t), F32),
        jax.ShapeDtypeStruct((b, KV_WIDTH, t), F32),
        jax.ShapeDtypeStruct((b, IDX_DIM, t), F32),
        jax.ShapeDtypeStruct((b, t, KV_WIDTH), BF16),
        jax.ShapeDtypeStruct((b, t, LANES), BF16),
        jax.ShapeDtypeStruct((b, nc, KV_WIDTH, tt), BF16),
        jax.ShapeDtypeStruct((b, nc, B_WIDTH, tt), BF16),
        jax.ShapeDtypeStruct((b, nc, IDX_WIDTH, tt), BF16),
        jax.ShapeDtypeStruct((b, nc, SUBLANES, tt), F32),
    )
    return pl.pallas_call(
        _front_prompt_kernel,
        grid=(b, nc),
        in_specs=[
            tok(D_MODEL),
            pl.BlockSpec((None, 6, D_MODEL), lambda i, j: (i, 0, 0)),
            const2(w_in),
            pl.BlockSpec((tt, 3 * LANES), lambda i, j: (j, 0)),
            const2(mavg), const2(gam), const2(bet), const2(wsp), const2(bsp),
        ],
        out_specs=[tok(A_WIDTH), pos_minor(KV_WIDTH), pos_minor(KV_WIDTH), pos_minor(IDX_DIM), tok(KV_WIDTH),
                   tok(LANES), tr(KV_WIDTH), tr(B_WIDTH), tr(IDX_WIDTH), tr(SUBLANES)],
        out_shape=out_shapes,
        compiler_params=pltpu.CompilerParams(
            dimension_semantics=("arbitrary", "arbitrary"), vmem_limit_bytes=VMEM_LIMIT),
        name="front_prompt",
    )(x, mod, w_in, rt, mavg, gam, bet, wsp, bsp)


def _order_key(score, kpos, idx_bits):
    offs = 1 << idx_bits
    bits = pltpu.bitcast(score, I32)
    key = bits ^ ((bits >> 31) & 0x7FFFFFFF)
    key = key + jnp.where(score > 0.0, offs, 0)
    return jnp.where(score == 0.0, offs - kpos, key)


def _select_threshold(count_ge, count_gt_eq_lt, shape, topk, idx_bits):
    offs = 1 << idx_bits
    first_candidates = (offs + 1, 1)
    max_steps = -(-(32 + len(first_candidates)) // STEPS_PER_CHECK) * STEPS_PER_CHECK

    def unresolved(carry):
        i, lo, hi, n_lo = carry
        pending = (n_lo != topk) & (hi - 1 > lo)
        return (i < max_steps) & (jnp.max(pending.astype(F32)) > 0.0)

    def bisect(carry):
        i, lo, hi, n_lo = carry
        for _ in range(STEPS_PER_CHECK):
            mid = (lo >> 1) + (hi >> 1) + (lo & hi & 1)
            for step, value in enumerate(first_candidates):
                forced = jnp.where(i == step, value, INT_MIN)
                mid = jnp.where((lo < forced) & (forced < hi), forced, mid)
            tot = count_ge(mid)
            take = tot >= topk
            lo = jnp.where(take, mid, lo)
            n_lo = jnp.where(take, tot, n_lo)
            hi = jnp.where(take, hi, mid)
            i = i + 1
        return i, lo, hi, n_lo

    never = jnp.full(shape, INT_MAX, I32)
    _, thr, _, n_lo = lax.while_loop(
        unresolved, bisect, (jnp.int32(0), jnp.full(shape, INT_MIN + 1, I32), jnp.full(shape, INT_MAX, I32), never))
    tied = (n_lo > topk) & (n_lo != INT_MAX)
    big = jnp.full(shape, offs, I32)

    def resolve_ties(_):
        n_gt, _ = count_gt_eq_lt(thr, big)
        need = topk - n_gt

        def idx_step(i, cut):
            cand = cut | lax.shift_left(jnp.int32(1), idx_bits - 1 - i)
            _, n_eq = count_gt_eq_lt(thr, cand)
            return jnp.where(n_eq <= need, cand, cut)

        cut = lax.fori_loop(0, idx_bits, idx_step, jnp.zeros(shape, I32))
        return jnp.where(tied, cut, big)

    any_tied = jnp.max(tied.astype(F32)) > 0.0
    cut = lax.cond(any_tied, resolve_ties, lambda _: big, 0)
    return thr, cut


def _fold_rows(x):
    acc = x[0:SUBLANES]
    for r in range(1, x.shape[0] // SUBLANES):
        acc = acc + x[r * SUBLANES:(r + 1) * SUBLANES]
    return acc


def _col_total(cnt8):
    return jnp.sum(cnt8.astype(F32), axis=0, keepdims=True).astype(I32)


def _attn_prompt_kernel(qit_ref, wit_ref, qt_ref, kiwib_ref, kb_ref, vt_ref, o_ref,
                        key_ref, w4_ref, wq_ref, s0_ref, s1_ref, bias_ref, m_ref, acc_ref, *, tq, topk, idx_bits):
    j = pl.program_id(1)
    kc = tq
    n_kc = j + 1

    @pl.when((pl.program_id(0) == 0) & (j == 0))
    def _():
        w4_ref[...] = jnp.zeros(w4_ref.shape, BF16)
        wq_ref[...] = jnp.zeros(wq_ref.shape, BF16)

    for h in range(IDX_HEADS):
        w4_ref[0:IDX_DIM, h * tq:(h + 1) * tq] = qit_ref[h * IDX_DIM:(h + 1) * IDX_DIM, :]
    for h in range(B_HEADS):
        n = h // B_GROUP
        wq_ref[h, n * HEAD_DIM:(n + 1) * HEAD_DIM, :] = qt_ref[h * HEAD_DIM:(h + 1) * HEAD_DIM, :]

    wit = wit_ref[...]
    krow = lax.broadcasted_iota(I32, (kc, tq), 0)
    qpos = j * tq + lax.broadcasted_iota(I32, (kc, tq), 1)

    n_chunks = key_ref.shape[0] - 1

    def score_pair(i, _):
        for c in (2 * i, 2 * i + 1):
            rows = pl.multiple_of(jnp.minimum(c, n_chunks - 1) * kc, kc)
            s = _dot(kiwib_ref[pl.ds(rows, kc), :], w4_ref[...])
            tot = jnp.maximum(s[:, 0:tq], 0.0) * wit[0:1, :]
            for h in range(1, IDX_HEADS):
                tot = tot + jnp.maximum(s[:, h * tq:(h + 1) * tq], 0.0) * wit[h:h + 1, :]
            kpos = c * kc + krow
            key_ref[c] = jnp.where(kpos <= qpos, _order_key(tot, kpos, idx_bits), INT_MIN)
        return 0

    lax.fori_loop(0, (n_kc + 1) // 2, score_pair, 0)

    def count_ge(cand):
        def body(i, cnt):
            for c in (2 * i, 2 * i + 1):
                cnt = cnt + _fold_rows(jnp.where(key_ref[c] >= cand, 1, 0))
            return cnt
        return _col_total(lax.fori_loop(0, (n_kc + 1) // 2, body, jnp.zeros((SUBLANES, tq), I32)))

    def count_gt_eq_lt(thr, pos):
        def body(c, carry):
            n_gt, n_eq = carry
            key = key_ref[c]
            n_gt = n_gt + _fold_rows(jnp.where(key > thr, 1, 0))
            n_eq = n_eq + _fold_rows(jnp.where((key == thr) & (c * kc + krow < pos), 1, 0))
            return n_gt, n_eq
        z = jnp.zeros((SUBLANES, tq), I32)
        n_gt, n_eq = lax.fori_loop(0, n_kc, body, (z, z))
        return _col_total(n_gt), _col_total(n_eq)

    thr, cut = _select_threshold(count_ge, count_gt_eq_lt, (1, tq), topk, idx_bits)

    m_ref[...] = jnp.full(m_ref.shape, jnp.finfo(F32).min, F32)
    acc_ref[...] = jnp.zeros(acc_ref.shape, F32)

    def key_block(c):
        return kb_ref[pl.ds(pl.multiple_of(c * kc, kc), kc), :]

    def selection_bias(c, open_bias):
        key = key_ref[c]
        sel = (key > thr) | ((key == thr) & (c * kc + krow < cut))
        bias_ref[...] = jnp.where(sel, open_bias, -jnp.inf)

    selection_bias(0, 0.0)
    kblk0 = key_block(0)
    for h in range(B_HEADS):
        s0_ref[h] = _dot(kblk0, wq_ref[h]) + bias_ref[...]

    def stage(c, src, dst, c_next, open_next):
        selection_bias(c_next, open_next)
        kblk_next = key_block(c_next)
        vtc = vt_ref[c]
        for h in range(B_HEADS):
            n = h // B_GROUP
            dst[h] = _dot(kblk_next, wq_ref[h]) + bias_ref[...]
            s = src[h]
            m_old = m_ref[h]
            m_new = jnp.maximum(m_old, jnp.max(s, axis=0, keepdims=True))
            p = jnp.exp2(s - m_new).astype(BF16)
            v_aug = jnp.concatenate([vtc[n * HEAD_DIM:(n + 1) * HEAD_DIM, :], ones_rows], axis=0)
            acc_ref[h] = jnp.exp2(m_old - m_new) * acc_ref[h] + _dot(v_aug, p)
            m_ref[h] = m_new

    ones_rows = jnp.where(lax.broadcasted_iota(I32, (DENOM_ROWS, kc), 0) == 0, 1.0, 0.0).astype(BF16)
    last = n_kc - 1

    def attend_pair(i, _):
        c0 = 2 * i
        c1 = jnp.minimum(c0 + 1, last)
        stage(c0, s0_ref, s1_ref, c1, jnp.where(c0 + 1 <= last, 0.0, -jnp.inf))
        stage(c1, s1_ref, s0_ref, jnp.minimum(c0 + 2, last), 0.0)
        return 0

    lax.fori_loop(0, (n_kc + 1) // 2, attend_pair, 0)
    out_t = jnp.concatenate([acc_ref[h, 0:HEAD_DIM, :] / acc_ref[h, HEAD_DIM:HEAD_DIM + 1, :]
                             for h in range(B_HEADS)], axis=0)
    o_ref[...] = jnp.transpose(out_t).astype(BF16)


def _attn_prompt(qit, wit, qt, kiwib, kb, vt):
    b, nc, _, tq = qt.shape
    t = nc * tq
    topk = min(TOPK_MAX, t // 4)
    idx_bits = max(1, (t - 1).bit_length())
    tr = lambda r: pl.BlockSpec((None, None, r, tq), lambda i, j: (i, j, 0, 0))
    full = lambda w: pl.BlockSpec((None, t, w), lambda i, j: (i, 0, 0))
    kernel = functools.partial(_attn_prompt_kernel, tq=tq, topk=topk, idx_bits=idx_bits)
    return pl.pallas_call(
        kernel,
        grid=(b, nc),
        in_specs=[tr(IDX_WIDTH), tr(SUBLANES), tr(B_WIDTH), full(LANES), full(KV_WIDTH),
                  pl.BlockSpec((None, nc, KV_WIDTH, tq), lambda i, j: (i, 0, 0, 0))],
        out_specs=pl.BlockSpec((None, tq, B_WIDTH), lambda i, j: (i, j, 0)),
        out_shape=jax.ShapeDtypeStruct((b, t, B_WIDTH), BF16),
        scratch_shapes=[
            pltpu.VMEM((nc + 1, tq, tq), I32),
            pltpu.VMEM((LANES, IDX_HEADS * tq), BF16),
            pltpu.VMEM((B_HEADS, KV_WIDTH, tq), BF16),
            pltpu.VMEM((B_HEADS, tq, tq), F32),
            pltpu.VMEM((B_HEADS, tq, tq), F32),
            pltpu.VMEM((tq, tq), F32),
            pltpu.VMEM((B_HEADS, 1, tq), F32),
            pltpu.VMEM((B_HEADS, HEAD_DIM + DENOM_ROWS, tq), F32),
        ],
        compiler_params=pltpu.CompilerParams(
            dimension_semantics=("arbitrary", "arbitrary"), vmem_limit_bytes=VMEM_LIMIT),
        name="attn_prompt",
    )(qit, wit, qt, kiwib, kb, vt)


def _rot_tables(pos):
    r = pos.shape[0]
    inv_freq = np.float32(ROPE_THETA) ** (-np.arange(ROT_HALF, dtype=np.float32) * np.float32(2.0) / np.float32(ROT_DIM))
    ang = pos.astype(np.float32)[:, None] * inv_freq[None, :]
    cos, sin = np.cos(ang), np.sin(ang)
    rest = HEAD_DIM - ROT_DIM
    c64 = np.concatenate([cos, cos, np.ones((r, rest), np.float32)], axis=1)
    lo64 = np.concatenate([-sin, np.zeros((r, HEAD_DIM - ROT_HALF), np.float32)], axis=1)
    hi64 = np.concatenate([np.zeros((r, ROT_HALF), np.float32), sin, np.zeros((r, rest), np.float32)], axis=1)
    return jnp.asarray(np.concatenate([c64, c64, lo64, lo64, hi64, hi64], axis=1).astype(np.float32))


def _head_avg_matrix():
    return jnp.kron(jnp.eye(256 // HEAD_DIM, dtype=F32), jnp.full((HEAD_DIM, HEAD_DIM), 1.0 / HEAD_DIM, F32)).astype(BF16)


def _pad_w_in(w_in):
    return jnp.pad(w_in, ((0, 0), (0, IN_PAD - IN_WIDTH))).astype(BF16)


def _front_sample_kernel(x_ref, shift_ref, scale_ref, w_ref, rt_ref, mavg_ref, gam_ref, bet_ref, wl_ref, bl_ref,
                         a_ref, q_ref, k_ref, v_ref, qi_ref, kiwi_ref, vg_ref, *, ts, db):
    one_scale = 1.0 + scale_ref[...]
    shift = shift_ref[...]
    h = jnp.concatenate([x_ref[t * db:(t + 1) * db, :] * one_scale + shift for t in range(ts)], axis=0).astype(BF16)
    u, vg, q, k, v, qi, kiwi = _project(h, w_ref, rt_ref[...], mavg_ref[...], gam_ref[...], bet_ref[...])
    q_ref[...] = q.astype(BF16)
    k_ref[...] = k
    v_ref[...] = v
    qi_ref[...] = qi.astype(BF16)
    kiwi_ref[...] = kiwi
    vg_ref[...] = vg
    for t in range(ts):
        s = bl_ref[t:t + 1, :]
        for src in range(t + 1):
            s = s + wl_ref[t * ts + src:t * ts + src + 1, :] * vg[src * db:(src + 1) * db, :]
        a_ref[t * db:(t + 1) * db, :] = (u[t * db:(t + 1) * db, :] * s).astype(BF16)


def _front_sample(x_tm, mod_s, w_in, rt, mavg, gam, bet, wl, bl, ts, db):
    r = ts * db
    whole = lambda a: pl.BlockSpec(a.shape, lambda i: (0,) * a.ndim)
    out = lambda w, dt: jax.ShapeDtypeStruct((r, w), dt)
    outs = (out(A_WIDTH, BF16), out(B_WIDTH, BF16), out(KV_WIDTH, F32), out(KV_WIDTH, F32),
            out(IDX_WIDTH, BF16), out(LANES, F32), out(A_WIDTH, F32))
    return pl.pallas_call(
        functools.partial(_front_sample_kernel, ts=ts, db=db),
        grid=(1,),
        in_specs=[
            whole(x_tm),
            pl.BlockSpec((db, D_MODEL), lambda i: (0, 0)),
            pl.BlockSpec((db, D_MODEL), lambda i: (0, 1)),
            whole(w_in), whole(rt), whole(mavg), whole(gam), whole(bet), whole(wl), whole(bl),
        ],
        out_specs=[pl.BlockSpec((r, s.shape[1]), lambda i: (0, 0)) for s in outs],
        out_shape=outs,
        compiler_params=pltpu.CompilerParams(dimension_semantics=("arbitrary",), vmem_limit_bytes=VMEM_LIMIT),
        name="front_sample",
    )(x_tm, mod_s, mod_s, w_in, rt, mavg, gam, bet, wl, bl)


def _idx_sample_kernel(pt_ref, qi4_ref, wcol_ref, qpos_ref, kinew_ref, kidx_hbm, bias_ref,
                       ki_buf, key_ref, sem, *, ga, ts, n_pages, page, lpad, topk, idx_bits):
    i = pl.program_id(0)
    n_steps = pl.num_programs(0)
    past = n_pages * page
    rows = ga * ts
    slot = i % 2

    def page_copy(step, to_slot, g, p):
        phys = pt_ref[step * ga + g, p]
        return pltpu.make_async_copy(kidx_hbm.at[phys], ki_buf.at[to_slot, g, :, pl.ds(p * page, page)],
                                     sem.at[to_slot])

    def start_all(step, to_slot):
        for g in range(ga):
            for p in range(n_pages):
                page_copy(step, to_slot, g, p).start()

    @pl.when(i == 0)
    def _():
        start_all(0, 0)

    @pl.when(i + 1 < n_steps)
    def _():
        start_all(i + 1, 1 - slot)

    ki_buf[slot, :, :, pl.ds(past, LANES)] = kinew_ref[...]
    for g in range(ga):
        for p in range(n_pages):
            page_copy(i, slot, g, p).wait()

    kpos = lax.broadcasted_iota(I32, (ts, lpad), 1)
    for g in range(ga):
        s = _dot(qi4_ref[g], ki_buf[slot, g].astype(BF16))
        r = jnp.maximum(s, 0.0) * wcol_ref[g]
        tot = r[0:ts]
        for h in range(1, IDX_HEADS):
            tot = tot + r[h * ts:(h + 1) * ts]
        adm = (kpos <= qpos_ref[g * ts:(g + 1) * ts, :]) & (kpos < past + ts)
        key_ref[g * ts:(g + 1) * ts, :] = jnp.where(adm, _order_key(tot, kpos, idx_bits), INT_MIN)

    sub = lpad // LANES
    lane128 = lax.broadcasted_iota(I32, (rows, LANES), 1)

    def row_total(cnt):
        tot = jnp.sum(cnt.astype(F32), axis=1, keepdims=True)
        return jnp.broadcast_to(tot, cnt.shape).astype(I32)

    def count_ge(cand):
        cnt = jnp.zeros((rows, LANES), I32)
        for s_ in range(sub):
            cnt = cnt + jnp.where(key_ref[:, s_ * LANES:(s_ + 1) * LANES] >= cand, 1, 0)
        return row_total(cnt)

    def count_gt_eq_lt(thr, pos):
        n_gt = jnp.zeros((rows, LANES), I32)
        n_eq = jnp.zeros((rows, LANES), I32)
        for s_ in range(sub):
            kk = key_ref[:, s_ * LANES:(s_ + 1) * LANES]
            n_gt = n_gt + jnp.where(kk > thr, 1, 0)
            n_eq = n_eq + jnp.where((kk == thr) & (s_ * LANES + lane128 < pos), 1, 0)
        return row_total(n_gt), row_total(n_eq)

    thr, cut = _select_threshold(count_ge, count_gt_eq_lt, (rows, LANES), topk, idx_bits)
    for s_ in range(sub):
        kk = key_ref[:, s_ * LANES:(s_ + 1) * LANES]
        sel = (kk > thr) | ((kk == thr) & (s_ * LANES + lane128 < cut))
        bias = jnp.where(sel, 0.0, -jnp.inf)
        for g in range(ga):
            bias_ref[g, :, s_ * LANES:(s_ + 1) * LANES] = bias[g * ts:(g + 1) * ts]


def _idx_sample(page_table, qi4, wcol, qpos, kinew_t, kidx_t, ts, ga):
    db = qi4.shape[0]
    n_pages = page_table.shape[1]
    page = kidx_t.shape[2]
    past = n_pages * page
    lpad = past + LANES
    topk = min(TOPK_MAX, (past + ts) // 4)
    idx_bits = max(1, (lpad - 1).bit_length())
    kernel = functools.partial(_idx_sample_kernel, ga=ga, ts=ts, n_pages=n_pages, page=page, lpad=lpad,
                               topk=topk, idx_bits=idx_bits)
    return pl.pallas_call(
        kernel,
        grid_spec=pltpu.PrefetchScalarGridSpec(
            num_scalar_prefetch=1,
            grid=(db // ga,),
            in_specs=[pl.BlockSpec((ga, IDX_HEADS * ts, IDX_DIM), lambda i, pt: (i, 0, 0)),
                      pl.BlockSpec((ga, IDX_HEADS * ts, 1), lambda i, pt: (i, 0, 0)),
                      pl.BlockSpec((ga * ts, 1), lambda i, pt: (i, 0)),
                      pl.BlockSpec((ga, IDX_DIM, LANES), lambda i, pt: (i, 0, 0)),
                      pl.BlockSpec(memory_space=pl.ANY)],
            out_specs=pl.BlockSpec((ga, ts, lpad), lambda i, pt: (i, 0, 0)),
            scratch_shapes=[
                pltpu.VMEM((2, ga, IDX_DIM, lpad), F32),
                pltpu.VMEM((ga * ts, lpad), I32),
                pltpu.SemaphoreType.DMA((2,)),
            ]),
        out_shape=jax.ShapeDtypeStruct((db, ts, lpad), F32),
        compiler_params=pltpu.CompilerParams(dimension_semantics=("arbitrary",), vmem_limit_bytes=VMEM_LIMIT),
        name="idx_sample",
    )(page_table, qi4, wcol, qpos, kinew_t, kidx_t)


def _attn_sample_kernel(pt_ref, q_ref, bias_ref, knew_ref, vnew_ref, k_hbm, v_hbm, o_ref,
                        k_buf, v_buf, sem, *, ts, n_pages, page):
    b = pl.program_id(0)
    nb = pl.num_programs(0)
    past = n_pages * page
    slot = b % 2

    def page_copies(seq, to_slot, p):
        phys = pt_ref[seq, p]
        dst = pl.ds(p * page, page)
        return (pltpu.make_async_copy(k_hbm.at[phys], k_buf.at[to_slot, :, :, dst], sem.at[0, to_slot]),
                pltpu.make_async_copy(v_hbm.at[phys], v_buf.at[to_slot, :, :, dst], sem.at[1, to_slot]))

    def start_all(seq, to_slot):
        for p in range(n_pages):
            for cp in page_copies(seq, to_slot, p):
                cp.start()

    @pl.when(b == 0)
    def _():
        start_all(0, 0)

    @pl.when(b + 1 < nb)
    def _():
        start_all(b + 1, 1 - slot)

    steps8, lanes = lax.broadcasted_iota(I32, (SUBLANES, LANES), 0), lax.broadcasted_iota(I32, (SUBLANES, LANES), 1)
    place = jnp.where(steps8 == lanes, 1.0, 0.0).astype(BF16)
    for new_ref, buf in ((knew_ref, k_buf), (vnew_ref, v_buf)):
        new = new_ref[...].astype(BF16)
        for n in range(B_KV_HEADS):
            buf[slot, n, :, pl.ds(past, LANES)] = lax.dot_general(
                new[:, n * HEAD_DIM:(n + 1) * HEAD_DIM], place, (((0,), (0,)), ((), ())), preferred_element_type=F32)
    for p in range(n_pages):
        for cp in page_copies(b, slot, p):
            cp.wait()

    q = q_ref[...]
    bias = bias_ref[...]
    bias2 = jnp.concatenate([bias] * B_GROUP, axis=0)
    outs = [None] * B_HEADS
    for n in range(B_KV_HEADS):
        kt = k_buf[slot, n].astype(BF16)
        vt = v_buf[slot, n].astype(BF16)
        qs = jnp.concatenate([q[:, (n * B_GROUP + g) * HEAD_DIM:(n * B_GROUP + g + 1) * HEAD_DIM]
                              for g in range(B_GROUP)], axis=0)
        sc = _dot(qs, kt) + bias2
        m = jnp.max(sc, axis=1, keepdims=True)
        p_ = jnp.exp2(sc - m)
        o = _dot_nt(p_.astype(BF16), vt) / jnp.sum(p_, axis=1, keepdims=True)
        for g in range(B_GROUP):
            outs[n * B_GROUP + g] = o[g * ts:(g + 1) * ts]
    o_ref[...] = jnp.concatenate(outs, axis=1).astype(BF16)


def _attn_sample(page_table, q, bias, knew_t, vnew_t, k_t, v_t):
    db, ts, _ = q.shape
    n_pages = page_table.shape[1]
    page = k_t.shape[3]
    lpad = bias.shape[2]
    seq3 = lambda a: pl.BlockSpec((None,) + a.shape[1:], lambda b, pt: (b,) + (0,) * (a.ndim - 1))
    anyspec = pl.BlockSpec(memory_space=pl.ANY)
    kernel = functools.partial(_attn_sample_kernel, ts=ts, n_pages=n_pages, page=page)
    return pl.pallas_call(
        kernel,
        grid_spec=pltpu.PrefetchScalarGridSpec(
            num_scalar_prefetch=1,
            grid=(db,),
            in_specs=[seq3(q), seq3(bias), seq3(knew_t), seq3(vnew_t), anyspec, anyspec],
            out_specs=pl.BlockSpec((None, ts, B_WIDTH), lambda b, pt: (b, 0, 0)),
            scratch_shapes=[
                pltpu.VMEM((2, B_KV_HEADS, HEAD_DIM, lpad), F32),
                pltpu.VMEM((2, B_KV_HEADS, HEAD_DIM, lpad), F32),
                pltpu.SemaphoreType.DMA((2, 2)),
            ]),
        out_shape=jax.ShapeDtypeStruct((db, ts, B_WIDTH), BF16),
        compiler_params=pltpu.CompilerParams(dimension_semantics=("arbitrary",), vmem_limit_bytes=VMEM_LIMIT),
        name="attn_sample",
    )(page_table, q, bias, knew_t, vnew_t, k_t, v_t)


def _layer_norm_rows(y, gam, bet):
    mu = jnp.mean(y, axis=1, keepdims=True)
    d = y - mu
    var = jnp.mean(d * d, axis=1, keepdims=True)
    return d * lax.rsqrt(var + LN_EPS) * gam + bet


def _store_row_tiles(ref, val):
    r = val.shape[0]
    for c in range(ROW_CHUNKS):
        ref[pl.ds(c, r, stride=ROW_CHUNKS), :] = val[:, c * LANES:(c + 1) * LANES]


def _load_row_tiles(ref, r):
    return jnp.concatenate([ref[pl.ds(c, r, stride=ROW_CHUNKS), :] for c in range(ROW_CHUNKS)], axis=1)


ROUTER_ROWS = SUBLANES + N_EXPERTS


def _route(logits_t):
    r = logits_t.shape[1]
    far = float(LANES)

    def softmax_rows(x):
        e = jnp.exp(x - jnp.max(x, axis=0, keepdims=True))
        return e / jnp.sum(e, axis=0, keepdims=True)

    def first_max(p):
        rows = lax.broadcasted_iota(I32, p.shape, 0).astype(F32)
        best = jnp.max(p, axis=0, keepdims=True)
        return best, jnp.min(jnp.where(p == best, rows, far), axis=0, keepdims=True), rows

    g_w, g_sel, _ = first_max(softmax_rows(logits_t[0:N_GROUPS]))
    el = jnp.zeros((EXPERTS_PER_GROUP, r), F32)
    for g in range(N_GROUPS):
        lo = SUBLANES + g * EXPERTS_PER_GROUP
        el = jnp.where(g_sel == float(g), logits_t[lo:lo + EXPERTS_PER_GROUP], el)
    ep = softmax_rows(el)
    p1, i1, rows = first_max(ep)
    p2, i2, _ = first_max(jnp.where(rows == i1, -1.0, ep))
    denom = p1 + p2
    base = g_sel * float(EXPERTS_PER_GROUP)
    return jnp.concatenate([base + i1, base + i2, g_w * p1 / denom, g_w * p2 / denom,
                            jnp.zeros((SUBLANES - 4, r), F32)], axis=0)


def _split_bf16(x):
    hi = x.astype(BF16)
    return hi, (x - hi.astype(F32)).astype(BF16)


def _mid_kernel(x_ref, a_ref, b_ref, g1_ref, sh2_ref, sc2_ref, wo_ref, gam_ref, bet_ref, wr_ref,
                x1_ref, h2_ref, route_ref, *, alpha):
    mixed = _dot(a_ref[...], wo_ref[0:A_WIDTH, :]) + _dot(b_ref[...], wo_ref[A_WIDTH:A_WIDTH + B_WIDTH, :])
    x1 = _layer_norm_rows(alpha * x_ref[...] + g1_ref[...] * mixed, gam_ref[...], bet_ref[...])
    x1_ref[...] = x1
    h2 = x1 * (1.0 + sc2_ref[...]) + sh2_ref[...]
    _store_row_tiles(h2_ref, h2)
    w_hi, w_lo = _split_bf16(wr_ref[...])
    h_hi, h_lo = _split_bf16(h2)
    logits_t = _dot_nt(w_hi, h_hi) + _dot_nt(w_hi, h_lo) + _dot_nt(w_lo, h_hi)
    route_ref[...] = _route(logits_t)


def _mod_spec(mod3, comp, tiles_per_group):
    rm = mod3.shape[1]
    return pl.BlockSpec((None, rm, D_MODEL), lambda i: (i // tiles_per_group, 0, comp))


def _mid(x, a, b, mod3, tiles_per_group, tile, w_out, gam, bet, w_router, alpha):
    n = x.shape[0]
    rows = lambda w: pl.BlockSpec((tile, w), lambda i: (i, 0))
    whole = lambda arr: pl.BlockSpec(arr.shape, lambda i: (0,) * arr.ndim)
    return pl.pallas_call(
        functools.partial(_mid_kernel, alpha=alpha),
        grid=(n // tile,),
        in_specs=[rows(D_MODEL), rows(A_WIDTH), rows(B_WIDTH),
                  _mod_spec(mod3, 2, tiles_per_group), _mod_spec(mod3, 3, tiles_per_group),
                  _mod_spec(mod3, 4, tiles_per_group),
                  whole(w_out), whole(gam), whole(bet), whole(w_router)],
        out_specs=[rows(D_MODEL), pl.BlockSpec((tile * ROW_CHUNKS, LANES), lambda i: (i, 0)),
                   pl.BlockSpec((SUBLANES, tile), lambda i: (0, i))],
        out_shape=(jax.ShapeDtypeStruct((n, D_MODEL), F32),
                   jax.ShapeDtypeStruct((n * ROW_CHUNKS, LANES), F32),
                   jax.ShapeDtypeStruct((SUBLANES, n), F32)),
        compiler_params=pltpu.CompilerParams(dimension_semantics=("arbitrary",), vmem_limit_bytes=VMEM_LIMIT),
        name="mid",
    )(x, a, b, mod3, mod3, mod3, w_out, gam, bet, w_router)


META_CNT, META_START, META_END, META_NACT = 0, 1, 2, 3


def _plan_kernel(route_ref, pos_ref, meta_ref, te_ref, carry_ref, starts_ref, *, nt, tm):
    phase = pl.program_id(0)
    i = pl.program_id(1)

    @pl.when(i == 0)
    def _():
        carry_ref[...] = jnp.zeros(carry_ref.shape, F32)

    eid = route_ref[0:2, :].astype(I32)
    e_iota = lax.broadcasted_iota(I32, (N_EXPERTS, nt), 0)
    hit0 = eid[0:1, :] == e_iota
    hit1 = eid[1:2, :] == e_iota
    onehot = jnp.where(hit0 | hit1, 1.0, 0.0)

    @pl.when(phase == 0)
    def _():
        pos_ref[...] = jnp.zeros(pos_ref.shape, I32)
        carry_ref[...] = carry_ref[...] + jnp.sum(onehot, axis=1, keepdims=True)

        @pl.when(i == pl.num_programs(1) - 1)
        def _():
            cnt = carry_ref[...]
            tiles = jnp.floor((cnt + float(tm - 1)) * (1.0 / tm))
            r = lax.broadcasted_iota(I32, (N_EXPERTS, N_EXPERTS), 0)
            c = lax.broadcasted_iota(I32, (N_EXPERTS, N_EXPERTS), 1)
            lower = jnp.where(r > c, 1.0, 0.0).astype(BF16)
            tiles_before = _dot(lower, jnp.broadcast_to(tiles, (N_EXPERTS, LANES)).astype(BF16))[:, 0:1]
            tiles_end = tiles_before + tiles
            starts_ref[...] = tiles_before * float(tm)

            diag = (lax.broadcasted_iota(I32, (N_EXPERTS, LANES), 0)
                    == lax.broadcasted_iota(I32, (N_EXPERTS, LANES), 1))

            def as_row(col):
                return jnp.sum(jnp.where(diag, jnp.broadcast_to(col, (N_EXPERTS, LANES)), 0.0),
                               axis=0, keepdims=True)

            nact = jnp.broadcast_to(jnp.max(tiles_end, axis=0, keepdims=True), (1, LANES))
            meta_ref[...] = jnp.concatenate(
                [as_row(cnt), as_row(tiles_before * float(tm)), as_row(tiles_end * float(tm)), nact,
                 jnp.zeros((SUBLANES - 4, LANES), F32)], axis=0).astype(I32)
            tile_i = lax.broadcasted_iota(I32, (N_EXPERTS, te_ref.shape[1]), 1).astype(F32)
            owner = jnp.sum(jnp.where(tiles_end <= tile_i, 1.0, 0.0), axis=0, keepdims=True)
            te_ref[...] = jnp.minimum(owner, float(N_EXPERTS - 1)).astype(I32)

    @pl.when(phase == 1)
    def _():
        upper = (lax.broadcasted_iota(I32, (nt, nt), 0) <= lax.broadcasted_iota(I32, (nt, nt), 1))
        incl = _dot(onehot.astype(BF16), jnp.where(upper, 1.0, 0.0).astype(BF16))
        slot = starts_ref[...] + carry_ref[...] + incl - 1.0
        pos_ref[0:1, :] = jnp.sum(jnp.where(hit0, slot, 0.0), axis=0, keepdims=True).astype(I32)
        pos_ref[1:2, :] = jnp.sum(jnp.where(hit1, slot, 0.0), axis=0, keepdims=True).astype(I32)
        carry_ref[...] = carry_ref[...] + jnp.sum(onehot, axis=1, keepdims=True)


def _plan(route_t, nt, tm):
    n = route_t.shape[1]
    n_tiles = (2 * n) // tm + N_EXPERTS
    te_width = pl.cdiv(n_tiles, LANES) * LANES
    pos, meta, te = pl.pallas_call(
        functools.partial(_plan_kernel, nt=nt, tm=tm),
        grid=(2, n // nt),
        in_specs=[pl.BlockSpec((SUBLANES, nt), lambda p, i: (0, i))],
        out_specs=[pl.BlockSpec((2, nt), lambda p, i: (0, i * p)),
                   pl.BlockSpec((SUBLANES, LANES), lambda p, i: (0, 0)),
                   pl.BlockSpec((1, te_width), lambda p, i: (0, 0))],
        out_shape=(jax.ShapeDtypeStruct((2, n), I32), jax.ShapeDtypeStruct((SUBLANES, LANES), I32),
                   jax.ShapeDtypeStruct((1, te_width), I32)),
        scratch_shapes=[pltpu.VMEM((N_EXPERTS, 1), F32), pltpu.VMEM((N_EXPERTS, 1), F32)],
        compiler_params=pltpu.CompilerParams(dimension_semantics=("arbitrary", "arbitrary")),
        name="plan",
    )(route_t)
    return pos, meta, te, n_tiles


def _row_copy(src, src_row, dst, dst_row, sem):
    return pltpu.make_async_copy(src.at[pl.ds(src_row * ROW_CHUNKS, ROW_CHUNKS)],
                                 dst.at[pl.ds(dst_row * ROW_CHUNKS, ROW_CHUNKS)], sem)


def _dispatch_kernel(meta_ref, pos_ref, hp_ref, hs_ref, xs_out, zero_ref, sem, *, tt, blocks_p, tm, n_tiles):
    i = pl.program_id(0)

    def scatter(h_ref):
        def issue(j, _):
            for k in range(2):
                _row_copy(h_ref, j, xs_out, pos_ref[k, j], sem.at[k]).start()
            return 0

        def drain(j, _):
            for k in range(2):
                _row_copy(h_ref, j, xs_out, pos_ref[k, j], sem.at[k]).wait()
            return 0

        lax.fori_loop(0, tt, issue, 0, unroll=ROW_DMA_UNROLL)
        lax.fori_loop(0, tt, drain, 0, unroll=ROW_DMA_UNROLL)

    @pl.when(i < blocks_p)
    def _():
        scatter(hp_ref)

    @pl.when(i >= blocks_p)
    def _():
        scatter(hs_ref)

    @pl.when(i == pl.num_programs(0) - 1)
    def _():
        zero_ref[...] = jnp.zeros(zero_ref.shape, F32)

        def zero_rows(first_row, n_rows):
            return pltpu.make_async_copy(zero_ref.at[pl.ds(0, n_rows * ROW_CHUNKS)],
                                         xs_out.at[pl.ds(first_row * ROW_CHUNKS, n_rows * ROW_CHUNKS)], sem.at[0])

        def start_row(r, c):
            zero_rows(r, 1).start()
            return c

        def wait_row(r, c):
            zero_rows(r, 1).wait()
            return c

        def per_expert(row_fn):
            def body(e, c):
                lo = meta_ref[META_START, e] + meta_ref[META_CNT, e]
                return lax.fori_loop(lo, meta_ref[META_END, e], row_fn, c)
            return body

        def start_tile(t, c):
            zero_rows(t * tm, tm).start()
            return c

        def wait_tile(t, c):
            zero_rows(t * tm, tm).wait()
            return c

        nact = meta_ref[META_NACT, 0]
        lax.fori_loop(0, N_EXPERTS, per_expert(start_row), 0)
        lax.fori_loop(nact, n_tiles, start_tile, 0)
        lax.fori_loop(0, N_EXPERTS, per_expert(wait_row), 0)
        lax.fori_loop(nact, n_tiles, wait_tile, 0)


def _dispatch(meta, pos, h2_p, h2_s, tt, tm, n_tiles):
    blocks_p = h2_p.shape[0] // (tt * ROW_CHUNKS)
    blocks_s = h2_s.shape[0] // (tt * ROW_CHUNKS)
    return pl.pallas_call(
        functools.partial(_dispatch_kernel, tt=tt, blocks_p=blocks_p, tm=tm, n_tiles=n_tiles),
        grid_spec=pltpu.PrefetchScalarGridSpec(
            num_scalar_prefetch=1,
            grid=(blocks_p + blocks_s,),
            in_specs=[pl.BlockSpec((2, tt), lambda i, m: (0, i), memory_space=pltpu.SMEM),
                      pl.BlockSpec((tt * ROW_CHUNKS, LANES), lambda i, m: (jnp.minimum(i, blocks_p - 1), 0)),
                      pl.BlockSpec((tt * ROW_CHUNKS, LANES), lambda i, m: (jnp.maximum(i - blocks_p, 0), 0))],
            out_specs=pl.BlockSpec(memory_space=pl.ANY),
            scratch_shapes=[pltpu.VMEM((tm * ROW_CHUNKS, LANES), F32), pltpu.SemaphoreType.DMA((2,))]),
        out_shape=jax.ShapeDtypeStruct((n_tiles * tm * ROW_CHUNKS, LANES), F32),
        compiler_params=pltpu.CompilerParams(dimension_semantics=("arbitrary",)),
        name="dispatch",
    )(meta, pos, h2_p, h2_s)


def _experts_kernel(te_ref, meta_ref, xs_ref, wg_ref, wu_ref, wd_ref, ys_ref, wg_b, wu_b, wd_b, *, tm):
    i = pl.program_id(0)
    active = i < meta_ref[META_NACT, 0]
    fresh = (i == 0) | (te_ref[0, i] != te_ref[0, jnp.maximum(i - 1, 0)])

    @pl.when(active & fresh)
    def _():
        wg_b[...] = wg_ref[...].astype(BF16)
        wu_b[...] = wu_ref[...].astype(BF16)
        wd_b[...] = wd_ref[...].astype(BF16)

    @pl.when(active)
    def _():
        x = _load_row_tiles(xs_ref, tm).astype(BF16)
        hid = (jax.nn.silu(_dot(x, wg_b[...])) * _dot(x, wu_b[...])).astype(BF16)
        _store_row_tiles(ys_ref, _dot(hid, wd_b[...]))

    @pl.when(jnp.logical_not(active))
    def _():
        ys_ref[...] = jnp.zeros(ys_ref.shape, F32)


def _experts(te, meta, xs, w_gate, w_up, w_down, tm, n_tiles):
    last_active = lambda i, m: jnp.minimum(i, m[META_NACT, 0] - 1)
    tile = pl.BlockSpec((tm * ROW_CHUNKS, LANES), lambda i, te, m: (last_active(i, m), 0))
    w_in_spec = pl.BlockSpec((None, D_MODEL, D_EXPERT), lambda i, te, m: (te[0, last_active(i, m)], 0, 0))
    w_out_spec = pl.BlockSpec((None, D_EXPERT, D_MODEL), lambda i, te, m: (te[0, last_active(i, m)], 0, 0))
    return pl.pallas_call(
        functools.partial(_experts_kernel, tm=tm),
        grid_spec=pltpu.PrefetchScalarGridSpec(
            num_scalar_prefetch=2,
            grid=(n_tiles,),
            in_specs=[tile, w_in_spec, w_in_spec, w_out_spec],
            out_specs=pl.BlockSpec((tm * ROW_CHUNKS, LANES), lambda i, te, m: (i, 0)),
            scratch_shapes=[pltpu.VMEM((D_MODEL, D_EXPERT), BF16), pltpu.VMEM((D_MODEL, D_EXPERT), BF16),
                            pltpu.VMEM((D_EXPERT, D_MODEL), BF16)]),
        out_shape=jax.ShapeDtypeStruct(xs.shape, F32),
        compiler_params=pltpu.CompilerParams(dimension_semantics=("arbitrary",), vmem_limit_bytes=VMEM_LIMIT),
        name="experts",
    )(te, meta, xs, w_gate, w_up, w_down)


def _combine_kernel(pos_ref, pos_next_ref, route_ref, x1_ref, g2_ref, gam_ref, bet_ref, ys_hbm, y_ref,
                    a0, a1, b0, b1, sem, *, tt, alpha):
    i = pl.program_id(0)
    sets = ((a0, a1), (b0, b1))

    def gather(pos, which, wait):
        def body(j, c):
            for k in range(2):
                cp = _row_copy(ys_hbm, pos[k, j], sets[which][k], j, sem.at[which, k])
                cp.wait() if wait else cp.start()
            return c
        lax.fori_loop(0, tt, body, 0, unroll=ROW_DMA_UNROLL)

    @pl.when(i == 0)
    def _():
        gather(pos_ref, 0, wait=False)

    def step(which):
        @pl.when(i + 1 < pl.num_programs(0))
        def _():
            gather(pos_next_ref, 1 - which, wait=False)

        gather(pos_ref, which, wait=True)
        route = route_ref[...]
        eye = lax.broadcasted_iota(I32, (tt, tt), 0) == lax.broadcasted_iota(I32, (tt, tt), 1)
        as_col = lambda row: jnp.sum(jnp.where(eye, jnp.broadcast_to(row, (tt, tt)), 0.0), axis=1, keepdims=True)
        buf0, buf1 = sets[which]
        f = as_col(route[2:3, :]) * _load_row_tiles(buf0, tt) + as_col(route[3:4, :]) * _load_row_tiles(buf1, tt)
        y_ref[...] = _layer_norm_rows(alpha * x1_ref[...] + g2_ref[...] * f, gam_ref[...], bet_ref[...])

    for which in range(2):
        pl.when(i % 2 == which)(functools.partial(step, which))


def _combine(pos, route_t, col_block0, x1, mod3, tiles_per_group, tt, gam, bet, ys, alpha):
    n = x1.shape[0]
    rows = lambda w: pl.BlockSpec((tt, w), lambda i: (i, 0))
    whole = lambda arr: pl.BlockSpec(arr.shape, lambda i: (0,) * arr.ndim)
    steps = n // tt
    buf = pltpu.VMEM((tt * ROW_CHUNKS, LANES), F32)
    return pl.pallas_call(
        functools.partial(_combine_kernel, tt=tt, alpha=alpha),
        grid=(steps,),
        in_specs=[pl.BlockSpec((2, tt), lambda i: (0, i + col_block0), memory_space=pltpu.SMEM),
                  pl.BlockSpec((2, tt), lambda i: (0, jnp.minimum(i + 1, steps - 1) + col_block0),
                               memory_space=pltpu.SMEM),
                  pl.BlockSpec((SUBLANES, tt), lambda i: (0, i + col_block0)),
                  rows(D_MODEL), _mod_spec(mod3, 5, tiles_per_group), whole(gam), whole(bet),
                  pl.BlockSpec(memory_space=pl.ANY)],
        out_specs=rows(D_MODEL),
        out_shape=jax.ShapeDtypeStruct((n, D_MODEL), F32),
        scratch_shapes=[buf, buf, buf, buf, pltpu.SemaphoreType.DMA((2, 2))],
        compiler_params=pltpu.CompilerParams(dimension_semantics=("arbitrary",), vmem_limit_bytes=VMEM_LIMIT),
        name="combine",
    )(pos, pos, route_t, x1, mod3, gam, bet, ys)


def kernel(x_prompt, x_sample, cache_k, cache_v, cache_kidx, page_table, c_prompt, c_sample, w_ada, b_ada, w_in,
           a_ln_g, a_ln_b, w_spatial, b_spatial, w_out, ln1_g, ln1_b, w_group_router, w_expert_router, w_gate,
           w_up, w_down, ln2_g, ln2_b):
    depth = w_ada.shape[0]
    assert depth == 1, "one trunk layer"
    alpha = (2.0 * depth) ** 0.25
    bsz, seq, d = x_prompt.shape
    db, ts, _ = x_sample.shape
    n_pages = page_table.shape[1]
    page = cache_k.shape[2]
    past = n_pages * page
    tile_p = min(PROMPT_TILE, seq)
    assert d == D_MODEL and seq % tile_p == 0 and tile_p % CHUNK == 0 and ts <= SUBLANES and page == LANES
    l = 0

    n_c = bsz + db
    n_c_pad = pl.cdiv(n_c, SUBLANES) * SUBLANES
    c_all = jnp.pad(jnp.concatenate([c_prompt, c_sample], axis=0), ((0, n_c_pad - n_c), (0, 0)))
    mod = _ada(c_all, w_ada[l], b_ada[l][None, :])
    mod_p = mod[:bsz]
    mod_s = mod[bsz:n_c]

    w_in_b = _pad_w_in(w_in[l])
    w_out_b = w_out[l].astype(BF16)
    mavg = _head_avg_matrix()
    gam_a = a_ln_g[l].reshape(1, A_WIDTH)
    bet_a = a_ln_b[l].reshape(1, A_WIDTH)
    w_router = jnp.concatenate([w_group_router[l].T, jnp.zeros((SUBLANES - N_GROUPS, d), F32),
                                w_expert_router[l].reshape(d, N_EXPERTS).T], axis=0)
    ln1 = (ln1_g[l][None, :], ln1_b[l][None, :])
    ln2 = (ln2_g[l][None, :], ln2_b[l][None, :])

    wsp = w_spatial[l].reshape(A_HEADS // 2, 2, CHUNK, CHUNK).transpose(0, 2, 1, 3).reshape(A_HEADS // 2, CHUNK, 2 * CHUNK)
    bsp = jnp.repeat(b_spatial[l].T, HEAD_DIM, axis=1)
    a_p, kt_p, vt32_p, kit_p, kb_p, kiwib_p, vt_p, qt_p, qit_p, wit_p = _front_prompt(
        x_prompt, mod_p.reshape(bsz, 6, d), w_in_b, _rot_tables(np.arange(seq)), mavg, gam_a, bet_a, wsp, bsp, tile_p)
    b_p = _attn_prompt(qit_p, wit_p, qt_p, kiwib_p, kb_p, vt_p)
    n_p = bsz * seq
    tile_m = min(MID_TILE, seq)
    assert seq % tile_m == 0
    x1_p, h2_p, route_p = _mid(x_prompt.reshape(n_p, d), a_p.reshape(n_p, A_WIDTH), b_p.reshape(n_p, B_WIDTH),
                               mod_p.reshape(bsz, 1, 6 * d), seq // tile_m, tile_m, w_out_b, *ln1, w_router, alpha)

    r_s = ts * db
    x_tm = x_sample.transpose(1, 0, 2).reshape(r_s, d)
    rt_s = _rot_tables(np.repeat(past + np.arange(ts), db))
    w_small = w_spatial[l][:, :ts, :ts]
    wl = jnp.repeat(w_small.transpose(1, 2, 0).reshape(ts * ts, A_HEADS), HEAD_DIM, axis=1)
    bl = jnp.repeat(b_spatial[l][:, :ts].T, HEAD_DIM, axis=1)
    a_s, q_s, k_s, v_s, qi_s, kiwi_s, vg_s = _front_sample(x_tm, mod_s, w_in_b, rt_s, mavg, gam_a, bet_a, wl, bl, ts, db)

    def seq_major(a):
        return a.reshape(ts, db, a.shape[-1]).transpose(1, 0, 2)

    def new_t(a, heads):
        a = a.reshape(ts, db, heads, HEAD_DIM).transpose(1, 2, 3, 0)
        return jnp.pad(a, ((0, 0), (0, 0), (0, 0), (0, LANES - ts)))

    ga = min(IDX_SEQS, db)
    assert db % ga == 0 and (ga * ts) % SUBLANES == 0
    qi4 = qi_s.reshape(ts, db, IDX_HEADS, IDX_DIM).transpose(1, 2, 0, 3).reshape(db, IDX_HEADS * ts, IDX_DIM)
    wcol = kiwi_s[:, WI_LANE:WI_LANE + IDX_HEADS].reshape(ts, db, IDX_HEADS).transpose(1, 2, 0).reshape(db, IDX_HEADS * ts, 1)
    qpos = jnp.tile(past + jnp.arange(ts, dtype=I32), db).reshape(db * ts, 1)
    kinew_t = new_t(kiwi_s[:, :IDX_DIM], 1)[:, 0]
    kidx_t = jnp.transpose(cache_kidx[l], (0, 2, 1))
    k_t = jnp.transpose(cache_k[l], (0, 2, 3, 1))
    v_t = jnp.transpose(cache_v[l], (0, 2, 3, 1))
    bias_s = _idx_sample(page_table, qi4, wcol, qpos, kinew_t, kidx_t, ts, ga)
    steps8 = lambda a: jnp.pad(seq_major(a), ((0, 0), (0, SUBLANES - ts), (0, 0)))
    b_s = _attn_sample(page_table, seq_major(q_s), bias_s, steps8(k_s), steps8(v_s), k_t, v_t)
    b_s_tm = b_s.transpose(1, 0, 2).reshape(r_s, B_WIDTH)
    x1_s, h2_s, route_s = _mid(x_tm, a_s, b_s_tm, mod_s.reshape(1, db, 6 * d), ts, db, w_out_b, *ln1, w_router, alpha)

    n_all = n_p + r_s
    tok_p = min(TOKEN_TILE, seq)
    rank_tile = min(RANK_TILE, n_all)
    assert n_all % rank_tile == 0 and n_p % tok_p == 0 and r_s % tok_p == 0 and n_p % db == 0
    route_all = jnp.concatenate([route_p, route_s], axis=1)
    pos, meta, te, n_tiles = _plan(route_all, rank_tile, MOE_TILE)
    xs = _dispatch(meta, pos, h2_p, h2_s, tok_p, MOE_TILE, n_tiles)
    ys = _experts(te, meta, xs, w_gate[l], w_up[l], w_down[l], MOE_TILE, n_tiles)
    y_p = _combine(pos, route_all, 0, x1_p, mod_p.reshape(bsz, 1, 6 * d), seq // tok_p, tok_p, *ln2, ys, alpha)
    y_s_tm = _combine(pos, route_all, n_p // db, x1_s, mod_s.reshape(1, db, 6 * d), ts, db, *ln2, ys, alpha)
    y_s = y_s_tm.reshape(ts, db, d).transpose(1, 0, 2)

    kv5 = lambda a, n, t: a.reshape(1, n, t, B_KV_HEADS, HEAD_DIM)
    kv5_t = lambda a: a.reshape(1, bsz, B_KV_HEADS, HEAD_DIM, seq).transpose(0, 1, 4, 2, 3)
    return (y_p.reshape(bsz, seq, d), y_s,
            kv5_t(kt_p), kv5_t(vt32_p), kit_p.transpose(0, 2, 1)[None],
            kv5(seq_major(k_s), db, ts), kv5(seq_major(v_s), db, ts), seq_major(kiwi_s)[..., :IDX_DIM][None],
            seq_major(vg_s).reshape(1, db, ts, A_HEADS, HEAD_DIM))
```
